```python
import jax
import jax.numpy as jnp
from jax import lax
import numpy as np

D_MODEL = 2048
BATCH = 2
SEQ = 4096
DEPTH = 1
DEC_BATCH = 32
DEC_SEQ = 8
PAST_LEN = 16384
PAGE_SIZE = 128

HEAD_DIM = 64
NSA_HEADS = D_MODEL // (2 * HEAD_DIM)
NSA_KV_HEADS = NSA_HEADS // 4
NSA_GROUP = NSA_HEADS // NSA_KV_HEADS
NSA_WIDTH = NSA_HEADS * HEAD_DIM
KV_WIDTH = NSA_KV_HEADS * 2 * HEAD_DIM
CMP_BLOCK = 32
CMP_STRIDE = 16
CMP_HIDDEN = 2 * HEAD_DIM
SEL_BLOCK = 64
SEL_TOPK = 16
WINDOW = 512
Q_BLOCK = 128
ROT_DIM = HEAD_DIM // 4
ROPE_THETA = 500000.0
RWKV_HEADS = D_MODEL // (2 * HEAD_DIM)
RWKV_WIDTH = RWKV_HEADS * HEAD_DIM
DECAY_LORA = max(32, int(round(1.8 * D_MODEL ** 0.5 / 32)) * 32)
AAA_LORA = DECAY_LORA
GATE_LORA = max(32, int(round(0.6 * D_MODEL ** 0.8 / 32)) * 32)
RWKV_PROJ = 3 * RWKV_WIDTH + DECAY_LORA + AAA_LORA + GATE_LORA
NSA_PROJ = NSA_WIDTH + 3 * KV_WIDTH + 3 * NSA_HEADS
N_IN = NSA_PROJ + RWKV_PROJ + 2 * D_MODEL
D_FF = 4 * D_MODEL
RMS_EPS = 1e-6
GN_EPS = HEAD_DIM * 1e-5
NEG_INF = -1e30
FORCE_SCORE = 1e6
CMP_PER_SEL = SEL_BLOCK // CMP_STRIDE
CMP_SUB = CMP_BLOCK // CMP_STRIDE
BLOCKS_PER_PAGE = PAGE_SIZE // SEL_BLOCK

kernel_name = 'nsa_rwkv7_parallel_hybrid_step'


def _offsets(sizes):
    out, acc = [], 0
    for s in sizes[:-1]:
        acc += s
        out.append(acc)
    return out


def rms_norm(x, g):
    xf = x.astype(jnp.float32)
    y = xf * lax.rsqrt(jnp.mean(xf * xf, axis=-1, keepdims=True) + RMS_EPS)
    return (y * g.astype(jnp.float32)).astype(x.dtype)


def rotary(x, pos):
    half = ROT_DIM // 2
    freqs = jnp.power(jnp.float32(ROPE_THETA), -jnp.arange(half, dtype=jnp.float32) * 2.0 / ROT_DIM)
    ang = pos.astype(jnp.float32)[:, None] * freqs[None, :]
    cos = jnp.cos(ang)[None, :, None, :]
    sin = jnp.sin(ang)[None, :, None, :]
    xf = x.astype(jnp.float32)
    x1, x2 = xf[..., :half], xf[..., half:ROT_DIM]
    out = jnp.concatenate([x1 * cos - x2 * sin, x2 * cos + x1 * sin, xf[..., ROT_DIM:]], axis=-1)
    return out.astype(x.dtype)


def rotate_kv(kv, pos):
    return jnp.stack([rotary(kv[..., 0, :], pos), kv[..., 1, :]], axis=-2)


def masked_softmax(s, mask):
    s = jnp.where(mask, s.astype(jnp.float32), NEG_INF)
    m = jnp.max(s, axis=-1, keepdims=True)
    e = jnp.where(mask, jnp.exp(s - m), 0.0)
    return e / jnp.maximum(jnp.sum(e, axis=-1, keepdims=True), 1e-30)


def pad_time(z, mult):
    pad = (-z.shape[1]) % mult
    return jnp.pad(z, [(0, 0), (0, pad)] + [(0, 0)] * (z.ndim - 2))


def chunk_proj(rows, cmp_w1):
    b, l = rows.shape[:2]
    ch = rows.reshape(b, l // CMP_STRIDE, CMP_STRIDE, NSA_KV_HEADS, 2, HEAD_DIM)
    first = jnp.einsum('bcrhkd,krde->bchke', ch, cmp_w1[:, :CMP_STRIDE])
    second = jnp.einsum('bcrhkd,krde->bchke', ch, cmp_w1[:, CMP_STRIDE:])
    return first, second


def compress(first, second, lp):
    pe_term = jnp.einsum('krd,krde->ke', lp['cmp_pe'], lp['cmp_w1'])
    h = jax.nn.gelu(first[:, :-1] + second[:, 1:] + pe_term + lp['cmp_b1'])
    return jnp.einsum('bchke,ked->bchkd', h, lp['cmp_w2']) + lp['cmp_b2']


def gather_blocks(blocks, idx):
    bb = jnp.arange(blocks.shape[0])[:, None, None, None]
    hh = jnp.arange(NSA_KV_HEADS)[None, None, :, None]
    return blocks[bb, idx, :, hh]


def nsa_core(q, q_pos, kv_cmp, cmp_end, fetch_sel, n_sel, kv_win, win_pos, gates):
    f32 = jnp.float32
    scale = HEAD_DIM ** -0.5
    s_c = jnp.einsum('bqhgd,bchd->bqhgc', q, kv_cmp[..., 0, :], preferred_element_type=f32) * scale
    vis_c = (cmp_end[None, :] <= q_pos[:, None])[None, :, None, None, :]
    p_c = masked_softmax(s_c, vis_c)
    o_c = jnp.einsum('bqhgc,bchd->bqhgd', p_c, kv_cmp[..., 1, :].astype(f32))
    imp = jnp.sum(p_c, axis=3)
    nc = imp.shape[-1]
    imp = jnp.pad(imp, ((0, 0), (0, 0), (0, 0), (0, CMP_PER_SEL * n_sel + CMP_PER_SEL + CMP_SUB - nc)))
    p_slc = sum(imp[..., m + n: m + n + CMP_PER_SEL * n_sel: CMP_PER_SEL]
                for m in range(CMP_PER_SEL) for n in range(CMP_SUB))
    blk = jnp.arange(n_sel)
    cur = (q_pos // SEL_BLOCK)[:, None]
    vis_s = blk[None, :] * SEL_BLOCK <= q_pos[:, None]
    forced = (blk[None, :] == 0) | (blk[None, :] == cur) | (blk[None, :] == cur - 1)
    score = jnp.where(forced[None, :, None, :], FORCE_SCORE, p_slc)
    score = jnp.where(vis_s[None, :, None, :], score, -1.0)
    top_s, idx = lax.top_k(score, min(SEL_TOPK, n_sel))
    valid = top_s >= 0.0
    sel = fetch_sel(idx)
    bq, tq, h, k_top = idx.shape
    s_s = jnp.einsum('bqhgd,bqhkld->bqhgkl', q, sel[..., 0, :], preferred_element_type=f32) * scale
    tok = idx[..., None] * SEL_BLOCK + jnp.arange(SEL_BLOCK)
    m_s = valid[..., None] & (tok <= q_pos[None, :, None, None, None])
    p_s = masked_softmax(s_s.reshape(bq, tq, h, NSA_GROUP, k_top * SEL_BLOCK),
                         m_s.reshape(bq, tq, h, 1, k_top * SEL_BLOCK))
    o_s = jnp.einsum('bqhgn,bqhnd->bqhgd', p_s,
                     sel[..., 1, :].reshape(bq, tq, h, k_top * SEL_BLOCK, HEAD_DIM).astype(f32))
    s_w = jnp.einsum('bqhgd,blhd->bqhgl', q, kv_win[..., 0, :], preferred_element_type=f32) * scale
    diff = q_pos[:, None] - win_pos[None, :]
    m_w = ((diff >= 0) & (diff <= WINDOW) & (win_pos[None, :] >= 0))[None, :, None, None, :]
    p_w = masked_softmax(s_w, m_w)
    o_w = jnp.einsum('bqhgl,blhd->bqhgd', p_w, kv_win[..., 1, :].astype(f32))
    g = gates.astype(f32)
    return (g[..., 0:1] * o_c + g[..., 1:2] * o_s + g[..., 2:3] * o_w).astype(q.dtype)


def project(xn, w_in, pos):
    b, t = xn.shape[:2]
    proj = xn @ w_in
    q, kvc, kvs, kvw, g_nsa, p_rw, g_merge = jnp.split(
        proj, _offsets([NSA_WIDTH, KV_WIDTH, KV_WIDTH, KV_WIDTH, 3 * NSA_HEADS, RWKV_PROJ, 2 * D_MODEL]), axis=-1)
    q = rotary(q.reshape(b, t, NSA_HEADS, HEAD_DIM), pos).reshape(b, t, NSA_KV_HEADS, NSA_GROUP, HEAD_DIM)
    kvc = rotate_kv(kvc.reshape(b, t, NSA_KV_HEADS, 2, HEAD_DIM), pos)
    kvs = rotate_kv(kvs.reshape(b, t, NSA_KV_HEADS, 2, HEAD_DIM), pos)
    kvw = rotate_kv(kvw.reshape(b, t, NSA_KV_HEADS, 2, HEAD_DIM), pos)
    g_nsa = jax.nn.sigmoid(g_nsa).reshape(b, t, NSA_KV_HEADS, NSA_GROUP, 3)
    g_a, g_b = jnp.split(jax.nn.sigmoid(g_merge), 2, axis=-1)
    return q, kvc, kvs, kvw, g_nsa, p_rw, g_a, g_b


def rwkv7_mix(p, p_prev, s0, lp):
    f32 = jnp.float32
    b, t = p.shape[:2]
    pf = p.astype(f32)
    shifted = jnp.concatenate([p_prev.astype(f32)[:, None], pf[:, :-1]], axis=1)
    xm = pf + (shifted - pf) * lp['rwkv_mu']
    r, k, v, xw, xa, xg = jnp.split(
        xm, _offsets([RWKV_WIDTH, RWKV_WIDTH, RWKV_WIDTH, DECAY_LORA, AAA_LORA, GATE_LORA]), axis=-1)
    w = -jax.nn.softplus(-(lp['rwkv_w0'] + jnp.tanh(xw) @ lp['rwkv_w2'])) - 0.5
    decay = jnp.exp(-jnp.exp(w))
    a = jax.nn.sigmoid(lp['rwkv_a0'] + xa @ lp['rwkv_a2'])
    g = jax.nn.sigmoid(xg) @ lp['rwkv_g2']
    heads = lambda z: z.reshape(b, t, RWKV_HEADS, HEAD_DIM)
    kk = heads(k * lp['rwkv_k_k'])
    kk = kk / jnp.maximum(jnp.sqrt(jnp.sum(kk * kk, axis=-1, keepdims=True)), 1e-12)
    k = heads(k * (1.0 + (a - 1.0) * lp['rwkv_k_a']))
    r, v, decay, a = heads(r), heads(v), heads(decay), heads(a)

    def step(S, inp):
        r_t, d_t, k_t, v_t, kk_t, a_t = inp
        sa = jnp.einsum('bhvk,bhk->bhv', S, -kk_t)
        S = (S * d_t[:, :, None, :] + sa[..., None] * (kk_t * a_t)[:, :, None, :]
             + v_t[..., None] * k_t[:, :, None, :])
        return S, jnp.einsum('bhvk,bhk->bhv', S, r_t)

    tm = lambda z: jnp.moveaxis(z, 1, 0)
    s_final, y = lax.scan(step, s0.astype(f32), (tm(r), tm(decay), tm(k), tm(v), tm(kk), tm(a)))
    y = jnp.moveaxis(y, 0, 1)
    mu = jnp.mean(y, axis=-1, keepdims=True)
    var = jnp.mean(jnp.square(y - mu), axis=-1, keepdims=True)
    y = ((y - mu) * lax.rsqrt(var + GN_EPS)).reshape(b, t, RWKV_WIDTH) * lp['rwkv_ln_g'] + lp['rwkv_ln_b']
    bonus = jnp.sum(r * k * lp['rwkv_r_k'], axis=-1, keepdims=True) * v
    y = (y + bonus.reshape(b, t, RWKV_WIDTH)) * g
    return y.astype(p.dtype), s_final, p[:, -1]


def merge_and_ffn(x, o_nsa, o_rwkv, g_a, g_b, lp):
    mix = g_a * (o_nsa @ lp['p_nsa']) + g_b * (o_rwkv @ lp['p_rwkv'])
    h = x + mix @ lp['w_out']
    hn = rms_norm(h, lp['norm2_g'])
    return h + jnp.square(jax.nn.relu(hn @ lp['mlp_w1'])) @ lp['mlp_w2']


def prompt_layer(x, lp):
    b, t = x.shape[:2]
    pos = jnp.arange(t, dtype=jnp.int32)
    xn = rms_norm(x, lp['norm1_g'])
    q, kvc, kvs, kvw, g_nsa, p_rw, g_a, g_b = project(xn, lp['w_in'], pos)
    kv_cmp = compress(*chunk_proj(kvc, lp['cmp_w1']), lp)
    cmp_end = jnp.arange(kv_cmp.shape[1], dtype=jnp.int32) * CMP_STRIDE + CMP_BLOCK - 1
    n_sel = t // SEL_BLOCK
    sel_blocks = kvs.reshape(b, n_sel, SEL_BLOCK, NSA_KV_HEADS, 2, HEAD_DIM)
    fetch = lambda idx: gather_blocks(sel_blocks, idx)
    kvw_pad = jnp.pad(kvw, ((0, 0), (WINDOW, 0), (0, 0), (0, 0), (0, 0)))

    def q_block(i):
        s = i * Q_BLOCK
        q_pos = s + jnp.arange(Q_BLOCK, dtype=jnp.int32)
        win_pos = s - WINDOW + jnp.arange(WINDOW + Q_BLOCK, dtype=jnp.int32)
        return nsa_core(lax.dynamic_slice_in_dim(q, s, Q_BLOCK, axis=1), q_pos, kv_cmp, cmp_end, fetch, n_sel,
                        lax.dynamic_slice_in_dim(kvw_pad, s, WINDOW + Q_BLOCK, axis=1), win_pos,
                        lax.dynamic_slice_in_dim(g_nsa, s, Q_BLOCK, axis=1))

    o = lax.map(q_block, jnp.arange(t // Q_BLOCK, dtype=jnp.int32))
    o_nsa = jnp.moveaxis(o, 0, 1).reshape(b, t, NSA_WIDTH)
    o_rw, s_rw, shift = rwkv7_mix(p_rw, jnp.zeros((b, RWKV_PROJ), x.dtype),
                                  jnp.zeros((b, RWKV_HEADS, HEAD_DIM, HEAD_DIM), jnp.float32), lp)
    y = merge_and_ffn(x, o_nsa, o_rw, g_a, g_b, lp)
    win = kvw[:, t - min(WINDOW, t):]
    return y, (kvc, kvs, win, s_rw, shift)


def sample_layer(x, l, cache_cmp_kv, cache_sel_kv, win_buf, s_rw, shift_prev, page_table, lp):
    b, t = x.shape[:2]
    pos = PAST_LEN + jnp.arange(t, dtype=jnp.int32)
    xn = rms_norm(x, lp['norm1_g'])
    q, kvc, kvs, kvw, g_nsa, p_rw, g_a, g_b = project(xn, lp['w_in'], pos)
    past_cmp = cache_cmp_kv[l, page_table].reshape(b, PAST_LEN, NSA_KV_HEADS, 2, HEAD_DIM)
    fp, sp = chunk_proj(past_cmp, lp['cmp_w1'])
    fn, sn = chunk_proj(pad_time(kvc, CMP_STRIDE), lp['cmp_w1'])
    kv_cmp = compress(jnp.concatenate([fp, fn], axis=1), jnp.concatenate([sp, sn], axis=1), lp)
    cmp_end = jnp.arange(kv_cmp.shape[1], dtype=jnp.int32) * CMP_STRIDE + CMP_BLOCK - 1
    n_past_blk = PAST_LEN // SEL_BLOCK
    new_blocks = pad_time(kvs, SEL_BLOCK).reshape(b, -1, SEL_BLOCK, NSA_KV_HEADS, 2, HEAD_DIM)
    n_new_blk = new_blocks.shape[1]
    pool_blocks = cache_sel_kv.reshape(DEPTH, -1, SEL_BLOCK, NSA_KV_HEADS, 2, HEAD_DIM)

    def fetch(idx):
        bb = jnp.arange(b)[:, None, None, None]
        hh = jnp.arange(NSA_KV_HEADS)[None, None, :, None]
        jp = jnp.minimum(idx, n_past_blk - 1)
        phys = page_table[bb, jp // BLOCKS_PER_PAGE] * BLOCKS_PER_PAGE + jp % BLOCKS_PER_PAGE
        ll = jnp.zeros_like(phys) + l
        from_past = pool_blocks[ll, phys, :, hh]
        from_new = gather_blocks(new_blocks, jnp.clip(idx - n_past_blk, 0, n_new_blk - 1))
        return jnp.where((idx < n_past_blk)[..., None, None, None], from_past, from_new)

    kv_win = jnp.concatenate([win_buf.astype(kvw.dtype), kvw], axis=1)
    w_buf = win_buf.shape[1]
    win_pos = PAST_LEN - w_buf + jnp.arange(w_buf + t, dtype=jnp.int32)
    o = nsa_core(q, pos, kv_cmp, cmp_end, fetch, n_past_blk + n_new_blk, kv_win, win_pos, g_nsa)
    o_nsa = o.reshape(b, t, NSA_WIDTH)
    o_rw, s_new, shift_new = rwkv7_mix(p_rw, shift_prev, s_rw, lp)
    y = merge_and_ffn(x, o_nsa, o_rw, g_a, g_b, lp)
    return y, (kvc, kvs, kv_win[:, t:], s_new, shift_new)


def setup_inputs(seed: int = 0) -> dict:
    key = jax.random.key(seed)
    ks = jax.random.split(key, 40)
    f32 = jnp.float32
    nrm = lambda k, shape, scale=1.0: scale * jax.random.normal(k, shape, f32)
    n_pages = PAST_LEN // PAGE_SIZE
    n_pool = (DEC_BATCH * n_pages * 5) // 4
    w_buf = min(WINDOW, PAST_LEN)
    page_table = jax.random.permutation(ks[7], n_pool)[:DEC_BATCH * n_pages].reshape(
        DEC_BATCH, n_pages).astype(jnp.int32)
    return {
        'x_prompt': nrm(ks[0], (BATCH, SEQ, D_MODEL)),
        'x_sample': nrm(ks[1], (DEC_BATCH, DEC_SEQ, D_MODEL)),
        'cache_cmp_kv': nrm(ks[2], (DEPTH, n_pool, PAGE_SIZE, NSA_KV_HEADS, 2, HEAD_DIM)),
        'cache_sel_kv': nrm(ks[3], (DEPTH, n_pool, PAGE_SIZE, NSA_KV_HEADS, 2, HEAD_DIM)),
        'state_nsa_win': nrm(ks[4], (DEPTH, DEC_BATCH, w_buf, NSA_KV_HEADS, 2, HEAD_DIM)),
        'state_rwkv': nrm(ks[5], (DEPTH, DEC_BATCH, RWKV_HEADS, HEAD_DIM, HEAD_DIM), 0.5),
        'state_rwkv_shift': nrm(ks[6], (DEPTH, DEC_BATCH, RWKV_PROJ)),
        'page_table': page_table,
        'norm1_g': 1.0 + nrm(ks[8], (DEPTH, D_MODEL), 0.05),
        'w_in': nrm(ks[9], (DEPTH, D_MODEL, N_IN), D_MODEL ** -0.5),
        'cmp_pe': nrm(ks[10], (DEPTH, 2, CMP_BLOCK, HEAD_DIM), 0.5),
        'cmp_w1': nrm(ks[11], (DEPTH, 2, CMP_BLOCK, HEAD_DIM, CMP_HIDDEN), (CMP_BLOCK * HEAD_DIM) ** -0.5),
        'cmp_b1': nrm(ks[12], (DEPTH, 2, CMP_HIDDEN), 0.02),
        'cmp_w2': nrm(ks[13], (DEPTH, 2, CMP_HIDDEN, HEAD_DIM), CMP_HIDDEN ** -0.5),
        'cmp_b2': nrm(ks[14], (DEPTH, 2, HEAD_DIM), 0.02),
        'rwkv_mu': jax.random.uniform(ks[15], (DEPTH, RWKV_PROJ), f32),
        'rwkv_w0': nrm(ks[16], (DEPTH, RWKV_WIDTH), 0.5),
        'rwkv_w2': nrm(ks[17], (DEPTH, DECAY_LORA, RWKV_WIDTH), 0.5 * DECAY_LORA ** -0.5),
        'rwkv_a0': nrm(ks[18], (DEPTH, RWKV_WIDTH), 0.5),
        'rwkv_a2': nrm(ks[19], (DEPTH, AAA_LORA, RWKV_WIDTH), AAA_LORA ** -0.5),
        'rwkv_g2': nrm(ks[20], (DEPTH, GATE_LORA, RWKV_WIDTH), GATE_LORA ** -0.5),
        'rwkv_k_k': 1.0 + nrm(ks[21], (DEPTH, RWKV_WIDTH), 0.1),
        'rwkv_k_a': 1.0 + nrm(ks[22], (DEPTH, RWKV_WIDTH), 0.1),
        'rwkv_r_k': nrm(ks[23], (DEPTH, RWKV_HEADS, HEAD_DIM), 0.3),
        'rwkv_ln_g': 1.0 + nrm(ks[24], (DEPTH, RWKV_WIDTH), 0.05),
        'rwkv_ln_b': nrm(ks[25], (DEPTH, RWKV_WIDTH), 0.02),
        'p_nsa': nrm(ks[26], (DEPTH, NSA_WIDTH, D_MODEL), NSA_WIDTH ** -0.5),
        'p_rwkv': nrm(ks[27], (DEPTH, RWKV_WIDTH, D_MODEL), RWKV_WIDTH ** -0.5),
        'w_out': nrm(ks[28], (DEPTH, D_MODEL, D_MODEL), D_MODEL ** -0.5),
        'norm2_g': 1.0 + nrm(ks[29], (DEPTH, D_MODEL), 0.05),
        'mlp_w1': nrm(ks[30], (DEPTH, D_MODEL, D_FF), D_MODEL ** -0.5),
        'mlp_w2': nrm(ks[31], (DEPTH, D_FF, D_MODEL), D_FF ** -0.5),
        'final_g': 1.0 + nrm(ks[32], (D_MODEL,), 0.05),
    }


def reference(x_prompt, x_sample, cache_cmp_kv, cache_sel_kv, state_nsa_win, state_rwkv, state_rwkv_shift,
              page_table, norm1_g, w_in, cmp_pe, cmp_w1, cmp_b1, cmp_w2, cmp_b2, rwkv_mu, rwkv_w0, rwkv_w2,
              rwkv_a0, rwkv_a2, rwkv_g2, rwkv_k_k, rwkv_k_a, rwkv_r_k, rwkv_ln_g, rwkv_ln_b, p_nsa, p_rwkv,
              w_out, norm2_g, mlp_w1, mlp_w2, final_g):
    hp, hs = x_prompt, x_sample
    st_p, st_s = [], []
    for l in range(DEPTH):
        lp = {'norm1_g': norm1_g[l], 'w_in': w_in[l], 'cmp_pe': cmp_pe[l], 'cmp_w1': cmp_w1[l],
              'cmp_b1': cmp_b1[l], 'cmp_w2': cmp_w2[l], 'cmp_b2': cmp_b2[l], 'rwkv_mu': rwkv_mu[l],
              'rwkv_w0': rwkv_w0[l], 'rwkv_w2': rwkv_w2[l], 'rwkv_a0': rwkv_a0[l], 'rwkv_a2': rwkv_a2[l],
              'rwkv_g2': rwkv_g2[l], 'rwkv_k_k': rwkv_k_k[l], 'rwkv_k_a': rwkv_k_a[l], 'rwkv_r_k': rwkv_r_k[l],
              'rwkv_ln_g': rwkv_ln_g[l], 'rwkv_ln_b': rwkv_ln_b[l], 'p_nsa': p_nsa[l], 'p_rwkv': p_rwkv[l],
              'w_out': w_out[l], 'norm2_g': norm2_g[l], 'mlp_w1': mlp_w1[l], 'mlp_w2': mlp_w2[l]}
        hp, sp = prompt_layer(hp, lp)
        hs, ss = sample_layer(hs, l, cache_cmp_kv, cache_sel_kv, state_nsa_win[l], state_rwkv[l],
                              state_rwkv_shift[l], page_table, lp)
        st_p.append(sp)
        st_s.append(ss)

    def stacked(states, i):
        return jnp.stack([s[i] for s in states])

    y_prompt = rms_norm(hp, final_g)
    y_sample = rms_norm(hs, final_g)
    return (y_prompt, y_sample,
            stacked(st_p, 0), stacked(st_s, 0),
            stacked(st_p, 1), stacked(st_s, 1),
            stacked(st_p, 2), stacked(st_s, 2),
            stacked(st_p, 3), stacked(st_s, 3),
            stacked(st_p, 4), stacked(st_s, 4))
```

```python
import functools

import jax
import jax.numpy as jnp
from jax import lax
from jax.experimental import pallas as pl
from jax.experimental.pallas import tpu as pltpu

D_MODEL = 2048
DEPTH = 1
PAST_LEN = 16384
PAGE_SIZE = 128

HEAD_DIM = 64
NSA_HEADS = D_MODEL // (2 * HEAD_DIM)
NSA_KV_HEADS = NSA_HEADS // 4
NSA_GROUP = NSA_HEADS // NSA_KV_HEADS
NSA_WIDTH = NSA_HEADS * HEAD_DIM
KV_WIDTH = NSA_KV_HEADS * 2 * HEAD_DIM
CMP_BLOCK = 32
CMP_STRIDE = 16
CMP_HIDDEN = 2 * HEAD_DIM
SEL_BLOCK = 64
SEL_TOPK = 16
WINDOW = 512
Q_BLOCK = 128
ROT_DIM = HEAD_DIM // 4
ROPE_THETA = 500000.0
RWKV_HEADS = D_MODEL // (2 * HEAD_DIM)
RWKV_WIDTH = RWKV_HEADS * HEAD_DIM
DECAY_LORA = max(32, int(round(1.8 * D_MODEL ** 0.5 / 32)) * 32)
AAA_LORA = DECAY_LORA
GATE_LORA = max(32, int(round(0.6 * D_MODEL ** 0.8 / 32)) * 32)
RWKV_PROJ = 3 * RWKV_WIDTH + DECAY_LORA + AAA_LORA + GATE_LORA
NSA_PROJ = NSA_WIDTH + 3 * KV_WIDTH + 3 * NSA_HEADS
N_IN = NSA_PROJ + RWKV_PROJ + 2 * D_MODEL
D_FF = 4 * D_MODEL
RMS_EPS = 1e-6
GN_EPS = HEAD_DIM * 1e-5
NEG_INF = -1e30
FORCE_SCORE = 1e6

LANES = 128
SUBLANES = 8
VMEM_LIMIT_BYTES = 56 * 1024 * 1024

RW0 = 0
LORA0 = 3 * RWKV_WIDTH
GN0 = RWKV_PROJ
RW_PAD = 3584
LORA_PAD = RW_PAD - LORA0
Q0 = RW_PAD
KVC0 = Q0 + NSA_WIDTH
KVS0 = KVC0 + KV_WIDTH
KVW0 = KVS0 + KV_WIDTH
GM0 = KVW0 + KV_WIDTH
P_WIDTH = GM0 + 2 * D_MODEL
PROJ_TN = 512

f32 = jnp.float32
bf16 = jnp.bfloat16


def _cparams(sem):
    return pltpu.CompilerParams(dimension_semantics=sem, vmem_limit_bytes=VMEM_LIMIT_BYTES)


def _rms(x, g):
    return x * lax.rsqrt(jnp.mean(x * x, axis=-1, keepdims=True) + RMS_EPS) * g


def _dot(a, b):
    return jnp.dot(a, b, preferred_element_type=f32)


def _dot_nt(a, b):
    return lax.dot_general(a, b, (((1,), (1,)), ((), ())), preferred_element_type=f32)


def _split_dot(x, e):
    hi = x.astype(bf16)
    lo = (x - hi.astype(f32)).astype(bf16)
    return _dot(hi, e) + _dot(lo, e)


def _rot_store(acc, tab_ref, o_ref):
    c, s1, s2 = tab_ref[0], tab_ref[1], tab_ref[2]
    for s in range(acc.shape[1] // LANES):
        x = acc[:, s * LANES:(s + 1) * LANES]
        o_ref[:, s * LANES:(s + 1) * LANES] = (
            x * c + pltpu.roll(x, LANES - ROT_DIM // 2, 1) * s1 + pltpu.roll(x, ROT_DIM // 2, 1) * s2)


def _proj_body(x_ref, g_ref, w_ref, tq_ref, tkv_ref, o_ref, xn_ref):
    j = pl.program_id(1)

    @pl.when(j == 0)
    def _():
        xn_ref[...] = _rms(x_ref[...], g_ref[...]).astype(bf16)

    acc = _dot(xn_ref[...], w_ref[...])
    gn_tile = GN0 // PROJ_TN

    @pl.when(j < gn_tile)
    def _():
        o_ref[...] = acc

    @pl.when(j == gn_tile)
    def _():
        lane = lax.broadcasted_iota(jnp.int32, acc.shape, 1)
        o_ref[...] = jnp.where(lane >= GN0 - gn_tile * PROJ_TN, jax.nn.sigmoid(acc), acc)

    @pl.when((j >= Q0 // PROJ_TN) & (j < KVC0 // PROJ_TN))
    def _():
        _rot_store(acc, tq_ref, o_ref)

    @pl.when((j >= KVC0 // PROJ_TN) & (j < GM0 // PROJ_TN))
    def _():
        _rot_store(acc, tkv_ref, o_ref)

    @pl.when(j >= GM0 // PROJ_TN)
    def _():
        o_ref[...] = jax.nn.sigmoid(acc)


def input_projection(x2d, g, w_all, tq, tkv, tm):
    m = x2d.shape[0]
    nt = tq.shape[1] // tm
    return pl.pallas_call(
        _proj_body,
        grid=(m // tm, P_WIDTH // PROJ_TN),
        in_specs=[pl.BlockSpec((tm, D_MODEL), lambda i, j: (i, 0)),
                  pl.BlockSpec((1, D_MODEL), lambda i, j: (0, 0)),
                  pl.BlockSpec((D_MODEL, PROJ_TN), lambda i, j: (0, j)),
                  pl.BlockSpec((3, tm, LANES), lambda i, j: (0, i % nt, 0)),
                  pl.BlockSpec((3, tm, LANES), lambda i, j: (0, i % nt, 0))],
        out_specs=pl.BlockSpec((tm, PROJ_TN), lambda i, j: (i, j)),
        out_shape=jax.ShapeDtypeStruct((m, P_WIDTH), f32),
        scratch_shapes=[pltpu.VMEM((tm, D_MODEL), bf16)],
        compiler_params=_cparams(("arbitrary", "arbitrary")),
        name="input_projection",
    )(x2d, g.reshape(1, D_MODEL), w_all, tq, tkv)


def _rot_tables(pos):
    half = ROT_DIM // 2
    freqs = jnp.power(jnp.float32(ROPE_THETA), -jnp.arange(half, dtype=f32) * 2.0 / ROT_DIM)
    ang = pos.astype(f32)[:, None] * freqs[None, :]
    cos, sin = jnp.cos(ang), jnp.sin(ang)
    lane = jnp.arange(LANES)

    def build(period):
        l = lane % period
        fi = l % half
        c = jnp.where(l < ROT_DIM, cos[:, fi], 1.0)
        s1 = jnp.where(l < half, -sin[:, fi], 0.0)
        s2 = jnp.where((l >= half) & (l < ROT_DIM), sin[:, fi], 0.0)
        return jnp.stack([c, s1, s2]).astype(f32)

    return build(HEAD_DIM), build(2 * HEAD_DIM)


def _merge_body(on_ref, or_ref, ga_ref, gb_ref, x_ref, pn_ref, pr_ref, wo_ref, g2_ref, h_ref, hn_ref):
    a = _dot(on_ref[...].astype(bf16), pn_ref[...])
    b = _dot(or_ref[...].astype(bf16), pr_ref[...])
    mix = ga_ref[...] * a + gb_ref[...] * b
    h = x_ref[...] + _dot(mix.astype(bf16), wo_ref[...])
    h_ref[...] = h
    hn_ref[...] = _rms(h, g2_ref[...]).astype(bf16)


def merge_project(o_nsa, o_rw, p_all, x2d, pn, pr, wo, g2, tm):
    m = x2d.shape[0]
    const = lambda i: (0, 0)
    return pl.pallas_call(
        _merge_body,
        grid=(m // tm,),
        in_specs=[pl.BlockSpec((tm, NSA_WIDTH), lambda i: (i, 0)),
                  pl.BlockSpec((tm, RWKV_WIDTH), lambda i: (i, 0)),
                  pl.BlockSpec((tm, D_MODEL), lambda i: (i, GM0 // D_MODEL)),
                  pl.BlockSpec((tm, D_MODEL), lambda i: (i, GM0 // D_MODEL + 1)),
                  pl.BlockSpec((tm, D_MODEL), lambda i: (i, 0)),
                  pl.BlockSpec((NSA_WIDTH, D_MODEL), const),
                  pl.BlockSpec((RWKV_WIDTH, D_MODEL), const),
                  pl.BlockSpec((D_MODEL, D_MODEL), const),
                  pl.BlockSpec((1, D_MODEL), const)],
        out_specs=[pl.BlockSpec((tm, D_MODEL), lambda i: (i, 0)),
                   pl.BlockSpec((tm, D_MODEL), lambda i: (i, 0))],
        out_shape=[jax.ShapeDtypeStruct((m, D_MODEL), f32),
                   jax.ShapeDtypeStruct((m, D_MODEL), bf16)],
        compiler_params=_cparams(("arbitrary",)),
        name="merge_project",
    )(o_nsa, o_rw, p_all, p_all, x2d, pn, pr, wo, g2.reshape(1, D_MODEL))


def _mlp_body(hn_ref, h_ref, w1_ref, w2_ref, fg_ref, y_ref, acc_ref):
    f = pl.program_id(1)
    u = jnp.square(jnp.maximum(_dot(hn_ref[...], w1_ref[...]), 0.0)).astype(bf16)
    contrib = _dot(u, w2_ref[...])

    @pl.when(f == 0)
    def _():
        acc_ref[...] = contrib

    @pl.when(f > 0)
    def _():
        acc_ref[...] += contrib

    @pl.when(f == pl.num_programs(1) - 1)
    def _():
        y_ref[...] = _rms(h_ref[...] + acc_ref[...], fg_ref[...])


def mlp_residual_norm(hn, h, w1, w2, fg, tm, tf=512):
    m = h.shape[0]
    return pl.pallas_call(
        _mlp_body,
        grid=(m // tm, D_FF // tf),
        in_specs=[pl.BlockSpec((tm, D_MODEL), lambda i, f: (i, 0)),
                  pl.BlockSpec((tm, D_MODEL), lambda i, f: (i, 0)),
                  pl.BlockSpec((D_MODEL, tf), lambda i, f: (0, f)),
                  pl.BlockSpec((tf, D_MODEL), lambda i, f: (f, 0)),
                  pl.BlockSpec((1, D_MODEL), lambda i, f: (0, 0))],
        out_specs=pl.BlockSpec((tm, D_MODEL), lambda i, f: (i, 0)),
        out_shape=jax.ShapeDtypeStruct((m, D_MODEL), f32),
        scratch_shapes=[pltpu.VMEM((tm, D_MODEL), f32)],
        compiler_params=_cparams(("arbitrary", "arbitrary")),
        name="mlp_residual_norm",
    )(hn, h, w1, w2, fg.reshape(1, D_MODEL))


def _rwkv_prep_body(*refs, tc, seq, has_prev):
    if has_prev:
        p_ref, halo_ref, prev_ref = refs[:3]
        refs = refs[3:]
    else:
        p_ref, halo_ref = refs[:2]
        prev_ref = None
        refs = refs[2:]
    (mu_ref, vec_ref, w2_ref, a2_ref, g2_ref, e_ref,
     r_out, d_out, k_out, v_out, kap_out, b_out, g_out, bonus_out, sh_ref) = refs
    i = pl.program_id(0)
    p = p_ref[...]
    sh_ref[0:SUBLANES, :] = halo_ref[...]
    sh_ref[SUBLANES:SUBLANES + tc, :] = p
    rolled = sh_ref[SUBLANES - 1:SUBLANES - 1 + tc, :]
    t_in_seq = (i * tc + lax.broadcasted_iota(jnp.int32, (tc, 1), 0)) % seq
    first = prev_ref[...] if has_prev else jnp.zeros_like(p)
    shifted = jnp.where(t_in_seq == 0, first, rolled)
    xm = p + (shifted - p) * mu_ref[...]
    r = xm[:, 0:RWKV_WIDTH]
    k = xm[:, RWKV_WIDTH:2 * RWKV_WIDTH]
    v = xm[:, 2 * RWKV_WIDTH:3 * RWKV_WIDTH]
    tail = xm[:, LORA0:RW_PAD]
    w0, a0, k_k, k_a, r_k = (vec_ref[n:n + 1, :] for n in range(5))
    w = -jax.nn.softplus(-(w0 + _dot(jnp.tanh(tail).astype(bf16), w2_ref[...]))) - 0.5
    a = jax.nn.sigmoid(a0 + _dot(tail.astype(bf16), a2_ref[...]))
    kk = k * k_k
    e = e_ref[...]
    kap = kk / jnp.maximum(jnp.sqrt(_split_dot(kk * kk, e)), 1e-12)
    kn = k * (1.0 + (a - 1.0) * k_a)
    r_out[...] = r
    d_out[...] = jnp.exp(-jnp.exp(w))
    k_out[...] = kn
    v_out[...] = v
    kap_out[...] = kap
    b_out[...] = kap * a
    g_out[...] = _dot(jax.nn.sigmoid(tail).astype(bf16), g2_ref[...])
    bonus_out[...] = _split_dot(r * kn * r_k, e) * v


def rwkv_prepare(p_all, prev_exp, mu, vecs, w2p, a2p, g2p, e, seq, tc):
    m = p_all.shape[0]
    has_prev = prev_exp is not None
    row = lambda i: (i, 0)
    const = lambda i: (0, 0)
    halo = lambda i: (jnp.maximum(i * (tc // SUBLANES) - 1, 0), 0)
    in_specs = [pl.BlockSpec((tc, RW_PAD), row), pl.BlockSpec((SUBLANES, RW_PAD), halo)]
    args = [p_all, p_all]
    if has_prev:
        in_specs.append(pl.BlockSpec((tc, RW_PAD), row))
        args.append(prev_exp)
    in_specs += [pl.BlockSpec((1, RW_PAD), const), pl.BlockSpec((SUBLANES, RWKV_WIDTH), const),
                 pl.BlockSpec((LORA_PAD, RWKV_WIDTH), const), pl.BlockSpec((LORA_PAD, RWKV_WIDTH), const),
                 pl.BlockSpec((LORA_PAD, RWKV_WIDTH), const), pl.BlockSpec((RWKV_WIDTH, RWKV_WIDTH), const)]
    args += [mu, vecs, w2p, a2p, g2p, e]
    return pl.pallas_call(
        functools.partial(_rwkv_prep_body, tc=tc, seq=seq, has_prev=has_prev),
        grid=(m // tc,),
        in_specs=in_specs,
        out_specs=[pl.BlockSpec((tc, RWKV_WIDTH), row)] * 8,
        out_shape=[jax.ShapeDtypeStruct((m, RWKV_WIDTH), f32)] * 8,
        scratch_shapes=[pltpu.VMEM((tc + SUBLANES, RW_PAD), f32)],
        compiler_params=_cparams(("arbitrary",)),
        name="rwkv_prepare",
    )(*args)


def _rwkv_scan_body(r_ref, d_ref, k_ref, v_ref, kap_ref, b_ref, s0_ref, y_ref, st_ref, s_ref, *, tc, nb):
    c = pl.program_id(1)
    pairs = RWKV_HEADS // 2
    zero = jnp.zeros((HEAD_DIM, HEAD_DIM), f32)

    @pl.when(c == 0)
    def _():
        for bi in range(nb):
            for h in range(RWKV_HEADS):
                s0 = s0_ref[bi, h]
                s_ref[bi * RWKV_HEADS + h] = jnp.concatenate([s0, zero] if h % 2 == 0 else [zero, s0], axis=1)

    lane = lax.broadcasted_iota(jnp.int32, (HEAD_DIM, LANES), 1)
    sub = lax.broadcasted_iota(jnp.int32, (HEAD_DIM, LANES), 0)
    lane_row = lax.broadcasted_iota(jnp.int32, (1, LANES), 1)
    diag = [lane == sub + HEAD_DIM * par for par in range(2)]
    half = [((lane_row >= HEAD_DIM * par) & (lane_row < HEAD_DIM * (par + 1))).astype(f32) for par in range(2)]

    def group(bi, g, carry):
        base = pl.multiple_of(g * SUBLANES, SUBLANES)
        for pr in range(pairs):
            cols = slice(pr * LANES, (pr + 1) * LANES)
            r8, d8, k8, v8, kap8, b8 = (ref[bi, pl.ds(base, SUBLANES), cols]
                                        for ref in (r_ref, d_ref, k_ref, v_ref, kap_ref, b_ref))
            ys = [[], []]
            for j in range(SUBLANES):
                row = slice(j, j + 1)
                for par in range(2):
                    idx = bi * RWKV_HEADS + pr * 2 + par
                    s = s_ref[idx]
                    sa = -jnp.sum(s * kap8[row], axis=1, keepdims=True)
                    vcol = jnp.sum(jnp.where(diag[par], v8[row], 0.0), axis=1, keepdims=True)
                    sn = s * d8[row] + sa * (b8[row] * half[par]) + vcol * (k8[row] * half[par])
                    s_ref[idx] = sn
                    ys[par].append(_dot_nt(r8, sn)[row, :])
            y_ref[bi, pl.ds(base, SUBLANES), cols] = jnp.concatenate(
                [jnp.concatenate(ys[0], axis=0), jnp.concatenate(ys[1], axis=0)], axis=1)
        return carry

    for bi in range(nb):
        lax.fori_loop(0, tc // SUBLANES, functools.partial(group, bi), 0)

    @pl.when(c == pl.num_programs(1) - 1)
    def _():
        for bi in range(nb):
            for h in range(RWKV_HEADS):
                par = h % 2
                st_ref[bi, h] = s_ref[bi * RWKV_HEADS + h][:, par * HEAD_DIM:(par + 1) * HEAD_DIM]


def rwkv_scan(r, d, k, v, kap, b, s0, tc, nb=2):
    nbatch, seq = r.shape[:2]
    blk = pl.BlockSpec((nb, tc, RWKV_WIDTH), lambda g, c: (g, c, 0))
    st = pl.BlockSpec((nb, RWKV_HEADS, HEAD_DIM, HEAD_DIM), lambda g, c: (g, 0, 0, 0))
    return pl.pallas_call(
        functools.partial(_rwkv_scan_body, tc=tc, nb=nb),
        grid=(nbatch // nb, seq // tc),
        in_specs=[blk] * 6 + [st],
        out_specs=[blk, st],
        out_shape=[jax.ShapeDtypeStruct((nbatch, seq, RWKV_WIDTH), f32),
                   jax.ShapeDtypeStruct((nbatch, RWKV_HEADS, HEAD_DIM, HEAD_DIM), f32)],
        scratch_shapes=[pltpu.VMEM((nb * RWKV_HEADS, HEAD_DIM, LANES), f32)],
        compiler_params=_cparams(("arbitrary", "arbitrary")),
        name="rwkv_scan",
    )(r, d, k, v, kap, b, s0)


def _rwkv_post_body(y_ref, g_ref, bonus_ref, vec_ref, e_ref, o_ref):
    y = y_ref[...]
    e = e_ref[...]
    ln_g, ln_b = vec_ref[5:6, :], vec_ref[6:7, :]
    mu = _split_dot(y, e) * (1.0 / HEAD_DIM)
    yc = y - mu
    var = _split_dot(yc * yc, e) * (1.0 / HEAD_DIM)
    o_ref[...] = (yc * lax.rsqrt(var + GN_EPS) * ln_g + ln_b + bonus_ref[...]) * g_ref[...]


def rwkv_output(y, g, bonus, vecs, e, tc):
    m = y.shape[0]
    row = lambda i: (i, 0)
    const = lambda i: (0, 0)
    return pl.pallas_call(
        _rwkv_post_body,
        grid=(m // tc,),
        in_specs=[pl.BlockSpec((tc, RWKV_WIDTH), row)] * 3 +
                 [pl.BlockSpec((SUBLANES, RWKV_WIDTH), const), pl.BlockSpec((RWKV_WIDTH, RWKV_WIDTH), const)],
        out_specs=pl.BlockSpec((tc, RWKV_WIDTH), row),
        out_shape=jax.ShapeDtypeStruct((m, RWKV_WIDTH), f32),
        compiler_params=_cparams(("arbitrary",)),
        name="rwkv_output",
    )(y, g, bonus, vecs, e)


def rwkv7(p_all, prev, s0, wts, nbatch, seq, tc_prep, tc_scan):
    prev_exp = None
    if prev is not None:
        prev_exp = jnp.repeat(jnp.pad(prev, ((0, 0), (0, RW_PAD - RWKV_PROJ))), seq, axis=0)
    r, d, k, v, kap, b, g, bonus = rwkv_prepare(
        p_all, prev_exp, wts['mu'], wts['vecs'], wts['w2p'], wts['a2p'], wts['g2p'], wts['e'], seq, tc_prep)
    sh = lambda z: z.reshape(nbatch, seq, RWKV_WIDTH)
    y, s_new = rwkv_scan(sh(r), sh(d), sh(k), sh(v), sh(kap), sh(b), s0, tc_scan)
    o = rwkv_output(y.reshape(nbatch * seq, RWKV_WIDTH), g, bonus, wts['vecs'], wts['e'], tc_prep)
    return o, s_new


def _compress_head(x_ref, nchunk, wcat_ref, pe_ref, w1f_ref, b1_ref, w2_ref, b2_ref, acc_ref):
    kv2 = 2 * HEAD_DIM
    pe_term = jnp.concatenate([_dot(pe_ref[k], w1f_ref[k])[0:1, :] for k in range(2)], axis=1)
    bias = pe_term + b1_ref[...]
    acc = jnp.zeros((nchunk, 4 * kv2), f32)
    for r in range(CMP_STRIDE):
        xr = x_ref[pl.ds(r, nchunk, stride=CMP_STRIDE), :]
        acc = acc + _dot(xr.astype(bf16), wcat_ref[r])
    acc_ref[0:nchunk, :] = acc
    acc_ref[nchunk:nchunk + SUBLANES, :] = jnp.zeros((SUBLANES, 4 * kv2), f32)
    first = jnp.concatenate([acc_ref[0:nchunk, 0:kv2], acc_ref[0:nchunk, 2 * kv2:3 * kv2]], axis=1)
    second = jnp.concatenate([acc_ref[1:nchunk + 1, kv2:2 * kv2], acc_ref[1:nchunk + 1, 3 * kv2:4 * kv2]], axis=1)
    hid = jax.nn.gelu(first + second + bias)
    return _dot(hid.astype(bf16), w2_ref[...]) + b2_ref[...]


def _compress_prompt_body(x_ref, wcat_ref, pe_ref, w1f_ref, b1_ref, w2_ref, b2_ref, o_ref, acc_ref, *, nchunk):
    o_ref[...] = _compress_head(x_ref, nchunk, wcat_ref, pe_ref, w1f_ref, b1_ref, w2_ref, b2_ref, acc_ref)


def _compress_weights(lp):
    w1 = lp['cmp_w1']
    z = jnp.zeros((CMP_STRIDE, HEAD_DIM, CMP_HIDDEN), f32)
    key_rows = jnp.concatenate([w1[0, :CMP_STRIDE], w1[0, CMP_STRIDE:], z, z], axis=-1)
    val_rows = jnp.concatenate([z, z, w1[1, :CMP_STRIDE], w1[1, CMP_STRIDE:]], axis=-1)
    wcat = jnp.concatenate([key_rows, val_rows], axis=1).astype(bf16)
    pe = jnp.broadcast_to(lp['cmp_pe'].reshape(2, 1, CMP_BLOCK * HEAD_DIM), (2, SUBLANES, CMP_BLOCK * HEAD_DIM))
    w2 = lp['cmp_w2']
    zz = jnp.zeros((CMP_HIDDEN, HEAD_DIM), f32)
    w2bd = jnp.concatenate([jnp.concatenate([w2[0], zz], axis=1), jnp.concatenate([zz, w2[1]], axis=1)], axis=0)
    return {'wcat': wcat, 'pe': pe.astype(bf16),
            'w1f': w1.reshape(2, CMP_BLOCK * HEAD_DIM, CMP_HIDDEN).astype(bf16),
            'b1': lp['cmp_b1'].reshape(1, 2 * CMP_HIDDEN), 'w2bd': w2bd.astype(bf16),
            'b2': lp['cmp_b2'].reshape(1, 2 * HEAD_DIM)}


def _cmp_weight_specs(nidx):
    c2 = lambda *a: (0, 0)
    c3 = lambda *a: (0, 0, 0)
    return [pl.BlockSpec((CMP_STRIDE, 2 * HEAD_DIM, 4 * CMP_HIDDEN), c3),
            pl.BlockSpec((2, SUBLANES, CMP_BLOCK * HEAD_DIM), c3),
            pl.BlockSpec((2, CMP_BLOCK * HEAD_DIM, CMP_HIDDEN), c3),
            pl.BlockSpec((1, 2 * CMP_HIDDEN), c2),
            pl.BlockSpec((2 * CMP_HIDDEN, 2 * HEAD_DIM), c2),
            pl.BlockSpec((1, 2 * HEAD_DIM), c2)]


def compress_prompt(p_all, cw, nbatch, seq):
    nchunk = seq // CMP_STRIDE
    kv2 = 2 * HEAD_DIM
    return pl.pallas_call(
        functools.partial(_compress_prompt_body, nchunk=nchunk),
        grid=(nbatch, NSA_KV_HEADS),
        in_specs=[pl.BlockSpec((seq, kv2), lambda b, h: (b, KVC0 // kv2 + h))] + _cmp_weight_specs(2),
        out_specs=pl.BlockSpec((nchunk, kv2), lambda b, h: (b, h)),
        out_shape=jax.ShapeDtypeStruct((nbatch * nchunk, KV_WIDTH), f32),
        scratch_shapes=[pltpu.VMEM((nchunk + SUBLANES, 4 * CMP_HIDDEN), f32)],
        compiler_params=_cparams(("arbitrary", "arbitrary")),
        name="compress_prompt",
    )(p_all, cw['wcat'], cw['pe'], cw['w1f'], cw['b1'], cw['w2bd'], cw['b2'])


def _masked_softmax(s, mask):
    s = jnp.where(mask, s, NEG_INF)
    m = jnp.max(s, axis=-1, keepdims=True)
    e = jnp.where(mask, jnp.exp(s - m), 0.0)
    return e / jnp.maximum(jnp.sum(e, axis=-1, keepdims=True), 1e-30)


def _nsa_prompt_body(q_ref, kc_ref, ks_ref, kw_ref, g_ref, wsel_ref, eexp_ref, o_ref, *, nsel, ncmp):
    i = pl.program_id(2)
    qb, grp = Q_BLOCK, NSA_GROUP
    rows = grp * qb
    q = q_ref[...]
    qs = jnp.concatenate([q[:, g * HEAD_DIM:(g + 1) * HEAD_DIM] for g in range(grp)], axis=0) * (HEAD_DIM ** -0.5)
    qp = jnp.concatenate([qs, jnp.zeros_like(qs)], axis=1).astype(bf16)
    qpos = i * qb + lax.broadcasted_iota(jnp.int32, (rows, 1), 0) % qb

    kc = kc_ref[...].astype(bf16)
    s_c = _dot_nt(qp, kc)
    cend = lax.broadcasted_iota(jnp.int32, (1, ncmp), 1) * CMP_STRIDE + (CMP_BLOCK - 1)
    p_c = _masked_softmax(s_c, cend <= qpos)
    o_c = _dot(p_c.astype(bf16), kc)
    imp = p_c[0:qb] + p_c[qb:2 * qb] + p_c[2 * qb:3 * qb] + p_c[3 * qb:4 * qb]
    imp_hi = imp.astype(bf16)
    imp_lo = (imp - imp_hi.astype(f32)).astype(bf16)
    wsel = wsel_ref[...]
    p_slc = _dot_nt(wsel, imp_hi) + _dot_nt(wsel, imp_lo)

    blk = lax.broadcasted_iota(jnp.int32, (nsel, qb), 0)
    qpos_t = i * qb + lax.broadcasted_iota(jnp.int32, (nsel, qb), 1)
    cur = qpos_t // SEL_BLOCK
    forced = (blk == 0) | (blk == cur) | (blk == cur - 1)
    score = jnp.where(forced, FORCE_SCORE, p_slc)
    score = jnp.where(blk * SEL_BLOCK <= qpos_t, score, -1.0)
    rank = jnp.zeros((nsel, qb), f32)
    for jp in range(nsel):
        row = score[jp:jp + 1, :]
        ahead = (row > score) | ((row == score) & (blk > jp))
        rank = rank + jnp.where(ahead, 1.0, 0.0)
    sel_t = jnp.where((rank < float(min(SEL_TOPK, nsel))) & (score >= 0.0), 1.0, 0.0)
    sel = sel_t.T.astype(bf16)

    def attend(ref, kt, carry, mask_fn):
        m, l, acc = carry
        k0 = pl.multiple_of(kt * qb, qb)
        kt_tile = ref[pl.ds(k0, qb), :].astype(bf16)
        s = _dot_nt(qp, kt_tile)
        kpos = k0 + lax.broadcasted_iota(jnp.int32, (1, qb), 1)
        mask = mask_fn(kt, kpos)
        s = jnp.where(mask, s, NEG_INF)
        m_new = jnp.maximum(m, jnp.max(s, axis=-1, keepdims=True))
        alpha = jnp.exp(m - m_new)
        e = jnp.where(mask, jnp.exp(s - m_new), 0.0)
        l = alpha * l + jnp.sum(e, axis=-1, keepdims=True)
        acc = alpha * acc + _dot(e.astype(bf16), kt_tile)
        return m_new, l, acc

    def sel_mask(kt, kpos):
        picked = _dot(sel, eexp_ref[kt])
        picked = jnp.concatenate([picked] * grp, axis=0)
        return (picked > 0.5) & (kpos <= qpos)

    def win_mask(kt, kpos):
        diff = qpos - kpos
        return (diff >= 0) & (diff <= WINDOW)

    init = (jnp.full((rows, 1), NEG_INF, f32), jnp.zeros((rows, 1), f32), jnp.zeros((rows, 2 * HEAD_DIM), f32))
    _, l_s, acc_s = lax.fori_loop(0, i + 1, lambda kt, c: attend(ks_ref, kt, c, sel_mask), init)
    _, l_w, acc_w = lax.fori_loop(jnp.maximum(i - WINDOW // qb, 0), i + 1,
                                  lambda kt, c: attend(kw_ref, kt, c, win_mask), init)
    o_s = acc_s / jnp.maximum(l_s, 1e-30)
    o_w = acc_w / jnp.maximum(l_w, 1e-30)

    gates = g_ref[0, 0]
    gate = lambda br: jnp.concatenate([gates[:, 3 * g + br:3 * g + br + 1] for g in range(grp)], axis=0)
    out = gate(0) * o_c + gate(1) * o_s + gate(2) * o_w
    o_ref[...] = jnp.concatenate([out[g * qb:(g + 1) * qb, HEAD_DIM:] for g in range(grp)], axis=1)


def nsa_prompt(p_all, kv_cmp, gates, nbatch, seq):
    nqb, nsel, ncmp = seq // Q_BLOCK, seq // SEL_BLOCK, seq // CMP_STRIDE
    kv2 = 2 * HEAD_DIM
    cidx = jnp.arange(ncmp)[None, :] - (SEL_BLOCK // CMP_STRIDE) * jnp.arange(nsel)[:, None]
    mult = jnp.array([1, 2, 2, 2, 1], f32)
    wsel = jnp.where((cidx >= 0) & (cidx <= 4), mult[jnp.clip(cidx, 0, 4)], 0.0).astype(bf16)
    key_blk = (jnp.arange(nqb)[:, None, None] * Q_BLOCK + jnp.arange(Q_BLOCK)[None, None, :]) // SEL_BLOCK
    eexp = (key_blk == jnp.arange(nsel)[None, :, None]).astype(bf16)
    return pl.pallas_call(
        functools.partial(_nsa_prompt_body, nsel=nsel, ncmp=ncmp),
        grid=(nbatch, NSA_KV_HEADS, nqb),
        in_specs=[pl.BlockSpec((Q_BLOCK, NSA_GROUP * HEAD_DIM),
                               lambda b, h, i: (b * nqb + i, Q0 // (NSA_GROUP * HEAD_DIM) + h)),
                  pl.BlockSpec((ncmp, kv2), lambda b, h, i: (b, h)),
                  pl.BlockSpec((seq, kv2), lambda b, h, i: (b, KVS0 // kv2 + h)),
                  pl.BlockSpec((seq, kv2), lambda b, h, i: (b, KVW0 // kv2 + h)),
                  pl.BlockSpec((1, 1, Q_BLOCK, 16), lambda b, h, i: (b, h, i, 0)),
                  pl.BlockSpec((nsel, ncmp), lambda b, h, i: (0, 0)),
                  pl.BlockSpec((nqb, nsel, Q_BLOCK), lambda b, h, i: (0, 0, 0))],
        out_specs=pl.BlockSpec((Q_BLOCK, NSA_GROUP * HEAD_DIM), lambda b, h, i: (b * nqb + i, h)),
        out_shape=jax.ShapeDtypeStruct((nbatch * seq, NSA_WIDTH), f32),
        compiler_params=_cparams(("arbitrary", "arbitrary", "arbitrary")),
        name="nsa_prompt",
    )(p_all, kv_cmp, p_all, p_all, gates, wsel, eexp)


def _branch_gates(p_all, nbatch, seq):
    g = p_all[:, GN0:GN0 + 3 * NSA_HEADS].reshape(nbatch, seq, NSA_KV_HEADS, 3 * NSA_GROUP)
    return jnp.pad(jnp.transpose(g, (0, 2, 1, 3)), ((0, 0), (0, 0), (0, 0), (0, 16 - 3 * NSA_GROUP)))


def _prepare_weights(lp):
    w_in = lp['w_in']
    nsa_end = NSA_WIDTH + 3 * KV_WIDTH
    w_all = jnp.concatenate([
        w_in[:, NSA_PROJ:NSA_PROJ + RWKV_PROJ], w_in[:, nsa_end:NSA_PROJ],
        jnp.zeros((D_MODEL, RW_PAD - RWKV_PROJ - 3 * NSA_HEADS), w_in.dtype),
        w_in[:, :nsa_end], w_in[:, NSA_PROJ + RWKV_PROJ:]], axis=1).astype(bf16)
    pad_rows = lambda w, r0: jnp.zeros((LORA_PAD, RWKV_WIDTH), f32).at[r0:r0 + w.shape[0]].set(w).astype(bf16)
    head = jnp.arange(RWKV_WIDTH) // HEAD_DIM
    vecs = jnp.stack([lp['rwkv_w0'], lp['rwkv_a0'], lp['rwkv_k_k'], lp['rwkv_k_a'],
                      lp['rwkv_r_k'].reshape(RWKV_WIDTH), lp['rwkv_ln_g'], lp['rwkv_ln_b'],
                      jnp.zeros((RWKV_WIDTH,), f32)])
    return {
        'w_all': w_all,
        'mu': jnp.pad(lp['rwkv_mu'], (0, RW_PAD - RWKV_PROJ)).reshape(1, RW_PAD),
        'vecs': vecs,
        'w2p': pad_rows(lp['rwkv_w2'], 0),
        'a2p': pad_rows(lp['rwkv_a2'], DECAY_LORA),
        'g2p': pad_rows(lp['rwkv_g2'], DECAY_LORA + AAA_LORA),
        'e': (head[:, None] == head[None, :]).astype(bf16),
        'p_nsa': lp['p_nsa'].astype(bf16), 'p_rwkv': lp['p_rwkv'].astype(bf16),
        'w_out': lp['w_out'].astype(bf16),
        'mlp_w1': lp['mlp_w1'].astype(bf16), 'mlp_w2': lp['mlp_w2'].astype(bf16),
    }


def _group_forward(x, pos_rows, prev, s0, lp, wts, final_g, tm, tc_prep, tc_scan, o_nsa_fn):
    nbatch, seq = x.shape[:2]
    x2d = x.reshape(nbatch * seq, D_MODEL)
    tq, tkv = _rot_tables(pos_rows)
    p_all = input_projection(x2d, lp['norm1_g'], wts['w_all'], tq, tkv, tm)
    o_rw, s_new = rwkv7(p_all, prev, s0, wts, nbatch, seq, tc_prep, tc_scan)
    o_nsa = o_nsa_fn(p_all)
    h, hn = merge_project(o_nsa, o_rw, p_all, x2d, wts['p_nsa'], wts['p_rwkv'], wts['w_out'], lp['norm2_g'],
                          min(tm, 256))
    y = mlp_residual_norm(hn, h, wts['mlp_w1'], wts['mlp_w2'], final_g, tm)
    return y.reshape(nbatch, seq, D_MODEL), p_all, s_new


def _kv_rows(p_all, col0, nbatch, seq):
    return p_all[:, col0:col0 + KV_WIDTH].reshape(nbatch, seq, NSA_KV_HEADS, 2, HEAD_DIM)


def _jsoftmax(s, mask):
    s = jnp.where(mask, s.astype(f32), NEG_INF)
    m = jnp.max(s, axis=-1, keepdims=True)
    e = jnp.where(mask, jnp.exp(s - m), 0.0)
    return e / jnp.maximum(jnp.sum(e, axis=-1, keepdims=True), 1e-30)


def _pad_time(z, mult):
    pad = (-z.shape[1]) % mult
    return jnp.pad(z, [(0, 0), (0, pad)] + [(0, 0)] * (z.ndim - 2))


def _chunk_proj(rows, cmp_w1):
    b, l = rows.shape[:2]
    ch = rows.reshape(b, l // CMP_STRIDE, CMP_STRIDE, NSA_KV_HEADS, 2, HEAD_DIM)
    first = jnp.einsum('bcrhkd,krde->bchke', ch, cmp_w1[:, :CMP_STRIDE])
    second = jnp.einsum('bcrhkd,krde->bchke', ch, cmp_w1[:, CMP_STRIDE:])
    return first, second


def _gather_blocks(blocks, idx):
    bb = jnp.arange(blocks.shape[0])[:, None, None, None]
    hh = jnp.arange(NSA_KV_HEADS)[None, None, :, None]
    return blocks[bb, idx, :, hh]


def _sample_nsa_jax(p_all, cache_cmp, cache_sel, win_buf, page_table, lp, b, t):
    scale = HEAD_DIM ** -0.5
    q = p_all[:, Q0:Q0 + NSA_WIDTH].reshape(b, t, NSA_KV_HEADS, NSA_GROUP, HEAD_DIM)
    kvc, kvs, kvw = (_kv_rows(p_all, c, b, t) for c in (KVC0, KVS0, KVW0))
    gates = p_all[:, GN0:GN0 + 3 * NSA_HEADS].reshape(b, t, NSA_KV_HEADS, NSA_GROUP, 3)
    q_pos = PAST_LEN + jnp.arange(t, dtype=jnp.int32)
    past_cmp = cache_cmp[page_table].reshape(b, PAST_LEN, NSA_KV_HEADS, 2, HEAD_DIM)
    fp, sp = _chunk_proj(past_cmp, lp['cmp_w1'])
    fn, sn = _chunk_proj(_pad_time(kvc, CMP_STRIDE), lp['cmp_w1'])
    first, second = jnp.concatenate([fp, fn], axis=1), jnp.concatenate([sp, sn], axis=1)
    pe_term = jnp.einsum('krd,krde->ke', lp['cmp_pe'], lp['cmp_w1'])
    hid = jax.nn.gelu(first[:, :-1] + second[:, 1:] + pe_term + lp['cmp_b1'])
    kv_cmp = jnp.einsum('bchke,ked->bchkd', hid, lp['cmp_w2']) + lp['cmp_b2']
    cmp_end = jnp.arange(kv_cmp.shape[1], dtype=jnp.int32) * CMP_STRIDE + CMP_BLOCK - 1
    n_past = PAST_LEN // SEL_BLOCK
    new_blocks = _pad_time(kvs, SEL_BLOCK).reshape(b, -1, SEL_BLOCK, NSA_KV_HEADS, 2, HEAD_DIM)
    n_new = new_blocks.shape[1]
    n_sel = n_past + n_new
    past_blocks = cache_sel[page_table].reshape(b, n_past, SEL_BLOCK, NSA_KV_HEADS, 2, HEAD_DIM)
    s_c = jnp.einsum('bqhgd,bchd->bqhgc', q, kv_cmp[..., 0, :], preferred_element_type=f32) * scale
    p_c = _jsoftmax(s_c, (cmp_end[None, :] <= q_pos[:, None])[None, :, None, None, :])
    o_c = jnp.einsum('bqhgc,bchd->bqhgd', p_c, kv_cmp[..., 1, :])
    imp = jnp.sum(p_c, axis=3)
    nc = imp.shape[-1]
    cps, sub = SEL_BLOCK // CMP_STRIDE, CMP_BLOCK // CMP_STRIDE
    imp = jnp.pad(imp, ((0, 0), (0, 0), (0, 0), (0, cps * n_sel + cps + sub - nc)))
    p_slc = sum(imp[..., m + n: m + n + cps * n_sel: cps] for m in range(cps) for n in range(sub))
    blk = jnp.arange(n_sel)
    cur = (q_pos // SEL_BLOCK)[:, None]
    forced = (blk[None, :] == 0) | (blk[None, :] == cur) | (blk[None, :] == cur - 1)
    score = jnp.where(forced[None, :, None, :], FORCE_SCORE, p_slc)
    score = jnp.where((blk[None, :] * SEL_BLOCK <= q_pos[:, None])[None, :, None, :], score, -1.0)
    top_s, idx = lax.top_k(score, SEL_TOPK)
    rows_all = jnp.concatenate([past_blocks, new_blocks], axis=1).reshape(
        b, n_sel * SEL_BLOCK, NSA_KV_HEADS, 2, HEAD_DIM)
    picked = jnp.any((idx[..., None] == blk) & (top_s >= 0.0)[..., None], axis=-2)
    tok = jnp.arange(n_sel * SEL_BLOCK)
    m_s = jnp.repeat(picked, SEL_BLOCK, axis=-1) & (tok[None, :] <= q_pos[:, None])[None, :, None, :]
    s_s = jnp.einsum('bqhgd,bnhd->bqhgn', q, rows_all[..., 0, :], preferred_element_type=f32) * scale
    p_s = _jsoftmax(s_s, m_s[:, :, :, None, :])
    o_s = jnp.einsum('bqhgn,bnhd->bqhgd', p_s, rows_all[..., 1, :])
    kv_win = jnp.concatenate([win_buf, kvw], axis=1)
    w_buf = win_buf.shape[1]
    win_pos = PAST_LEN - w_buf + jnp.arange(w_buf + t, dtype=jnp.int32)
    s_w = jnp.einsum('bqhgd,blhd->bqhgl', q, kv_win[..., 0, :], preferred_element_type=f32) * scale
    diff = q_pos[:, None] - win_pos[None, :]
    p_w = _jsoftmax(s_w, ((diff >= 0) & (diff <= WINDOW) & (win_pos[None, :] >= 0))[None, :, None, None, :])
    o_w = jnp.einsum('bqhgl,blhd->bqhgd', p_w, kv_win[..., 1, :])
    o = gates[..., 0:1] * o_c + gates[..., 1:2] * o_s + gates[..., 2:3] * o_w
    return o.reshape(b * t, NSA_WIDTH), kv_win[:, t:]


def kernel(x_prompt, x_sample, cache_cmp_kv, cache_sel_kv, state_nsa_win, state_rwkv, state_rwkv_shift,
           page_table, norm1_g, w_in, cmp_pe, cmp_w1, cmp_b1, cmp_w2, cmp_b2, rwkv_mu, rwkv_w0, rwkv_w2,
           rwkv_a0, rwkv_a2, rwkv_g2, rwkv_k_k, rwkv_k_a, rwkv_r_k, rwkv_ln_g, rwkv_ln_b, p_nsa, p_rwkv,
           w_out, norm2_g, mlp_w1, mlp_w2, final_g):
    l = 0
    lp = {'norm1_g': norm1_g[l], 'w_in': w_in[l], 'cmp_pe': cmp_pe[l], 'cmp_w1': cmp_w1[l],
          'cmp_b1': cmp_b1[l], 'cmp_w2': cmp_w2[l], 'cmp_b2': cmp_b2[l], 'rwkv_mu': rwkv_mu[l],
          'rwkv_w0': rwkv_w0[l], 'rwkv_w2': rwkv_w2[l], 'rwkv_a0': rwkv_a0[l], 'rwkv_a2': rwkv_a2[l],
          'rwkv_g2': rwkv_g2[l], 'rwkv_k_k': rwkv_k_k[l], 'rwkv_k_a': rwkv_k_a[l], 'rwkv_r_k': rwkv_r_k[l],
          'rwkv_ln_g': rwkv_ln_g[l], 'rwkv_ln_b': rwkv_ln_b[l], 'p_nsa': p_nsa[l], 'p_rwkv': p_rwkv[l],
          'w_out': w_out[l], 'norm2_g': norm2_g[l], 'mlp_w1': mlp_w1[l], 'mlp_w2': mlp_w2[l]}
    wts = _prepare_weights(lp)
    bp, tp = x_prompt.shape[:2]
    bs, ts = x_sample.shape[:2]
    cw = _compress_weights(lp)
    zero_state = jnp.zeros((bp, RWKV_HEADS, HEAD_DIM, HEAD_DIM), f32)

    def prompt_nsa(p_all):
        return nsa_prompt(p_all, compress_prompt(p_all, cw, bp, tp), _branch_gates(p_all, bp, tp), bp, tp)

    yp, pp, sp = _group_forward(x_prompt, jnp.arange(tp, dtype=jnp.int32), None, zero_state, lp, wts, final_g,
                                512, 256, 128, prompt_nsa)

    win_new = []

    def sample_nsa(p_all):
        o, win = _sample_nsa_jax(p_all, cache_cmp_kv[l], cache_sel_kv[l], state_nsa_win[l], page_table, lp, bs, ts)
        win_new.append(win)
        return o

    pos_s = PAST_LEN + jnp.arange(bs * ts, dtype=jnp.int32) % ts
    ys, ps, ss = _group_forward(x_sample, pos_s, state_rwkv_shift[l], state_rwkv[l], lp, wts, final_g,
                                256, 256, ts, sample_nsa)
    wlen = min(WINDOW, tp)
    shift = lambda p_all, nb, t: p_all.reshape(nb, t, P_WIDTH)[:, -1, RW0:RW0 + RWKV_PROJ]
    return (yp, ys,
            _kv_rows(pp, KVC0, bp, tp)[None], _kv_rows(ps, KVC0, bs, ts)[None],
            _kv_rows(pp, KVS0, bp, tp)[None], _kv_rows(ps, KVS0, bs, ts)[None],
            _kv_rows(pp, KVW0, bp, tp)[None, :, tp - wlen:], win_new[0][None],
            sp[None], ss[None],
            shift(pp, bp, tp)[None], shift(ps, bs, ts)[None])
```

```python
import functools

import jax
import jax.numpy as jnp
from jax import lax
from jax.experimental import pallas as pl
from jax.experimental.pallas import tpu as pltpu

D_MODEL = 2048
DEPTH = 1
PAST_LEN = 16384
PAGE_SIZE = 128

HEAD_DIM = 64
NSA_HEADS = D_MODEL // (2 * HEAD_DIM)
NSA_KV_HEADS = NSA_HEADS // 4
NSA_GROUP = NSA_HEADS // NSA_KV_HEADS
NSA_WIDTH = NSA_HEADS * HEAD_DIM
KV_WIDTH = NSA_KV_HEADS * 2 * HEAD_DIM
CMP_BLOCK = 32
CMP_STRIDE = 16
CMP_HIDDEN = 2 * HEAD_DIM
SEL_BLOCK = 64
SEL_TOPK = 16
WINDOW = 512
Q_BLOCK = 128
ROT_DIM = HEAD_DIM // 4
ROPE_THETA = 500000.0
RWKV_HEADS = D_MODEL // (2 * HEAD_DIM)
RWKV_WIDTH = RWKV_HEADS * HEAD_DIM
DECAY_LORA = max(32, int(round(1.8 * D_MODEL ** 0.5 / 32)) * 32)
AAA_LORA = DECAY_LORA
GATE_LORA = max(32, int(round(0.6 * D_MODEL ** 0.8 / 32)) * 32)
RWKV_PROJ = 3 * RWKV_WIDTH + DECAY_LORA + AAA_LORA + GATE_LORA
NSA_PROJ = NSA_WIDTH + 3 * KV_WIDTH + 3 * NSA_HEADS
N_IN = NSA_PROJ + RWKV_PROJ + 2 * D_MODEL
D_FF = 4 * D_MODEL
RMS_EPS = 1e-6
GN_EPS = HEAD_DIM * 1e-5
NEG_INF = -1e30
FORCE_SCORE = 1e6

LANES = 128
SUBLANES = 8
VMEM_LIMIT_BYTES = 56 * 1024 * 1024

RW0 = 0
LORA0 = 3 * RWKV_WIDTH
GN0 = RWKV_PROJ
RW_PAD = 3584
LORA_PAD = RW_PAD - LORA0
Q0 = RW_PAD
KVC0 = Q0 + NSA_WIDTH
KVS0 = KVC0 + KV_WIDTH
KVW0 = KVS0 + KV_WIDTH
GM0 = KVW0 + KV_WIDTH
P_WIDTH = GM0 + 2 * D_MODEL
PROJ_TN = 512

f32 = jnp.float32
bf16 = jnp.bfloat16


def _cparams(sem):
    return pltpu.CompilerParams(dimension_semantics=sem, vmem_limit_bytes=VMEM_LIMIT_BYTES)


def _rms(x, g):
    return x * lax.rsqrt(jnp.mean(x * x, axis=-1, keepdims=True) + RMS_EPS) * g


def _dot(a, b):
    return jnp.dot(a, b, preferred_element_type=f32)


def _dot_nt(a, b):
    return lax.dot_general(a, b, (((1,), (1,)), ((), ())), preferred_element_type=f32)


def _split_dot(x, e):
    hi = x.astype(bf16)
    lo = (x - hi.astype(f32)).astype(bf16)
    return _dot(hi, e) + _dot(lo, e)


def _rot_store(acc, tab_ref, o_ref):
    c, s1, s2 = tab_ref[0], tab_ref[1], tab_ref[2]
    for s in range(acc.shape[1] // LANES):
        x = acc[:, s * LANES:(s + 1) * LANES]
        o_ref[:, s * LANES:(s + 1) * LANES] = (
            x * c + pltpu.roll(x, LANES - ROT_DIM // 2, 1) * s1 + pltpu.roll(x, ROT_DIM // 2, 1) * s2)


def _proj_body(x_ref, g_ref, w_ref, tq_ref, tkv_ref, o_ref, xn_ref):
    j = pl.program_id(1)

    @pl.when(j == 0)
    def _():
        xn_ref[...] = _rms(x_ref[...], g_ref[...]).astype(bf16)

    acc = _dot(xn_ref[...], w_ref[...])
    gn_tile = GN0 // PROJ_TN

    @pl.when(j < gn_tile)
    def _():
        o_ref[...] = acc

    @pl.when(j == gn_tile)
    def _():
        lane = lax.broadcasted_iota(jnp.int32, acc.shape, 1)
        o_ref[...] = jnp.where(lane >= GN0 - gn_tile * PROJ_TN, jax.nn.sigmoid(acc), acc)

    @pl.when((j >= Q0 // PROJ_TN) & (j < KVC0 // PROJ_TN))
    def _():
        _rot_store(acc, tq_ref, o_ref)

    @pl.when((j >= KVC0 // PROJ_TN) & (j < GM0 // PROJ_TN))
    def _():
        _rot_store(acc, tkv_ref, o_ref)

    @pl.when(j >= GM0 // PROJ_TN)
    def _():
        o_ref[...] = jax.nn.sigmoid(acc)


def input_projection(x2d, g, w_all, tq, tkv, tm):
    m = x2d.shape[0]
    nt = tq.shape[1] // tm
    return pl.pallas_call(
        _proj_body,
        grid=(m // tm, P_WIDTH // PROJ_TN),
        in_specs=[pl.BlockSpec((tm, D_MODEL), lambda i, j: (i, 0)),
                  pl.BlockSpec((1, D_MODEL), lambda i, j: (0, 0)),
                  pl.BlockSpec((D_MODEL, PROJ_TN), lambda i, j: (0, j)),
                  pl.BlockSpec((3, tm, LANES), lambda i, j: (0, i % nt, 0)),
                  pl.BlockSpec((3, tm, LANES), lambda i, j: (0, i % nt, 0))],
        out_specs=pl.BlockSpec((tm, PROJ_TN), lambda i, j: (i, j)),
        out_shape=jax.ShapeDtypeStruct((m, P_WIDTH), f32),
        scratch_shapes=[pltpu.VMEM((tm, D_MODEL), bf16)],
        compiler_params=_cparams(("arbitrary", "arbitrary")),
        name="input_projection",
    )(x2d, g.reshape(1, D_MODEL), w_all, tq, tkv)


def _rot_tables(pos):
    half = ROT_DIM // 2
    freqs = jnp.power(jnp.float32(ROPE_THETA), -jnp.arange(half, dtype=f32) * 2.0 / ROT_DIM)
    ang = pos.astype(f32)[:, None] * freqs[None, :]
    cos, sin = jnp.cos(ang), jnp.sin(ang)
    lane = jnp.arange(LANES)

    def build(period):
        l = lane % period
        fi = l % half
        c = jnp.where(l < ROT_DIM, cos[:, fi], 1.0)
        s1 = jnp.where(l < half, -sin[:, fi], 0.0)
        s2 = jnp.where((l >= half) & (l < ROT_DIM), sin[:, fi], 0.0)
        return jnp.stack([c, s1, s2]).astype(f32)

    return build(HEAD_DIM), build(2 * HEAD_DIM)


def _merge_body(on_ref, or_ref, ga_ref, gb_ref, x_ref, pn_ref, pr_ref, wo_ref, g2_ref, h_ref, hn_ref):
    a = _dot(on_ref[...].astype(bf16), pn_ref[...])
    b = _dot(or_ref[...].astype(bf16), pr_ref[...])
    mix = ga_ref[...] * a + gb_ref[...] * b
    h = x_ref[...] + _dot(mix.astype(bf16), wo_ref[...])
    h_ref[...] = h
    hn_ref[...] = _rms(h, g2_ref[...]).astype(bf16)


def merge_project(o_nsa, o_rw, p_all, x2d, pn, pr, wo, g2, tm):
    m = x2d.shape[0]
    const = lambda i: (0, 0)
    return pl.pallas_call(
        _merge_body,
        grid=(m // tm,),
        in_specs=[pl.BlockSpec((tm, NSA_WIDTH), lambda i: (i, 0)),
                  pl.BlockSpec((tm, RWKV_WIDTH), lambda i: (i, 0)),
                  pl.BlockSpec((tm, D_MODEL), lambda i: (i, GM0 // D_MODEL)),
                  pl.BlockSpec((tm, D_MODEL), lambda i: (i, GM0 // D_MODEL + 1)),
                  pl.BlockSpec((tm, D_MODEL), lambda i: (i, 0)),
                  pl.BlockSpec((NSA_WIDTH, D_MODEL), const),
                  pl.BlockSpec((RWKV_WIDTH, D_MODEL), const),
                  pl.BlockSpec((D_MODEL, D_MODEL), const),
                  pl.BlockSpec((1, D_MODEL), const)],
        out_specs=[pl.BlockSpec((tm, D_MODEL), lambda i: (i, 0)),
                   pl.BlockSpec((tm, D_MODEL), lambda i: (i, 0))],
        out_shape=[jax.ShapeDtypeStruct((m, D_MODEL), f32),
                   jax.ShapeDtypeStruct((m, D_MODEL), bf16)],
        compiler_params=_cparams(("arbitrary",)),
        name="merge_project",
    )(o_nsa, o_rw, p_all, p_all, x2d, pn, pr, wo, g2.reshape(1, D_MODEL))


def _mlp_body(hn_ref, h_ref, w1_ref, w2_ref, fg_ref, y_ref, acc_ref):
    f = pl.program_id(1)
    u = jnp.square(jnp.maximum(_dot(hn_ref[...], w1_ref[...]), 0.0)).astype(bf16)
    contrib = _dot(u, w2_ref[...])

    @pl.when(f == 0)
    def _():
        acc_ref[...] = contrib

    @pl.when(f > 0)
    def _():
        acc_ref[...] += contrib

    @pl.when(f == pl.num_programs(1) - 1)
    def _():
        y_ref[...] = _rms(h_ref[...] + acc_ref[...], fg_ref[...])


def mlp_residual_norm(hn, h, w1, w2, fg, tm, tf=512):
    m = h.shape[0]
    return pl.pallas_call(
        _mlp_body,
        grid=(m // tm, D_FF // tf),
        in_specs=[pl.BlockSpec((tm, D_MODEL), lambda i, f: (i, 0)),
                  pl.BlockSpec((tm, D_MODEL), lambda i, f: (i, 0)),
                  pl.BlockSpec((D_MODEL, tf), lambda i, f: (0, f)),
                  pl.BlockSpec((tf, D_MODEL), lambda i, f: (f, 0)),
                  pl.BlockSpec((1, D_MODEL), lambda i, f: (0, 0))],
        out_specs=pl.BlockSpec((tm, D_MODEL), lambda i, f: (i, 0)),
        out_shape=jax.ShapeDtypeStruct((m, D_MODEL), f32),
        scratch_shapes=[pltpu.VMEM((tm, D_MODEL), f32)],
        compiler_params=_cparams(("arbitrary", "arbitrary")),
        name="mlp_residual_norm",
    )(hn, h, w1, w2, fg.reshape(1, D_MODEL))


def _rwkv_prep_body(*refs, tc, seq, has_prev):
    if has_prev:
        p_ref, halo_ref, prev_ref = refs[:3]
        refs = refs[3:]
    else:
        p_ref, halo_ref = refs[:2]
        prev_ref = None
        refs = refs[2:]
    (mu_ref, vec_ref, w2_ref, a2_ref, g2_ref, e_ref,
     r_out, d_out, k_out, v_out, kap_out, b_out, g_out, bonus_out, sh_ref) = refs
    i = pl.program_id(0)
    p = p_ref[...]
    sh_ref[0:SUBLANES, :] = halo_ref[...]
    sh_ref[SUBLANES:SUBLANES + tc, :] = p
    rolled = sh_ref[SUBLANES - 1:SUBLANES - 1 + tc, :]
    t_in_seq = (i * tc + lax.broadcasted_iota(jnp.int32, (tc, 1), 0)) % seq
    first = prev_ref[...] if has_prev else jnp.zeros_like(p)
    shifted = jnp.where(t_in_seq == 0, first, rolled)
    xm = p + (shifted - p) * mu_ref[...]
    r = xm[:, 0:RWKV_WIDTH]
    k = xm[:, RWKV_WIDTH:2 * RWKV_WIDTH]
    v = xm[:, 2 * RWKV_WIDTH:3 * RWKV_WIDTH]
    tail = xm[:, LORA0:RW_PAD]
    w0, a0, k_k, k_a, r_k = (vec_ref[n:n + 1, :] for n in range(5))
    w = -jax.nn.softplus(-(w0 + _dot(jnp.tanh(tail).astype(bf16), w2_ref[...]))) - 0.5
    a = jax.nn.sigmoid(a0 + _dot(tail.astype(bf16), a2_ref[...]))
    kk = k * k_k
    e = e_ref[...]
    kap = kk / jnp.maximum(jnp.sqrt(_split_dot(kk * kk, e)), 1e-12)
    kn = k * (1.0 + (a - 1.0) * k_a)
    r_out[...] = r
    d_out[...] = jnp.exp(-jnp.exp(w))
    k_out[...] = kn
    v_out[...] = v
    kap_out[...] = kap
    b_out[...] = kap * a
    g_out[...] = _dot(jax.nn.sigmoid(tail).astype(bf16), g2_ref[...])
    bonus_out[...] = _split_dot(r * kn * r_k, e) * v


def rwkv_prepare(p_all, prev_exp, mu, vecs, w2p, a2p, g2p, e, seq, tc):
    m = p_all.shape[0]
    has_prev = prev_exp is not None
    row = lambda i: (i, 0)
    const = lambda i: (0, 0)
    halo = lambda i: (jnp.maximum(i * (tc // SUBLANES) - 1, 0), 0)
    in_specs = [pl.BlockSpec((tc, RW_PAD), row), pl.BlockSpec((SUBLANES, RW_PAD), halo)]
    args = [p_all, p_all]
    if has_prev:
        in_specs.append(pl.BlockSpec((tc, RW_PAD), row))
        args.append(prev_exp)
    in_specs += [pl.BlockSpec((1, RW_PAD), const), pl.BlockSpec((SUBLANES, RWKV_WIDTH), const),
                 pl.BlockSpec((LORA_PAD, RWKV_WIDTH), const), pl.BlockSpec((LORA_PAD, RWKV_WIDTH), const),
                 pl.BlockSpec((LORA_PAD, RWKV_WIDTH), const), pl.BlockSpec((RWKV_WIDTH, RWKV_WIDTH), const)]
    args += [mu, vecs, w2p, a2p, g2p, e]
    return pl.pallas_call(
        functools.partial(_rwkv_prep_body, tc=tc, seq=seq, has_prev=has_prev),
        grid=(m // tc,),
        in_specs=in_specs,
        out_specs=[pl.BlockSpec((tc, RWKV_WIDTH), row)] * 8,
        out_shape=[jax.ShapeDtypeStruct((m, RWKV_WIDTH), f32)] * 8,
        scratch_shapes=[pltpu.VMEM((tc + SUBLANES, RW_PAD), f32)],
        compiler_params=_cparams(("arbitrary",)),
        name="rwkv_prepare",
    )(*args)


def _rwkv_scan_body(r_ref, d_ref, k_ref, v_ref, kap_ref, b_ref, s0_ref, y_ref, st_ref, s_ref, *, tc, nb):
    c = pl.program_id(1)
    pairs = RWKV_HEADS // 2
    zero = jnp.zeros((HEAD_DIM, HEAD_DIM), f32)

    @pl.when(c == 0)
    def _():
        for bi in range(nb):
            for h in range(RWKV_HEADS):
                s0 = s0_ref[bi, h]
                s_ref[bi * RWKV_HEADS + h] = jnp.concatenate([s0, zero] if h % 2 == 0 else [zero, s0], axis=1)

    lane = lax.broadcasted_iota(jnp.int32, (HEAD_DIM, LANES), 1)
    sub = lax.broadcasted_iota(jnp.int32, (HEAD_DIM, LANES), 0)
    lane_row = lax.broadcasted_iota(jnp.int32, (1, LANES), 1)
    diag = [lane == sub + HEAD_DIM * par for par in range(2)]
    half = [((lane_row >= HEAD_DIM * par) & (lane_row < HEAD_DIM * (par + 1))).astype(f32) for par in range(2)]

    def group(bi, g, carry):
        base = pl.multiple_of(g * SUBLANES, SUBLANES)
        for pr in range(pairs):
            cols = slice(pr * LANES, (pr + 1) * LANES)
            r8, d8, k8, v8, kap8, b8 = (ref[bi, pl.ds(base, SUBLANES), cols]
                                        for ref in (r_ref, d_ref, k_ref, v_ref, kap_ref, b_ref))
            ys = [[], []]
            for j in range(SUBLANES):
                row = slice(j, j + 1)
                for par in range(2):
                    idx = bi * RWKV_HEADS + pr * 2 + par
                    s = s_ref[idx]
                    sa = -jnp.sum(s * kap8[row], axis=1, keepdims=True)
                    vcol = jnp.sum(jnp.where(diag[par], v8[row], 0.0), axis=1, keepdims=True)
                    sn = s * d8[row] + sa * (b8[row] * half[par]) + vcol * (k8[row] * half[par])
                    s_ref[idx] = sn
                    ys[par].append(_dot_nt(r8, sn)[row, :])
            y_ref[bi, pl.ds(base, SUBLANES), cols] = jnp.concatenate(
                [jnp.concatenate(ys[0], axis=0), jnp.concatenate(ys[1], axis=0)], axis=1)
        return carry

    for bi in range(nb):
        lax.fori_loop(0, tc // SUBLANES, functools.partial(group, bi), 0)

    @pl.when(c == pl.num_programs(1) - 1)
    def _():
        for bi in range(nb):
            for h in range(RWKV_HEADS):
                par = h % 2
                st_ref[bi, h] = s_ref[bi * RWKV_HEADS + h][:, par * HEAD_DIM:(par + 1) * HEAD_DIM]


def rwkv_scan(r, d, k, v, kap, b, s0, tc, nb=2):
    nbatch, seq = r.shape[:2]
    blk = pl.BlockSpec((nb, tc, RWKV_WIDTH), lambda g, c: (g, c, 0))
    st = pl.BlockSpec((nb, RWKV_HEADS, HEAD_DIM, HEAD_DIM), lambda g, c: (g, 0, 0, 0))
    return pl.pallas_call(
        functools.partial(_rwkv_scan_body, tc=tc, nb=nb),
        grid=(nbatch // nb, seq // tc),
        in_specs=[blk] * 6 + [st],
        out_specs=[blk, st],
        out_shape=[jax.ShapeDtypeStruct((nbatch, seq, RWKV_WIDTH), f32),
                   jax.ShapeDtypeStruct((nbatch, RWKV_HEADS, HEAD_DIM, HEAD_DIM), f32)],
        scratch_shapes=[pltpu.VMEM((nb * RWKV_HEADS, HEAD_DIM, LANES), f32)],
        compiler_params=_cparams(("arbitrary", "arbitrary")),
        name="rwkv_scan",
    )(r, d, k, v, kap, b, s0)


def _rwkv_post_body(y_ref, g_ref, bonus_ref, vec_ref, e_ref, o_ref):
    y = y_ref[...]
    e = e_ref[...]
    ln_g, ln_b = vec_ref[5:6, :], vec_ref[6:7, :]
    mu = _split_dot(y, e) * (1.0 / HEAD_DIM)
    yc = y - mu
    var = _split_dot(yc * yc, e) * (1.0 / HEAD_DIM)
    o_ref[...] = (yc * lax.rsqrt(var + GN_EPS) * ln_g + ln_b + bonus_ref[...]) * g_ref[...]


def rwkv_output(y, g, bonus, vecs, e, tc):
    m = y.shape[0]
    row = lambda i: (i, 0)
    const = lambda i: (0, 0)
    return pl.pallas_call(
        _rwkv_post_body,
        grid=(m // tc,),
        in_specs=[pl.BlockSpec((tc, RWKV_WIDTH), row)] * 3 +
                 [pl.BlockSpec((SUBLANES, RWKV_WIDTH), const), pl.BlockSpec((RWKV_WIDTH, RWKV_WIDTH), const)],
        out_specs=pl.BlockSpec((tc, RWKV_WIDTH), row),
        out_shape=jax.ShapeDtypeStruct((m, RWKV_WIDTH), f32),
        compiler_params=_cparams(("arbitrary",)),
        name="rwkv_output",
    )(y, g, bonus, vecs, e)


def rwkv7(p_all, prev, s0, wts, nbatch, seq, tc_prep, tc_scan):
    prev_exp = None
    if prev is not None:
        prev_exp = jnp.repeat(jnp.pad(prev, ((0, 0), (0, RW_PAD - RWKV_PROJ))), seq, axis=0)
    r, d, k, v, kap, b, g, bonus = rwkv_prepare(
        p_all, prev_exp, wts['mu'], wts['vecs'], wts['w2p'], wts['a2p'], wts['g2p'], wts['e'], seq, tc_prep)
    sh = lambda z: z.reshape(nbatch, seq, RWKV_WIDTH)
    y, s_new = rwkv_scan(sh(r), sh(d), sh(k), sh(v), sh(kap), sh(b), s0, tc_scan)
    o = rwkv_output(y.reshape(nbatch * seq, RWKV_WIDTH), g, bonus, wts['vecs'], wts['e'], tc_prep)
    return o, s_new


def _compress_head(x_ref, nchunk, wcat_ref, pe_ref, w1f_ref, b1_ref, w2_ref, b2_ref, acc_ref):
    kv2 = 2 * HEAD_DIM
    pe_term = jnp.concatenate([_dot(pe_ref[k], w1f_ref[k])[0:1, :] for k in range(2)], axis=1)
    bias = pe_term + b1_ref[...]
    acc = jnp.zeros((nchunk, 4 * kv2), f32)
    half = CMP_STRIDE // 2
    for r in range(half):
        xr = jnp.concatenate([x_ref[pl.ds(r, nchunk, stride=CMP_STRIDE), :],
                              x_ref[pl.ds(r + half, nchunk, stride=CMP_STRIDE), :]], axis=1)
        acc = acc + _dot(xr.astype(bf16), wcat_ref[r])
    acc_ref[0:nchunk, :] = acc
    acc_ref[nchunk:nchunk + SUBLANES, :] = jnp.zeros((SUBLANES, 4 * kv2), f32)
    first = jnp.concatenate([acc_ref[0:nchunk, 0:kv2], acc_ref[0:nchunk, 2 * kv2:3 * kv2]], axis=1)
    second = jnp.concatenate([acc_ref[1:nchunk + 1, kv2:2 * kv2], acc_ref[1:nchunk + 1, 3 * kv2:4 * kv2]], axis=1)
    hid = jax.nn.gelu(first + second + bias)
    return _dot(hid.astype(bf16), w2_ref[...]) + b2_ref[...]


def _compress_prompt_body(x_ref, wcat_ref, pe_ref, w1f_ref, b1_ref, w2_ref, b2_ref, o_ref, acc_ref, *, nchunk):
    o_ref[...] = _compress_head(x_ref, nchunk, wcat_ref, pe_ref, w1f_ref, b1_ref, w2_ref, b2_ref, acc_ref)


def _compress_weights(lp):
    w1 = lp['cmp_w1']
    z = jnp.zeros((CMP_STRIDE, HEAD_DIM, CMP_HIDDEN), f32)
    key_rows = jnp.concatenate([w1[0, :CMP_STRIDE], w1[0, CMP_STRIDE:], z, z], axis=-1)
    val_rows = jnp.concatenate([z, z, w1[1, :CMP_STRIDE], w1[1, CMP_STRIDE:]], axis=-1)
    wcat = jnp.concatenate([key_rows, val_rows], axis=1)
    wcat = jnp.concatenate([wcat[:CMP_STRIDE // 2], wcat[CMP_STRIDE // 2:]], axis=1).astype(bf16)
    pe = jnp.broadcast_to(lp['cmp_pe'].reshape(2, 1, CMP_BLOCK * HEAD_DIM), (2, SUBLANES, CMP_BLOCK * HEAD_DIM))
    w2 = lp['cmp_w2']
    zz = jnp.zeros((CMP_HIDDEN, HEAD_DIM), f32)
    w2bd = jnp.concatenate([jnp.concatenate([w2[0], zz], axis=1), jnp.concatenate([zz, w2[1]], axis=1)], axis=0)
    return {'wcat': wcat, 'pe': pe.astype(bf16),
            'w1f': w1.reshape(2, CMP_BLOCK * HEAD_DIM, CMP_HIDDEN).astype(bf16),
            'b1': lp['cmp_b1'].reshape(1, 2 * CMP_HIDDEN), 'w2bd': w2bd.astype(bf16),
            'b2': lp['cmp_b2'].reshape(1, 2 * HEAD_DIM)}


def _cmp_weight_specs(nidx):
    c2 = lambda *a: (0, 0)
    c3 = lambda *a: (0, 0, 0)
    return [pl.BlockSpec((CMP_STRIDE // 2, 4 * HEAD_DIM, 4 * CMP_HIDDEN), c3),
            pl.BlockSpec((2, SUBLANES, CMP_BLOCK * HEAD_DIM), c3),
            pl.BlockSpec((2, CMP_BLOCK * HEAD_DIM, CMP_HIDDEN), c3),
            pl.BlockSpec((1, 2 * CMP_HIDDEN), c2),
            pl.BlockSpec((2 * CMP_HIDDEN, 2 * HEAD_DIM), c2),
            pl.BlockSpec((1, 2 * HEAD_DIM), c2)]


def compress_prompt(p_all, cw, nbatch, seq):
    nchunk = seq // CMP_STRIDE
    kv2 = 2 * HEAD_DIM
    return pl.pallas_call(
        functools.partial(_compress_prompt_body, nchunk=nchunk),
        grid=(nbatch, NSA_KV_HEADS),
        in_specs=[pl.BlockSpec((seq, kv2), lambda b, h: (b, KVC0 // kv2 + h))] + _cmp_weight_specs(2),
        out_specs=pl.BlockSpec((nchunk, kv2), lambda b, h: (b, h)),
        out_shape=jax.ShapeDtypeStruct((nbatch * nchunk, KV_WIDTH), f32),
        scratch_shapes=[pltpu.VMEM((nchunk + SUBLANES, 4 * CMP_HIDDEN), f32)],
        compiler_params=_cparams(("arbitrary", "arbitrary")),
        name="compress_prompt",
    )(p_all, cw['wcat'], cw['pe'], cw['w1f'], cw['b1'], cw['w2bd'], cw['b2'])


def _masked_softmax(s, mask):
    s = jnp.where(mask, s, NEG_INF)
    m = jnp.max(s, axis=-1, keepdims=True)
    e = jnp.where(mask, jnp.exp(s - m), 0.0)
    return e / jnp.maximum(jnp.sum(e, axis=-1, keepdims=True), 1e-30)


def _nsa_prompt_body(q_ref, kc_ref, ks_ref, kw_ref, g_ref, wsel_ref, eexp_ref, o_ref, *, nsel, ncmp):
    i = pl.program_id(2)
    qb, grp = Q_BLOCK, NSA_GROUP
    rows = grp * qb
    q = q_ref[...]
    qs = jnp.concatenate([q[:, g * HEAD_DIM:(g + 1) * HEAD_DIM] for g in range(grp)], axis=0) * (HEAD_DIM ** -0.5)
    qp = jnp.concatenate([qs, jnp.zeros_like(qs)], axis=1).astype(bf16)
    qpos = i * qb + lax.broadcasted_iota(jnp.int32, (rows, 1), 0) % qb

    kc = kc_ref[...].astype(bf16)
    s_c = _dot_nt(qp, kc)
    cend = lax.broadcasted_iota(jnp.int32, (1, ncmp), 1) * CMP_STRIDE + (CMP_BLOCK - 1)
    p_c = _masked_softmax(s_c, cend <= qpos)
    o_c = _dot(p_c.astype(bf16), kc)
    imp = p_c[0:qb] + p_c[qb:2 * qb] + p_c[2 * qb:3 * qb] + p_c[3 * qb:4 * qb]
    imp_hi = imp.astype(bf16)
    imp_lo = (imp - imp_hi.astype(f32)).astype(bf16)
    wsel = wsel_ref[...]
    p_slc = _dot_nt(wsel, imp_hi) + _dot_nt(wsel, imp_lo)

    blk = lax.broadcasted_iota(jnp.int32, (nsel, qb), 0)
    qpos_t = i * qb + lax.broadcasted_iota(jnp.int32, (nsel, qb), 1)
    cur = qpos_t // SEL_BLOCK
    forced = (blk == 0) | (blk == cur) | (blk == cur - 1)
    score = jnp.where(forced, FORCE_SCORE, p_slc)
    score = jnp.where(blk * SEL_BLOCK <= qpos_t, score, -1.0)
    rank = jnp.zeros((nsel, qb), f32)
    for jp in range(nsel):
        row = score[jp:jp + 1, :]
        ahead = (row > score) | ((row == score) & (blk > jp))
        rank = rank + jnp.where(ahead, 1.0, 0.0)
    sel_t = jnp.where((rank < float(min(SEL_TOPK, nsel))) & (score >= 0.0), 1.0, 0.0)
    sel = sel_t.T.astype(bf16)

    def attend(ref, kt, carry, mask_fn):
        m, l, acc = carry
        k0 = pl.multiple_of(kt * qb, qb)
        kt_tile = ref[pl.ds(k0, qb), :].astype(bf16)
        s = _dot_nt(qp, kt_tile)
        kpos = k0 + lax.broadcasted_iota(jnp.int32, (1, qb), 1)
        mask = mask_fn(kt, kpos)
        s = jnp.where(mask, s, NEG_INF)
        m_new = jnp.maximum(m, jnp.max(s, axis=-1, keepdims=True))
        alpha = jnp.exp(m - m_new)
        e = jnp.where(mask, jnp.exp(s - m_new), 0.0)
        l = alpha * l + jnp.sum(e, axis=-1, keepdims=True)
        acc = alpha * acc + _dot(e.astype(bf16), kt_tile)
        return m_new, l, acc

    def sel_mask(kt, kpos):
        picked = _dot(sel, eexp_ref[kt])
        picked = jnp.concatenate([picked] * grp, axis=0)
        return (picked > 0.5) & (kpos <= qpos)

    def win_mask(kt, kpos):
        diff = qpos - kpos
        return (diff >= 0) & (diff <= WINDOW)

    init = (jnp.full((rows, 1), NEG_INF, f32), jnp.zeros((rows, 1), f32), jnp.zeros((rows, 2 * HEAD_DIM), f32))
    _, l_s, acc_s = lax.fori_loop(0, i + 1, lambda kt, c: attend(ks_ref, kt, c, sel_mask), init)
    _, l_w, acc_w = lax.fori_loop(jnp.maximum(i - WINDOW // qb, 0), i + 1,
                                  lambda kt, c: attend(kw_ref, kt, c, win_mask), init)
    o_s = acc_s / jnp.maximum(l_s, 1e-30)
    o_w = acc_w / jnp.maximum(l_w, 1e-30)

    gates = g_ref[0, 0]
    gate = lambda br: jnp.concatenate([gates[:, 3 * g + br:3 * g + br + 1] for g in range(grp)], axis=0)
    out = gate(0) * o_c + gate(1) * o_s + gate(2) * o_w
    o_ref[...] = jnp.concatenate([out[g * qb:(g + 1) * qb, HEAD_DIM:] for g in range(grp)], axis=1)


def nsa_prompt(p_all, kv_cmp, gates, nbatch, seq):
    nqb, nsel, ncmp = seq // Q_BLOCK, seq // SEL_BLOCK, seq // CMP_STRIDE
    kv2 = 2 * HEAD_DIM
    cidx = jnp.arange(ncmp)[None, :] - (SEL_BLOCK // CMP_STRIDE) * jnp.arange(nsel)[:, None]
    mult = jnp.array([1, 2, 2, 2, 1], f32)
    wsel = jnp.where((cidx >= 0) & (cidx <= 4), mult[jnp.clip(cidx, 0, 4)], 0.0).astype(bf16)
    key_blk = (jnp.arange(nqb)[:, None, None] * Q_BLOCK + jnp.arange(Q_BLOCK)[None, None, :]) // SEL_BLOCK
    eexp = (key_blk == jnp.arange(nsel)[None, :, None]).astype(bf16)
    return pl.pallas_call(
        functools.partial(_nsa_prompt_body, nsel=nsel, ncmp=ncmp),
        grid=(nbatch, NSA_KV_HEADS, nqb),
        in_specs=[pl.BlockSpec((Q_BLOCK, NSA_GROUP * HEAD_DIM),
                               lambda b, h, i: (b * nqb + i, Q0 // (NSA_GROUP * HEAD_DIM) + h)),
                  pl.BlockSpec((ncmp, kv2), lambda b, h, i: (b, h)),
                  pl.BlockSpec((seq, kv2), lambda b, h, i: (b, KVS0 // kv2 + h)),
                  pl.BlockSpec((seq, kv2), lambda b, h, i: (b, KVW0 // kv2 + h)),
                  pl.BlockSpec((1, 1, Q_BLOCK, 16), lambda b, h, i: (b, h, i, 0)),
                  pl.BlockSpec((nsel, ncmp), lambda b, h, i: (0, 0)),
                  pl.BlockSpec((nqb, nsel, Q_BLOCK), lambda b, h, i: (0, 0, 0))],
        out_specs=pl.BlockSpec((Q_BLOCK, NSA_GROUP * HEAD_DIM), lambda b, h, i: (b * nqb + i, h)),
        out_shape=jax.ShapeDtypeStruct((nbatch * seq, NSA_WIDTH), f32),
        compiler_params=_cparams(("arbitrary", "arbitrary", "arbitrary")),
        name="nsa_prompt",
    )(p_all, kv_cmp, p_all, p_all, gates, wsel, eexp)


def _branch_gates(p_all, nbatch, seq):
    g = p_all[:, GN0:GN0 + 3 * NSA_HEADS].reshape(nbatch, seq, NSA_KV_HEADS, 3 * NSA_GROUP)
    return jnp.pad(jnp.transpose(g, (0, 2, 1, 3)), ((0, 0), (0, 0), (0, 0), (0, 16 - 3 * NSA_GROUP)))


PAGES_PER_STEP = 16
CMP_HALVES = 2
TAIL_ROWS = CMP_STRIDE


def _compress_sample_body(pt_ref, *refs, nchunk, half_rows):
    npg = PAGES_PER_STEP
    pages, (nxt_ref, newc_ref) = refs[:npg], refs[npg:npg + 2]
    wcat_ref, pe_ref, w1f_ref, b1_ref, w2_ref, b2_ref, o_ref, xs_ref, acc_ref = refs[npg + 2:]
    half, s = pl.program_id(1), pl.program_id(2)
    kv2 = 2 * HEAD_DIM
    for k in range(npg):
        base = pl.multiple_of((s * npg + k) * PAGE_SIZE, PAGE_SIZE)
        for h in range(NSA_KV_HEADS):
            xs_ref[h, pl.ds(base, PAGE_SIZE), :] = pages[k][0, :, h * kv2:(h + 1) * kv2]

    @pl.when(s == pl.num_programs(2) - 1)
    def _():
        nblk = half_rows // CMP_STRIDE
        pad_rows = xs_ref.shape[1] - half_rows - TAIL_ROWS
        for h in range(NSA_KV_HEADS):
            lanes = slice(h * kv2, (h + 1) * kv2)
            tail = jnp.where(half == CMP_HALVES - 1, newc_ref[0, :, lanes], nxt_ref[0, :, lanes])
            xs_ref[h, half_rows:half_rows + TAIL_ROWS, :] = tail
            xs_ref[h, half_rows + TAIL_ROWS:, :] = jnp.zeros((pad_rows, kv2), f32)
            out = _compress_head(xs_ref.at[h], nchunk, wcat_ref, pe_ref, w1f_ref, b1_ref, w2_ref, b2_ref, acc_ref)
            o_ref[0, :, lanes] = out[0:nblk]


def compress_sample(cache, page_table, newc, cw):
    nbatch, npages = page_table.shape
    npg = PAGES_PER_STEP
    half_pages = npages // CMP_HALVES
    half_rows = half_pages * PAGE_SIZE
    nsteps = half_pages // npg
    nchunk = half_rows // CMP_STRIDE + SUBLANES
    page_spec = lambda k: pl.BlockSpec(
        (1, PAGE_SIZE, KV_WIDTH), lambda b, hf, s, pt: (pt[b, hf * half_pages + s * npg + k], 0, 0))
    nxt_spec = pl.BlockSpec(
        (1, TAIL_ROWS, KV_WIDTH), lambda b, hf, s, pt: (pt[b, jnp.minimum((hf + 1) * half_pages, npages - 1)], 0, 0))
    return pl.pallas_call(
        functools.partial(_compress_sample_body, nchunk=nchunk, half_rows=half_rows),
        grid_spec=pltpu.PrefetchScalarGridSpec(
            num_scalar_prefetch=1,
            grid=(nbatch, CMP_HALVES, nsteps),
            in_specs=[page_spec(k) for k in range(npg)] + [nxt_spec] +
                     [pl.BlockSpec((1, TAIL_ROWS, KV_WIDTH), lambda b, hf, s, pt: (b, 0, 0))] + _cmp_weight_specs(4),
            out_specs=pl.BlockSpec((1, half_rows // CMP_STRIDE, KV_WIDTH), lambda b, hf, s, pt: (b, hf, 0)),
            scratch_shapes=[pltpu.VMEM((NSA_KV_HEADS, nchunk * CMP_STRIDE, 2 * HEAD_DIM), f32),
                            pltpu.VMEM((nchunk + SUBLANES, 4 * CMP_HIDDEN), f32)]),
        out_shape=jax.ShapeDtypeStruct((nbatch, npages * PAGE_SIZE // CMP_STRIDE, KV_WIDTH), f32),
        compiler_params=_cparams(("arbitrary", "arbitrary", "arbitrary")),
        name="compress_sample",
    )(page_table, *([cache] * (npg + 1)), newc, cw['wcat'], cw['pe'], cw['w1f'], cw['b1'], cw['w2bd'], cw['b2'])


def _split_dot_r(e, x):
    hi = x.astype(bf16)
    lo = (x - hi.astype(f32)).astype(bf16)
    return _dot(e, hi) + _dot(e, lo)


def _softmax_rows(s, mask):
    s = jnp.where(mask, s, NEG_INF)
    m = jnp.max(s, axis=0, keepdims=True)
    e = jnp.where(mask, jnp.exp(s - m), 0.0)
    return e / jnp.maximum(jnp.sum(e, axis=0, keepdims=True), 1e-30)


def _row_to_col(row):
    return jnp.broadcast_to(row, (SUBLANES, LANES)).T[:, 0:1]


def _nsa_sample_body(pt_ref, *refs, tnew, nsel, nselp):
    npg = PAGES_PER_STEP
    pages = refs[:npg]
    (qb_ref, kc_ref, win_ref, neww_ref, news_ref, g_ref, wsel_ref, gm_ref, o_ref,
     sel_ref, score_ref, st_ref, m_ref, l_ref, acc_ref, oc_ref, ow_ref) = refs[npg:]
    s = pl.program_id(1)
    qb = qb_ref[0]
    col = lax.broadcasted_iota(jnp.int32, (1, LANES), 1)
    tq = col % tnew
    qpos = PAST_LEN + tq
    pad_tail = lambda ref: jnp.concatenate(
        [ref[0], jnp.zeros((LANES - TAIL_ROWS, KV_WIDTH), f32)], axis=0).astype(bf16)

    @pl.when(s == 0)
    def _():
        kc = kc_ref[0].astype(bf16)
        ncmp = kc.shape[0]
        cend = lax.broadcasted_iota(jnp.int32, (ncmp, 1), 0) * CMP_STRIDE + (CMP_BLOCK - 1)
        p_c = _softmax_rows(_dot(kc, qb), cend <= qpos)
        oc_ref[...] = _dot(p_c.T.astype(bf16), kc)
        imp = _split_dot(p_c, gm_ref[...])
        p_slc = _split_dot_r(wsel_ref[...], imp)
        blk = lax.broadcasted_iota(jnp.int32, (nselp, LANES), 0)
        cur = qpos // SEL_BLOCK
        forced = (blk == 0) | (blk == cur) | (blk == cur - 1)
        score = jnp.where(forced, FORCE_SCORE, p_slc)
        score = jnp.where((blk * SEL_BLOCK <= qpos) & (blk < nsel), score, -1.0)
        score_ref[...] = score

        def rank_group(gi, rank):
            rows = score_ref[pl.ds(pl.multiple_of(gi * SUBLANES, SUBLANES), SUBLANES), :]
            for j in range(SUBLANES):
                row = rows[j:j + 1, :]
                ahead = (row > score) | ((row == score) & (blk > gi * SUBLANES + j))
                rank = rank + jnp.where(ahead, 1.0, 0.0)
            return rank

        rank = lax.fori_loop(0, nselp // SUBLANES, rank_group, jnp.zeros((nselp, LANES), f32))
        sel_ref[...] = jnp.where((rank < float(SEL_TOPK)) & (score >= 0.0), 1.0, 0.0)

        kw = jnp.concatenate([win_ref[0].astype(bf16), pad_tail(neww_ref)], axis=0)
        idx = lax.broadcasted_iota(jnp.int32, (kw.shape[0], 1), 0)
        wbuf = win_ref.shape[1]
        p_w = _softmax_rows(_dot(kw, qb), (idx <= wbuf + tq) & (idx >= wbuf + tq - WINDOW))
        ow_ref[...] = _dot(p_w.T.astype(bf16), kw)

        m_ref[...] = jnp.full(m_ref.shape, NEG_INF, f32)
        l_ref[...] = jnp.zeros(l_ref.shape, f32)
        acc_ref[...] = jnp.zeros(acc_ref.shape, f32)

    def fold(tiles):
        m_old = m_ref[0:1, :]
        m_new = m_old
        for sc, _ in tiles:
            m_new = jnp.maximum(m_new, jnp.max(sc, axis=0, keepdims=True))
        alpha = jnp.exp(m_old - m_new)
        lsum = alpha * l_ref[0:1, :]
        acc = _row_to_col(alpha) * acc_ref[...]
        for sc, vals in tiles:
            e = jnp.where(sc > 0.5 * NEG_INF, jnp.exp(sc - m_new), 0.0)
            lsum = lsum + jnp.sum(e, axis=0, keepdims=True)
            acc = acc + _dot(e.T.astype(bf16), vals)
        m_ref[...] = jnp.broadcast_to(m_new, m_ref.shape)
        l_ref[...] = jnp.broadcast_to(lsum, l_ref.shape)
        acc_ref[...] = acc

    key_hi = lax.broadcasted_iota(jnp.int32, (PAGE_SIZE, LANES), 0) >= SEL_BLOCK
    tiles = []
    for k in range(npg):
        pg = pages[k][0].astype(bf16)
        grp = sel_ref[pl.ds(pl.multiple_of(s * 2 * npg + (2 * k // SUBLANES) * SUBLANES, SUBLANES), SUBLANES), :]
        r0 = (2 * k) % SUBLANES
        mask = jnp.where(key_hi, grp[r0 + 1:r0 + 2, :], grp[r0:r0 + 1, :]) > 0.5
        st_ref[k * PAGE_SIZE:(k + 1) * PAGE_SIZE, :] = jnp.where(mask, _dot(pg, qb), NEG_INF)
        tiles.append((st_ref[k * PAGE_SIZE:(k + 1) * PAGE_SIZE, :], pg))
    fold(tiles)

    @pl.when(s == pl.num_programs(1) - 1)
    def _():
        ns = pad_tail(news_ref)
        kidx = lax.broadcasted_iota(jnp.int32, (LANES, 1), 0)
        nb = nsel - 1
        grp = sel_ref[(nb // SUBLANES) * SUBLANES:(nb // SUBLANES + 1) * SUBLANES, :]
        mask = (grp[nb % SUBLANES:nb % SUBLANES + 1, :] > 0.5) & (kidx <= tq)
        fold([(jnp.where(mask, _dot(ns, qb), NEG_INF), ns)])
        o_s = acc_ref[...] / jnp.maximum(_row_to_col(l_ref[0:1, :]), 1e-30)
        rows_per_head = LANES // NSA_KV_HEADS
        for h in range(NSA_KV_HEADS):
            rows = slice(h * rows_per_head, (h + 1) * rows_per_head)
            lanes = slice(h * 2 * HEAD_DIM, (h + 1) * 2 * HEAD_DIM)
            o_ref[0, rows, :] = (g_ref[0, 0, rows, :] * oc_ref[rows, lanes] + g_ref[0, 1, rows, :] * o_s[rows, lanes]
                                 + g_ref[0, 2, rows, :] * ow_ref[rows, lanes])


def nsa_sample(cache_sel, page_table, qblk, kv_cmp, win_buf, neww, news, gates, tnew):
    nbatch, npages = page_table.shape
    npg = PAGES_PER_STEP
    ncmp = kv_cmp.shape[1]
    nsel = npages * (PAGE_SIZE // SEL_BLOCK) + 1
    nselp = -(-nsel // SUBLANES) * SUBLANES
    cidx = jnp.arange(ncmp)[None, :] - (SEL_BLOCK // CMP_STRIDE) * jnp.arange(nselp)[:, None]
    mult = jnp.array([1, 2, 2, 2, 1], f32)
    wsel = jnp.where((cidx >= 0) & (cidx <= 4), mult[jnp.clip(cidx, 0, 4)], 0.0).astype(bf16)
    c = jnp.arange(LANES)
    same = (c[:, None] // (NSA_GROUP * tnew) == c[None, :] // (NSA_GROUP * tnew)) & (c[:, None] % tnew == c[None, :] % tnew)
    gm = same.astype(bf16)
    b3 = lambda b, s, pt: (b, 0, 0)
    c2 = lambda b, s, pt: (0, 0)
    page_spec = lambda k: pl.BlockSpec((1, PAGE_SIZE, KV_WIDTH), lambda b, s, pt: (pt[b, s * npg + k], 0, 0))
    wbuf = win_buf.shape[1]
    return pl.pallas_call(
        functools.partial(_nsa_sample_body, tnew=tnew, nsel=nsel, nselp=nselp),
        grid_spec=pltpu.PrefetchScalarGridSpec(
            num_scalar_prefetch=1,
            grid=(nbatch, npages // npg),
            in_specs=[page_spec(k) for k in range(npg)] + [
                pl.BlockSpec((1, KV_WIDTH, LANES), b3),
                pl.BlockSpec((1, ncmp, KV_WIDTH), b3),
                pl.BlockSpec((1, wbuf, KV_WIDTH), b3),
                pl.BlockSpec((1, TAIL_ROWS, KV_WIDTH), b3),
                pl.BlockSpec((1, TAIL_ROWS, KV_WIDTH), b3),
                pl.BlockSpec((1, 3, LANES, LANES), lambda b, s, pt: (b, 0, 0, 0)),
                pl.BlockSpec((nselp, ncmp), c2),
                pl.BlockSpec((LANES, LANES), c2)],
            out_specs=pl.BlockSpec((1, LANES, LANES), b3),
            scratch_shapes=[pltpu.VMEM((nselp, LANES), f32), pltpu.VMEM((nselp, LANES), f32),
                            pltpu.VMEM((npg * PAGE_SIZE, LANES), f32),
                            pltpu.VMEM((SUBLANES, LANES), f32), pltpu.VMEM((SUBLANES, LANES), f32),
                            pltpu.VMEM((LANES, KV_WIDTH), f32), pltpu.VMEM((LANES, KV_WIDTH), f32),
                            pltpu.VMEM((LANES, KV_WIDTH), f32)]),
        out_shape=jax.ShapeDtypeStruct((nbatch, LANES, LANES), f32),
        compiler_params=_cparams(("arbitrary", "arbitrary")),
        name="nsa_sample",
    )(page_table, *([cache_sel] * npg), qblk, kv_cmp, win_buf, neww, news, gates, wsel, gm)


def _sample_nsa_inputs(p_all, b, t):
    q = p_all[:, Q0:Q0 + NSA_WIDTH].reshape(b, t, NSA_KV_HEADS, NSA_GROUP, HEAD_DIM) * (HEAD_DIM ** -0.5)
    qt = jnp.transpose(q, (0, 2, 4, 3, 1)).reshape(b, NSA_KV_HEADS, HEAD_DIM, NSA_GROUP * t)
    qt = jnp.pad(qt, ((0, 0), (0, 0), (0, HEAD_DIM), (0, 0)))
    eye = jnp.eye(NSA_KV_HEADS, dtype=f32)
    qblk = (qt[:, :, :, None, :] * eye[None, :, None, :, None]).reshape(b, KV_WIDTH, LANES).astype(bf16)
    g = p_all[:, GN0:GN0 + 3 * NSA_HEADS].reshape(b, t, NSA_KV_HEADS, NSA_GROUP, 3)
    g = jnp.transpose(g, (0, 4, 2, 3, 1)).reshape(b, 3, LANES)
    gates = jnp.broadcast_to(g[..., None], (b, 3, LANES, LANES))
    tail = lambda c0: jnp.pad(p_all[:, c0:c0 + KV_WIDTH].reshape(b, t, KV_WIDTH), ((0, 0), (0, TAIL_ROWS - t), (0, 0)))
    return qblk, gates, tail(KVC0), tail(KVS0), tail(KVW0)


def _prepare_weights(lp):
    w_in = lp['w_in']
    nsa_end = NSA_WIDTH + 3 * KV_WIDTH
    w_all = jnp.concatenate([
        w_in[:, NSA_PROJ:NSA_PROJ + RWKV_PROJ], w_in[:, nsa_end:NSA_PROJ],
        jnp.zeros((D_MODEL, RW_PAD - RWKV_PROJ - 3 * NSA_HEADS), w_in.dtype),
        w_in[:, :nsa_end], w_in[:, NSA_PROJ + RWKV_PROJ:]], axis=1).astype(bf16)
    pad_rows = lambda w, r0: jnp.zeros((LORA_PAD, RWKV_WIDTH), f32).at[r0:r0 + w.shape[0]].set(w).astype(bf16)
    head = jnp.arange(RWKV_WIDTH) // HEAD_DIM
    vecs = jnp.stack([lp['rwkv_w0'], lp['rwkv_a0'], lp['rwkv_k_k'], lp['rwkv_k_a'],
                      lp['rwkv_r_k'].reshape(RWKV_WIDTH), lp['rwkv_ln_g'], lp['rwkv_ln_b'],
                      jnp.zeros((RWKV_WIDTH,), f32)])
    return {
        'w_all': w_all,
        'mu': jnp.pad(lp['rwkv_mu'], (0, RW_PAD - RWKV_PROJ)).reshape(1, RW_PAD),
        'vecs': vecs,
        'w2p': pad_rows(lp['rwkv_w2'], 0),
        'a2p': pad_rows(lp['rwkv_a2'], DECAY_LORA),
        'g2p': pad_rows(lp['rwkv_g2'], DECAY_LORA + AAA_LORA),
        'e': (head[:, None] == head[None, :]).astype(bf16),
        'p_nsa': lp['p_nsa'].astype(bf16), 'p_rwkv': lp['p_rwkv'].astype(bf16),
        'w_out': lp['w_out'].astype(bf16),
        'mlp_w1': lp['mlp_w1'].astype(bf16), 'mlp_w2': lp['mlp_w2'].astype(bf16),
    }


def _group_forward(x, pos_rows, prev, s0, lp, wts, final_g, tm, tc_prep, tc_scan, o_nsa_fn):
    nbatch, seq = x.shape[:2]
    x2d = x.reshape(nbatch * seq, D_MODEL)
    tq, tkv = _rot_tables(pos_rows)
    p_all = input_projection(x2d, lp['norm1_g'], wts['w_all'], tq, tkv, tm)
    o_rw, s_new = rwkv7(p_all, prev, s0, wts, nbatch, seq, tc_prep, tc_scan)
    o_nsa = o_nsa_fn(p_all)
    h, hn = merge_project(o_nsa, o_rw, p_all, x2d, wts['p_nsa'], wts['p_rwkv'], wts['w_out'], lp['norm2_g'],
                          min(tm, 256))
    y = mlp_residual_norm(hn, h, wts['mlp_w1'], wts['mlp_w2'], final_g, tm)
    return y.reshape(nbatch, seq, D_MODEL), p_all, s_new


def _kv_rows(p_all, col0, nbatch, seq):
    return p_all[:, col0:col0 + KV_WIDTH].reshape(nbatch, seq, NSA_KV_HEADS, 2, HEAD_DIM)


def sample_nsa_attention(p_all, cache_cmp, cache_sel, win_buf, page_table, cw, b, t):
    qblk, gates, newc, news, neww = _sample_nsa_inputs(p_all, b, t)
    pool = cache_cmp.shape[0]
    kv_cmp = compress_sample(cache_cmp.reshape(pool, PAGE_SIZE, KV_WIDTH), page_table, newc, cw)
    o = nsa_sample(cache_sel.reshape(pool, PAGE_SIZE, KV_WIDTH), page_table, qblk, kv_cmp,
                   win_buf.reshape(b, win_buf.shape[1], KV_WIDTH), neww, news, gates, t)
    o = o[..., HEAD_DIM:].reshape(b, NSA_KV_HEADS, NSA_GROUP, t, HEAD_DIM)
    return jnp.transpose(o, (0, 3, 1, 2, 4)).reshape(b * t, NSA_WIDTH)


def _jsoftmax(s, mask):
    s = jnp.where(mask, s.astype(f32), NEG_INF)
    m = jnp.max(s, axis=-1, keepdims=True)
    e = jnp.where(mask, jnp.exp(s - m), 0.0)
    return e / jnp.maximum(jnp.sum(e, axis=-1, keepdims=True), 1e-30)


def _pad_time(z, mult):
    pad = (-z.shape[1]) % mult
    return jnp.pad(z, [(0, 0), (0, pad)] + [(0, 0)] * (z.ndim - 2))


def _chunk_proj(rows, cmp_w1):
    b, l = rows.shape[:2]
    ch = rows.reshape(b, l // CMP_STRIDE, CMP_STRIDE, NSA_KV_HEADS, 2, HEAD_DIM)
    first = jnp.einsum('bcrhkd,krde->bchke', ch, cmp_w1[:, :CMP_STRIDE])
    second = jnp.einsum('bcrhkd,krde->bchke', ch, cmp_w1[:, CMP_STRIDE:])
    return first, second


def _gather_blocks(blocks, idx):
    bb = jnp.arange(blocks.shape[0])[:, None, None, None]
    hh = jnp.arange(NSA_KV_HEADS)[None, None, :, None]
    return blocks[bb, idx, :, hh]


def _sample_nsa_jax(p_all, cache_cmp, cache_sel, win_buf, page_table, lp, b, t):
    scale = HEAD_DIM ** -0.5
    q = p_all[:, Q0:Q0 + NSA_WIDTH].reshape(b, t, NSA_KV_HEADS, NSA_GROUP, HEAD_DIM)
    kvc, kvs, kvw = (_kv_rows(p_all, c, b, t) for c in (KVC0, KVS0, KVW0))
    gates = p_all[:, GN0:GN0 + 3 * NSA_HEADS].reshape(b, t, NSA_KV_HEADS, NSA_GROUP, 3)
    q_pos = PAST_LEN + jnp.arange(t, dtype=jnp.int32)
    past_cmp = cache_cmp[page_table].reshape(b, PAST_LEN, NSA_KV_HEADS, 2, HEAD_DIM)
    fp, sp = _chunk_proj(past_cmp, lp['cmp_w1'])
    fn, sn = _chunk_proj(_pad_time(kvc, CMP_STRIDE), lp['cmp_w1'])
    first, second = jnp.concatenate([fp, fn], axis=1), jnp.concatenate([sp, sn], axis=1)
    pe_term = jnp.einsum('krd,krde->ke', lp['cmp_pe'], lp['cmp_w1'])
    hid = jax.nn.gelu(first[:, :-1] + second[:, 1:] + pe_term + lp['cmp_b1'])
    kv_cmp = jnp.einsum('bchke,ked->bchkd', hid, lp['cmp_w2']) + lp['cmp_b2']
    cmp_end = jnp.arange(kv_cmp.shape[1], dtype=jnp.int32) * CMP_STRIDE + CMP_BLOCK - 1
    n_past = PAST_LEN // SEL_BLOCK
    new_blocks = _pad_time(kvs, SEL_BLOCK).reshape(b, -1, SEL_BLOCK, NSA_KV_HEADS, 2, HEAD_DIM)
    n_new = new_blocks.shape[1]
    n_sel = n_past + n_new
    past_blocks = cache_sel[page_table].reshape(b, n_past, SEL_BLOCK, NSA_KV_HEADS, 2, HEAD_DIM)
    s_c = jnp.einsum('bqhgd,bchd->bqhgc', q, kv_cmp[..., 0, :], preferred_element_type=f32) * scale
    p_c = _jsoftmax(s_c, (cmp_end[None, :] <= q_pos[:, None])[None, :, None, None, :])
    o_c = jnp.einsum('bqhgc,bchd->bqhgd', p_c, kv_cmp[..., 1, :])
    imp = jnp.sum(p_c, axis=3)
    nc = imp.shape[-1]
    cps, sub = SEL_BLOCK // CMP_STRIDE, CMP_BLOCK // CMP_STRIDE
    imp = jnp.pad(imp, ((0, 0), (0, 0), (0, 0), (0, cps * n_sel + cps + sub - nc)))
    p_slc = sum(imp[..., m + n: m + n + cps * n_sel: cps] for m in range(cps) for n in range(sub))
    blk = jnp.arange(n_sel)
    cur = (q_pos // SEL_BLOCK)[:, None]
    forced = (blk[None, :] == 0) | (blk[None, :] == cur) | (blk[None, :] == cur - 1)
    score = jnp.where(forced[None, :, None, :], FORCE_SCORE, p_slc)
    score = jnp.where((blk[None, :] * SEL_BLOCK <= q_pos[:, None])[None, :, None, :], score, -1.0)
    top_s, idx = lax.top_k(score, SEL_TOPK)
    rows_all = jnp.concatenate([past_blocks, new_blocks], axis=1).reshape(
        b, n_sel * SEL_BLOCK, NSA_KV_HEADS, 2, HEAD_DIM)
    picked = jnp.any((idx[..., None] == blk) & (top_s >= 0.0)[..., None], axis=-2)
    tok = jnp.arange(n_sel * SEL_BLOCK)
    m_s = jnp.repeat(picked, SEL_BLOCK, axis=-1) & (tok[None, :] <= q_pos[:, None])[None, :, None, :]
    s_s = jnp.einsum('bqhgd,bnhd->bqhgn', q, rows_all[..., 0, :], preferred_element_type=f32) * scale
    p_s = _jsoftmax(s_s, m_s[:, :, :, None, :])
    o_s = jnp.einsum('bqhgn,bnhd->bqhgd', p_s, rows_all[..., 1, :])
    kv_win = jnp.concatenate([win_buf, kvw], axis=1)
    w_buf = win_buf.shape[1]
    win_pos = PAST_LEN - w_buf + jnp.arange(w_buf + t, dtype=jnp.int32)
    s_w = jnp.einsum('bqhgd,blhd->bqhgl', q, kv_win[..., 0, :], preferred_element_type=f32) * scale
    diff = q_pos[:, None] - win_pos[None, :]
    p_w = _jsoftmax(s_w, ((diff >= 0) & (diff <= WINDOW) & (win_pos[None, :] >= 0))[None, :, None, None, :])
    o_w = jnp.einsum('bqhgl,blhd->bqhgd', p_w, kv_win[..., 1, :])
    o = gates[..., 0:1] * o_c + gates[..., 1:2] * o_s + gates[..., 2:3] * o_w
    return o.reshape(b * t, NSA_WIDTH), kv_win[:, t:]


def kernel(x_prompt, x_sample, cache_cmp_kv, cache_sel_kv, state_nsa_win, state_rwkv, state_rwkv_shift,
           page_table, norm1_g, w_in, cmp_pe, cmp_w1, cmp_b1, cmp_w2, cmp_b2, rwkv_mu, rwkv_w0, rwkv_w2,
           rwkv_a0, rwkv_a2, rwkv_g2, rwkv_k_k, rwkv_k_a, rwkv_r_k, rwkv_ln_g, rwkv_ln_b, p_nsa, p_rwkv,
           w_out, norm2_g, mlp_w1, mlp_w2, final_g):
    l = 0
    lp = {'norm1_g': norm1_g[l], 'w_in': w_in[l], 'cmp_pe': cmp_pe[l], 'cmp_w1': cmp_w1[l],
          'cmp_b1': cmp_b1[l], 'cmp_w2': cmp_w2[l], 'cmp_b2': cmp_b2[l], 'rwkv_mu': rwkv_mu[l],
          'rwkv_w0': rwkv_w0[l], 'rwkv_w2': rwkv_w2[l], 'rwkv_a0': rwkv_a0[l], 'rwkv_a2': rwkv_a2[l],
          'rwkv_g2': rwkv_g2[l], 'rwkv_k_k': rwkv_k_k[l], 'rwkv_k_a': rwkv_k_a[l], 'rwkv_r_k': rwkv_r_k[l],
          'rwkv_ln_g': rwkv_ln_g[l], 'rwkv_ln_b': rwkv_ln_b[l], 'p_nsa': p_nsa[l], 'p_rwkv': p_rwkv[l],
          'w_out': w_out[l], 'norm2_g': norm2_g[l], 'mlp_w1': mlp_w1[l], 'mlp_w2': mlp_w2[l]}
    wts = _prepare_weights(lp)
    bp, tp = x_prompt.shape[:2]
    bs, ts = x_sample.shape[:2]
    cw = _compress_weights(lp)
    zero_state = jnp.zeros((bp, RWKV_HEADS, HEAD_DIM, HEAD_DIM), f32)

    def prompt_nsa(p_all):
        return nsa_prompt(p_all, compress_prompt(p_all, cw, bp, tp), _branch_gates(p_all, bp, tp), bp, tp)

    yp, pp, sp = _group_forward(x_prompt, jnp.arange(tp, dtype=jnp.int32), None, zero_state, lp, wts, final_g,
                                512, 256, 128, prompt_nsa)

    def sample_nsa(p_all):
        return sample_nsa_attention(p_all, cache_cmp_kv[l], cache_sel_kv[l], state_nsa_win[l], page_table, cw, bs, ts)

    pos_s = PAST_LEN + jnp.arange(bs * ts, dtype=jnp.int32) % ts
    ys, ps, ss = _group_forward(x_sample, pos_s, state_rwkv_shift[l], state_rwkv[l], lp, wts, final_g,
                                256, 256, ts, sample_nsa)
    win_new = [jnp.concatenate([state_nsa_win[l], _kv_rows(ps, KVW0, bs, ts)], axis=1)[:, ts:]]
    wlen = min(WINDOW, tp)
    shift = lambda p_all, nb, t: p_all.reshape(nb, t, P_WIDTH)[:, -1, RW0:RW0 + RWKV_PROJ]
    return (yp, ys,
            _kv_rows(pp, KVC0, bp, tp)[None], _kv_rows(ps, KVC0, bs, ts)[None],
            _kv_rows(pp, KVS0, bp, tp)[None], _kv_rows(ps, KVS0, bs, ts)[None],
            _kv_rows(pp, KVW0, bp, tp)[None, :, tp - wlen:], win_new[0][None],
            sp[None], ss[None],
            shift(pp, bp, tp)[None], shift(ps, bs, ts)[None])
```

```python
import functools

import jax
import jax.numpy as jnp
from jax import lax
from jax.experimental import pallas as pl
from jax.experimental.pallas import tpu as pltpu

D_MODEL = 2048
DEPTH = 1
PAST_LEN = 16384
PAGE_SIZE = 128

HEAD_DIM = 64
NSA_HEADS = D_MODEL // (2 * HEAD_DIM)
NSA_KV_HEADS = NSA_HEADS // 4
NSA_GROUP = NSA_HEADS // NSA_KV_HEADS
NSA_WIDTH = NSA_HEADS * HEAD_DIM
KV_WIDTH = NSA_KV_HEADS * 2 * HEAD_DIM
CMP_BLOCK = 32
CMP_STRIDE = 16
CMP_HIDDEN = 2 * HEAD_DIM
SEL_BLOCK = 64
SEL_TOPK = 16
WINDOW = 512
Q_BLOCK = 128
ROT_DIM = HEAD_DIM // 4
ROPE_THETA = 500000.0
RWKV_HEADS = D_MODEL // (2 * HEAD_DIM)
RWKV_WIDTH = RWKV_HEADS * HEAD_DIM
DECAY_LORA = max(32, int(round(1.8 * D_MODEL ** 0.5 / 32)) * 32)
AAA_LORA = DECAY_LORA
GATE_LORA = max(32, int(round(0.6 * D_MODEL ** 0.8 / 32)) * 32)
RWKV_PROJ = 3 * RWKV_WIDTH + DECAY_LORA + AAA_LORA + GATE_LORA
NSA_PROJ = NSA_WIDTH + 3 * KV_WIDTH + 3 * NSA_HEADS
N_IN = NSA_PROJ + RWKV_PROJ + 2 * D_MODEL
D_FF = 4 * D_MODEL
RMS_EPS = 1e-6
GN_EPS = HEAD_DIM * 1e-5
NEG_INF = -1e30
FORCE_SCORE = 1e6

LANES = 128
SUBLANES = 8
VMEM_LIMIT_BYTES = 56 * 1024 * 1024

RW0 = 0
LORA0 = 3 * RWKV_WIDTH
GN0 = RWKV_PROJ
RW_PAD = 3584
LORA_PAD = RW_PAD - LORA0
Q0 = RW_PAD
KVC0 = Q0 + NSA_WIDTH
KVS0 = KVC0 + KV_WIDTH
KVW0 = KVS0 + KV_WIDTH
GM0 = KVW0 + KV_WIDTH
P_WIDTH = GM0 + 2 * D_MODEL
PROJ_TN = 512

f32 = jnp.float32
bf16 = jnp.bfloat16


def _cparams(sem):
    return pltpu.CompilerParams(dimension_semantics=sem, vmem_limit_bytes=VMEM_LIMIT_BYTES)


def _rms(x, g):
    return x * lax.rsqrt(jnp.mean(x * x, axis=-1, keepdims=True) + RMS_EPS) * g


def _dot(a, b):
    return jnp.dot(a, b, preferred_element_type=f32)


def _dot_nt(a, b):
    return lax.dot_general(a, b, (((1,), (1,)), ((), ())), preferred_element_type=f32)


def _split_dot(x, e):
    hi = x.astype(bf16)
    lo = (x - hi.astype(f32)).astype(bf16)
    return _dot(hi, e) + _dot(lo, e)


def _rot_store(acc, tab_ref, o_ref):
    c, s1, s2 = tab_ref[0], tab_ref[1], tab_ref[2]
    for s in range(acc.shape[1] // LANES):
        x = acc[:, s * LANES:(s + 1) * LANES]
        o_ref[:, s * LANES:(s + 1) * LANES] = (
            x * c + pltpu.roll(x, LANES - ROT_DIM // 2, 1) * s1 + pltpu.roll(x, ROT_DIM // 2, 1) * s2)


def _proj_body(x_ref, g_ref, w_ref, tq_ref, tkv_ref, o_ref, xn_ref):
    j = pl.program_id(1)

    @pl.when(j == 0)
    def _():
        xn_ref[...] = _rms(x_ref[...], g_ref[...]).astype(bf16)

    acc = _dot(xn_ref[...], w_ref[...])
    gn_tile = GN0 // PROJ_TN

    @pl.when(j < gn_tile)
    def _():
        o_ref[...] = acc

    @pl.when(j == gn_tile)
    def _():
        lane = lax.broadcasted_iota(jnp.int32, acc.shape, 1)
        o_ref[...] = jnp.where(lane >= GN0 - gn_tile * PROJ_TN, jax.nn.sigmoid(acc), acc)

    @pl.when((j >= Q0 // PROJ_TN) & (j < KVC0 // PROJ_TN))
    def _():
        _rot_store(acc, tq_ref, o_ref)

    @pl.when((j >= KVC0 // PROJ_TN) & (j < GM0 // PROJ_TN))
    def _():
        _rot_store(acc, tkv_ref, o_ref)

    @pl.when(j >= GM0 // PROJ_TN)
    def _():
        o_ref[...] = jax.nn.sigmoid(acc)


def input_projection(x2d, g, w_all, tq, tkv, tm):
    m = x2d.shape[0]
    nt = tq.shape[1] // tm
    return pl.pallas_call(
        _proj_body,
        grid=(m // tm, P_WIDTH // PROJ_TN),
        in_specs=[pl.BlockSpec((tm, D_MODEL), lambda i, j: (i, 0)),
                  pl.BlockSpec((1, D_MODEL), lambda i, j: (0, 0)),
                  pl.BlockSpec((D_MODEL, PROJ_TN), lambda i, j: (0, j)),
                  pl.BlockSpec((3, tm, LANES), lambda i, j: (0, i % nt, 0)),
                  pl.BlockSpec((3, tm, LANES), lambda i, j: (0, i % nt, 0))],
        out_specs=pl.BlockSpec((tm, PROJ_TN), lambda i, j: (i, j)),
        out_shape=jax.ShapeDtypeStruct((m, P_WIDTH), f32),
        scratch_shapes=[pltpu.VMEM((tm, D_MODEL), bf16)],
        compiler_params=_cparams(("arbitrary", "arbitrary")),
        name="input_projection",
    )(x2d, g.reshape(1, D_MODEL), w_all, tq, tkv)


def _rot_tables(pos):
    half = ROT_DIM // 2
    freqs = jnp.power(jnp.float32(ROPE_THETA), -jnp.arange(half, dtype=f32) * 2.0 / ROT_DIM)
    ang = pos.astype(f32)[:, None] * freqs[None, :]
    cos, sin = jnp.cos(ang), jnp.sin(ang)
    lane = jnp.arange(LANES)

    def build(period):
        l = lane % period
        fi = l % half
        c = jnp.where(l < ROT_DIM, cos[:, fi], 1.0)
        s1 = jnp.where(l < half, -sin[:, fi], 0.0)
        s2 = jnp.where((l >= half) & (l < ROT_DIM), sin[:, fi], 0.0)
        return jnp.stack([c, s1, s2]).astype(f32)

    return build(HEAD_DIM), build(2 * HEAD_DIM)


def _merge_body(on_ref, or_ref, ga_ref, gb_ref, x_ref, pn_ref, pr_ref, wo_ref, g2_ref, h_ref, hn_ref):
    a = _dot(on_ref[...].astype(bf16), pn_ref[...])
    b = _dot(or_ref[...].astype(bf16), pr_ref[...])
    mix = ga_ref[...] * a + gb_ref[...] * b
    h = x_ref[...] + _dot(mix.astype(bf16), wo_ref[...])
    h_ref[...] = h
    hn_ref[...] = _rms(h, g2_ref[...]).astype(bf16)


def merge_project(o_nsa, o_rw, p_all, x2d, pn, pr, wo, g2, tm):
    m = x2d.shape[0]
    const = lambda i: (0, 0)
    return pl.pallas_call(
        _merge_body,
        grid=(m // tm,),
        in_specs=[pl.BlockSpec((tm, NSA_WIDTH), lambda i: (i, 0)),
                  pl.BlockSpec((tm, RWKV_WIDTH), lambda i: (i, 0)),
                  pl.BlockSpec((tm, D_MODEL), lambda i: (i, GM0 // D_MODEL)),
                  pl.BlockSpec((tm, D_MODEL), lambda i: (i, GM0 // D_MODEL + 1)),
                  pl.BlockSpec((tm, D_MODEL), lambda i: (i, 0)),
                  pl.BlockSpec((NSA_WIDTH, D_MODEL), const),
                  pl.BlockSpec((RWKV_WIDTH, D_MODEL), const),
                  pl.BlockSpec((D_MODEL, D_MODEL), const),
                  pl.BlockSpec((1, D_MODEL), const)],
        out_specs=[pl.BlockSpec((tm, D_MODEL), lambda i: (i, 0)),
                   pl.BlockSpec((tm, D_MODEL), lambda i: (i, 0))],
        out_shape=[jax.ShapeDtypeStruct((m, D_MODEL), f32),
                   jax.ShapeDtypeStruct((m, D_MODEL), bf16)],
        compiler_params=_cparams(("arbitrary",)),
        name="merge_project",
    )(o_nsa, o_rw, p_all, p_all, x2d, pn, pr, wo, g2.reshape(1, D_MODEL))


def _mlp_body(hn_ref, h_ref, w1_ref, w2_ref, fg_ref, y_ref, acc_ref):
    f = pl.program_id(1)
    u = jnp.square(jnp.maximum(_dot(hn_ref[...], w1_ref[...]), 0.0)).astype(bf16)
    contrib = _dot(u, w2_ref[...])

    @pl.when(f == 0)
    def _():
        acc_ref[...] = contrib

    @pl.when(f > 0)
    def _():
        acc_ref[...] += contrib

    @pl.when(f == pl.num_programs(1) - 1)
    def _():
        y_ref[...] = _rms(h_ref[...] + acc_ref[...], fg_ref[...])


def mlp_residual_norm(hn, h, w1, w2, fg, tm, tf=512):
    m = h.shape[0]
    return pl.pallas_call(
        _mlp_body,
        grid=(m // tm, D_FF // tf),
        in_specs=[pl.BlockSpec((tm, D_MODEL), lambda i, f: (i, 0)),
                  pl.BlockSpec((tm, D_MODEL), lambda i, f: (i, 0)),
                  pl.BlockSpec((D_MODEL, tf), lambda i, f: (0, f)),
                  pl.BlockSpec((tf, D_MODEL), lambda i, f: (f, 0)),
                  pl.BlockSpec((1, D_MODEL), lambda i, f: (0, 0))],
        out_specs=pl.BlockSpec((tm, D_MODEL), lambda i, f: (i, 0)),
        out_shape=jax.ShapeDtypeStruct((m, D_MODEL), f32),
        scratch_shapes=[pltpu.VMEM((tm, D_MODEL), f32)],
        compiler_params=_cparams(("arbitrary", "arbitrary")),
        name="mlp_residual_norm",
    )(hn, h, w1, w2, fg.reshape(1, D_MODEL))


def _rwkv_prep_body(*refs, tc, seq, has_prev):
    if has_prev:
        p_ref, halo_ref, prev_ref = refs[:3]
        refs = refs[3:]
    else:
        p_ref, halo_ref = refs[:2]
        prev_ref = None
        refs = refs[2:]
    (mu_ref, vec_ref, w2_ref, a2_ref, g2_ref, e_ref,
     r_out, d_out, k_out, v_out, kap_out, b_out, g_out, bonus_out, sh_ref) = refs
    i = pl.program_id(0)
    p = p_ref[...]
    sh_ref[0:SUBLANES, :] = halo_ref[...]
    sh_ref[SUBLANES:SUBLANES + tc, :] = p
    rolled = sh_ref[SUBLANES - 1:SUBLANES - 1 + tc, :]
    t_in_seq = (i * tc + lax.broadcasted_iota(jnp.int32, (tc, 1), 0)) % seq
    first = prev_ref[...] if has_prev else jnp.zeros_like(p)
    shifted = jnp.where(t_in_seq == 0, first, rolled)
    xm = p + (shifted - p) * mu_ref[...]
    r = xm[:, 0:RWKV_WIDTH]
    k = xm[:, RWKV_WIDTH:2 * RWKV_WIDTH]
    v = xm[:, 2 * RWKV_WIDTH:3 * RWKV_WIDTH]
    tail = xm[:, LORA0:RW_PAD]
    w0, a0, k_k, k_a, r_k = (vec_ref[n:n + 1, :] for n in range(5))
    w = -jax.nn.softplus(-(w0 + _dot(jnp.tanh(tail).astype(bf16), w2_ref[...]))) - 0.5
    a = jax.nn.sigmoid(a0 + _dot(tail.astype(bf16), a2_ref[...]))
    kk = k * k_k
    e = e_ref[...]
    kap = kk / jnp.maximum(jnp.sqrt(_split_dot(kk * kk, e)), 1e-12)
    kn = k * (1.0 + (a - 1.0) * k_a)
    r_out[...] = r
    d_out[...] = jnp.exp(-jnp.exp(w))
    k_out[...] = kn
    v_out[...] = v
    kap_out[...] = kap
    b_out[...] = kap * a
    g_out[...] = _dot(jax.nn.sigmoid(tail).astype(bf16), g2_ref[...])
    bonus_out[...] = _split_dot(r * kn * r_k, e) * v


def rwkv_prepare(p_all, prev_exp, mu, vecs, w2p, a2p, g2p, e, seq, tc):
    m = p_all.shape[0]
    has_prev = prev_exp is not None
    row = lambda i: (i, 0)
    const = lambda i: (0, 0)
    halo = lambda i: (jnp.maximum(i * (tc // SUBLANES) - 1, 0), 0)
    in_specs = [pl.BlockSpec((tc, RW_PAD), row), pl.BlockSpec((SUBLANES, RW_PAD), halo)]
    args = [p_all, p_all]
    if has_prev:
        in_specs.append(pl.BlockSpec((tc, RW_PAD), row))
        args.append(prev_exp)
    in_specs += [pl.BlockSpec((1, RW_PAD), const), pl.BlockSpec((SUBLANES, RWKV_WIDTH), const),
                 pl.BlockSpec((LORA_PAD, RWKV_WIDTH), const), pl.BlockSpec((LORA_PAD, RWKV_WIDTH), const),
                 pl.BlockSpec((LORA_PAD, RWKV_WIDTH), const), pl.BlockSpec((RWKV_WIDTH, RWKV_WIDTH), const)]
    args += [mu, vecs, w2p, a2p, g2p, e]
    return pl.pallas_call(
        functools.partial(_rwkv_prep_body, tc=tc, seq=seq, has_prev=has_prev),
        grid=(m // tc,),
        in_specs=in_specs,
        out_specs=[pl.BlockSpec((tc, RWKV_WIDTH), row)] * 8,
        out_shape=[jax.ShapeDtypeStruct((m, RWKV_WIDTH), f32)] * 8,
        scratch_shapes=[pltpu.VMEM((tc + SUBLANES, RW_PAD), f32)],
        compiler_params=_cparams(("arbitrary",)),
        name="rwkv_prepare",
    )(*args)


def _rwkv_scan_body(r_ref, d_ref, k_ref, v_ref, kap_ref, b_ref, s0_ref, y_ref, st_ref, s_ref, *, tc, nb):
    c = pl.program_id(1)
    pairs = RWKV_HEADS // 2
    zero = jnp.zeros((HEAD_DIM, HEAD_DIM), f32)

    @pl.when(c == 0)
    def _():
        for bi in range(nb):
            for h in range(RWKV_HEADS):
                s0 = s0_ref[bi, h]
                s_ref[bi * RWKV_HEADS + h] = jnp.concatenate([s0, zero] if h % 2 == 0 else [zero, s0], axis=1)

    lane = lax.broadcasted_iota(jnp.int32, (HEAD_DIM, LANES), 1)
    sub = lax.broadcasted_iota(jnp.int32, (HEAD_DIM, LANES), 0)
    lane_row = lax.broadcasted_iota(jnp.int32, (1, LANES), 1)
    diag = [lane == sub + HEAD_DIM * par for par in range(2)]
    half = [((lane_row >= HEAD_DIM * par) & (lane_row < HEAD_DIM * (par + 1))).astype(f32) for par in range(2)]

    def group(bi, g, carry):
        base = pl.multiple_of(g * SUBLANES, SUBLANES)
        for pr in range(pairs):
            cols = slice(pr * LANES, (pr + 1) * LANES)
            r8, d8, k8, v8, kap8, b8 = (ref[bi, pl.ds(base, SUBLANES), cols]
                                        for ref in (r_ref, d_ref, k_ref, v_ref, kap_ref, b_ref))
            ys = [[], []]
            for j in range(SUBLANES):
                row = slice(j, j + 1)
                for par in range(2):
                    idx = bi * RWKV_HEADS + pr * 2 + par
                    s = s_ref[idx]
                    sa = -jnp.sum(s * kap8[row], axis=1, keepdims=True)
                    vcol = jnp.sum(jnp.where(diag[par], v8[row], 0.0), axis=1, keepdims=True)
                    sn = s * d8[row] + sa * (b8[row] * half[par]) + vcol * (k8[row] * half[par])
                    s_ref[idx] = sn
                    ys[par].append(_dot_nt(r8, sn)[row, :])
            y_ref[bi, pl.ds(base, SUBLANES), cols] = jnp.concatenate(
                [jnp.concatenate(ys[0], axis=0), jnp.concatenate(ys[1], axis=0)], axis=1)
        return carry

    for bi in range(nb):
        lax.fori_loop(0, tc // SUBLANES, functools.partial(group, bi), 0)

    @pl.when(c == pl.num_programs(1) - 1)
    def _():
        for bi in range(nb):
            for h in range(RWKV_HEADS):
                par = h % 2
                st_ref[bi, h] = s_ref[bi * RWKV_HEADS + h][:, par * HEAD_DIM:(par + 1) * HEAD_DIM]


def rwkv_scan(r, d, k, v, kap, b, s0, tc, nb=2):
    nbatch, seq = r.shape[:2]
    blk = pl.BlockSpec((nb, tc, RWKV_WIDTH), lambda g, c: (g, c, 0))
    st = pl.BlockSpec((nb, RWKV_HEADS, HEAD_DIM, HEAD_DIM), lambda g, c: (g, 0, 0, 0))
    return pl.pallas_call(
        functools.partial(_rwkv_scan_body, tc=tc, nb=nb),
        grid=(nbatch // nb, seq // tc),
        in_specs=[blk] * 6 + [st],
        out_specs=[blk, st],
        out_shape=[jax.ShapeDtypeStruct((nbatch, seq, RWKV_WIDTH), f32),
                   jax.ShapeDtypeStruct((nbatch, RWKV_HEADS, HEAD_DIM, HEAD_DIM), f32)],
        scratch_shapes=[pltpu.VMEM((nb * RWKV_HEADS, HEAD_DIM, LANES), f32)],
        compiler_params=_cparams(("arbitrary", "arbitrary")),
        name="rwkv_scan",
    )(r, d, k, v, kap, b, s0)


def _rwkv_post_body(y_ref, g_ref, bonus_ref, vec_ref, e_ref, o_ref):
    y = y_ref[...]
    e = e_ref[...]
    ln_g, ln_b = vec_ref[5:6, :], vec_ref[6:7, :]
    mu = _split_dot(y, e) * (1.0 / HEAD_DIM)
    yc = y - mu
    var = _split_dot(yc * yc, e) * (1.0 / HEAD_DIM)
    o_ref[...] = (yc * lax.rsqrt(var + GN_EPS) * ln_g + ln_b + bonus_ref[...]) * g_ref[...]


def rwkv_output(y, g, bonus, vecs, e, tc):
    m = y.shape[0]
    row = lambda i: (i, 0)
    const = lambda i: (0, 0)
    return pl.pallas_call(
        _rwkv_post_body,
        grid=(m // tc,),
        in_specs=[pl.BlockSpec((tc, RWKV_WIDTH), row)] * 3 +
                 [pl.BlockSpec((SUBLANES, RWKV_WIDTH), const), pl.BlockSpec((RWKV_WIDTH, RWKV_WIDTH), const)],
        out_specs=pl.BlockSpec((tc, RWKV_WIDTH), row),
        out_shape=jax.ShapeDtypeStruct((m, RWKV_WIDTH), f32),
        compiler_params=_cparams(("arbitrary",)),
        name="rwkv_output",
    )(y, g, bonus, vecs, e)


def rwkv7(p_all, prev, s0, wts, nbatch, seq, tc_prep, tc_scan):
    prev_exp = None
    if prev is not None:
        prev_exp = jnp.repeat(jnp.pad(prev, ((0, 0), (0, RW_PAD - RWKV_PROJ))), seq, axis=0)
    r, d, k, v, kap, b, g, bonus = rwkv_prepare(
        p_all, prev_exp, wts['mu'], wts['vecs'], wts['w2p'], wts['a2p'], wts['g2p'], wts['e'], seq, tc_prep)
    sh = lambda z: z.reshape(nbatch, seq, RWKV_WIDTH)
    y, s_new = rwkv_scan(sh(r), sh(d), sh(k), sh(v), sh(kap), sh(b), s0, tc_scan)
    o = rwkv_output(y.reshape(nbatch * seq, RWKV_WIDTH), g, bonus, wts['vecs'], wts['e'], tc_prep)
    return o, s_new


def _compress_head(x_ref, nchunk, wcat_ref, pe_ref, w1f_ref, b1_ref, w2_ref, b2_ref, acc_ref):
    kv2 = 2 * HEAD_DIM
    pe_term = jnp.concatenate([_dot(pe_ref[k], w1f_ref[k])[0:1, :] for k in range(2)], axis=1)
    bias = pe_term + b1_ref[...]
    acc = jnp.zeros((nchunk, 4 * kv2), f32)
    half = CMP_STRIDE // 2
    for r in range(half):
        xr = jnp.concatenate([x_ref[pl.ds(r, nchunk, stride=CMP_STRIDE), :],
                              x_ref[pl.ds(r + half, nchunk, stride=CMP_STRIDE), :]], axis=1)
        acc = acc + _dot(xr.astype(bf16), wcat_ref[r])
    acc_ref[0:nchunk, :] = acc
    acc_ref[nchunk:nchunk + SUBLANES, :] = jnp.zeros((SUBLANES, 4 * kv2), f32)
    first = jnp.concatenate([acc_ref[0:nchunk, 0:kv2], acc_ref[0:nchunk, 2 * kv2:3 * kv2]], axis=1)
    second = jnp.concatenate([acc_ref[1:nchunk + 1, kv2:2 * kv2], acc_ref[1:nchunk + 1, 3 * kv2:4 * kv2]], axis=1)
    hid = jax.nn.gelu(first + second + bias)
    return _dot(hid.astype(bf16), w2_ref[...]) + b2_ref[...]


def _compress_prompt_body(x_ref, wcat_ref, pe_ref, w1f_ref, b1_ref, w2_ref, b2_ref, o_ref, acc_ref, *, nchunk):
    o_ref[...] = _compress_head(x_ref, nchunk, wcat_ref, pe_ref, w1f_ref, b1_ref, w2_ref, b2_ref, acc_ref)


def _compress_weights(lp):
    w1 = lp['cmp_w1']
    z = jnp.zeros((CMP_STRIDE, HEAD_DIM, CMP_HIDDEN), f32)
    key_rows = jnp.concatenate([w1[0, :CMP_STRIDE], w1[0, CMP_STRIDE:], z, z], axis=-1)
    val_rows = jnp.concatenate([z, z, w1[1, :CMP_STRIDE], w1[1, CMP_STRIDE:]], axis=-1)
    wcat = jnp.concatenate([key_rows, val_rows], axis=1)
    wcat = jnp.concatenate([wcat[:CMP_STRIDE // 2], wcat[CMP_STRIDE // 2:]], axis=1).astype(bf16)
    pe = jnp.broadcast_to(lp['cmp_pe'].reshape(2, 1, CMP_BLOCK * HEAD_DIM), (2, SUBLANES, CMP_BLOCK * HEAD_DIM))
    w2 = lp['cmp_w2']
    zz = jnp.zeros((CMP_HIDDEN, HEAD_DIM), f32)
    w2bd = jnp.concatenate([jnp.concatenate([w2[0], zz], axis=1), jnp.concatenate([zz, w2[1]], axis=1)], axis=0)
    return {'wcat': wcat, 'pe': pe.astype(bf16),
            'w1f': w1.reshape(2, CMP_BLOCK * HEAD_DIM, CMP_HIDDEN).astype(bf16),
            'b1': lp['cmp_b1'].reshape(1, 2 * CMP_HIDDEN), 'w2bd': w2bd.astype(bf16),
            'b2': lp['cmp_b2'].reshape(1, 2 * HEAD_DIM)}


def _cmp_weight_specs(nidx):
    c2 = lambda *a: (0, 0)
    c3 = lambda *a: (0, 0, 0)
    return [pl.BlockSpec((CMP_STRIDE // 2, 4 * HEAD_DIM, 4 * CMP_HIDDEN), c3),
            pl.BlockSpec((2, SUBLANES, CMP_BLOCK * HEAD_DIM), c3),
            pl.BlockSpec((2, CMP_BLOCK * HEAD_DIM, CMP_HIDDEN), c3),
            pl.BlockSpec((1, 2 * CMP_HIDDEN), c2),
            pl.BlockSpec((2 * CMP_HIDDEN, 2 * HEAD_DIM), c2),
            pl.BlockSpec((1, 2 * HEAD_DIM), c2)]


def compress_prompt(p_all, cw, nbatch, seq):
    nchunk = seq // CMP_STRIDE
    kv2 = 2 * HEAD_DIM
    return pl.pallas_call(
        functools.partial(_compress_prompt_body, nchunk=nchunk),
        grid=(nbatch, NSA_KV_HEADS),
        in_specs=[pl.BlockSpec((seq, kv2), lambda b, h: (b, KVC0 // kv2 + h))] + _cmp_weight_specs(2),
        out_specs=pl.BlockSpec((nchunk, kv2), lambda b, h: (b, h)),
        out_shape=jax.ShapeDtypeStruct((nbatch * nchunk, KV_WIDTH), f32),
        scratch_shapes=[pltpu.VMEM((nchunk + SUBLANES, 4 * CMP_HIDDEN), f32)],
        compiler_params=_cparams(("arbitrary", "arbitrary")),
        name="compress_prompt",
    )(p_all, cw['wcat'], cw['pe'], cw['w1f'], cw['b1'], cw['w2bd'], cw['b2'])


KEY_TILE = 256


def _masked_softmax(s, mask):
    s = jnp.where(mask, s, NEG_INF)
    m = jnp.max(s, axis=-1, keepdims=True)
    e = jnp.where(mask, jnp.exp(s - m), 0.0)
    return e / jnp.maximum(jnp.sum(e, axis=-1, keepdims=True), 1e-30)


def _nsa_prompt_body(q_ref, kc_ref, ks_ref, kw_ref, g_ref, wsel_ref, eexp_ref, o_ref, *, nsel, ncmp):
    i = pl.program_id(2)
    qb, grp = Q_BLOCK, NSA_GROUP
    rows = grp * qb
    q = q_ref[...]
    qs = jnp.concatenate([q[:, g * HEAD_DIM:(g + 1) * HEAD_DIM] for g in range(grp)], axis=0) * (HEAD_DIM ** -0.5)
    qp = jnp.concatenate([qs, jnp.zeros_like(qs)], axis=1).astype(bf16)
    qpos = i * qb + lax.broadcasted_iota(jnp.int32, (rows, 1), 0) % qb

    kc = kc_ref[...].astype(bf16)
    s_c = _dot_nt(qp, kc)
    cend = lax.broadcasted_iota(jnp.int32, (1, ncmp), 1) * CMP_STRIDE + (CMP_BLOCK - 1)
    p_c = _masked_softmax(s_c, cend <= qpos)
    o_c = _dot(p_c.astype(bf16), kc)
    imp = p_c[0:qb] + p_c[qb:2 * qb] + p_c[2 * qb:3 * qb] + p_c[3 * qb:4 * qb]
    imp_hi = imp.astype(bf16)
    imp_lo = (imp - imp_hi.astype(f32)).astype(bf16)
    wsel = wsel_ref[...]
    p_slc = _dot_nt(wsel, imp_hi) + _dot_nt(wsel, imp_lo)

    blk = lax.broadcasted_iota(jnp.int32, (nsel, qb), 0)
    qpos_t = i * qb + lax.broadcasted_iota(jnp.int32, (nsel, qb), 1)
    cur = qpos_t // SEL_BLOCK
    forced = (blk == 0) | (blk == cur) | (blk == cur - 1)
    score = jnp.where(forced, FORCE_SCORE, p_slc)
    score = jnp.where(blk * SEL_BLOCK <= qpos_t, score, -1.0)
    rank = jnp.zeros((nsel, qb), f32)
    for jp in range(nsel):
        row = score[jp:jp + 1, :]
        ahead = (row > score) | ((row == score) & (blk > jp))
        rank = rank + jnp.where(ahead, 1.0, 0.0)
    sel_t = jnp.where((rank < float(min(SEL_TOPK, nsel))) & (score >= 0.0), 1.0, 0.0)
    sel = sel_t.T.astype(bf16)

    qpos_q = qpos[0:qb]
    ones_keys = lax.broadcasted_iota(jnp.int32, (1, 2 * HEAD_DIM), 1) < HEAD_DIM

    def attend(ref, kt, carry, mask_fn):
        k0 = pl.multiple_of(kt * KEY_TILE, KEY_TILE)
        kt_tile = ref[pl.ds(k0, KEY_TILE), :].astype(bf16)
        ones_v = jnp.where(ones_keys, jnp.ones_like(kt_tile), kt_tile)
        mask = mask_fn(kt, k0 + lax.broadcasted_iota(jnp.int32, (1, KEY_TILE), 1))
        new = []
        for g in range(grp):
            m, acc = carry[g]
            s = jnp.where(mask, _dot_nt(qp[g * qb:(g + 1) * qb], kt_tile), NEG_INF)
            m_new = jnp.maximum(m, jnp.max(s, axis=-1, keepdims=True))
            e = jnp.where(mask, jnp.exp(s - m_new), 0.0)
            new.append((m_new, jnp.exp(m - m_new) * acc + _dot(e.astype(bf16), ones_v)))
        return tuple(new)

    def sel_mask(kt, kpos):
        return (_dot(sel, eexp_ref[kt]) > 0.5) & (kpos <= qpos_q)

    def win_mask(kt, kpos):
        diff = qpos_q - kpos
        return (diff >= 0) & (diff <= WINDOW)

    per_tile = KEY_TILE // qb
    init = tuple((jnp.full((qb, 1), NEG_INF, f32), jnp.zeros((qb, 2 * HEAD_DIM), f32)) for _ in range(grp))
    res_s = lax.fori_loop(0, i // per_tile + 1, lambda kt, c: attend(ks_ref, kt, c, sel_mask), init)
    res_w = lax.fori_loop(jnp.maximum(i - WINDOW // qb, 0) // per_tile, i // per_tile + 1,
                          lambda kt, c: attend(kw_ref, kt, c, win_mask), init)
    finish = lambda res: jnp.concatenate([acc / jnp.maximum(acc[:, 0:1], 1e-30) for _, acc in res], axis=0)
    o_s, o_w = finish(res_s), finish(res_w)

    gates = g_ref[0, 0]
    gate = lambda br: jnp.concatenate([gates[:, 3 * g + br:3 * g + br + 1] for g in range(grp)], axis=0)
    out = gate(0) * o_c + gate(1) * o_s + gate(2) * o_w
    o_ref[...] = jnp.concatenate([out[g * qb:(g + 1) * qb, HEAD_DIM:] for g in range(grp)], axis=1)


def nsa_prompt(p_all, kv_cmp, gates, nbatch, seq):
    nqb, nsel, ncmp = seq // Q_BLOCK, seq // SEL_BLOCK, seq // CMP_STRIDE
    kv2 = 2 * HEAD_DIM
    cidx = jnp.arange(ncmp)[None, :] - (SEL_BLOCK // CMP_STRIDE) * jnp.arange(nsel)[:, None]
    mult = jnp.array([1, 2, 2, 2, 1], f32)
    wsel = jnp.where((cidx >= 0) & (cidx <= 4), mult[jnp.clip(cidx, 0, 4)], 0.0).astype(bf16)
    ntile = seq // KEY_TILE
    key_blk = (jnp.arange(ntile)[:, None, None] * KEY_TILE + jnp.arange(KEY_TILE)[None, None, :]) // SEL_BLOCK
    eexp = (key_blk == jnp.arange(nsel)[None, :, None]).astype(bf16)
    return pl.pallas_call(
        functools.partial(_nsa_prompt_body, nsel=nsel, ncmp=ncmp),
        grid=(nbatch, NSA_KV_HEADS, nqb),
        in_specs=[pl.BlockSpec((Q_BLOCK, NSA_GROUP * HEAD_DIM),
                               lambda b, h, i: (b * nqb + i, Q0 // (NSA_GROUP * HEAD_DIM) + h)),
                  pl.BlockSpec((ncmp, kv2), lambda b, h, i: (b, h)),
                  pl.BlockSpec((seq, kv2), lambda b, h, i: (b, KVS0 // kv2 + h)),
                  pl.BlockSpec((seq, kv2), lambda b, h, i: (b, KVW0 // kv2 + h)),
                  pl.BlockSpec((1, 1, Q_BLOCK, 16), lambda b, h, i: (b, h, i, 0)),
                  pl.BlockSpec((nsel, ncmp), lambda b, h, i: (0, 0)),
                  pl.BlockSpec((ntile, nsel, KEY_TILE), lambda b, h, i: (0, 0, 0))],
        out_specs=pl.BlockSpec((Q_BLOCK, NSA_GROUP * HEAD_DIM), lambda b, h, i: (b * nqb + i, h)),
        out_shape=jax.ShapeDtypeStruct((nbatch * seq, NSA_WIDTH), f32),
        compiler_params=_cparams(("arbitrary", "arbitrary", "arbitrary")),
        name="nsa_prompt",
    )(p_all, kv_cmp, p_all, p_all, gates, wsel, eexp)


def _branch_gates(p_all, nbatch, seq):
    g = p_all[:, GN0:GN0 + 3 * NSA_HEADS].reshape(nbatch, seq, NSA_KV_HEADS, 3 * NSA_GROUP)
    return jnp.pad(jnp.transpose(g, (0, 2, 1, 3)), ((0, 0), (0, 0), (0, 0), (0, 16 - 3 * NSA_GROUP)))


PAGES_PER_STEP = 16
CMP_HALVES = 2
TAIL_ROWS = CMP_STRIDE


def _compress_sample_body(pt_ref, *refs, nchunk, half_rows):
    npg = PAGES_PER_STEP
    pages, (nxt_ref, newc_ref) = refs[:npg], refs[npg:npg + 2]
    wcat_ref, pe_ref, w1f_ref, b1_ref, w2_ref, b2_ref, o_ref, xs_ref, acc_ref = refs[npg + 2:]
    half, s = pl.program_id(1), pl.program_id(2)
    kv2 = 2 * HEAD_DIM
    for k in range(npg):
        base = pl.multiple_of((s * npg + k) * PAGE_SIZE, PAGE_SIZE)
        for h in range(NSA_KV_HEADS):
            xs_ref[h, pl.ds(base, PAGE_SIZE), :] = pages[k][0, :, h * kv2:(h + 1) * kv2]

    @pl.when(s == pl.num_programs(2) - 1)
    def _():
        nblk = half_rows // CMP_STRIDE
        pad_rows = xs_ref.shape[1] - half_rows - TAIL_ROWS
        for h in range(NSA_KV_HEADS):
            lanes = slice(h * kv2, (h + 1) * kv2)
            tail = jnp.where(half == CMP_HALVES - 1, newc_ref[0, :, lanes], nxt_ref[0, :, lanes])
            xs_ref[h, half_rows:half_rows + TAIL_ROWS, :] = tail
            xs_ref[h, half_rows + TAIL_ROWS:, :] = jnp.zeros((pad_rows, kv2), f32)
            out = _compress_head(xs_ref.at[h], nchunk, wcat_ref, pe_ref, w1f_ref, b1_ref, w2_ref, b2_ref, acc_ref)
            o_ref[0, :, lanes] = out[0:nblk]


def compress_sample(cache, page_table, newc, cw):
    nbatch, npages = page_table.shape
    npg = PAGES_PER_STEP
    half_pages = npages // CMP_HALVES
    half_rows = half_pages * PAGE_SIZE
    nsteps = half_pages // npg
    nchunk = half_rows // CMP_STRIDE + SUBLANES
    page_spec = lambda k: pl.BlockSpec(
        (1, PAGE_SIZE, KV_WIDTH), lambda b, hf, s, pt: (pt[b, hf * half_pages + s * npg + k], 0, 0))
    nxt_spec = pl.BlockSpec(
        (1, TAIL_ROWS, KV_WIDTH), lambda b, hf, s, pt: (pt[b, jnp.minimum((hf + 1) * half_pages, npages - 1)], 0, 0))
    return pl.pallas_call(
        functools.partial(_compress_sample_body, nchunk=nchunk, half_rows=half_rows),
        grid_spec=pltpu.PrefetchScalarGridSpec(
            num_scalar_prefetch=1,
            grid=(nbatch, CMP_HALVES, nsteps),
            in_specs=[page_spec(k) for k in range(npg)] + [nxt_spec] +
                     [pl.BlockSpec((1, TAIL_ROWS, KV_WIDTH), lambda b, hf, s, pt: (b, 0, 0))] + _cmp_weight_specs(4),
            out_specs=pl.BlockSpec((1, half_rows // CMP_STRIDE, KV_WIDTH), lambda b, hf, s, pt: (b, hf, 0)),
            scratch_shapes=[pltpu.VMEM((NSA_KV_HEADS, nchunk * CMP_STRIDE, 2 * HEAD_DIM), f32),
                            pltpu.VMEM((nchunk + SUBLANES, 4 * CMP_HIDDEN), f32)]),
        out_shape=jax.ShapeDtypeStruct((nbatch, npages * PAGE_SIZE // CMP_STRIDE, KV_WIDTH), f32),
        compiler_params=_cparams(("arbitrary", "arbitrary", "arbitrary")),
        name="compress_sample",
    )(page_table, *([cache] * (npg + 1)), newc, cw['wcat'], cw['pe'], cw['w1f'], cw['b1'], cw['w2bd'], cw['b2'])


def _split_dot_r(e, x):
    hi = x.astype(bf16)
    lo = (x - hi.astype(f32)).astype(bf16)
    return _dot(e, hi) + _dot(e, lo)


def _softmax_rows(s, mask):
    s = jnp.where(mask, s, NEG_INF)
    m = jnp.max(s, axis=0, keepdims=True)
    e = jnp.where(mask, jnp.exp(s - m), 0.0)
    return e / jnp.maximum(jnp.sum(e, axis=0, keepdims=True), 1e-30)


def _row_to_col(row):
    return jnp.broadcast_to(row, (SUBLANES, LANES)).T[:, 0:1]


def _nsa_sample_body(pt_ref, *refs, tnew, nsel, nselp):
    npg = PAGES_PER_STEP
    pages = refs[:npg]
    (qb_ref, kc_ref, win_ref, neww_ref, news_ref, g_ref, wsel_ref, gm_ref, o_ref,
     sel_ref, score_ref, st_ref, m_ref, l_ref, acc_ref, oc_ref, ow_ref) = refs[npg:]
    s = pl.program_id(1)
    qb = qb_ref[0]
    col = lax.broadcasted_iota(jnp.int32, (1, LANES), 1)
    tq = col % tnew
    qpos = PAST_LEN + tq
    pad_tail = lambda ref: jnp.concatenate(
        [ref[0], jnp.zeros((LANES - TAIL_ROWS, KV_WIDTH), f32)], axis=0).astype(bf16)

    @pl.when(s == 0)
    def _():
        kc = kc_ref[0].astype(bf16)
        ncmp = kc.shape[0]
        cend = lax.broadcasted_iota(jnp.int32, (ncmp, 1), 0) * CMP_STRIDE + (CMP_BLOCK - 1)
        p_c = _softmax_rows(_dot(kc, qb), cend <= qpos)
        oc_ref[...] = _dot(p_c.T.astype(bf16), kc)
        imp = _split_dot(p_c, gm_ref[...])
        p_slc = _split_dot_r(wsel_ref[...], imp)
        blk = lax.broadcasted_iota(jnp.int32, (nselp, LANES), 0)
        cur = qpos // SEL_BLOCK
        forced = (blk == 0) | (blk == cur) | (blk == cur - 1)
        score = jnp.where(forced, FORCE_SCORE, p_slc)
        score = jnp.where((blk * SEL_BLOCK <= qpos) & (blk < nsel), score, -1.0)
        score_ref[...] = score

        def rank_group(gi, rank):
            rows = score_ref[pl.ds(pl.multiple_of(gi * SUBLANES, SUBLANES), SUBLANES), :]
            for j in range(SUBLANES):
                row = rows[j:j + 1, :]
                ahead = (row > score) | ((row == score) & (blk > gi * SUBLANES + j))
                rank = rank + jnp.where(ahead, 1.0, 0.0)
            return rank

        rank = lax.fori_loop(0, nselp // SUBLANES, rank_group, jnp.zeros((nselp, LANES), f32))
        sel_ref[...] = jnp.where((rank < float(SEL_TOPK)) & (score >= 0.0), 1.0, 0.0)

        kw = jnp.concatenate([win_ref[0].astype(bf16), pad_tail(neww_ref)], axis=0)
        idx = lax.broadcasted_iota(jnp.int32, (kw.shape[0], 1), 0)
        wbuf = win_ref.shape[1]
        p_w = _softmax_rows(_dot(kw, qb), (idx <= wbuf + tq) & (idx >= wbuf + tq - WINDOW))
        ow_ref[...] = _dot(p_w.T.astype(bf16), kw)

        m_ref[...] = jnp.full(m_ref.shape, NEG_INF, f32)
        l_ref[...] = jnp.zeros(l_ref.shape, f32)
        acc_ref[...] = jnp.zeros(acc_ref.shape, f32)

    def fold(tiles):
        m_old = m_ref[0:1, :]
        m_new = m_old
        for sc, _ in tiles:
            m_new = jnp.maximum(m_new, jnp.max(sc, axis=0, keepdims=True))
        alpha = jnp.exp(m_old - m_new)
        lsum = alpha * l_ref[0:1, :]
        acc = _row_to_col(alpha) * acc_ref[...]
        for sc, vals in tiles:
            e = jnp.where(sc > 0.5 * NEG_INF, jnp.exp(sc - m_new), 0.0)
            lsum = lsum + jnp.sum(e, axis=0, keepdims=True)
            acc = acc + _dot(e.T.astype(bf16), vals)
        m_ref[...] = jnp.broadcast_to(m_new, m_ref.shape)
        l_ref[...] = jnp.broadcast_to(lsum, l_ref.shape)
        acc_ref[...] = acc

    key_hi = lax.broadcasted_iota(jnp.int32, (PAGE_SIZE, LANES), 0) >= SEL_BLOCK
    tiles = []
    for k in range(npg):
        pg = pages[k][0].astype(bf16)
        grp = sel_ref[pl.ds(pl.multiple_of(s * 2 * npg + (2 * k // SUBLANES) * SUBLANES, SUBLANES), SUBLANES), :]
        r0 = (2 * k) % SUBLANES
        mask = jnp.where(key_hi, grp[r0 + 1:r0 + 2, :], grp[r0:r0 + 1, :]) > 0.5
        st_ref[k * PAGE_SIZE:(k + 1) * PAGE_SIZE, :] = jnp.where(mask, _dot(pg, qb), NEG_INF)
        tiles.append((st_ref[k * PAGE_SIZE:(k + 1) * PAGE_SIZE, :], pg))
    fold(tiles)

    @pl.when(s == pl.num_programs(1) - 1)
    def _():
        ns = pad_tail(news_ref)
        kidx = lax.broadcasted_iota(jnp.int32, (LANES, 1), 0)
        nb = nsel - 1
        grp = sel_ref[(nb // SUBLANES) * SUBLANES:(nb // SUBLANES + 1) * SUBLANES, :]
        mask = (grp[nb % SUBLANES:nb % SUBLANES + 1, :] > 0.5) & (kidx <= tq)
        fold([(jnp.where(mask, _dot(ns, qb), NEG_INF), ns)])
        o_s = acc_ref[...] / jnp.maximum(_row_to_col(l_ref[0:1, :]), 1e-30)
        rows_per_head = LANES // NSA_KV_HEADS
        for h in range(NSA_KV_HEADS):
            rows = slice(h * rows_per_head, (h + 1) * rows_per_head)
            lanes = slice(h * 2 * HEAD_DIM, (h + 1) * 2 * HEAD_DIM)
            o_ref[0, rows, :] = (g_ref[0, 0, rows, :] * oc_ref[rows, lanes] + g_ref[0, 1, rows, :] * o_s[rows, lanes]
                                 + g_ref[0, 2, rows, :] * ow_ref[rows, lanes])


def nsa_sample(cache_sel, page_table, qblk, kv_cmp, win_buf, neww, news, gates, tnew):
    nbatch, npages = page_table.shape
    npg = PAGES_PER_STEP
    ncmp = kv_cmp.shape[1]
    nsel = npages * (PAGE_SIZE // SEL_BLOCK) + 1
    nselp = -(-nsel // SUBLANES) * SUBLANES
    cidx = jnp.arange(ncmp)[None, :] - (SEL_BLOCK // CMP_STRIDE) * jnp.arange(nselp)[:, None]
    mult = jnp.array([1, 2, 2, 2, 1], f32)
    wsel = jnp.where((cidx >= 0) & (cidx <= 4), mult[jnp.clip(cidx, 0, 4)], 0.0).astype(bf16)
    c = jnp.arange(LANES)
    same = (c[:, None] // (NSA_GROUP * tnew) == c[None, :] // (NSA_GROUP * tnew)) & (c[:, None] % tnew == c[None, :] % tnew)
    gm = same.astype(bf16)
    b3 = lambda b, s, pt: (b, 0, 0)
    c2 = lambda b, s, pt: (0, 0)
    page_spec = lambda k: pl.BlockSpec((1, PAGE_SIZE, KV_WIDTH), lambda b, s, pt: (pt[b, s * npg + k], 0, 0))
    wbuf = win_buf.shape[1]
    return pl.pallas_call(
        functools.partial(_nsa_sample_body, tnew=tnew, nsel=nsel, nselp=nselp),
        grid_spec=pltpu.PrefetchScalarGridSpec(
            num_scalar_prefetch=1,
            grid=(nbatch, npages // npg),
            in_specs=[page_spec(k) for k in range(npg)] + [
                pl.BlockSpec((1, KV_WIDTH, LANES), b3),
                pl.BlockSpec((1, ncmp, KV_WIDTH), b3),
                pl.BlockSpec((1, wbuf, KV_WIDTH), b3),
                pl.BlockSpec((1, TAIL_ROWS, KV_WIDTH), b3),
                pl.BlockSpec((1, TAIL_ROWS, KV_WIDTH), b3),
                pl.BlockSpec((1, 3, LANES, LANES), lambda b, s, pt: (b, 0, 0, 0)),
                pl.BlockSpec((nselp, ncmp), c2),
                pl.BlockSpec((LANES, LANES), c2)],
            out_specs=pl.BlockSpec((1, LANES, LANES), b3),
            scratch_shapes=[pltpu.VMEM((nselp, LANES), f32), pltpu.VMEM((nselp, LANES), f32),
                            pltpu.VMEM((npg * PAGE_SIZE, LANES), f32),
                            pltpu.VMEM((SUBLANES, LANES), f32), pltpu.VMEM((SUBLANES, LANES), f32),
                            pltpu.VMEM((LANES, KV_WIDTH), f32), pltpu.VMEM((LANES, KV_WIDTH), f32),
                            pltpu.VMEM((LANES, KV_WIDTH), f32)]),
        out_shape=jax.ShapeDtypeStruct((nbatch, LANES, LANES), f32),
        compiler_params=_cparams(("arbitrary", "arbitrary")),
        name="nsa_sample",
    )(page_table, *([cache_sel] * npg), qblk, kv_cmp, win_buf, neww, news, gates, wsel, gm)


def _sample_nsa_inputs(p_all, b, t):
    q = p_all[:, Q0:Q0 + NSA_WIDTH].reshape(b, t, NSA_KV_HEADS, NSA_GROUP, HEAD_DIM) * (HEAD_DIM ** -0.5)
    qt = jnp.transpose(q, (0, 2, 4, 3, 1)).reshape(b, NSA_KV_HEADS, HEAD_DIM, NSA_GROUP * t)
    qt = jnp.pad(qt, ((0, 0), (0, 0), (0, HEAD_DIM), (0, 0)))
    eye = jnp.eye(NSA_KV_HEADS, dtype=f32)
    qblk = (qt[:, :, :, None, :] * eye[None, :, None, :, None]).reshape(b, KV_WIDTH, LANES).astype(bf16)
    g = p_all[:, GN0:GN0 + 3 * NSA_HEADS].reshape(b, t, NSA_KV_HEADS, NSA_GROUP, 3)
    g = jnp.transpose(g, (0, 4, 2, 3, 1)).reshape(b, 3, LANES)
    gates = jnp.broadcast_to(g[..., None], (b, 3, LANES, LANES))
    tail = lambda c0: jnp.pad(p_all[:, c0:c0 + KV_WIDTH].reshape(b, t, KV_WIDTH), ((0, 0), (0, TAIL_ROWS - t), (0, 0)))
    return qblk, gates, tail(KVC0), tail(KVS0), tail(KVW0)


def _prepare_weights(lp):
    w_in = lp['w_in']
    nsa_end = NSA_WIDTH + 3 * KV_WIDTH
    w_all = jnp.concatenate([
        w_in[:, NSA_PROJ:NSA_PROJ + RWKV_PROJ], w_in[:, nsa_end:NSA_PROJ],
        jnp.zeros((D_MODEL, RW_PAD - RWKV_PROJ - 3 * NSA_HEADS), w_in.dtype),
        w_in[:, :nsa_end], w_in[:, NSA_PROJ + RWKV_PROJ:]], axis=1).astype(bf16)
    pad_rows = lambda w, r0: jnp.zeros((LORA_PAD, RWKV_WIDTH), f32).at[r0:r0 + w.shape[0]].set(w).astype(bf16)
    head = jnp.arange(RWKV_WIDTH) // HEAD_DIM
    vecs = jnp.stack([lp['rwkv_w0'], lp['rwkv_a0'], lp['rwkv_k_k'], lp['rwkv_k_a'],
                      lp['rwkv_r_k'].reshape(RWKV_WIDTH), lp['rwkv_ln_g'], lp['rwkv_ln_b'],
                      jnp.zeros((RWKV_WIDTH,), f32)])
    return {
        'w_all': w_all,
        'mu': jnp.pad(lp['rwkv_mu'], (0, RW_PAD - RWKV_PROJ)).reshape(1, RW_PAD),
        'vecs': vecs,
        'w2p': pad_rows(lp['rwkv_w2'], 0),
        'a2p': pad_rows(lp['rwkv_a2'], DECAY_LORA),
        'g2p': pad_rows(lp['rwkv_g2'], DECAY_LORA + AAA_LORA),
        'e': (head[:, None] == head[None, :]).astype(bf16),
        'p_nsa': lp['p_nsa'].astype(bf16), 'p_rwkv': lp['p_rwkv'].astype(bf16),
        'w_out': lp['w_out'].astype(bf16),
        'mlp_w1': lp['mlp_w1'].astype(bf16), 'mlp_w2': lp['mlp_w2'].astype(bf16),
    }


def _group_forward(x, pos_rows, prev, s0, lp, wts, final_g, tm, tc_prep, tc_scan, o_nsa_fn):
    nbatch, seq = x.shape[:2]
    x2d = x.reshape(nbatch * seq, D_MODEL)
    tq, tkv = _rot_tables(pos_rows)
    p_all = input_projection(x2d, lp['norm1_g'], wts['w_all'], tq, tkv, tm)
    o_rw, s_new = rwkv7(p_all, prev, s0, wts, nbatch, seq, tc_prep, tc_scan)
    o_nsa = o_nsa_fn(p_all)
    h, hn = merge_project(o_nsa, o_rw, p_all, x2d, wts['p_nsa'], wts['p_rwkv'], wts['w_out'], lp['norm2_g'],
                          min(tm, 256))
    y = mlp_residual_norm(hn, h, wts['mlp_w1'], wts['mlp_w2'], final_g, tm)
    return y.reshape(nbatch, seq, D_MODEL), p_all, s_new


def _kv_rows(p_all, col0, nbatch, seq):
    return p_all[:, col0:col0 + KV_WIDTH].reshape(nbatch, seq, NSA_KV_HEADS, 2, HEAD_DIM)


def sample_nsa_attention(p_all, cache_cmp, cache_sel, win_buf, page_table, cw, b, t):
    qblk, gates, newc, news, neww = _sample_nsa_inputs(p_all, b, t)
    pool = cache_cmp.shape[0]
    kv_cmp = compress_sample(cache_cmp.reshape(pool, PAGE_SIZE, KV_WIDTH), page_table, newc, cw)
    o = nsa_sample(cache_sel.reshape(pool, PAGE_SIZE, KV_WIDTH), page_table, qblk, kv_cmp,
                   win_buf.reshape(b, win_buf.shape[1], KV_WIDTH), neww, news, gates, t)
    o = o[..., HEAD_DIM:].reshape(b, NSA_KV_HEADS, NSA_GROUP, t, HEAD_DIM)
    return jnp.transpose(o, (0, 3, 1, 2, 4)).reshape(b * t, NSA_WIDTH)


def kernel(x_prompt, x_sample, cache_cmp_kv, cache_sel_kv, state_nsa_win, state_rwkv, state_rwkv_shift,
           page_table, norm1_g, w_in, cmp_pe, cmp_w1, cmp_b1, cmp_w2, cmp_b2, rwkv_mu, rwkv_w0, rwkv_w2,
           rwkv_a0, rwkv_a2, rwkv_g2, rwkv_k_k, rwkv_k_a, rwkv_r_k, rwkv_ln_g, rwkv_ln_b, p_nsa, p_rwkv,
           w_out, norm2_g, mlp_w1, mlp_w2, final_g):
    l = 0
    lp = {'norm1_g': norm1_g[l], 'w_in': w_in[l], 'cmp_pe': cmp_pe[l], 'cmp_w1': cmp_w1[l],
          'cmp_b1': cmp_b1[l], 'cmp_w2': cmp_w2[l], 'cmp_b2': cmp_b2[l], 'rwkv_mu': rwkv_mu[l],
          'rwkv_w0': rwkv_w0[l], 'rwkv_w2': rwkv_w2[l], 'rwkv_a0': rwkv_a0[l], 'rwkv_a2': rwkv_a2[l],
          'rwkv_g2': rwkv_g2[l], 'rwkv_k_k': rwkv_k_k[l], 'rwkv_k_a': rwkv_k_a[l], 'rwkv_r_k': rwkv_r_k[l],
          'rwkv_ln_g': rwkv_ln_g[l], 'rwkv_ln_b': rwkv_ln_b[l], 'p_nsa': p_nsa[l], 'p_rwkv': p_rwkv[l],
          'w_out': w_out[l], 'norm2_g': norm2_g[l], 'mlp_w1': mlp_w1[l], 'mlp_w2': mlp_w2[l]}
    wts = _prepare_weights(lp)
    bp, tp = x_prompt.shape[:2]
    bs, ts = x_sample.shape[:2]
    cw = _compress_weights(lp)
    zero_state = jnp.zeros((bp, RWKV_HEADS, HEAD_DIM, HEAD_DIM), f32)

    def prompt_nsa(p_all):
        return nsa_prompt(p_all, compress_prompt(p_all, cw, bp, tp), _branch_gates(p_all, bp, tp), bp, tp)

    yp, pp, sp = _group_forward(x_prompt, jnp.arange(tp, dtype=jnp.int32), None, zero_state, lp, wts, final_g,
                                512, 256, 128, prompt_nsa)

    def sample_nsa(p_all):
        return sample_nsa_attention(p_all, cache_cmp_kv[l], cache_sel_kv[l], state_nsa_win[l], page_table, cw, bs, ts)

    pos_s = PAST_LEN + jnp.arange(bs * ts, dtype=jnp.int32) % ts
    ys, ps, ss = _group_forward(x_sample, pos_s, state_rwkv_shift[l], state_rwkv[l], lp, wts, final_g,
                                256, 256, ts, sample_nsa)
    win_new = [jnp.concatenate([state_nsa_win[l], _kv_rows(ps, KVW0, bs, ts)], axis=1)[:, ts:]]
    wlen = min(WINDOW, tp)
    shift = lambda p_all, nb, t: p_all.reshape(nb, t, P_WIDTH)[:, -1, RW0:RW0 + RWKV_PROJ]
    return (yp, ys,
            _kv_rows(pp, KVC0, bp, tp)[None], _kv_rows(ps, KVC0, bs, ts)[None],
            _kv_rows(pp, KVS0, bp, tp)[None], _kv_rows(ps, KVS0, bs, ts)[None],
            _kv_rows(pp, KVW0, bp, tp)[None, :, tp - wlen:], win_new[0][None],
            sp[None], ss[None],
            shift(pp, bp, tp)[None], shift(ps, bs, ts)[None])
```

```python
import functools

import jax
import jax.numpy as jnp
from jax import lax
from jax.experimental import pallas as pl
from jax.experimental.pallas import tpu as pltpu

D_MODEL = 2048
DEPTH = 1
PAST_LEN = 16384
PAGE_SIZE = 128

HEAD_DIM = 64
NSA_HEADS = D_MODEL // (2 * HEAD_DIM)
NSA_KV_HEADS = NSA_HEADS // 4
NSA_GROUP = NSA_HEADS // NSA_KV_HEADS
NSA_WIDTH = NSA_HEADS * HEAD_DIM
KV_WIDTH = NSA_KV_HEADS * 2 * HEAD_DIM
CMP_BLOCK = 32
CMP_STRIDE = 16
CMP_HIDDEN = 2 * HEAD_DIM
SEL_BLOCK = 64
SEL_TOPK = 16
WINDOW = 512
Q_BLOCK = 128
ROT_DIM = HEAD_DIM // 4
ROPE_THETA = 500000.0
RWKV_HEADS = D_MODEL // (2 * HEAD_DIM)
RWKV_WIDTH = RWKV_HEADS * HEAD_DIM
DECAY_LORA = max(32, int(round(1.8 * D_MODEL ** 0.5 / 32)) * 32)
AAA_LORA = DECAY_LORA
GATE_LORA = max(32, int(round(0.6 * D_MODEL ** 0.8 / 32)) * 32)
RWKV_PROJ = 3 * RWKV_WIDTH + DECAY_LORA + AAA_LORA + GATE_LORA
NSA_PROJ = NSA_WIDTH + 3 * KV_WIDTH + 3 * NSA_HEADS
N_IN = NSA_PROJ + RWKV_PROJ + 2 * D_MODEL
D_FF = 4 * D_MODEL
RMS_EPS = 1e-6
GN_EPS = HEAD_DIM * 1e-5
NEG_INF = -1e30
FORCE_SCORE = 1e6

LANES = 128
SUBLANES = 8
VMEM_LIMIT_BYTES = 56 * 1024 * 1024

RW0 = 0
LORA0 = 3 * RWKV_WIDTH
GN0 = RWKV_PROJ
RW_PAD = 3584
LORA_PAD = RW_PAD - LORA0
Q0 = RW_PAD
KVC0 = Q0 + NSA_WIDTH
KVS0 = KVC0 + KV_WIDTH
KVW0 = KVS0 + KV_WIDTH
GM0 = KVW0 + KV_WIDTH
P_WIDTH = GM0 + 2 * D_MODEL
PROJ_TN = 512

f32 = jnp.float32
bf16 = jnp.bfloat16


def _cparams(sem):
    return pltpu.CompilerParams(dimension_semantics=sem, vmem_limit_bytes=VMEM_LIMIT_BYTES)


def _rms(x, g):
    return x * lax.rsqrt(jnp.mean(x * x, axis=-1, keepdims=True) + RMS_EPS) * g


def _dot(a, b):
    return jnp.dot(a, b, preferred_element_type=f32)


def _dot_nt(a, b):
    return lax.dot_general(a, b, (((1,), (1,)), ((), ())), preferred_element_type=f32)


def _split_dot(x, e):
    hi = x.astype(bf16)
    lo = (x - hi.astype(f32)).astype(bf16)
    return _dot(hi, e) + _dot(lo, e)


def _rot_store(acc, tab_ref, o_ref):
    c, s1, s2 = tab_ref[0], tab_ref[1], tab_ref[2]
    for s in range(acc.shape[1] // LANES):
        x = acc[:, s * LANES:(s + 1) * LANES]
        o_ref[:, s * LANES:(s + 1) * LANES] = (
            x * c + pltpu.roll(x, LANES - ROT_DIM // 2, 1) * s1 + pltpu.roll(x, ROT_DIM // 2, 1) * s2)


def _proj_body(x_ref, g_ref, w_ref, tq_ref, tkv_ref, o_ref, xn_ref):
    j = pl.program_id(1)

    @pl.when(j == 0)
    def _():
        xn_ref[...] = _rms(x_ref[...], g_ref[...]).astype(bf16)

    acc = _dot(xn_ref[...], w_ref[...])
    gn_tile = GN0 // PROJ_TN

    @pl.when(j < gn_tile)
    def _():
        o_ref[...] = acc

    @pl.when(j == gn_tile)
    def _():
        lane = lax.broadcasted_iota(jnp.int32, acc.shape, 1)
        o_ref[...] = jnp.where(lane >= GN0 - gn_tile * PROJ_TN, jax.nn.sigmoid(acc), acc)

    @pl.when((j >= Q0 // PROJ_TN) & (j < KVC0 // PROJ_TN))
    def _():
        _rot_store(acc, tq_ref, o_ref)

    @pl.when((j >= KVC0 // PROJ_TN) & (j < GM0 // PROJ_TN))
    def _():
        _rot_store(acc, tkv_ref, o_ref)

    @pl.when(j >= GM0 // PROJ_TN)
    def _():
        o_ref[...] = jax.nn.sigmoid(acc)


def input_projection(x2d, g, w_all, tq, tkv, tm):
    m = x2d.shape[0]
    nt = tq.shape[1] // tm
    return pl.pallas_call(
        _proj_body,
        grid=(m // tm, P_WIDTH // PROJ_TN),
        in_specs=[pl.BlockSpec((tm, D_MODEL), lambda i, j: (i, 0)),
                  pl.BlockSpec((1, D_MODEL), lambda i, j: (0, 0)),
                  pl.BlockSpec((D_MODEL, PROJ_TN), lambda i, j: (0, j)),
                  pl.BlockSpec((3, tm, LANES), lambda i, j: (0, i % nt, 0)),
                  pl.BlockSpec((3, tm, LANES), lambda i, j: (0, i % nt, 0))],
        out_specs=pl.BlockSpec((tm, PROJ_TN), lambda i, j: (i, j)),
        out_shape=jax.ShapeDtypeStruct((m, P_WIDTH), f32),
        scratch_shapes=[pltpu.VMEM((tm, D_MODEL), bf16)],
        compiler_params=_cparams(("arbitrary", "arbitrary")),
        name="input_projection",
    )(x2d, g.reshape(1, D_MODEL), w_all, tq, tkv)


def _rot_tables(pos):
    half = ROT_DIM // 2
    freqs = jnp.power(jnp.float32(ROPE_THETA), -jnp.arange(half, dtype=f32) * 2.0 / ROT_DIM)
    ang = pos.astype(f32)[:, None] * freqs[None, :]
    cos, sin = jnp.cos(ang), jnp.sin(ang)
    lane = jnp.arange(LANES)

    def build(period):
        l = lane % period
        fi = l % half
        c = jnp.where(l < ROT_DIM, cos[:, fi], 1.0)
        s1 = jnp.where(l < half, -sin[:, fi], 0.0)
        s2 = jnp.where((l >= half) & (l < ROT_DIM), sin[:, fi], 0.0)
        return jnp.stack([c, s1, s2]).astype(f32)

    return build(HEAD_DIM), build(2 * HEAD_DIM)


def _merge_body(on_ref, or_ref, ga_ref, gb_ref, x_ref, pn_ref, pr_ref, wo_ref, g2_ref, h_ref, hn_ref):
    a = _dot(on_ref[...].astype(bf16), pn_ref[...])
    b = _dot(or_ref[...].astype(bf16), pr_ref[...])
    mix = ga_ref[...] * a + gb_ref[...] * b
    h = x_ref[...] + _dot(mix.astype(bf16), wo_ref[...])
    h_ref[...] = h
    hn_ref[...] = _rms(h, g2_ref[...]).astype(bf16)


def merge_project(o_nsa, o_rw, p_all, x2d, pn, pr, wo, g2, tm):
    m = x2d.shape[0]
    const = lambda i: (0, 0)
    return pl.pallas_call(
        _merge_body,
        grid=(m // tm,),
        in_specs=[pl.BlockSpec((tm, NSA_WIDTH), lambda i: (i, 0)),
                  pl.BlockSpec((tm, RWKV_WIDTH), lambda i: (i, 0)),
                  pl.BlockSpec((tm, D_MODEL), lambda i: (i, GM0 // D_MODEL)),
                  pl.BlockSpec((tm, D_MODEL), lambda i: (i, GM0 // D_MODEL + 1)),
                  pl.BlockSpec((tm, D_MODEL), lambda i: (i, 0)),
                  pl.BlockSpec((NSA_WIDTH, D_MODEL), const),
                  pl.BlockSpec((RWKV_WIDTH, D_MODEL), const),
                  pl.BlockSpec((D_MODEL, D_MODEL), const),
                  pl.BlockSpec((1, D_MODEL), const)],
        out_specs=[pl.BlockSpec((tm, D_MODEL), lambda i: (i, 0)),
                   pl.BlockSpec((tm, D_MODEL), lambda i: (i, 0))],
        out_shape=[jax.ShapeDtypeStruct((m, D_MODEL), f32),
                   jax.ShapeDtypeStruct((m, D_MODEL), bf16)],
        compiler_params=_cparams(("arbitrary",)),
        name="merge_project",
    )(o_nsa, o_rw, p_all, p_all, x2d, pn, pr, wo, g2.reshape(1, D_MODEL))


def _mlp_body(hn_ref, h_ref, w1_ref, w2_ref, fg_ref, y_ref, acc_ref):
    f = pl.program_id(1)
    u = jnp.square(jnp.maximum(_dot(hn_ref[...], w1_ref[...]), 0.0)).astype(bf16)
    contrib = _dot(u, w2_ref[...])

    @pl.when(f == 0)
    def _():
        acc_ref[...] = contrib

    @pl.when(f > 0)
    def _():
        acc_ref[...] += contrib

    @pl.when(f == pl.num_programs(1) - 1)
    def _():
        y_ref[...] = _rms(h_ref[...] + acc_ref[...], fg_ref[...])


def mlp_residual_norm(hn, h, w1, w2, fg, tm, tf=512):
    m = h.shape[0]
    return pl.pallas_call(
        _mlp_body,
        grid=(m // tm, D_FF // tf),
        in_specs=[pl.BlockSpec((tm, D_MODEL), lambda i, f: (i, 0)),
                  pl.BlockSpec((tm, D_MODEL), lambda i, f: (i, 0)),
                  pl.BlockSpec((D_MODEL, tf), lambda i, f: (0, f)),
                  pl.BlockSpec((tf, D_MODEL), lambda i, f: (f, 0)),
                  pl.BlockSpec((1, D_MODEL), lambda i, f: (0, 0))],
        out_specs=pl.BlockSpec((tm, D_MODEL), lambda i, f: (i, 0)),
        out_shape=jax.ShapeDtypeStruct((m, D_MODEL), f32),
        scratch_shapes=[pltpu.VMEM((tm, D_MODEL), f32)],
        compiler_params=_cparams(("arbitrary", "arbitrary")),
        name="mlp_residual_norm",
    )(hn, h, w1, w2, fg.reshape(1, D_MODEL))


def _rwkv_prep_body(*refs, tc, seq, has_prev):
    if has_prev:
        p_ref, halo_ref, prev_ref = refs[:3]
        refs = refs[3:]
    else:
        p_ref, halo_ref = refs[:2]
        prev_ref = None
        refs = refs[2:]
    (mu_ref, vec_ref, w2_ref, a2_ref, g2_ref, e_ref,
     r_out, d_out, k_out, v_out, kap_out, b_out, g_out, bonus_out, sh_ref) = refs
    i = pl.program_id(0)
    p = p_ref[...]
    sh_ref[0:SUBLANES, :] = halo_ref[...]
    sh_ref[SUBLANES:SUBLANES + tc, :] = p
    rolled = sh_ref[SUBLANES - 1:SUBLANES - 1 + tc, :]
    t_in_seq = (i * tc + lax.broadcasted_iota(jnp.int32, (tc, 1), 0)) % seq
    first = prev_ref[...] if has_prev else jnp.zeros_like(p)
    shifted = jnp.where(t_in_seq == 0, first, rolled)
    xm = p + (shifted - p) * mu_ref[...]
    r = xm[:, 0:RWKV_WIDTH]
    k = xm[:, RWKV_WIDTH:2 * RWKV_WIDTH]
    v = xm[:, 2 * RWKV_WIDTH:3 * RWKV_WIDTH]
    tail = xm[:, LORA0:RW_PAD]
    w0, a0, k_k, k_a, r_k = (vec_ref[n:n + 1, :] for n in range(5))
    w = -jax.nn.softplus(-(w0 + _dot(jnp.tanh(tail).astype(bf16), w2_ref[...]))) - 0.5
    a = jax.nn.sigmoid(a0 + _dot(tail.astype(bf16), a2_ref[...]))
    kk = k * k_k
    e = e_ref[...]
    kap = kk / jnp.maximum(jnp.sqrt(_split_dot(kk * kk, e)), 1e-12)
    kn = k * (1.0 + (a - 1.0) * k_a)
    r_out[...] = r
    d_out[...] = -jnp.exp(w)
    k_out[...] = kn
    v_out[...] = v
    kap_out[...] = kap
    b_out[...] = kap * a
    g_out[...] = _dot(jax.nn.sigmoid(tail).astype(bf16), g2_ref[...])
    bonus_out[...] = _split_dot(r * kn * r_k, e) * v


def rwkv_prepare(p_all, prev_exp, mu, vecs, w2p, a2p, g2p, e, seq, tc):
    m = p_all.shape[0]
    has_prev = prev_exp is not None
    row = lambda i: (i, 0)
    const = lambda i: (0, 0)
    halo = lambda i: (jnp.maximum(i * (tc // SUBLANES) - 1, 0), 0)
    in_specs = [pl.BlockSpec((tc, RW_PAD), row), pl.BlockSpec((SUBLANES, RW_PAD), halo)]
    args = [p_all, p_all]
    if has_prev:
        in_specs.append(pl.BlockSpec((tc, RW_PAD), row))
        args.append(prev_exp)
    in_specs += [pl.BlockSpec((1, RW_PAD), const), pl.BlockSpec((SUBLANES, RWKV_WIDTH), const),
                 pl.BlockSpec((LORA_PAD, RWKV_WIDTH), const), pl.BlockSpec((LORA_PAD, RWKV_WIDTH), const),
                 pl.BlockSpec((LORA_PAD, RWKV_WIDTH), const), pl.BlockSpec((RWKV_WIDTH, RWKV_WIDTH), const)]
    args += [mu, vecs, w2p, a2p, g2p, e]
    return pl.pallas_call(
        functools.partial(_rwkv_prep_body, tc=tc, seq=seq, has_prev=has_prev),
        grid=(m // tc,),
        in_specs=in_specs,
        out_specs=[pl.BlockSpec((tc, RWKV_WIDTH), row)] * 8,
        out_shape=[jax.ShapeDtypeStruct((m, RWKV_WIDTH), f32)] * 8,
        scratch_shapes=[pltpu.VMEM((tc + SUBLANES, RW_PAD), f32)],
        compiler_params=_cparams(("arbitrary",)),
        name="rwkv_prepare",
    )(*args)


RWKV_CHUNK = 64
RWKV_PAIRS_PER_STEP = 4


def _split(x):
    hi = x.astype(bf16)
    return hi, (x - hi.astype(f32)).astype(bf16)


def _dot3(a, b, nt=False):
    d = _dot_nt if nt else _dot
    a_hi, a_lo = _split(a)
    b_hi, b_lo = _split(b)
    return d(a_hi, b_hi) + d(a_hi, b_lo) + d(a_lo, b_hi)


def _dot1(a, b, nt=False):
    return (_dot_nt if nt else _dot)(a.astype(bf16), b.astype(bf16))


def _rwkv_chunk_body(r_ref, ld_ref, k_ref, v_ref, kap_ref, b_ref, s0_ref, y_ref, st_ref, s_ref):
    c = pl.program_id(2)
    C = RWKV_CHUNK
    zero = jnp.zeros((HEAD_DIM, HEAD_DIM), f32)

    @pl.when(c == 0)
    def _():
        for p in range(RWKV_PAIRS_PER_STEP):
            top = jnp.concatenate([s0_ref[0, 2 * p], zero], axis=1)
            bot = jnp.concatenate([zero, s0_ref[0, 2 * p + 1]], axis=1)
            s_ref[p] = jnp.concatenate([top, bot], axis=0)

    row = lax.broadcasted_iota(jnp.int32, (2 * C, LANES), 0)
    lane = lax.broadcasted_iota(jnp.int32, (2 * C, LANES), 1)
    top, bot = (row < C) & (lane < HEAD_DIM), (row >= C) & (lane >= HEAD_DIM)
    strict, incl = (lane % HEAD_DIM) < (row % C), (lane % HEAD_DIM) <= (row % C)
    eye = jnp.where(row == lane, 1.0, 0.0)
    lane1 = lax.broadcasted_iota(jnp.int32, (1, LANES), 1)
    m_a, m_b = jnp.where(lane1 < HEAD_DIM, 1.0, 0.0), jnp.where(lane1 >= HEAD_DIM, 1.0, 0.0)
    stack = lambda x: jnp.concatenate([x * m_a, x * m_b], axis=0)
    tri = jnp.where(lax.broadcasted_iota(jnp.int32, (C, C), 1) <= lax.broadcasted_iota(jnp.int32, (C, C), 0),
                    1.0, 0.0).astype(bf16)
    pick = lambda cond_a, xa, cond_b, xb: jnp.where(cond_a, xa, 0.0) + jnp.where(cond_b, xb, 0.0)

    l_hi, rest = _split(ld_ref[0])
    l_mid, l_lo = _split(rest.astype(f32))
    cum_all = _dot(tri, l_hi) + _dot(tri, l_mid) + _dot(tri, l_lo)

    for p in range(RWKV_PAIRS_PER_STEP):
        lanes = slice(p * LANES, (p + 1) * LANES)
        r_, ld_, k_, v_, kap_, b_ = (ref[0, :, lanes] for ref in (r_ref, ld_ref, k_ref, v_ref, kap_ref, b_ref))
        cum = cum_all[:, lanes]
        g, g_prev, g_inv = jnp.exp(cum), jnp.exp(cum - ld_), jnp.exp(-cum)
        g_end = g[C - 1:C, :]
        kap_t, r_t, b_t, k_t = kap_ * g_prev, r_ * g, b_ * g_inv, k_ * g_inv
        kst, rst = stack(kap_t), stack(r_t)
        gram = _dot3(jnp.concatenate([kst, rst], axis=0), jnp.concatenate([b_t, k_t], axis=0), nt=True)
        ga, gr = gram[0:2 * C], gram[2 * C:4 * C]
        ga_r, gr_r = pltpu.roll(ga, HEAD_DIM, 1), pltpu.roll(gr, HEAD_DIM, 1)
        l_p = pick(top & strict, ga, bot & strict, ga_r)
        a_ak = pick(top & strict, ga_r, bot & strict, ga)
        a_rb = pick(top & incl, gr, bot & incl, gr_r)
        a_rk = pick(top & incl, gr_r, bot & incl, gr)
        inv, power = eye - l_p, l_p
        for _ in range(C.bit_length() - 2):
            power = _dot3(power, power)
            inv = _dot3(inv, eye + power)
        sst = s_ref[p]
        sst_t = sst.T
        vst = stack(v_)
        u = -_dot3(inv, _dot3(jnp.concatenate([kst, a_ak], axis=1), jnp.concatenate([sst_t, vst], axis=0)))
        y = _dot1(jnp.concatenate([rst, a_rb, a_rk], axis=1), jnp.concatenate([sst_t, u, vst], axis=0))
        y_ref[0, :, lanes] = y[0:C] + y[C:2 * C]
        uv = jnp.concatenate([u, vst], axis=0)
        bk = jnp.concatenate([stack(b_t * g_end), stack(k_t * g_end)], axis=0)
        s_ref[p] = sst * g_end + _dot3(uv.T, bk)

    @pl.when(c == pl.num_programs(2) - 1)
    def _():
        for p in range(RWKV_PAIRS_PER_STEP):
            st_ref[0, 2 * p] = s_ref[p][0:HEAD_DIM, 0:HEAD_DIM]
            st_ref[0, 2 * p + 1] = s_ref[p][HEAD_DIM:, HEAD_DIM:]


def rwkv_chunk_scan(r, ld, k, v, kap, b, s0):
    nbatch, seq = r.shape[:2]
    width = RWKV_PAIRS_PER_STEP * LANES
    blk = pl.BlockSpec((1, RWKV_CHUNK, width), lambda bi, pg, c: (bi, c, pg))
    st = pl.BlockSpec((1, 2 * RWKV_PAIRS_PER_STEP, HEAD_DIM, HEAD_DIM), lambda bi, pg, c: (bi, pg, 0, 0))
    return pl.pallas_call(
        _rwkv_chunk_body,
        grid=(nbatch, RWKV_WIDTH // width, seq // RWKV_CHUNK),
        in_specs=[blk] * 6 + [st],
        out_specs=[blk, st],
        out_shape=[jax.ShapeDtypeStruct((nbatch, seq, RWKV_WIDTH), f32),
                   jax.ShapeDtypeStruct((nbatch, RWKV_HEADS, HEAD_DIM, HEAD_DIM), f32)],
        scratch_shapes=[pltpu.VMEM((RWKV_PAIRS_PER_STEP, 2 * RWKV_CHUNK, LANES), f32)],
        compiler_params=_cparams(("arbitrary", "arbitrary", "arbitrary")),
        name="rwkv_chunk_scan",
    )(r, ld, k, v, kap, b, s0)


def _rwkv_post_body(y_ref, g_ref, bonus_ref, vec_ref, e_ref, o_ref):
    y = y_ref[...]
    e = e_ref[...]
    ln_g, ln_b = vec_ref[5:6, :], vec_ref[6:7, :]
    mu = _split_dot(y, e) * (1.0 / HEAD_DIM)
    yc = y - mu
    var = _split_dot(yc * yc, e) * (1.0 / HEAD_DIM)
    o_ref[...] = (yc * lax.rsqrt(var + GN_EPS) * ln_g + ln_b + bonus_ref[...]) * g_ref[...]


def rwkv_output(y, g, bonus, vecs, e, tc):
    m = y.shape[0]
    row = lambda i: (i, 0)
    const = lambda i: (0, 0)
    return pl.pallas_call(
        _rwkv_post_body,
        grid=(m // tc,),
        in_specs=[pl.BlockSpec((tc, RWKV_WIDTH), row)] * 3 +
                 [pl.BlockSpec((SUBLANES, RWKV_WIDTH), const), pl.BlockSpec((RWKV_WIDTH, RWKV_WIDTH), const)],
        out_specs=pl.BlockSpec((tc, RWKV_WIDTH), row),
        out_shape=jax.ShapeDtypeStruct((m, RWKV_WIDTH), f32),
        compiler_params=_cparams(("arbitrary",)),
        name="rwkv_output",
    )(y, g, bonus, vecs, e)


def rwkv7(p_all, prev, s0, wts, nbatch, seq, tc_prep):
    prev_exp = None
    if prev is not None:
        prev_exp = jnp.repeat(jnp.pad(prev, ((0, 0), (0, RW_PAD - RWKV_PROJ))), seq, axis=0)
    r, ld, k, v, kap, b, g, bonus = rwkv_prepare(
        p_all, prev_exp, wts['mu'], wts['vecs'], wts['w2p'], wts['a2p'], wts['g2p'], wts['e'], seq, tc_prep)
    pad = (-seq) % RWKV_CHUNK
    sh = lambda z: jnp.pad(z.reshape(nbatch, seq, RWKV_WIDTH), ((0, 0), (0, pad), (0, 0)))
    y, s_new = rwkv_chunk_scan(sh(r), sh(ld), sh(k), sh(v), sh(kap), sh(b), s0)
    o = rwkv_output(y[:, :seq].reshape(nbatch * seq, RWKV_WIDTH), g, bonus, wts['vecs'], wts['e'], tc_prep)
    return o, s_new


def _compress_head(x_ref, nchunk, wcat_ref, pe_ref, w1f_ref, b1_ref, w2_ref, b2_ref, acc_ref):
    kv2 = 2 * HEAD_DIM
    pe_term = jnp.concatenate([_dot(pe_ref[k], w1f_ref[k])[0:1, :] for k in range(2)], axis=1)
    bias = pe_term + b1_ref[...]
    acc = jnp.zeros((nchunk, 4 * kv2), f32)
    half = CMP_STRIDE // 2
    for r in range(half):
        xr = jnp.concatenate([x_ref[pl.ds(r, nchunk, stride=CMP_STRIDE), :],
                              x_ref[pl.ds(r + half, nchunk, stride=CMP_STRIDE), :]], axis=1)
        acc = acc + _dot(xr.astype(bf16), wcat_ref[r])
    acc_ref[0:nchunk, :] = acc
    acc_ref[nchunk:nchunk + SUBLANES, :] = jnp.zeros((SUBLANES, 4 * kv2), f32)
    first = jnp.concatenate([acc_ref[0:nchunk, 0:kv2], acc_ref[0:nchunk, 2 * kv2:3 * kv2]], axis=1)
    second = jnp.concatenate([acc_ref[1:nchunk + 1, kv2:2 * kv2], acc_ref[1:nchunk + 1, 3 * kv2:4 * kv2]], axis=1)
    hid = jax.nn.gelu(first + second + bias)
    return _dot(hid.astype(bf16), w2_ref[...]) + b2_ref[...]


def _compress_prompt_body(x_ref, wcat_ref, pe_ref, w1f_ref, b1_ref, w2_ref, b2_ref, o_ref, acc_ref, *, nchunk):
    o_ref[...] = _compress_head(x_ref, nchunk, wcat_ref, pe_ref, w1f_ref, b1_ref, w2_ref, b2_ref, acc_ref)


def _compress_weights(lp):
    w1 = lp['cmp_w1']
    z = jnp.zeros((CMP_STRIDE, HEAD_DIM, CMP_HIDDEN), f32)
    key_rows = jnp.concatenate([w1[0, :CMP_STRIDE], w1[0, CMP_STRIDE:], z, z], axis=-1)
    val_rows = jnp.concatenate([z, z, w1[1, :CMP_STRIDE], w1[1, CMP_STRIDE:]], axis=-1)
    wcat = jnp.concatenate([key_rows, val_rows], axis=1)
    wcat = jnp.concatenate([wcat[:CMP_STRIDE // 2], wcat[CMP_STRIDE // 2:]], axis=1).astype(bf16)
    pe = jnp.broadcast_to(lp['cmp_pe'].reshape(2, 1, CMP_BLOCK * HEAD_DIM), (2, SUBLANES, CMP_BLOCK * HEAD_DIM))
    w2 = lp['cmp_w2']
    zz = jnp.zeros((CMP_HIDDEN, HEAD_DIM), f32)
    w2bd = jnp.concatenate([jnp.concatenate([w2[0], zz], axis=1), jnp.concatenate([zz, w2[1]], axis=1)], axis=0)
    return {'wcat': wcat, 'pe': pe.astype(bf16),
            'w1f': w1.reshape(2, CMP_BLOCK * HEAD_DIM, CMP_HIDDEN).astype(bf16),
            'b1': lp['cmp_b1'].reshape(1, 2 * CMP_HIDDEN), 'w2bd': w2bd.astype(bf16),
            'b2': lp['cmp_b2'].reshape(1, 2 * HEAD_DIM)}


def _cmp_weight_specs(nidx):
    c2 = lambda *a: (0, 0)
    c3 = lambda *a: (0, 0, 0)
    return [pl.BlockSpec((CMP_STRIDE // 2, 4 * HEAD_DIM, 4 * CMP_HIDDEN), c3),
            pl.BlockSpec((2, SUBLANES, CMP_BLOCK * HEAD_DIM), c3),
            pl.BlockSpec((2, CMP_BLOCK * HEAD_DIM, CMP_HIDDEN), c3),
            pl.BlockSpec((1, 2 * CMP_HIDDEN), c2),
            pl.BlockSpec((2 * CMP_HIDDEN, 2 * HEAD_DIM), c2),
            pl.BlockSpec((1, 2 * HEAD_DIM), c2)]


def compress_prompt(p_all, cw, nbatch, seq):
    nchunk = seq // CMP_STRIDE
    kv2 = 2 * HEAD_DIM
    return pl.pallas_call(
        functools.partial(_compress_prompt_body, nchunk=nchunk),
        grid=(nbatch, NSA_KV_HEADS),
        in_specs=[pl.BlockSpec((seq, kv2), lambda b, h: (b, KVC0 // kv2 + h))] + _cmp_weight_specs(2),
        out_specs=pl.BlockSpec((nchunk, kv2), lambda b, h: (b, h)),
        out_shape=jax.ShapeDtypeStruct((nbatch * nchunk, KV_WIDTH), f32),
        scratch_shapes=[pltpu.VMEM((nchunk + SUBLANES, 4 * CMP_HIDDEN), f32)],
        compiler_params=_cparams(("arbitrary", "arbitrary")),
        name="compress_prompt",
    )(p_all, cw['wcat'], cw['pe'], cw['w1f'], cw['b1'], cw['w2bd'], cw['b2'])


KEY_TILE = 256


def _masked_softmax(s, mask):
    s = jnp.where(mask, s, NEG_INF)
    m = jnp.max(s, axis=-1, keepdims=True)
    e = jnp.where(mask, jnp.exp(s - m), 0.0)
    return e / jnp.maximum(jnp.sum(e, axis=-1, keepdims=True), 1e-30)


def _nsa_prompt_body(q_ref, kc_ref, ks_ref, kw_ref, g_ref, wsel_ref, eexp_ref, o_ref, *, nsel, ncmp):
    i = pl.program_id(2)
    qb, grp = Q_BLOCK, NSA_GROUP
    rows = grp * qb
    q = q_ref[...]
    qs = jnp.concatenate([q[:, g * HEAD_DIM:(g + 1) * HEAD_DIM] for g in range(grp)], axis=0) * (HEAD_DIM ** -0.5)
    qp = jnp.concatenate([qs, jnp.zeros_like(qs)], axis=1).astype(bf16)
    qpos = i * qb + lax.broadcasted_iota(jnp.int32, (rows, 1), 0) % qb

    kc = kc_ref[...].astype(bf16)
    s_c = _dot_nt(qp, kc)
    cend = lax.broadcasted_iota(jnp.int32, (1, ncmp), 1) * CMP_STRIDE + (CMP_BLOCK - 1)
    p_c = _masked_softmax(s_c, cend <= qpos)
    o_c = _dot(p_c.astype(bf16), kc)
    imp = p_c[0:qb] + p_c[qb:2 * qb] + p_c[2 * qb:3 * qb] + p_c[3 * qb:4 * qb]
    imp_hi = imp.astype(bf16)
    imp_lo = (imp - imp_hi.astype(f32)).astype(bf16)
    wsel = wsel_ref[...]
    p_slc = _dot_nt(wsel, imp_hi) + _dot_nt(wsel, imp_lo)

    blk = lax.broadcasted_iota(jnp.int32, (nsel, qb), 0)
    qpos_t = i * qb + lax.broadcasted_iota(jnp.int32, (nsel, qb), 1)
    cur = qpos_t // SEL_BLOCK
    forced = (blk == 0) | (blk == cur) | (blk == cur - 1)
    score = jnp.where(forced, FORCE_SCORE, p_slc)
    score = jnp.where(blk * SEL_BLOCK <= qpos_t, score, -1.0)
    rank = jnp.zeros((nsel, qb), f32)
    for jp in range(nsel):
        row = score[jp:jp + 1, :]
        ahead = (row > score) | ((row == score) & (blk > jp))
        rank = rank + jnp.where(ahead, 1.0, 0.0)
    sel_t = jnp.where((rank < float(min(SEL_TOPK, nsel))) & (score >= 0.0), 1.0, 0.0)
    sel = sel_t.T.astype(bf16)

    qpos_q = qpos[0:qb]
    ones_keys = lax.broadcasted_iota(jnp.int32, (1, 2 * HEAD_DIM), 1) < HEAD_DIM

    def attend(ref, kt, carry, mask_fn):
        k0 = pl.multiple_of(kt * KEY_TILE, KEY_TILE)
        kt_tile = ref[pl.ds(k0, KEY_TILE), :].astype(bf16)
        ones_v = jnp.where(ones_keys, jnp.ones_like(kt_tile), kt_tile)
        mask = mask_fn(kt, k0 + lax.broadcasted_iota(jnp.int32, (1, KEY_TILE), 1))
        new = []
        for g in range(grp):
            m, acc = carry[g]
            s = jnp.where(mask, _dot_nt(qp[g * qb:(g + 1) * qb], kt_tile), NEG_INF)
            m_new = jnp.maximum(m, jnp.max(s, axis=-1, keepdims=True))
            e = jnp.where(mask, jnp.exp(s - m_new), 0.0)
            new.append((m_new, jnp.exp(m - m_new) * acc + _dot(e.astype(bf16), ones_v)))
        return tuple(new)

    def sel_mask(kt, kpos):
        return (_dot(sel, eexp_ref[kt]) > 0.5) & (kpos <= qpos_q)

    def win_mask(kt, kpos):
        diff = qpos_q - kpos
        return (diff >= 0) & (diff <= WINDOW)

    per_tile = KEY_TILE // qb
    init = tuple((jnp.full((qb, 1), NEG_INF, f32), jnp.zeros((qb, 2 * HEAD_DIM), f32)) for _ in range(grp))
    res_s = lax.fori_loop(0, i // per_tile + 1, lambda kt, c: attend(ks_ref, kt, c, sel_mask), init)
    res_w = lax.fori_loop(jnp.maximum(i - WINDOW // qb, 0) // per_tile, i // per_tile + 1,
                          lambda kt, c: attend(kw_ref, kt, c, win_mask), init)
    finish = lambda res: jnp.concatenate([acc / jnp.maximum(acc[:, 0:1], 1e-30) for _, acc in res], axis=0)
    o_s, o_w = finish(res_s), finish(res_w)

    gates = g_ref[0, 0]
    gate = lambda br: jnp.concatenate([gates[:, 3 * g + br:3 * g + br + 1] for g in range(grp)], axis=0)
    out = gate(0) * o_c + gate(1) * o_s + gate(2) * o_w
    o_ref[...] = jnp.concatenate([out[g * qb:(g + 1) * qb, HEAD_DIM:] for g in range(grp)], axis=1)


def nsa_prompt(p_all, kv_cmp, gates, nbatch, seq):
    nqb, nsel, ncmp = seq // Q_BLOCK, seq // SEL_BLOCK, seq // CMP_STRIDE
    kv2 = 2 * HEAD_DIM
    cidx = jnp.arange(ncmp)[None, :] - (SEL_BLOCK // CMP_STRIDE) * jnp.arange(nsel)[:, None]
    mult = jnp.array([1, 2, 2, 2, 1], f32)
    wsel = jnp.where((cidx >= 0) & (cidx <= 4), mult[jnp.clip(cidx, 0, 4)], 0.0).astype(bf16)
    ntile = seq // KEY_TILE
    key_blk = (jnp.arange(ntile)[:, None, None] * KEY_TILE + jnp.arange(KEY_TILE)[None, None, :]) // SEL_BLOCK
    eexp = (key_blk == jnp.arange(nsel)[None, :, None]).astype(bf16)
    return pl.pallas_call(
        functools.partial(_nsa_prompt_body, nsel=nsel, ncmp=ncmp),
        grid=(nbatch, NSA_KV_HEADS, nqb),
        in_specs=[pl.BlockSpec((Q_BLOCK, NSA_GROUP * HEAD_DIM),
                               lambda b, h, i: (b * nqb + i, Q0 // (NSA_GROUP * HEAD_DIM) + h)),
                  pl.BlockSpec((ncmp, kv2), lambda b, h, i: (b, h)),
                  pl.BlockSpec((seq, kv2), lambda b, h, i: (b, KVS0 // kv2 + h)),
                  pl.BlockSpec((seq, kv2), lambda b, h, i: (b, KVW0 // kv2 + h)),
                  pl.BlockSpec((1, 1, Q_BLOCK, 16), lambda b, h, i: (b, h, i, 0)),
                  pl.BlockSpec((nsel, ncmp), lambda b, h, i: (0, 0)),
                  pl.BlockSpec((ntile, nsel, KEY_TILE), lambda b, h, i: (0, 0, 0))],
        out_specs=pl.BlockSpec((Q_BLOCK, NSA_GROUP * HEAD_DIM), lambda b, h, i: (b * nqb + i, h)),
        out_shape=jax.ShapeDtypeStruct((nbatch * seq, NSA_WIDTH), f32),
        compiler_params=_cparams(("arbitrary", "arbitrary", "arbitrary")),
        name="nsa_prompt",
    )(p_all, kv_cmp, p_all, p_all, gates, wsel, eexp)


def _branch_gates(p_all, nbatch, seq):
    g = p_all[:, GN0:GN0 + 3 * NSA_HEADS].reshape(nbatch, seq, NSA_KV_HEADS, 3 * NSA_GROUP)
    return jnp.pad(jnp.transpose(g, (0, 2, 1, 3)), ((0, 0), (0, 0), (0, 0), (0, 16 - 3 * NSA_GROUP)))


PAGES_PER_STEP = 16
CMP_HALVES = 2
TAIL_ROWS = CMP_STRIDE


def _compress_sample_body(pt_ref, *refs, nchunk, half_rows):
    npg = PAGES_PER_STEP
    pages, (nxt_ref, newc_ref) = refs[:npg], refs[npg:npg + 2]
    wcat_ref, pe_ref, w1f_ref, b1_ref, w2_ref, b2_ref, o_ref, xs_ref, acc_ref = refs[npg + 2:]
    half, s = pl.program_id(1), pl.program_id(2)
    kv2 = 2 * HEAD_DIM
    for k in range(npg):
        base = pl.multiple_of((s * npg + k) * PAGE_SIZE, PAGE_SIZE)
        for h in range(NSA_KV_HEADS):
            xs_ref[h, pl.ds(base, PAGE_SIZE), :] = pages[k][0, :, h * kv2:(h + 1) * kv2]

    @pl.when(s == pl.num_programs(2) - 1)
    def _():
        nblk = half_rows // CMP_STRIDE
        pad_rows = xs_ref.shape[1] - half_rows - TAIL_ROWS
        for h in range(NSA_KV_HEADS):
            lanes = slice(h * kv2, (h + 1) * kv2)
            tail = jnp.where(half == CMP_HALVES - 1, newc_ref[0, :, lanes], nxt_ref[0, :, lanes])
            xs_ref[h, half_rows:half_rows + TAIL_ROWS, :] = tail
            xs_ref[h, half_rows + TAIL_ROWS:, :] = jnp.zeros((pad_rows, kv2), f32)
            out = _compress_head(xs_ref.at[h], nchunk, wcat_ref, pe_ref, w1f_ref, b1_ref, w2_ref, b2_ref, acc_ref)
            o_ref[0, :, lanes] = out[0:nblk]


def compress_sample(cache, page_table, newc, cw):
    nbatch, npages = page_table.shape
    npg = PAGES_PER_STEP
    half_pages = npages // CMP_HALVES
    half_rows = half_pages * PAGE_SIZE
    nsteps = half_pages // npg
    nchunk = half_rows // CMP_STRIDE + SUBLANES
    page_spec = lambda k: pl.BlockSpec(
        (1, PAGE_SIZE, KV_WIDTH), lambda b, hf, s, pt: (pt[b, hf * half_pages + s * npg + k], 0, 0))
    nxt_spec = pl.BlockSpec(
        (1, TAIL_ROWS, KV_WIDTH), lambda b, hf, s, pt: (pt[b, jnp.minimum((hf + 1) * half_pages, npages - 1)], 0, 0))
    return pl.pallas_call(
        functools.partial(_compress_sample_body, nchunk=nchunk, half_rows=half_rows),
        grid_spec=pltpu.PrefetchScalarGridSpec(
            num_scalar_prefetch=1,
            grid=(nbatch, CMP_HALVES, nsteps),
            in_specs=[page_spec(k) for k in range(npg)] + [nxt_spec] +
                     [pl.BlockSpec((1, TAIL_ROWS, KV_WIDTH), lambda b, hf, s, pt: (b, 0, 0))] + _cmp_weight_specs(4),
            out_specs=pl.BlockSpec((1, half_rows // CMP_STRIDE, KV_WIDTH), lambda b, hf, s, pt: (b, hf, 0)),
            scratch_shapes=[pltpu.VMEM((NSA_KV_HEADS, nchunk * CMP_STRIDE, 2 * HEAD_DIM), f32),
                            pltpu.VMEM((nchunk + SUBLANES, 4 * CMP_HIDDEN), f32)]),
        out_shape=jax.ShapeDtypeStruct((nbatch, npages * PAGE_SIZE // CMP_STRIDE, KV_WIDTH), f32),
        compiler_params=_cparams(("arbitrary", "arbitrary", "arbitrary")),
        name="compress_sample",
    )(page_table, *([cache] * (npg + 1)), newc, cw['wcat'], cw['pe'], cw['w1f'], cw['b1'], cw['w2bd'], cw['b2'])


def _split_dot_r(e, x):
    hi = x.astype(bf16)
    lo = (x - hi.astype(f32)).astype(bf16)
    return _dot(e, hi) + _dot(e, lo)


def _softmax_rows(s, mask):
    s = jnp.where(mask, s, NEG_INF)
    m = jnp.max(s, axis=0, keepdims=True)
    e = jnp.where(mask, jnp.exp(s - m), 0.0)
    return e / jnp.maximum(jnp.sum(e, axis=0, keepdims=True), 1e-30)


def _row_to_col(row):
    return jnp.broadcast_to(row, (SUBLANES, LANES)).T[:, 0:1]


def _nsa_sample_body(pt_ref, *refs, tnew, nsel, nselp):
    npg = PAGES_PER_STEP
    pages = refs[:npg]
    (qb_ref, kc_ref, win_ref, neww_ref, news_ref, g_ref, wsel_ref, gm_ref, o_ref,
     sel_ref, score_ref, st_ref, m_ref, l_ref, acc_ref, oc_ref, ow_ref) = refs[npg:]
    s = pl.program_id(1)
    qb = qb_ref[0]
    col = lax.broadcasted_iota(jnp.int32, (1, LANES), 1)
    tq = col % tnew
    qpos = PAST_LEN + tq
    pad_tail = lambda ref: jnp.concatenate(
        [ref[0], jnp.zeros((LANES - TAIL_ROWS, KV_WIDTH), f32)], axis=0).astype(bf16)

    @pl.when(s == 0)
    def _():
        kc = kc_ref[0].astype(bf16)
        ncmp = kc.shape[0]
        cend = lax.broadcasted_iota(jnp.int32, (ncmp, 1), 0) * CMP_STRIDE + (CMP_BLOCK - 1)
        p_c = _softmax_rows(_dot(kc, qb), cend <= qpos)
        oc_ref[...] = _dot(p_c.T.astype(bf16), kc)
        imp = _split_dot(p_c, gm_ref[...])
        p_slc = _split_dot_r(wsel_ref[...], imp)
        blk = lax.broadcasted_iota(jnp.int32, (nselp, LANES), 0)
        cur = qpos // SEL_BLOCK
        forced = (blk == 0) | (blk == cur) | (blk == cur - 1)
        score = jnp.where(forced, FORCE_SCORE, p_slc)
        score = jnp.where((blk * SEL_BLOCK <= qpos) & (blk < nsel), score, -1.0)
        score_ref[...] = score

        def rank_group(gi, rank):
            rows = score_ref[pl.ds(pl.multiple_of(gi * SUBLANES, SUBLANES), SUBLANES), :]
            for j in range(SUBLANES):
                row = rows[j:j + 1, :]
                ahead = (row > score) | ((row == score) & (blk > gi * SUBLANES + j))
                rank = rank + jnp.where(ahead, 1.0, 0.0)
            return rank

        rank = lax.fori_loop(0, nselp // SUBLANES, rank_group, jnp.zeros((nselp, LANES), f32))
        sel_ref[...] = jnp.where((rank < float(SEL_TOPK)) & (score >= 0.0), 1.0, 0.0)

        kw = jnp.concatenate([win_ref[0].astype(bf16), pad_tail(neww_ref)], axis=0)
        idx = lax.broadcasted_iota(jnp.int32, (kw.shape[0], 1), 0)
        wbuf = win_ref.shape[1]
        p_w = _softmax_rows(_dot(kw, qb), (idx <= wbuf + tq) & (idx >= wbuf + tq - WINDOW))
        ow_ref[...] = _dot(p_w.T.astype(bf16), kw)

        m_ref[...] = jnp.full(m_ref.shape, NEG_INF, f32)
        l_ref[...] = jnp.zeros(l_ref.shape, f32)
        acc_ref[...] = jnp.zeros(acc_ref.shape, f32)

    def fold(tiles):
        m_old = m_ref[0:1, :]
        m_new = m_old
        for sc, _ in tiles:
            m_new = jnp.maximum(m_new, jnp.max(sc, axis=0, keepdims=True))
        alpha = jnp.exp(m_old - m_new)
        lsum = alpha * l_ref[0:1, :]
        acc = _row_to_col(alpha) * acc_ref[...]
        for sc, vals in tiles:
            e = jnp.where(sc > 0.5 * NEG_INF, jnp.exp(sc - m_new), 0.0)
            lsum = lsum + jnp.sum(e, axis=0, keepdims=True)
            acc = acc + _dot(e.T.astype(bf16), vals)
        m_ref[...] = jnp.broadcast_to(m_new, m_ref.shape)
        l_ref[...] = jnp.broadcast_to(lsum, l_ref.shape)
        acc_ref[...] = acc

    key_hi = lax.broadcasted_iota(jnp.int32, (PAGE_SIZE, LANES), 0) >= SEL_BLOCK
    tiles = []
    for k in range(npg):
        pg = pages[k][0].astype(bf16)
        grp = sel_ref[pl.ds(pl.multiple_of(s * 2 * npg + (2 * k // SUBLANES) * SUBLANES, SUBLANES), SUBLANES), :]
        r0 = (2 * k) % SUBLANES
        mask = jnp.where(key_hi, grp[r0 + 1:r0 + 2, :], grp[r0:r0 + 1, :]) > 0.5
        st_ref[k * PAGE_SIZE:(k + 1) * PAGE_SIZE, :] = jnp.where(mask, _dot(pg, qb), NEG_INF)
        tiles.append((st_ref[k * PAGE_SIZE:(k + 1) * PAGE_SIZE, :], pg))
    fold(tiles)

    @pl.when(s == pl.num_programs(1) - 1)
    def _():
        ns = pad_tail(news_ref)
        kidx = lax.broadcasted_iota(jnp.int32, (LANES, 1), 0)
        nb = nsel - 1
        grp = sel_ref[(nb // SUBLANES) * SUBLANES:(nb // SUBLANES + 1) * SUBLANES, :]
        mask = (grp[nb % SUBLANES:nb % SUBLANES + 1, :] > 0.5) & (kidx <= tq)
        fold([(jnp.where(mask, _dot(ns, qb), NEG_INF), ns)])
        o_s = acc_ref[...] / jnp.maximum(_row_to_col(l_ref[0:1, :]), 1e-30)
        rows_per_head = LANES // NSA_KV_HEADS
        for h in range(NSA_KV_HEADS):
            rows = slice(h * rows_per_head, (h + 1) * rows_per_head)
            lanes = slice(h * 2 * HEAD_DIM, (h + 1) * 2 * HEAD_DIM)
            o_ref[0, rows, :] = (g_ref[0, 0, rows, :] * oc_ref[rows, lanes] + g_ref[0, 1, rows, :] * o_s[rows, lanes]
                                 + g_ref[0, 2, rows, :] * ow_ref[rows, lanes])


def nsa_sample(cache_sel, page_table, qblk, kv_cmp, win_buf, neww, news, gates, tnew):
    nbatch, npages = page_table.shape
    npg = PAGES_PER_STEP
    ncmp = kv_cmp.shape[1]
    nsel = npages * (PAGE_SIZE // SEL_BLOCK) + 1
    nselp = -(-nsel // SUBLANES) * SUBLANES
    cidx = jnp.arange(ncmp)[None, :] - (SEL_BLOCK // CMP_STRIDE) * jnp.arange(nselp)[:, None]
    mult = jnp.array([1, 2, 2, 2, 1], f32)
    wsel = jnp.where((cidx >= 0) & (cidx <= 4), mult[jnp.clip(cidx, 0, 4)], 0.0).astype(bf16)
    c = jnp.arange(LANES)
    same = (c[:, None] // (NSA_GROUP * tnew) == c[None, :] // (NSA_GROUP * tnew)) & (c[:, None] % tnew == c[None, :] % tnew)
    gm = same.astype(bf16)
    b3 = lambda b, s, pt: (b, 0, 0)
    c2 = lambda b, s, pt: (0, 0)
    page_spec = lambda k: pl.BlockSpec((1, PAGE_SIZE, KV_WIDTH), lambda b, s, pt: (pt[b, s * npg + k], 0, 0))
    wbuf = win_buf.shape[1]
    return pl.pallas_call(
        functools.partial(_nsa_sample_body, tnew=tnew, nsel=nsel, nselp=nselp),
        grid_spec=pltpu.PrefetchScalarGridSpec(
            num_scalar_prefetch=1,
            grid=(nbatch, npages // npg),
            in_specs=[page_spec(k) for k in range(npg)] + [
                pl.BlockSpec((1, KV_WIDTH, LANES), b3),
                pl.BlockSpec((1, ncmp, KV_WIDTH), b3),
                pl.BlockSpec((1, wbuf, KV_WIDTH), b3),
                pl.BlockSpec((1, TAIL_ROWS, KV_WIDTH), b3),
                pl.BlockSpec((1, TAIL_ROWS, KV_WIDTH), b3),
                pl.BlockSpec((1, 3, LANES, LANES), lambda b, s, pt: (b, 0, 0, 0)),
                pl.BlockSpec((nselp, ncmp), c2),
                pl.BlockSpec((LANES, LANES), c2)],
            out_specs=pl.BlockSpec((1, LANES, LANES), b3),
            scratch_shapes=[pltpu.VMEM((nselp, LANES), f32), pltpu.VMEM((nselp, LANES), f32),
                            pltpu.VMEM((npg * PAGE_SIZE, LANES), f32),
                            pltpu.VMEM((SUBLANES, LANES), f32), pltpu.VMEM((SUBLANES, LANES), f32),
                            pltpu.VMEM((LANES, KV_WIDTH), f32), pltpu.VMEM((LANES, KV_WIDTH), f32),
                            pltpu.VMEM((LANES, KV_WIDTH), f32)]),
        out_shape=jax.ShapeDtypeStruct((nbatch, LANES, LANES), f32),
        compiler_params=_cparams(("arbitrary", "arbitrary")),
        name="nsa_sample",
    )(page_table, *([cache_sel] * npg), qblk, kv_cmp, win_buf, neww, news, gates, wsel, gm)


def _sample_nsa_inputs(p_all, b, t):
    q = p_all[:, Q0:Q0 + NSA_WIDTH].reshape(b, t, NSA_KV_HEADS, NSA_GROUP, HEAD_DIM) * (HEAD_DIM ** -0.5)
    qt = jnp.transpose(q, (0, 2, 4, 3, 1)).reshape(b, NSA_KV_HEADS, HEAD_DIM, NSA_GROUP * t)
    qt = jnp.pad(qt, ((0, 0), (0, 0), (0, HEAD_DIM), (0, 0)))
    eye = jnp.eye(NSA_KV_HEADS, dtype=f32)
    qblk = (qt[:, :, :, None, :] * eye[None, :, None, :, None]).reshape(b, KV_WIDTH, LANES).astype(bf16)
    g = p_all[:, GN0:GN0 + 3 * NSA_HEADS].reshape(b, t, NSA_KV_HEADS, NSA_GROUP, 3)
    g = jnp.transpose(g, (0, 4, 2, 3, 1)).reshape(b, 3, LANES)
    gates = jnp.broadcast_to(g[..., None], (b, 3, LANES, LANES))
    tail = lambda c0: jnp.pad(p_all[:, c0:c0 + KV_WIDTH].reshape(b, t, KV_WIDTH), ((0, 0), (0, TAIL_ROWS - t), (0, 0)))
    return qblk, gates, tail(KVC0), tail(KVS0), tail(KVW0)


def _prepare_weights(lp):
    w_in = lp['w_in']
    nsa_end = NSA_WIDTH + 3 * KV_WIDTH
    w_all = jnp.concatenate([
        w_in[:, NSA_PROJ:NSA_PROJ + RWKV_PROJ], w_in[:, nsa_end:NSA_PROJ],
        jnp.zeros((D_MODEL, RW_PAD - RWKV_PROJ - 3 * NSA_HEADS), w_in.dtype),
        w_in[:, :nsa_end], w_in[:, NSA_PROJ + RWKV_PROJ:]], axis=1).astype(bf16)
    pad_rows = lambda w, r0: jnp.zeros((LORA_PAD, RWKV_WIDTH), f32).at[r0:r0 + w.shape[0]].set(w).astype(bf16)
    head = jnp.arange(RWKV_WIDTH) // HEAD_DIM
    vecs = jnp.stack([lp['rwkv_w0'], lp['rwkv_a0'], lp['rwkv_k_k'], lp['rwkv_k_a'],
                      lp['rwkv_r_k'].reshape(RWKV_WIDTH), lp['rwkv_ln_g'], lp['rwkv_ln_b'],
                      jnp.zeros((RWKV_WIDTH,), f32)])
    return {
        'w_all': w_all,
        'mu': jnp.pad(lp['rwkv_mu'], (0, RW_PAD - RWKV_PROJ)).reshape(1, RW_PAD),
        'vecs': vecs,
        'w2p': pad_rows(lp['rwkv_w2'], 0),
        'a2p': pad_rows(lp['rwkv_a2'], DECAY_LORA),
        'g2p': pad_rows(lp['rwkv_g2'], DECAY_LORA + AAA_LORA),
        'e': (head[:, None] == head[None, :]).astype(bf16),
        'p_nsa': lp['p_nsa'].astype(bf16), 'p_rwkv': lp['p_rwkv'].astype(bf16),
        'w_out': lp['w_out'].astype(bf16),
        'mlp_w1': lp['mlp_w1'].astype(bf16), 'mlp_w2': lp['mlp_w2'].astype(bf16),
    }


def _group_forward(x, pos_rows, prev, s0, lp, wts, final_g, tm, tc_prep, o_nsa_fn):
    nbatch, seq = x.shape[:2]
    x2d = x.reshape(nbatch * seq, D_MODEL)
    tq, tkv = _rot_tables(pos_rows)
    p_all = input_projection(x2d, lp['norm1_g'], wts['w_all'], tq, tkv, tm)
    o_rw, s_new = rwkv7(p_all, prev, s0, wts, nbatch, seq, tc_prep)
    o_nsa = o_nsa_fn(p_all)
    h, hn = merge_project(o_nsa, o_rw, p_all, x2d, wts['p_nsa'], wts['p_rwkv'], wts['w_out'], lp['norm2_g'],
                          min(tm, 256))
    y = mlp_residual_norm(hn, h, wts['mlp_w1'], wts['mlp_w2'], final_g, tm)
    return y.reshape(nbatch, seq, D_MODEL), p_all, s_new


def _kv_rows(p_all, col0, nbatch, seq):
    return p_all[:, col0:col0 + KV_WIDTH].reshape(nbatch, seq, NSA_KV_HEADS, 2, HEAD_DIM)


def sample_nsa_attention(p_all, cache_cmp, cache_sel, win_buf, page_table, cw, b, t):
    qblk, gates, newc, news, neww = _sample_nsa_inputs(p_all, b, t)
    pool = cache_cmp.shape[0]
    kv_cmp = compress_sample(cache_cmp.reshape(pool, PAGE_SIZE, KV_WIDTH), page_table, newc, cw)
    o = nsa_sample(cache_sel.reshape(pool, PAGE_SIZE, KV_WIDTH), page_table, qblk, kv_cmp,
                   win_buf.reshape(b, win_buf.shape[1], KV_WIDTH), neww, news, gates, t)
    o = o[..., HEAD_DIM:].reshape(b, NSA_KV_HEADS, NSA_GROUP, t, HEAD_DIM)
    return jnp.transpose(o, (0, 3, 1, 2, 4)).reshape(b * t, NSA_WIDTH)


def kernel(x_prompt, x_sample, cache_cmp_kv, cache_sel_kv, state_nsa_win, state_rwkv, state_rwkv_shift,
           page_table, norm1_g, w_in, cmp_pe, cmp_w1, cmp_b1, cmp_w2, cmp_b2, rwkv_mu, rwkv_w0, rwkv_w2,
           rwkv_a0, rwkv_a2, rwkv_g2, rwkv_k_k, rwkv_k_a, rwkv_r_k, rwkv_ln_g, rwkv_ln_b, p_nsa, p_rwkv,
           w_out, norm2_g, mlp_w1, mlp_w2, final_g):
    l = 0
    lp = {'norm1_g': norm1_g[l], 'w_in': w_in[l], 'cmp_pe': cmp_pe[l], 'cmp_w1': cmp_w1[l],
          'cmp_b1': cmp_b1[l], 'cmp_w2': cmp_w2[l], 'cmp_b2': cmp_b2[l], 'rwkv_mu': rwkv_mu[l],
          'rwkv_w0': rwkv_w0[l], 'rwkv_w2': rwkv_w2[l], 'rwkv_a0': rwkv_a0[l], 'rwkv_a2': rwkv_a2[l],
          'rwkv_g2': rwkv_g2[l], 'rwkv_k_k': rwkv_k_k[l], 'rwkv_k_a': rwkv_k_a[l], 'rwkv_r_k': rwkv_r_k[l],
          'rwkv_ln_g': rwkv_ln_g[l], 'rwkv_ln_b': rwkv_ln_b[l], 'p_nsa': p_nsa[l], 'p_rwkv': p_rwkv[l],
          'w_out': w_out[l], 'norm2_g': norm2_g[l], 'mlp_w1': mlp_w1[l], 'mlp_w2': mlp_w2[l]}
    wts = _prepare_weights(lp)
    bp, tp = x_prompt.shape[:2]
    bs, ts = x_sample.shape[:2]
    cw = _compress_weights(lp)
    zero_state = jnp.zeros((bp, RWKV_HEADS, HEAD_DIM, HEAD_DIM), f32)

    def prompt_nsa(p_all):
        return nsa_prompt(p_all, compress_prompt(p_all, cw, bp, tp), _branch_gates(p_all, bp, tp), bp, tp)

    yp, pp, sp = _group_forward(x_prompt, jnp.arange(tp, dtype=jnp.int32), None, zero_state, lp, wts, final_g,
                                512, 256, prompt_nsa)

    def sample_nsa(p_all):
        return sample_nsa_attention(p_all, cache_cmp_kv[l], cache_sel_kv[l], state_nsa_win[l], page_table, cw, bs, ts)

    pos_s = PAST_LEN + jnp.arange(bs * ts, dtype=jnp.int32) % ts
    ys, ps, ss = _group_forward(x_sample, pos_s, state_rwkv_shift[l], state_rwkv[l], lp, wts, final_g,
                                256, 256, sample_nsa)
    win_new = [jnp.concatenate([state_nsa_win[l], _kv_rows(ps, KVW0, bs, ts)], axis=1)[:, ts:]]
    wlen = min(WINDOW, tp)
    shift = lambda p_all, nb, t: p_all.reshape(nb, t, P_WIDTH)[:, -1, RW0:RW0 + RWKV_PROJ]
    return (yp, ys,
            _kv_rows(pp, KVC0, bp, tp)[None], _kv_rows(ps, KVC0, bs, ts)[None],
            _kv_rows(pp, KVS0, bp, tp)[None], _kv_rows(ps, KVS0, bs, ts)[None],
            _kv_rows(pp, KVW0, bp, tp)[None, :, tp - wlen:], win_new[0][None],
            sp[None], ss[None],
            shift(pp, bp, tp)[None], shift(ps, bs, ts)[None])
```

```python
import functools

import jax
import jax.numpy as jnp
from jax import lax
from jax.experimental import pallas as pl
from jax.experimental.pallas import tpu as pltpu

D_MODEL = 2048
DEPTH = 1
PAST_LEN = 16384
PAGE_SIZE = 128

HEAD_DIM = 64
NSA_HEADS = D_MODEL // (2 * HEAD_DIM)
NSA_KV_HEADS = NSA_HEADS // 4
NSA_GROUP = NSA_HEADS // NSA_KV_HEADS
NSA_WIDTH = NSA_HEADS * HEAD_DIM
KV_WIDTH = NSA_KV_HEADS * 2 * HEAD_DIM
CMP_BLOCK = 32
CMP_STRIDE = 16
CMP_HIDDEN = 2 * HEAD_DIM
SEL_BLOCK = 64
SEL_TOPK = 16
WINDOW = 512
Q_BLOCK = 128
ROT_DIM = HEAD_DIM // 4
ROPE_THETA = 500000.0
RWKV_HEADS = D_MODEL // (2 * HEAD_DIM)
RWKV_WIDTH = RWKV_HEADS * HEAD_DIM
DECAY_LORA = max(32, int(round(1.8 * D_MODEL ** 0.5 / 32)) * 32)
AAA_LORA = DECAY_LORA
GATE_LORA = max(32, int(round(0.6 * D_MODEL ** 0.8 / 32)) * 32)
RWKV_PROJ = 3 * RWKV_WIDTH + DECAY_LORA + AAA_LORA + GATE_LORA
NSA_PROJ = NSA_WIDTH + 3 * KV_WIDTH + 3 * NSA_HEADS
N_IN = NSA_PROJ + RWKV_PROJ + 2 * D_MODEL
D_FF = 4 * D_MODEL
RMS_EPS = 1e-6
GN_EPS = HEAD_DIM * 1e-5
NEG_INF = -1e30
FORCE_SCORE = 1e6

LANES = 128
SUBLANES = 8
VMEM_LIMIT_BYTES = 56 * 1024 * 1024

RW0 = 0
LORA0 = 3 * RWKV_WIDTH
GN0 = RWKV_PROJ
RW_PAD = 3584
LORA_PAD = RW_PAD - LORA0
Q0 = RW_PAD
KVC0 = Q0 + NSA_WIDTH
KVS0 = KVC0 + KV_WIDTH
KVW0 = KVS0 + KV_WIDTH
GM0 = KVW0 + KV_WIDTH
P_WIDTH = GM0 + 2 * D_MODEL
PROJ_TN = 512

f32 = jnp.float32
bf16 = jnp.bfloat16


def _cparams(sem):
    return pltpu.CompilerParams(dimension_semantics=sem, vmem_limit_bytes=VMEM_LIMIT_BYTES)


def _rms(x, g):
    return x * lax.rsqrt(jnp.mean(x * x, axis=-1, keepdims=True) + RMS_EPS) * g


def _dot(a, b):
    return jnp.dot(a, b, preferred_element_type=f32)


def _dot_nt(a, b):
    return lax.dot_general(a, b, (((1,), (1,)), ((), ())), preferred_element_type=f32)


def _split_dot(x, e):
    hi = x.astype(bf16)
    lo = (x - hi.astype(f32)).astype(bf16)
    return _dot(hi, e) + _dot(lo, e)


def _rot_store(acc, tab_ref, o_ref):
    c, s1, s2 = tab_ref[0], tab_ref[1], tab_ref[2]
    for s in range(acc.shape[1] // LANES):
        x = acc[:, s * LANES:(s + 1) * LANES]
        o_ref[:, s * LANES:(s + 1) * LANES] = (
            x * c + pltpu.roll(x, LANES - ROT_DIM // 2, 1) * s1 + pltpu.roll(x, ROT_DIM // 2, 1) * s2)


def _proj_body(x_ref, g_ref, w_ref, tq_ref, tkv_ref, o_ref, xn_ref):
    j = pl.program_id(1)

    @pl.when(j == 0)
    def _():
        xn_ref[...] = _rms(x_ref[...], g_ref[...]).astype(bf16)

    acc = _dot(xn_ref[...], w_ref[...])
    gn_tile = GN0 // PROJ_TN

    @pl.when(j < gn_tile)
    def _():
        o_ref[...] = acc

    @pl.when(j == gn_tile)
    def _():
        lane = lax.broadcasted_iota(jnp.int32, acc.shape, 1)
        o_ref[...] = jnp.where(lane >= GN0 - gn_tile * PROJ_TN, jax.nn.sigmoid(acc), acc)

    @pl.when((j >= Q0 // PROJ_TN) & (j < KVC0 // PROJ_TN))
    def _():
        _rot_store(acc, tq_ref, o_ref)

    @pl.when((j >= KVC0 // PROJ_TN) & (j < GM0 // PROJ_TN))
    def _():
        _rot_store(acc, tkv_ref, o_ref)

    @pl.when(j >= GM0 // PROJ_TN)
    def _():
        o_ref[...] = jax.nn.sigmoid(acc)


def input_projection(x2d, g, w_all, tq, tkv, tm):
    m = x2d.shape[0]
    nt = tq.shape[1] // tm
    return pl.pallas_call(
        _proj_body,
        grid=(m // tm, P_WIDTH // PROJ_TN),
        in_specs=[pl.BlockSpec((tm, D_MODEL), lambda i, j: (i, 0)),
                  pl.BlockSpec((1, D_MODEL), lambda i, j: (0, 0)),
                  pl.BlockSpec((D_MODEL, PROJ_TN), lambda i, j: (0, j)),
                  pl.BlockSpec((3, tm, LANES), lambda i, j: (0, i % nt, 0)),
                  pl.BlockSpec((3, tm, LANES), lambda i, j: (0, i % nt, 0))],
        out_specs=pl.BlockSpec((tm, PROJ_TN), lambda i, j: (i, j)),
        out_shape=jax.ShapeDtypeStruct((m, P_WIDTH), f32),
        scratch_shapes=[pltpu.VMEM((tm, D_MODEL), bf16)],
        compiler_params=_cparams(("arbitrary", "arbitrary")),
        name="input_projection",
    )(x2d, g.reshape(1, D_MODEL), w_all, tq, tkv)


def _rot_tables(pos):
    half = ROT_DIM // 2
    freqs = jnp.power(jnp.float32(ROPE_THETA), -jnp.arange(half, dtype=f32) * 2.0 / ROT_DIM)
    ang = pos.astype(f32)[:, None] * freqs[None, :]
    cos, sin = jnp.cos(ang), jnp.sin(ang)
    lane = jnp.arange(LANES)

    def build(period):
        l = lane % period
        fi = l % half
        c = jnp.where(l < ROT_DIM, cos[:, fi], 1.0)
        s1 = jnp.where(l < half, -sin[:, fi], 0.0)
        s2 = jnp.where((l >= half) & (l < ROT_DIM), sin[:, fi], 0.0)
        return jnp.stack([c, s1, s2]).astype(f32)

    return build(HEAD_DIM), build(2 * HEAD_DIM)


def _merge_body(on_ref, or_ref, ga_ref, gb_ref, x_ref, pn_ref, pr_ref, wo_ref, g2_ref, h_ref, hn_ref):
    a = _dot(on_ref[...].astype(bf16), pn_ref[...])
    b = _dot(or_ref[...].astype(bf16), pr_ref[...])
    mix = ga_ref[...] * a + gb_ref[...] * b
    h = x_ref[...] + _dot(mix.astype(bf16), wo_ref[...])
    h_ref[...] = h
    hn_ref[...] = _rms(h, g2_ref[...]).astype(bf16)


def merge_project(o_nsa, o_rw, p_all, x2d, pn, pr, wo, g2, tm):
    m = x2d.shape[0]
    const = lambda i: (0, 0)
    return pl.pallas_call(
        _merge_body,
        grid=(m // tm,),
        in_specs=[pl.BlockSpec((tm, NSA_WIDTH), lambda i: (i, 0)),
                  pl.BlockSpec((tm, RWKV_WIDTH), lambda i: (i, 0)),
                  pl.BlockSpec((tm, D_MODEL), lambda i: (i, GM0 // D_MODEL)),
                  pl.BlockSpec((tm, D_MODEL), lambda i: (i, GM0 // D_MODEL + 1)),
                  pl.BlockSpec((tm, D_MODEL), lambda i: (i, 0)),
                  pl.BlockSpec((NSA_WIDTH, D_MODEL), const),
                  pl.BlockSpec((RWKV_WIDTH, D_MODEL), const),
                  pl.BlockSpec((D_MODEL, D_MODEL), const),
                  pl.BlockSpec((1, D_MODEL), const)],
        out_specs=[pl.BlockSpec((tm, D_MODEL), lambda i: (i, 0)),
                   pl.BlockSpec((tm, D_MODEL), lambda i: (i, 0))],
        out_shape=[jax.ShapeDtypeStruct((m, D_MODEL), f32),
                   jax.ShapeDtypeStruct((m, D_MODEL), bf16)],
        compiler_params=_cparams(("arbitrary",)),
        name="merge_project",
    )(o_nsa, o_rw, p_all, p_all, x2d, pn, pr, wo, g2.reshape(1, D_MODEL))


def _mlp_body(hn_ref, h_ref, w1_ref, w2_ref, fg_ref, y_ref, acc_ref):
    f = pl.program_id(1)
    u = jnp.square(jnp.maximum(_dot(hn_ref[...], w1_ref[...]), 0.0)).astype(bf16)
    contrib = _dot(u, w2_ref[...])

    @pl.when(f == 0)
    def _():
        acc_ref[...] = contrib

    @pl.when(f > 0)
    def _():
        acc_ref[...] += contrib

    @pl.when(f == pl.num_programs(1) - 1)
    def _():
        y_ref[...] = _rms(h_ref[...] + acc_ref[...], fg_ref[...])


def mlp_residual_norm(hn, h, w1, w2, fg, tm, tf=512):
    m = h.shape[0]
    return pl.pallas_call(
        _mlp_body,
        grid=(m // tm, D_FF // tf),
        in_specs=[pl.BlockSpec((tm, D_MODEL), lambda i, f: (i, 0)),
                  pl.BlockSpec((tm, D_MODEL), lambda i, f: (i, 0)),
                  pl.BlockSpec((D_MODEL, tf), lambda i, f: (0, f)),
                  pl.BlockSpec((tf, D_MODEL), lambda i, f: (f, 0)),
                  pl.BlockSpec((1, D_MODEL), lambda i, f: (0, 0))],
        out_specs=pl.BlockSpec((tm, D_MODEL), lambda i, f: (i, 0)),
        out_shape=jax.ShapeDtypeStruct((m, D_MODEL), f32),
        scratch_shapes=[pltpu.VMEM((tm, D_MODEL), f32)],
        compiler_params=_cparams(("arbitrary", "arbitrary")),
        name="mlp_residual_norm",
    )(hn, h, w1, w2, fg.reshape(1, D_MODEL))


def _rwkv_prep_body(*refs, tc, seq, has_prev):
    if has_prev:
        p_ref, halo_ref, prev_ref = refs[:3]
        refs = refs[3:]
    else:
        p_ref, halo_ref = refs[:2]
        prev_ref = None
        refs = refs[2:]
    (mu_ref, vec_ref, w2_ref, a2_ref, g2_ref, e_ref,
     r_out, d_out, k_out, v_out, kap_out, b_out, g_out, bonus_out, sh_ref) = refs
    i = pl.program_id(0)
    p = p_ref[...]
    sh_ref[0:SUBLANES, :] = halo_ref[...]
    sh_ref[SUBLANES:SUBLANES + tc, :] = p
    rolled = sh_ref[SUBLANES - 1:SUBLANES - 1 + tc, :]
    t_in_seq = (i * tc + lax.broadcasted_iota(jnp.int32, (tc, 1), 0)) % seq
    first = prev_ref[...] if has_prev else jnp.zeros_like(p)
    shifted = jnp.where(t_in_seq == 0, first, rolled)
    xm = p + (shifted - p) * mu_ref[...]
    r = xm[:, 0:RWKV_WIDTH]
    k = xm[:, RWKV_WIDTH:2 * RWKV_WIDTH]
    v = xm[:, 2 * RWKV_WIDTH:3 * RWKV_WIDTH]
    tail = xm[:, LORA0:RW_PAD]
    w0, a0, k_k, k_a, r_k = (vec_ref[n:n + 1, :] for n in range(5))
    w = -jax.nn.softplus(-(w0 + _dot(jnp.tanh(tail).astype(bf16), w2_ref[...]))) - 0.5
    a = jax.nn.sigmoid(a0 + _dot(tail.astype(bf16), a2_ref[...]))
    kk = k * k_k
    e = e_ref[...]
    kap = kk / jnp.maximum(jnp.sqrt(_split_dot(kk * kk, e)), 1e-12)
    kn = k * (1.0 + (a - 1.0) * k_a)
    r_out[...] = r
    d_out[...] = -jnp.exp(w)
    k_out[...] = kn
    v_out[...] = v
    kap_out[...] = kap
    b_out[...] = kap * a
    g_out[...] = _dot(jax.nn.sigmoid(tail).astype(bf16), g2_ref[...])
    bonus_out[...] = _split_dot(r * kn * r_k, e) * v


def rwkv_prepare(p_all, prev_exp, mu, vecs, w2p, a2p, g2p, e, seq, tc):
    m = p_all.shape[0]
    has_prev = prev_exp is not None
    row = lambda i: (i, 0)
    const = lambda i: (0, 0)
    halo = lambda i: (jnp.maximum(i * (tc // SUBLANES) - 1, 0), 0)
    in_specs = [pl.BlockSpec((tc, RW_PAD), row), pl.BlockSpec((SUBLANES, RW_PAD), halo)]
    args = [p_all, p_all]
    if has_prev:
        in_specs.append(pl.BlockSpec((tc, RW_PAD), row))
        args.append(prev_exp)
    in_specs += [pl.BlockSpec((1, RW_PAD), const), pl.BlockSpec((SUBLANES, RWKV_WIDTH), const),
                 pl.BlockSpec((LORA_PAD, RWKV_WIDTH), const), pl.BlockSpec((LORA_PAD, RWKV_WIDTH), const),
                 pl.BlockSpec((LORA_PAD, RWKV_WIDTH), const), pl.BlockSpec((RWKV_WIDTH, RWKV_WIDTH), const)]
    args += [mu, vecs, w2p, a2p, g2p, e]
    return pl.pallas_call(
        functools.partial(_rwkv_prep_body, tc=tc, seq=seq, has_prev=has_prev),
        grid=(m // tc,),
        in_specs=in_specs,
        out_specs=[pl.BlockSpec((tc, RWKV_WIDTH), row)] * 8,
        out_shape=[jax.ShapeDtypeStruct((m, RWKV_WIDTH), f32)] * 8,
        scratch_shapes=[pltpu.VMEM((tc + SUBLANES, RW_PAD), f32)],
        compiler_params=_cparams(("arbitrary",)),
        name="rwkv_prepare",
    )(*args)


RWKV_CHUNK = 64
RWKV_PAIRS_PER_STEP = 8


def _split(x):
    hi = x.astype(bf16)
    return hi, (x - hi.astype(f32)).astype(bf16)


def _dot3(a, b, nt=False):
    d = _dot_nt if nt else _dot
    a_hi, a_lo = _split(a)
    b_hi, b_lo = _split(b)
    return d(a_hi, b_hi) + d(a_hi, b_lo) + d(a_lo, b_hi)


def _dot1(a, b, nt=False):
    return (_dot_nt if nt else _dot)(a.astype(bf16), b.astype(bf16))


def _rwkv_chunk_body(r_ref, ld_ref, k_ref, v_ref, kap_ref, b_ref, s0_ref, y_ref, st_ref, s_ref):
    c = pl.program_id(2)
    C = RWKV_CHUNK
    zero = jnp.zeros((HEAD_DIM, HEAD_DIM), f32)

    @pl.when(c == 0)
    def _():
        for p in range(RWKV_PAIRS_PER_STEP):
            top = jnp.concatenate([s0_ref[0, 2 * p], zero], axis=1)
            bot = jnp.concatenate([zero, s0_ref[0, 2 * p + 1]], axis=1)
            s_ref[p] = jnp.concatenate([top, bot], axis=0)

    row = lax.broadcasted_iota(jnp.int32, (2 * C, LANES), 0)
    lane = lax.broadcasted_iota(jnp.int32, (2 * C, LANES), 1)
    top, bot = (row < C) & (lane < HEAD_DIM), (row >= C) & (lane >= HEAD_DIM)
    strict, incl = (lane % HEAD_DIM) < (row % C), (lane % HEAD_DIM) <= (row % C)
    eye = jnp.where(row == lane, 1.0, 0.0)
    lane1 = lax.broadcasted_iota(jnp.int32, (1, LANES), 1)
    m_a, m_b = jnp.where(lane1 < HEAD_DIM, 1.0, 0.0), jnp.where(lane1 >= HEAD_DIM, 1.0, 0.0)
    stack = lambda x: jnp.concatenate([x * m_a, x * m_b], axis=0)
    tri = jnp.where(lax.broadcasted_iota(jnp.int32, (C, C), 1) <= lax.broadcasted_iota(jnp.int32, (C, C), 0),
                    1.0, 0.0).astype(bf16)
    pick = lambda cond_a, xa, cond_b, xb: jnp.where(cond_a, xa, 0.0) + jnp.where(cond_b, xb, 0.0)

    l_hi, rest = _split(ld_ref[0])
    l_mid, l_lo = _split(rest.astype(f32))
    cum_all = _dot(tri, l_hi) + _dot(tri, l_mid) + _dot(tri, l_lo)

    pairs = range(RWKV_PAIRS_PER_STEP)
    lanes = [slice(p * LANES, (p + 1) * LANES) for p in pairs]
    kst, rst, b_t, k_t, g_end, vst, gram = [], [], [], [], [], [], []
    for p in pairs:
        r_, ld_, k_, kap_, b_ = (ref[0, :, lanes[p]] for ref in (r_ref, ld_ref, k_ref, kap_ref, b_ref))
        cum = cum_all[:, lanes[p]]
        g, g_prev, g_inv = jnp.exp(cum), jnp.exp(cum - ld_), jnp.exp(-cum)
        g_end.append(g[C - 1:C, :])
        kst.append(stack(kap_ * g_prev))
        rst.append(stack(r_ * g))
        b_t.append(b_ * g_inv)
        k_t.append(k_ * g_inv)
        vst.append(stack(v_ref[0, :, lanes[p]]))
        gram.append(_dot3(jnp.concatenate([kst[p], rst[p]], axis=0),
                          jnp.concatenate([b_t[p], k_t[p]], axis=0), nt=True))
    l_p, a_ak, a_rb, a_rk = [], [], [], []
    for p in pairs:
        ga, gr = gram[p][0:2 * C], gram[p][2 * C:4 * C]
        ga_r, gr_r = pltpu.roll(ga, HEAD_DIM, 1), pltpu.roll(gr, HEAD_DIM, 1)
        l_p.append(pick(top & strict, ga, bot & strict, ga_r))
        a_ak.append(pick(top & strict, ga_r, bot & strict, ga))
        a_rb.append(pick(top & incl, gr, bot & incl, gr_r))
        a_rk.append(pick(top & incl, gr_r, bot & incl, gr))
    inv, power = [eye - l for l in l_p], list(l_p)
    for _ in range(C.bit_length() - 2):
        power = [_dot1(m, m) for m in power]
        inv = [_dot1(x, eye + m) for x, m in zip(inv, power)]
    resid = [eye - _dot3(eye + l, x) for l, x in zip(l_p, inv)]
    inv = [x + _dot1(x, rs) for x, rs in zip(inv, resid)]
    sst = [s_ref[p] for p in pairs]
    sst_t = [s.T for s in sst]
    rhs = [_dot3(jnp.concatenate([kst[p], a_ak[p]], axis=1), jnp.concatenate([sst_t[p], vst[p]], axis=0))
           for p in pairs]
    u = [-_dot3(inv[p], rhs[p]) for p in pairs]
    for p in pairs:
        y = _dot1(jnp.concatenate([rst[p], a_rb[p], a_rk[p]], axis=1),
                  jnp.concatenate([sst_t[p], u[p], vst[p]], axis=0))
        y_ref[0, :, lanes[p]] = y[0:C] + y[C:2 * C]
    for p in pairs:
        uv = jnp.concatenate([u[p], vst[p]], axis=0)
        bk = jnp.concatenate([stack(b_t[p] * g_end[p]), stack(k_t[p] * g_end[p])], axis=0)
        s_ref[p] = sst[p] * g_end[p] + _dot3(uv.T, bk)

    @pl.when(c == pl.num_programs(2) - 1)
    def _():
        for p in range(RWKV_PAIRS_PER_STEP):
            st_ref[0, 2 * p] = s_ref[p][0:HEAD_DIM, 0:HEAD_DIM]
            st_ref[0, 2 * p + 1] = s_ref[p][HEAD_DIM:, HEAD_DIM:]


def rwkv_chunk_scan(r, ld, k, v, kap, b, s0):
    nbatch, seq = r.shape[:2]
    width = RWKV_PAIRS_PER_STEP * LANES
    blk = pl.BlockSpec((1, RWKV_CHUNK, width), lambda bi, pg, c: (bi, c, pg))
    st = pl.BlockSpec((1, 2 * RWKV_PAIRS_PER_STEP, HEAD_DIM, HEAD_DIM), lambda bi, pg, c: (bi, pg, 0, 0))
    return pl.pallas_call(
        _rwkv_chunk_body,
        grid=(nbatch, RWKV_WIDTH // width, seq // RWKV_CHUNK),
        in_specs=[blk] * 6 + [st],
        out_specs=[blk, st],
        out_shape=[jax.ShapeDtypeStruct((nbatch, seq, RWKV_WIDTH), f32),
                   jax.ShapeDtypeStruct((nbatch, RWKV_HEADS, HEAD_DIM, HEAD_DIM), f32)],
        scratch_shapes=[pltpu.VMEM((RWKV_PAIRS_PER_STEP, 2 * RWKV_CHUNK, LANES), f32)],
        compiler_params=_cparams(("arbitrary", "arbitrary", "arbitrary")),
        name="rwkv_chunk_scan",
    )(r, ld, k, v, kap, b, s0)


def _rwkv_post_body(y_ref, g_ref, bonus_ref, vec_ref, e_ref, o_ref):
    y = y_ref[...]
    e = e_ref[...]
    ln_g, ln_b = vec_ref[5:6, :], vec_ref[6:7, :]
    mu = _split_dot(y, e) * (1.0 / HEAD_DIM)
    yc = y - mu
    var = _split_dot(yc * yc, e) * (1.0 / HEAD_DIM)
    o_ref[...] = (yc * lax.rsqrt(var + GN_EPS) * ln_g + ln_b + bonus_ref[...]) * g_ref[...]


def rwkv_output(y, g, bonus, vecs, e, tc):
    m = y.shape[0]
    row = lambda i: (i, 0)
    const = lambda i: (0, 0)
    return pl.pallas_call(
        _rwkv_post_body,
        grid=(m // tc,),
        in_specs=[pl.BlockSpec((tc, RWKV_WIDTH), row)] * 3 +
                 [pl.BlockSpec((SUBLANES, RWKV_WIDTH), const), pl.BlockSpec((RWKV_WIDTH, RWKV_WIDTH), const)],
        out_specs=pl.BlockSpec((tc, RWKV_WIDTH), row),
        out_shape=jax.ShapeDtypeStruct((m, RWKV_WIDTH), f32),
        compiler_params=_cparams(("arbitrary",)),
        name="rwkv_output",
    )(y, g, bonus, vecs, e)


def rwkv7(p_all, prev, s0, wts, nbatch, seq, tc_prep):
    prev_exp = None
    if prev is not None:
        prev_exp = jnp.repeat(jnp.pad(prev, ((0, 0), (0, RW_PAD - RWKV_PROJ))), seq, axis=0)
    r, ld, k, v, kap, b, g, bonus = rwkv_prepare(
        p_all, prev_exp, wts['mu'], wts['vecs'], wts['w2p'], wts['a2p'], wts['g2p'], wts['e'], seq, tc_prep)
    pad = (-seq) % RWKV_CHUNK
    sh = lambda z: jnp.pad(z.reshape(nbatch, seq, RWKV_WIDTH), ((0, 0), (0, pad), (0, 0)))
    y, s_new = rwkv_chunk_scan(sh(r), sh(ld), sh(k), sh(v), sh(kap), sh(b), s0)
    o = rwkv_output(y[:, :seq].reshape(nbatch * seq, RWKV_WIDTH), g, bonus, wts['vecs'], wts['e'], tc_prep)
    return o, s_new


def _compress_head(x_ref, nchunk, wcat_ref, pe_ref, w1f_ref, b1_ref, w2_ref, b2_ref, acc_ref):
    kv2 = 2 * HEAD_DIM
    pe_term = jnp.concatenate([_dot(pe_ref[k], w1f_ref[k])[0:1, :] for k in range(2)], axis=1)
    bias = pe_term + b1_ref[...]
    acc = jnp.zeros((nchunk, 4 * kv2), f32)
    half = CMP_STRIDE // 2
    for r in range(half):
        xr = jnp.concatenate([x_ref[pl.ds(r, nchunk, stride=CMP_STRIDE), :],
                              x_ref[pl.ds(r + half, nchunk, stride=CMP_STRIDE), :]], axis=1)
        acc = acc + _dot(xr.astype(bf16), wcat_ref[r])
    acc_ref[0:nchunk, :] = acc
    acc_ref[nchunk:nchunk + SUBLANES, :] = jnp.zeros((SUBLANES, 4 * kv2), f32)
    first = jnp.concatenate([acc_ref[0:nchunk, 0:kv2], acc_ref[0:nchunk, 2 * kv2:3 * kv2]], axis=1)
    second = jnp.concatenate([acc_ref[1:nchunk + 1, kv2:2 * kv2], acc_ref[1:nchunk + 1, 3 * kv2:4 * kv2]], axis=1)
    hid = jax.nn.gelu(first + second + bias)
    return _dot(hid.astype(bf16), w2_ref[...]) + b2_ref[...]


def _compress_prompt_body(x_ref, wcat_ref, pe_ref, w1f_ref, b1_ref, w2_ref, b2_ref, o_ref, acc_ref, *, nchunk):
    o_ref[...] = _compress_head(x_ref, nchunk, wcat_ref, pe_ref, w1f_ref, b1_ref, w2_ref, b2_ref, acc_ref)


def _compress_weights(lp):
    w1 = lp['cmp_w1']
    z = jnp.zeros((CMP_STRIDE, HEAD_DIM, CMP_HIDDEN), f32)
    key_rows = jnp.concatenate([w1[0, :CMP_STRIDE], w1[0, CMP_STRIDE:], z, z], axis=-1)
    val_rows = jnp.concatenate([z, z, w1[1, :CMP_STRIDE], w1[1, CMP_STRIDE:]], axis=-1)
    wcat = jnp.concatenate([key_rows, val_rows], axis=1)
    wcat = jnp.concatenate([wcat[:CMP_STRIDE // 2], wcat[CMP_STRIDE // 2:]], axis=1).astype(bf16)
    pe = jnp.broadcast_to(lp['cmp_pe'].reshape(2, 1, CMP_BLOCK * HEAD_DIM), (2, SUBLANES, CMP_BLOCK * HEAD_DIM))
    w2 = lp['cmp_w2']
    zz = jnp.zeros((CMP_HIDDEN, HEAD_DIM), f32)
    w2bd = jnp.concatenate([jnp.concatenate([w2[0], zz], axis=1), jnp.concatenate([zz, w2[1]], axis=1)], axis=0)
    return {'wcat': wcat, 'pe': pe.astype(bf16),
            'w1f': w1.reshape(2, CMP_BLOCK * HEAD_DIM, CMP_HIDDEN).astype(bf16),
            'b1': lp['cmp_b1'].reshape(1, 2 * CMP_HIDDEN), 'w2bd': w2bd.astype(bf16),
            'b2': lp['cmp_b2'].reshape(1, 2 * HEAD_DIM)}


def _cmp_weight_specs(nidx):
    c2 = lambda *a: (0, 0)
    c3 = lambda *a: (0, 0, 0)
    return [pl.BlockSpec((CMP_STRIDE // 2, 4 * HEAD_DIM, 4 * CMP_HIDDEN), c3),
            pl.BlockSpec((2, SUBLANES, CMP_BLOCK * HEAD_DIM), c3),
            pl.BlockSpec((2, CMP_BLOCK * HEAD_DIM, CMP_HIDDEN), c3),
            pl.BlockSpec((1, 2 * CMP_HIDDEN), c2),
            pl.BlockSpec((2 * CMP_HIDDEN, 2 * HEAD_DIM), c2),
            pl.BlockSpec((1, 2 * HEAD_DIM), c2)]


def compress_prompt(p_all, cw, nbatch, seq):
    nchunk = seq // CMP_STRIDE
    kv2 = 2 * HEAD_DIM
    return pl.pallas_call(
        functools.partial(_compress_prompt_body, nchunk=nchunk),
        grid=(nbatch, NSA_KV_HEADS),
        in_specs=[pl.BlockSpec((seq, kv2), lambda b, h: (b, KVC0 // kv2 + h))] + _cmp_weight_specs(2),
        out_specs=pl.BlockSpec((nchunk, kv2), lambda b, h: (b, h)),
        out_shape=jax.ShapeDtypeStruct((nbatch * nchunk, KV_WIDTH), f32),
        scratch_shapes=[pltpu.VMEM((nchunk + SUBLANES, 4 * CMP_HIDDEN), f32)],
        compiler_params=_cparams(("arbitrary", "arbitrary")),
        name="compress_prompt",
    )(p_all, cw['wcat'], cw['pe'], cw['w1f'], cw['b1'], cw['w2bd'], cw['b2'])


KEY_TILE = 256


def _masked_softmax(s, mask):
    s = jnp.where(mask, s, NEG_INF)
    m = jnp.max(s, axis=-1, keepdims=True)
    e = jnp.where(mask, jnp.exp(s - m), 0.0)
    return e / jnp.maximum(jnp.sum(e, axis=-1, keepdims=True), 1e-30)


def _nsa_prompt_body(q_ref, kc_ref, ks_ref, kw_ref, g_ref, wsel_ref, eexp_ref, o_ref, *, nsel, ncmp):
    i = pl.program_id(2)
    qb, grp = Q_BLOCK, NSA_GROUP
    rows = grp * qb
    q = q_ref[...]
    qs = jnp.concatenate([q[:, g * HEAD_DIM:(g + 1) * HEAD_DIM] for g in range(grp)], axis=0) * (HEAD_DIM ** -0.5)
    qp = jnp.concatenate([qs, jnp.zeros_like(qs)], axis=1).astype(bf16)
    qpos = i * qb + lax.broadcasted_iota(jnp.int32, (rows, 1), 0) % qb

    kc = kc_ref[...].astype(bf16)
    s_c = _dot_nt(qp, kc)
    cend = lax.broadcasted_iota(jnp.int32, (1, ncmp), 1) * CMP_STRIDE + (CMP_BLOCK - 1)
    p_c = _masked_softmax(s_c, cend <= qpos)
    o_c = _dot(p_c.astype(bf16), kc)
    imp = p_c[0:qb] + p_c[qb:2 * qb] + p_c[2 * qb:3 * qb] + p_c[3 * qb:4 * qb]
    imp_hi = imp.astype(bf16)
    imp_lo = (imp - imp_hi.astype(f32)).astype(bf16)
    wsel = wsel_ref[...]
    p_slc = _dot_nt(wsel, imp_hi) + _dot_nt(wsel, imp_lo)

    blk = lax.broadcasted_iota(jnp.int32, (nsel, qb), 0)
    qpos_t = i * qb + lax.broadcasted_iota(jnp.int32, (nsel, qb), 1)
    cur = qpos_t // SEL_BLOCK
    forced = (blk == 0) | (blk == cur) | (blk == cur - 1)
    score = jnp.where(forced, FORCE_SCORE, p_slc)
    score = jnp.where(blk * SEL_BLOCK <= qpos_t, score, -1.0)
    rank = jnp.zeros((nsel, qb), f32)
    for jp in range(nsel):
        row = score[jp:jp + 1, :]
        ahead = (row > score) | ((row == score) & (blk > jp))
        rank = rank + jnp.where(ahead, 1.0, 0.0)
    sel_t = jnp.where((rank < float(min(SEL_TOPK, nsel))) & (score >= 0.0), 1.0, 0.0)
    sel = sel_t.T.astype(bf16)

    qpos_q = qpos[0:qb]
    ones_keys = lax.broadcasted_iota(jnp.int32, (1, 2 * HEAD_DIM), 1) < HEAD_DIM

    def attend(ref, kt, carry, mask_fn):
        k0 = pl.multiple_of(kt * KEY_TILE, KEY_TILE)
        kt_tile = ref[pl.ds(k0, KEY_TILE), :].astype(bf16)
        ones_v = jnp.where(ones_keys, jnp.ones_like(kt_tile), kt_tile)
        mask = mask_fn(kt, k0 + lax.broadcasted_iota(jnp.int32, (1, KEY_TILE), 1))
        new = []
        for g in range(grp):
            m, acc = carry[g]
            s = jnp.where(mask, _dot_nt(qp[g * qb:(g + 1) * qb], kt_tile), NEG_INF)
            m_new = jnp.maximum(m, jnp.max(s, axis=-1, keepdims=True))
            e = jnp.where(mask, jnp.exp(s - m_new), 0.0)
            new.append((m_new, jnp.exp(m - m_new) * acc + _dot(e.astype(bf16), ones_v)))
        return tuple(new)

    def sel_mask(kt, kpos):
        return (_dot(sel, eexp_ref[kt]) > 0.5) & (kpos <= qpos_q)

    def win_mask(kt, kpos):
        diff = qpos_q - kpos
        return (diff >= 0) & (diff <= WINDOW)

    per_tile = KEY_TILE // qb
    init = tuple((jnp.full((qb, 1), NEG_INF, f32), jnp.zeros((qb, 2 * HEAD_DIM), f32)) for _ in range(grp))
    res_s = lax.fori_loop(0, i // per_tile + 1, lambda kt, c: attend(ks_ref, kt, c, sel_mask), init)
    res_w = lax.fori_loop(jnp.maximum(i - WINDOW // qb, 0) // per_tile, i // per_tile + 1,
                          lambda kt, c: attend(kw_ref, kt, c, win_mask), init)
    finish = lambda res: jnp.concatenate([acc / jnp.maximum(acc[:, 0:1], 1e-30) for _, acc in res], axis=0)
    o_s, o_w = finish(res_s), finish(res_w)

    gates = g_ref[0, 0]
    gate = lambda br: jnp.concatenate([gates[:, 3 * g + br:3 * g + br + 1] for g in range(grp)], axis=0)
    out = gate(0) * o_c + gate(1) * o_s + gate(2) * o_w
    o_ref[...] = jnp.concatenate([out[g * qb:(g + 1) * qb, HEAD_DIM:] for g in range(grp)], axis=1)


def nsa_prompt(p_all, kv_cmp, gates, nbatch, seq):
    nqb, nsel, ncmp = seq // Q_BLOCK, seq // SEL_BLOCK, seq // CMP_STRIDE
    kv2 = 2 * HEAD_DIM
    cidx = jnp.arange(ncmp)[None, :] - (SEL_BLOCK // CMP_STRIDE) * jnp.arange(nsel)[:, None]
    mult = jnp.array([1, 2, 2, 2, 1], f32)
    wsel = jnp.where((cidx >= 0) & (cidx <= 4), mult[jnp.clip(cidx, 0, 4)], 0.0).astype(bf16)
    ntile = seq // KEY_TILE
    key_blk = (jnp.arange(ntile)[:, None, None] * KEY_TILE + jnp.arange(KEY_TILE)[None, None, :]) // SEL_BLOCK
    eexp = (key_blk == jnp.arange(nsel)[None, :, None]).astype(bf16)
    return pl.pallas_call(
        functools.partial(_nsa_prompt_body, nsel=nsel, ncmp=ncmp),
        grid=(nbatch, NSA_KV_HEADS, nqb),
        in_specs=[pl.BlockSpec((Q_BLOCK, NSA_GROUP * HEAD_DIM),
                               lambda b, h, i: (b * nqb + i, Q0 // (NSA_GROUP * HEAD_DIM) + h)),
                  pl.BlockSpec((ncmp, kv2), lambda b, h, i: (b, h)),
                  pl.BlockSpec((seq, kv2), lambda b, h, i: (b, KVS0 // kv2 + h)),
                  pl.BlockSpec((seq, kv2), lambda b, h, i: (b, KVW0 // kv2 + h)),
                  pl.BlockSpec((1, 1, Q_BLOCK, 16), lambda b, h, i: (b, h, i, 0)),
                  pl.BlockSpec((nsel, ncmp), lambda b, h, i: (0, 0)),
                  pl.BlockSpec((ntile, nsel, KEY_TILE), lambda b, h, i: (0, 0, 0))],
        out_specs=pl.BlockSpec((Q_BLOCK, NSA_GROUP * HEAD_DIM), lambda b, h, i: (b * nqb + i, h)),
        out_shape=jax.ShapeDtypeStruct((nbatch * seq, NSA_WIDTH), f32),
        compiler_params=_cparams(("arbitrary", "arbitrary", "arbitrary")),
        name="nsa_prompt",
    )(p_all, kv_cmp, p_all, p_all, gates, wsel, eexp)


def _branch_gates(p_all, nbatch, seq):
    g = p_all[:, GN0:GN0 + 3 * NSA_HEADS].reshape(nbatch, seq, NSA_KV_HEADS, 3 * NSA_GROUP)
    return jnp.pad(jnp.transpose(g, (0, 2, 1, 3)), ((0, 0), (0, 0), (0, 0), (0, 16 - 3 * NSA_GROUP)))


PAGES_PER_STEP = 16
CMP_HALVES = 2
TAIL_ROWS = CMP_STRIDE


def _compress_sample_body(pt_ref, *refs, nchunk, half_rows):
    npg = PAGES_PER_STEP
    pages, (nxt_ref, newc_ref) = refs[:npg], refs[npg:npg + 2]
    wcat_ref, pe_ref, w1f_ref, b1_ref, w2_ref, b2_ref, o_ref, xs_ref, acc_ref = refs[npg + 2:]
    half, s = pl.program_id(1), pl.program_id(2)
    kv2 = 2 * HEAD_DIM
    for k in range(npg):
        base = pl.multiple_of((s * npg + k) * PAGE_SIZE, PAGE_SIZE)
        for h in range(NSA_KV_HEADS):
            xs_ref[h, pl.ds(base, PAGE_SIZE), :] = pages[k][0, :, h * kv2:(h + 1) * kv2]

    @pl.when(s == pl.num_programs(2) - 1)
    def _():
        nblk = half_rows // CMP_STRIDE
        pad_rows = xs_ref.shape[1] - half_rows - TAIL_ROWS
        for h in range(NSA_KV_HEADS):
            lanes = slice(h * kv2, (h + 1) * kv2)
            tail = jnp.where(half == CMP_HALVES - 1, newc_ref[0, :, lanes], nxt_ref[0, :, lanes])
            xs_ref[h, half_rows:half_rows + TAIL_ROWS, :] = tail
            xs_ref[h, half_rows + TAIL_ROWS:, :] = jnp.zeros((pad_rows, kv2), f32)
            out = _compress_head(xs_ref.at[h], nchunk, wcat_ref, pe_ref, w1f_ref, b1_ref, w2_ref, b2_ref, acc_ref)
            o_ref[0, :, lanes] = out[0:nblk]


def compress_sample(cache, page_table, newc, cw):
    nbatch, npages = page_table.shape
    npg = PAGES_PER_STEP
    half_pages = npages // CMP_HALVES
    half_rows = half_pages * PAGE_SIZE
    nsteps = half_pages // npg
    nchunk = half_rows // CMP_STRIDE + SUBLANES
    page_spec = lambda k: pl.BlockSpec(
        (1, PAGE_SIZE, KV_WIDTH), lambda b, hf, s, pt: (pt[b, hf * half_pages + s * npg + k], 0, 0))
    nxt_spec = pl.BlockSpec(
        (1, TAIL_ROWS, KV_WIDTH), lambda b, hf, s, pt: (pt[b, jnp.minimum((hf + 1) * half_pages, npages - 1)], 0, 0))
    return pl.pallas_call(
        functools.partial(_compress_sample_body, nchunk=nchunk, half_rows=half_rows),
        grid_spec=pltpu.PrefetchScalarGridSpec(
            num_scalar_prefetch=1,
            grid=(nbatch, CMP_HALVES, nsteps),
            in_specs=[page_spec(k) for k in range(npg)] + [nxt_spec] +
                     [pl.BlockSpec((1, TAIL_ROWS, KV_WIDTH), lambda b, hf, s, pt: (b, 0, 0))] + _cmp_weight_specs(4),
            out_specs=pl.BlockSpec((1, half_rows // CMP_STRIDE, KV_WIDTH), lambda b, hf, s, pt: (b, hf, 0)),
            scratch_shapes=[pltpu.VMEM((NSA_KV_HEADS, nchunk * CMP_STRIDE, 2 * HEAD_DIM), f32),
                            pltpu.VMEM((nchunk + SUBLANES, 4 * CMP_HIDDEN), f32)]),
        out_shape=jax.ShapeDtypeStruct((nbatch, npages * PAGE_SIZE // CMP_STRIDE, KV_WIDTH), f32),
        compiler_params=_cparams(("arbitrary", "arbitrary", "arbitrary")),
        name="compress_sample",
    )(page_table, *([cache] * (npg + 1)), newc, cw['wcat'], cw['pe'], cw['w1f'], cw['b1'], cw['w2bd'], cw['b2'])


def _split_dot_r(e, x):
    hi = x.astype(bf16)
    lo = (x - hi.astype(f32)).astype(bf16)
    return _dot(e, hi) + _dot(e, lo)


def _softmax_rows(s, mask):
    s = jnp.where(mask, s, NEG_INF)
    m = jnp.max(s, axis=0, keepdims=True)
    e = jnp.where(mask, jnp.exp(s - m), 0.0)
    return e / jnp.maximum(jnp.sum(e, axis=0, keepdims=True), 1e-30)


def _row_to_col(row):
    return jnp.broadcast_to(row, (SUBLANES, LANES)).T[:, 0:1]


def _nsa_sample_body(pt_ref, *refs, tnew, nsel, nselp):
    npg = PAGES_PER_STEP
    pages = refs[:npg]
    (qb_ref, qr_ref, kc_ref, win_ref, neww_ref, news_ref, g_ref, wsel_ref, gm_ref, pick_ref, eexp_ref, o_ref,
     selq_ref, score_ref, m_ref, l_ref, acc_ref, oc_ref, ow_ref) = refs[npg:]
    s = pl.program_id(1)
    nselq = selq_ref.shape[1]
    rows_per_head = LANES // NSA_KV_HEADS
    qb = qb_ref[0]
    col = lax.broadcasted_iota(jnp.int32, (1, LANES), 1)
    tq = col % tnew
    qpos = PAST_LEN + tq
    pad_tail = lambda ref: jnp.concatenate(
        [ref[0], jnp.zeros((LANES - TAIL_ROWS, KV_WIDTH), f32)], axis=0).astype(bf16)

    @pl.when(s == 0)
    def _():
        kc = kc_ref[0].astype(bf16)
        ncmp = kc.shape[0]
        cend = lax.broadcasted_iota(jnp.int32, (ncmp, 1), 0) * CMP_STRIDE + (CMP_BLOCK - 1)
        p_c = _softmax_rows(_dot(kc, qb), cend <= qpos)
        oc_ref[...] = _dot(p_c.T.astype(bf16), kc)
        imp = _split_dot(p_c, gm_ref[...])
        p_slc = _split_dot_r(wsel_ref[...], imp)
        blk = lax.broadcasted_iota(jnp.int32, (nselp, LANES), 0)
        cur = qpos // SEL_BLOCK
        forced = (blk == 0) | (blk == cur) | (blk == cur - 1)
        score = jnp.where(forced, FORCE_SCORE, p_slc)
        score = jnp.where((blk * SEL_BLOCK <= qpos) & (blk < nsel), score, -1.0)
        score_ref[...] = score

        def rank_group(gi, rank):
            rows = score_ref[pl.ds(pl.multiple_of(gi * SUBLANES, SUBLANES), SUBLANES), :]
            for j in range(SUBLANES):
                row = rows[j:j + 1, :]
                ahead = (row > score) | ((row == score) & (blk > gi * SUBLANES + j))
                rank = rank + jnp.where(ahead, 1.0, 0.0)
            return rank

        rank = lax.fori_loop(0, nselp // SUBLANES, rank_group, jnp.zeros((nselp, LANES), f32))
        sel_t = jnp.where((rank < float(SEL_TOPK)) & (score >= 0.0), 1.0, 0.0)
        selq_ref[...] = jnp.concatenate([sel_t, jnp.zeros((nselq - nselp, LANES), f32)], axis=0).T

        kw = jnp.concatenate([win_ref[0].astype(bf16), pad_tail(neww_ref)], axis=0)
        idx = lax.broadcasted_iota(jnp.int32, (kw.shape[0], 1), 0)
        wbuf = win_ref.shape[1]
        p_w = _softmax_rows(_dot(kw, qb), (idx <= wbuf + tq) & (idx >= wbuf + tq - WINDOW))
        ow_ref[...] = _dot(p_w.T.astype(bf16), kw)

        m_ref[...] = jnp.full(m_ref.shape, NEG_INF, f32)
        l_ref[...] = jnp.zeros(l_ref.shape, f32)
        acc_ref[...] = jnp.zeros(acc_ref.shape, f32)

    def fold(rows, sc, mask, e_dot_v):
        sc = jnp.where(mask, sc, NEG_INF)
        m_old = m_ref[rows, 0:1]
        m_new = jnp.maximum(m_old, jnp.max(sc, axis=1, keepdims=True))
        alpha = jnp.exp(m_old - m_new)
        e = jnp.where(mask, jnp.exp(sc - m_new), 0.0)
        l_ref[rows, :] = jnp.broadcast_to(alpha * l_ref[rows, 0:1] + jnp.sum(e, axis=1, keepdims=True),
                                          (rows_per_head, LANES))
        acc_ref[rows, :] = alpha * acc_ref[rows, :] + e_dot_v(e.astype(bf16))
        m_ref[rows, :] = jnp.broadcast_to(m_new, (rows_per_head, LANES))

    for h in range(NSA_KV_HEADS):
        rows = slice(h * rows_per_head, (h + 1) * rows_per_head)
        k_t = jnp.concatenate([pages[k][0, h, 0] for k in range(npg)], axis=1).astype(bf16)
        v_t = jnp.concatenate([pages[k][0, h, 1] for k in range(npg)], axis=1).astype(bf16)
        step_sel = _dot(selq_ref[rows, :].astype(bf16), pick_ref[s])
        picked = _dot(step_sel.astype(bf16), eexp_ref[...]) > 0.5
        fold(rows, _dot(qr_ref[0, rows, :], k_t), picked, lambda e: _dot_nt(e, v_t))

    @pl.when(s == pl.num_programs(1) - 1)
    def _():
        ns = news_ref[0]
        kidx = lax.broadcasted_iota(jnp.int32, (1, TAIL_ROWS), 1)
        t_row = lax.broadcasted_iota(jnp.int32, (rows_per_head, 1), 0) % tnew
        nb = nsel - 1
        for h in range(NSA_KV_HEADS):
            rows = slice(h * rows_per_head, (h + 1) * rows_per_head)
            k_n = ns[:, h * 2 * HEAD_DIM:h * 2 * HEAD_DIM + HEAD_DIM].astype(bf16)
            v_n = ns[:, h * 2 * HEAD_DIM + HEAD_DIM:(h + 1) * 2 * HEAD_DIM].astype(bf16)
            mask = (selq_ref[rows, nb:nb + 1] > 0.5) & (kidx <= t_row)
            fold(rows, _dot_nt(qr_ref[0, rows, :], k_n), mask, lambda e: _dot(e, v_n))
            o_s = acc_ref[rows, :] / jnp.maximum(l_ref[rows, 0:1], 1e-30)
            vals = slice(h * 2 * HEAD_DIM + HEAD_DIM, (h + 1) * 2 * HEAD_DIM)
            o_ref[0, rows, :] = (g_ref[0, 0, rows, 0:HEAD_DIM] * oc_ref[rows, vals] + g_ref[0, 1, rows, 0:HEAD_DIM] * o_s
                                 + g_ref[0, 2, rows, 0:HEAD_DIM] * ow_ref[rows, vals])


def nsa_sample(cache_sel_t, page_table, qblk, qrows, kv_cmp, win_buf, neww, news, gates, tnew):
    nbatch, npages = page_table.shape
    npg = PAGES_PER_STEP
    ncmp = kv_cmp.shape[1]
    nsel = npages * (PAGE_SIZE // SEL_BLOCK) + 1
    nselp = -(-nsel // SUBLANES) * SUBLANES
    cidx = jnp.arange(ncmp)[None, :] - (SEL_BLOCK // CMP_STRIDE) * jnp.arange(nselp)[:, None]
    mult = jnp.array([1, 2, 2, 2, 1], f32)
    wsel = jnp.where((cidx >= 0) & (cidx <= 4), mult[jnp.clip(cidx, 0, 4)], 0.0).astype(bf16)
    c = jnp.arange(LANES)
    same = (c[:, None] // (NSA_GROUP * tnew) == c[None, :] // (NSA_GROUP * tnew)) & (c[:, None] % tnew == c[None, :] % tnew)
    gm = same.astype(bf16)
    nsteps = npages // npg
    blocks_per_step = npg * (PAGE_SIZE // SEL_BLOCK)
    nselq = -(-nselp // LANES) * LANES
    blk_of = jnp.arange(nsteps)[:, None, None] * blocks_per_step + jnp.arange(LANES)[None, None, :]
    pick = ((jnp.arange(nselq)[None, :, None] == blk_of) & (jnp.arange(LANES) < blocks_per_step)).astype(bf16)
    eexp = (jnp.arange(LANES)[:, None] == jnp.arange(npg * PAGE_SIZE)[None, :] // SEL_BLOCK).astype(bf16)
    b3 = lambda b, s, pt: (b, 0, 0)
    c2 = lambda b, s, pt: (0, 0)
    page_spec = lambda k: pl.BlockSpec((1, NSA_KV_HEADS, 2, HEAD_DIM, PAGE_SIZE),
                                       lambda b, s, pt: (pt[b, s * npg + k], 0, 0, 0, 0))
    wbuf = win_buf.shape[1]
    return pl.pallas_call(
        functools.partial(_nsa_sample_body, tnew=tnew, nsel=nsel, nselp=nselp),
        grid_spec=pltpu.PrefetchScalarGridSpec(
            num_scalar_prefetch=1,
            grid=(nbatch, nsteps),
            in_specs=[page_spec(k) for k in range(npg)] + [
                pl.BlockSpec((1, KV_WIDTH, LANES), b3),
                pl.BlockSpec((1, LANES, HEAD_DIM), b3),
                pl.BlockSpec((1, ncmp, KV_WIDTH), b3),
                pl.BlockSpec((1, wbuf, KV_WIDTH), b3),
                pl.BlockSpec((1, TAIL_ROWS, KV_WIDTH), b3),
                pl.BlockSpec((1, TAIL_ROWS, KV_WIDTH), b3),
                pl.BlockSpec((1, 3, LANES, LANES), lambda b, s, pt: (b, 0, 0, 0)),
                pl.BlockSpec((nselp, ncmp), c2),
                pl.BlockSpec((LANES, LANES), c2),
                pl.BlockSpec((nsteps, nselq, LANES), lambda b, s, pt: (0, 0, 0)),
                pl.BlockSpec((LANES, npg * PAGE_SIZE), c2)],
            out_specs=pl.BlockSpec((1, LANES, HEAD_DIM), b3),
            scratch_shapes=[pltpu.VMEM((LANES, nselq), f32), pltpu.VMEM((nselp, LANES), f32),
                            pltpu.VMEM((LANES, LANES), f32), pltpu.VMEM((LANES, LANES), f32),
                            pltpu.VMEM((LANES, HEAD_DIM), f32),
                            pltpu.VMEM((LANES, KV_WIDTH), f32), pltpu.VMEM((LANES, KV_WIDTH), f32)]),
        out_shape=jax.ShapeDtypeStruct((nbatch, LANES, HEAD_DIM), f32),
        compiler_params=_cparams(("arbitrary", "arbitrary")),
        name="nsa_sample",
    )(page_table, *([cache_sel_t] * npg), qblk, qrows, kv_cmp, win_buf, neww, news, gates, wsel, gm, pick, eexp)


def _sample_nsa_inputs(p_all, b, t):
    q = p_all[:, Q0:Q0 + NSA_WIDTH].reshape(b, t, NSA_KV_HEADS, NSA_GROUP, HEAD_DIM) * (HEAD_DIM ** -0.5)
    qt = jnp.transpose(q, (0, 2, 4, 3, 1)).reshape(b, NSA_KV_HEADS, HEAD_DIM, NSA_GROUP * t)
    qt = jnp.pad(qt, ((0, 0), (0, 0), (0, HEAD_DIM), (0, 0)))
    eye = jnp.eye(NSA_KV_HEADS, dtype=f32)
    qblk = (qt[:, :, :, None, :] * eye[None, :, None, :, None]).reshape(b, KV_WIDTH, LANES).astype(bf16)
    qrows = jnp.transpose(q, (0, 2, 3, 1, 4)).reshape(b, LANES, HEAD_DIM).astype(bf16)
    g = p_all[:, GN0:GN0 + 3 * NSA_HEADS].reshape(b, t, NSA_KV_HEADS, NSA_GROUP, 3)
    g = jnp.transpose(g, (0, 4, 2, 3, 1)).reshape(b, 3, LANES)
    gates = jnp.broadcast_to(g[..., None], (b, 3, LANES, LANES))
    tail = lambda c0: jnp.pad(p_all[:, c0:c0 + KV_WIDTH].reshape(b, t, KV_WIDTH), ((0, 0), (0, TAIL_ROWS - t), (0, 0)))
    return qblk, qrows, gates, tail(KVC0), tail(KVS0), tail(KVW0)


def _prepare_weights(lp):
    w_in = lp['w_in']
    nsa_end = NSA_WIDTH + 3 * KV_WIDTH
    w_all = jnp.concatenate([
        w_in[:, NSA_PROJ:NSA_PROJ + RWKV_PROJ], w_in[:, nsa_end:NSA_PROJ],
        jnp.zeros((D_MODEL, RW_PAD - RWKV_PROJ - 3 * NSA_HEADS), w_in.dtype),
        w_in[:, :nsa_end], w_in[:, NSA_PROJ + RWKV_PROJ:]], axis=1).astype(bf16)
    pad_rows = lambda w, r0: jnp.zeros((LORA_PAD, RWKV_WIDTH), f32).at[r0:r0 + w.shape[0]].set(w).astype(bf16)
    head = jnp.arange(RWKV_WIDTH) // HEAD_DIM
    vecs = jnp.stack([lp['rwkv_w0'], lp['rwkv_a0'], lp['rwkv_k_k'], lp['rwkv_k_a'],
                      lp['rwkv_r_k'].reshape(RWKV_WIDTH), lp['rwkv_ln_g'], lp['rwkv_ln_b'],
                      jnp.zeros((RWKV_WIDTH,), f32)])
    return {
        'w_all': w_all,
        'mu': jnp.pad(lp['rwkv_mu'], (0, RW_PAD - RWKV_PROJ)).reshape(1, RW_PAD),
        'vecs': vecs,
        'w2p': pad_rows(lp['rwkv_w2'], 0),
        'a2p': pad_rows(lp['rwkv_a2'], DECAY_LORA),
        'g2p': pad_rows(lp['rwkv_g2'], DECAY_LORA + AAA_LORA),
        'e': (head[:, None] == head[None, :]).astype(bf16),
        'p_nsa': lp['p_nsa'].astype(bf16), 'p_rwkv': lp['p_rwkv'].astype(bf16),
        'w_out': lp['w_out'].astype(bf16),
        'mlp_w1': lp['mlp_w1'].astype(bf16), 'mlp_w2': lp['mlp_w2'].astype(bf16),
    }


def _group_forward(x, pos_rows, prev, s0, lp, wts, final_g, tm, tc_prep, o_nsa_fn):
    nbatch, seq = x.shape[:2]
    x2d = x.reshape(nbatch * seq, D_MODEL)
    tq, tkv = _rot_tables(pos_rows)
    p_all = input_projection(x2d, lp['norm1_g'], wts['w_all'], tq, tkv, tm)
    o_rw, s_new = rwkv7(p_all, prev, s0, wts, nbatch, seq, tc_prep)
    o_nsa = o_nsa_fn(p_all)
    h, hn = merge_project(o_nsa, o_rw, p_all, x2d, wts['p_nsa'], wts['p_rwkv'], wts['w_out'], lp['norm2_g'],
                          min(tm, 256))
    y = mlp_residual_norm(hn, h, wts['mlp_w1'], wts['mlp_w2'], final_g, tm)
    return y.reshape(nbatch, seq, D_MODEL), p_all, s_new


def _kv_rows(p_all, col0, nbatch, seq):
    return p_all[:, col0:col0 + KV_WIDTH].reshape(nbatch, seq, NSA_KV_HEADS, 2, HEAD_DIM)


def sample_nsa_attention(p_all, cache_cmp, cache_sel, win_buf, page_table, cw, b, t):
    qblk, qrows, gates, newc, news, neww = _sample_nsa_inputs(p_all, b, t)
    pool = cache_cmp.shape[0]
    kv_cmp = compress_sample(cache_cmp.reshape(pool, PAGE_SIZE, KV_WIDTH), page_table, newc, cw)
    cache_sel_t = jnp.transpose(cache_sel, (0, 2, 3, 4, 1))
    o = nsa_sample(cache_sel_t, page_table, qblk, qrows, kv_cmp,
                   win_buf.reshape(b, win_buf.shape[1], KV_WIDTH), neww, news, gates, t)
    o = o.reshape(b, NSA_KV_HEADS, NSA_GROUP, t, HEAD_DIM)
    return jnp.transpose(o, (0, 3, 1, 2, 4)).reshape(b * t, NSA_WIDTH)


def kernel(x_prompt, x_sample, cache_cmp_kv, cache_sel_kv, state_nsa_win, state_rwkv, state_rwkv_shift,
           page_table, norm1_g, w_in, cmp_pe, cmp_w1, cmp_b1, cmp_w2, cmp_b2, rwkv_mu, rwkv_w0, rwkv_w2,
           rwkv_a0, rwkv_a2, rwkv_g2, rwkv_k_k, rwkv_k_a, rwkv_r_k, rwkv_ln_g, rwkv_ln_b, p_nsa, p_rwkv,
           w_out, norm2_g, mlp_w1, mlp_w2, final_g):
    l = 0
    lp = {'norm1_g': norm1_g[l], 'w_in': w_in[l], 'cmp_pe': cmp_pe[l], 'cmp_w1': cmp_w1[l],
          'cmp_b1': cmp_b1[l], 'cmp_w2': cmp_w2[l], 'cmp_b2': cmp_b2[l], 'rwkv_mu': rwkv_mu[l],
          'rwkv_w0': rwkv_w0[l], 'rwkv_w2': rwkv_w2[l], 'rwkv_a0': rwkv_a0[l], 'rwkv_a2': rwkv_a2[l],
          'rwkv_g2': rwkv_g2[l], 'rwkv_k_k': rwkv_k_k[l], 'rwkv_k_a': rwkv_k_a[l], 'rwkv_r_k': rwkv_r_k[l],
          'rwkv_ln_g': rwkv_ln_g[l], 'rwkv_ln_b': rwkv_ln_b[l], 'p_nsa': p_nsa[l], 'p_rwkv': p_rwkv[l],
          'w_out': w_out[l], 'norm2_g': norm2_g[l], 'mlp_w1': mlp_w1[l], 'mlp_w2': mlp_w2[l]}
    wts = _prepare_weights(lp)
    bp, tp = x_prompt.shape[:2]
    bs, ts = x_sample.shape[:2]
    cw = _compress_weights(lp)
    zero_state = jnp.zeros((bp, RWKV_HEADS, HEAD_DIM, HEAD_DIM), f32)

    def prompt_nsa(p_all):
        return nsa_prompt(p_all, compress_prompt(p_all, cw, bp, tp), _branch_gates(p_all, bp, tp), bp, tp)

    yp, pp, sp = _group_forward(x_prompt, jnp.arange(tp, dtype=jnp.int32), None, zero_state, lp, wts, final_g,
                                512, 256, prompt_nsa)

    def sample_nsa(p_all):
        return sample_nsa_attention(p_all, cache_cmp_kv[l], cache_sel_kv[l], state_nsa_win[l], page_table, cw, bs, ts)

    pos_s = PAST_LEN + jnp.arange(bs * ts, dtype=jnp.int32) % ts
    ys, ps, ss = _group_forward(x_sample, pos_s, state_rwkv_shift[l], state_rwkv[l], lp, wts, final_g,
                                256, 256, sample_nsa)
    win_new = [jnp.concatenate([state_nsa_win[l], _kv_rows(ps, KVW0, bs, ts)], axis=1)[:, ts:]]
    wlen = min(WINDOW, tp)
    shift = lambda p_all, nb, t: p_all.reshape(nb, t, P_WIDTH)[:, -1, RW0:RW0 + RWKV_PROJ]
    return (yp, ys,
            _kv_rows(pp, KVC0, bp, tp)[None], _kv_rows(ps, KVC0, bs, ts)[None],
            _kv_rows(pp, KVS0, bp, tp)[None], _kv_rows(ps, KVS0, bs, ts)[None],
            _kv_rows(pp, KVW0, bp, tp)[None, :, tp - wlen:], win_new[0][None],
            sp[None], ss[None],
            shift(pp, bp, tp)[None], shift(ps, bs, ts)[None])
```

```python
import functools

import jax
import jax.numpy as jnp
from jax import lax
from jax.experimental import pallas as pl
from jax.experimental.pallas import tpu as pltpu

D_MODEL = 2048
DEPTH = 1
PAST_LEN = 16384
PAGE_SIZE = 128

HEAD_DIM = 64
NSA_HEADS = D_MODEL // (2 * HEAD_DIM)
NSA_KV_HEADS = NSA_HEADS // 4
NSA_GROUP = NSA_HEADS // NSA_KV_HEADS
NSA_WIDTH = NSA_HEADS * HEAD_DIM
KV_WIDTH = NSA_KV_HEADS * 2 * HEAD_DIM
CMP_BLOCK = 32
CMP_STRIDE = 16
CMP_HIDDEN = 2 * HEAD_DIM
SEL_BLOCK = 64
SEL_TOPK = 16
WINDOW = 512
Q_BLOCK = 128
ROT_DIM = HEAD_DIM // 4
ROPE_THETA = 500000.0
RWKV_HEADS = D_MODEL // (2 * HEAD_DIM)
RWKV_WIDTH = RWKV_HEADS * HEAD_DIM
DECAY_LORA = max(32, int(round(1.8 * D_MODEL ** 0.5 / 32)) * 32)
AAA_LORA = DECAY_LORA
GATE_LORA = max(32, int(round(0.6 * D_MODEL ** 0.8 / 32)) * 32)
RWKV_PROJ = 3 * RWKV_WIDTH + DECAY_LORA + AAA_LORA + GATE_LORA
NSA_PROJ = NSA_WIDTH + 3 * KV_WIDTH + 3 * NSA_HEADS
N_IN = NSA_PROJ + RWKV_PROJ + 2 * D_MODEL
D_FF = 4 * D_MODEL
RMS_EPS = 1e-6
GN_EPS = HEAD_DIM * 1e-5
NEG_INF = -1e30
FORCE_SCORE = 1e6

LANES = 128
SUBLANES = 8
VMEM_LIMIT_BYTES = 56 * 1024 * 1024

RW0 = 0
LORA0 = 3 * RWKV_WIDTH
GN0 = RWKV_PROJ
RW_PAD = 3584
LORA_PAD = RW_PAD - LORA0
Q0 = RW_PAD
KVC0 = Q0 + NSA_WIDTH
KVS0 = KVC0 + KV_WIDTH
KVW0 = KVS0 + KV_WIDTH
GM0 = KVW0 + KV_WIDTH
P_WIDTH = GM0 + 2 * D_MODEL
PROJ_TN = 512

f32 = jnp.float32
bf16 = jnp.bfloat16


def _cparams(sem):
    return pltpu.CompilerParams(dimension_semantics=sem, vmem_limit_bytes=VMEM_LIMIT_BYTES)


def _rms(x, g):
    return x * lax.rsqrt(jnp.mean(x * x, axis=-1, keepdims=True) + RMS_EPS) * g


def _dot(a, b):
    return jnp.dot(a, b, preferred_element_type=f32)


def _dot_nt(a, b):
    return lax.dot_general(a, b, (((1,), (1,)), ((), ())), preferred_element_type=f32)


def _split_dot(x, e):
    hi = x.astype(bf16)
    lo = (x - hi.astype(f32)).astype(bf16)
    return _dot(hi, e) + _dot(lo, e)


def _rot_store(acc, tab_ref, o_ref):
    c, s1, s2 = tab_ref[0], tab_ref[1], tab_ref[2]
    for s in range(acc.shape[1] // LANES):
        x = acc[:, s * LANES:(s + 1) * LANES]
        o_ref[:, s * LANES:(s + 1) * LANES] = (
            x * c + pltpu.roll(x, LANES - ROT_DIM // 2, 1) * s1 + pltpu.roll(x, ROT_DIM // 2, 1) * s2)


def _proj_body(x_ref, g_ref, w_ref, tq_ref, tkv_ref, o_ref, xn_ref):
    j = pl.program_id(1)

    @pl.when(j == 0)
    def _():
        xn_ref[...] = _rms(x_ref[...], g_ref[...]).astype(bf16)

    acc = _dot(xn_ref[...], w_ref[...])
    gn_tile = GN0 // PROJ_TN

    @pl.when(j < gn_tile)
    def _():
        o_ref[...] = acc

    @pl.when(j == gn_tile)
    def _():
        lane = lax.broadcasted_iota(jnp.int32, acc.shape, 1)
        o_ref[...] = jnp.where(lane >= GN0 - gn_tile * PROJ_TN, jax.nn.sigmoid(acc), acc)

    @pl.when((j >= Q0 // PROJ_TN) & (j < KVC0 // PROJ_TN))
    def _():
        _rot_store(acc, tq_ref, o_ref)

    @pl.when((j >= KVC0 // PROJ_TN) & (j < GM0 // PROJ_TN))
    def _():
        _rot_store(acc, tkv_ref, o_ref)

    @pl.when(j >= GM0 // PROJ_TN)
    def _():
        o_ref[...] = jax.nn.sigmoid(acc)


def input_projection(x2d, g, w_all, tq, tkv, tm):
    m = x2d.shape[0]
    nt = tq.shape[1] // tm
    return pl.pallas_call(
        _proj_body,
        grid=(m // tm, P_WIDTH // PROJ_TN),
        in_specs=[pl.BlockSpec((tm, D_MODEL), lambda i, j: (i, 0)),
                  pl.BlockSpec((1, D_MODEL), lambda i, j: (0, 0)),
                  pl.BlockSpec((D_MODEL, PROJ_TN), lambda i, j: (0, j)),
                  pl.BlockSpec((3, tm, LANES), lambda i, j: (0, i % nt, 0)),
                  pl.BlockSpec((3, tm, LANES), lambda i, j: (0, i % nt, 0))],
        out_specs=pl.BlockSpec((tm, PROJ_TN), lambda i, j: (i, j)),
        out_shape=jax.ShapeDtypeStruct((m, P_WIDTH), f32),
        scratch_shapes=[pltpu.VMEM((tm, D_MODEL), bf16)],
        compiler_params=_cparams(("arbitrary", "arbitrary")),
        name="input_projection",
    )(x2d, g.reshape(1, D_MODEL), w_all, tq, tkv)


def _rot_tables(pos):
    half = ROT_DIM // 2
    freqs = jnp.power(jnp.float32(ROPE_THETA), -jnp.arange(half, dtype=f32) * 2.0 / ROT_DIM)
    ang = pos.astype(f32)[:, None] * freqs[None, :]
    cos, sin = jnp.cos(ang), jnp.sin(ang)
    lane = jnp.arange(LANES)

    def build(period):
        l = lane % period
        fi = l % half
        c = jnp.where(l < ROT_DIM, cos[:, fi], 1.0)
        s1 = jnp.where(l < half, -sin[:, fi], 0.0)
        s2 = jnp.where((l >= half) & (l < ROT_DIM), sin[:, fi], 0.0)
        return jnp.stack([c, s1, s2]).astype(f32)

    return build(HEAD_DIM), build(2 * HEAD_DIM)


def _merge_body(on_ref, or_ref, ga_ref, gb_ref, x_ref, pn_ref, pr_ref, wo_ref, g2_ref, h_ref, hn_ref):
    a = _dot(on_ref[...].astype(bf16), pn_ref[...])
    b = _dot(or_ref[...].astype(bf16), pr_ref[...])
    mix = ga_ref[...] * a + gb_ref[...] * b
    h = x_ref[...] + _dot(mix.astype(bf16), wo_ref[...])
    h_ref[...] = h
    hn_ref[...] = _rms(h, g2_ref[...]).astype(bf16)


def merge_project(o_nsa, o_rw, p_all, x2d, pn, pr, wo, g2, tm):
    m = x2d.shape[0]
    const = lambda i: (0, 0)
    return pl.pallas_call(
        _merge_body,
        grid=(m // tm,),
        in_specs=[pl.BlockSpec((tm, NSA_WIDTH), lambda i: (i, 0)),
                  pl.BlockSpec((tm, RWKV_WIDTH), lambda i: (i, 0)),
                  pl.BlockSpec((tm, D_MODEL), lambda i: (i, GM0 // D_MODEL)),
                  pl.BlockSpec((tm, D_MODEL), lambda i: (i, GM0 // D_MODEL + 1)),
                  pl.BlockSpec((tm, D_MODEL), lambda i: (i, 0)),
                  pl.BlockSpec((NSA_WIDTH, D_MODEL), const),
                  pl.BlockSpec((RWKV_WIDTH, D_MODEL), const),
                  pl.BlockSpec((D_MODEL, D_MODEL), const),
                  pl.BlockSpec((1, D_MODEL), const)],
        out_specs=[pl.BlockSpec((tm, D_MODEL), lambda i: (i, 0)),
                   pl.BlockSpec((tm, D_MODEL), lambda i: (i, 0))],
        out_shape=[jax.ShapeDtypeStruct((m, D_MODEL), f32),
                   jax.ShapeDtypeStruct((m, D_MODEL), bf16)],
        compiler_params=_cparams(("arbitrary",)),
        name="merge_project",
    )(o_nsa, o_rw, p_all, p_all, x2d, pn, pr, wo, g2.reshape(1, D_MODEL))


def _mlp_body(hn_ref, h_ref, w1_ref, w2_ref, fg_ref, y_ref, acc_ref):
    f = pl.program_id(1)
    u = jnp.square(jnp.maximum(_dot(hn_ref[...], w1_ref[...]), 0.0)).astype(bf16)
    contrib = _dot(u, w2_ref[...])

    @pl.when(f == 0)
    def _():
        acc_ref[...] = contrib

    @pl.when(f > 0)
    def _():
        acc_ref[...] += contrib

    @pl.when(f == pl.num_programs(1) - 1)
    def _():
        y_ref[...] = _rms(h_ref[...] + acc_ref[...], fg_ref[...])


def mlp_residual_norm(hn, h, w1, w2, fg, tm, tf=512):
    m = h.shape[0]
    return pl.pallas_call(
        _mlp_body,
        grid=(m // tm, D_FF // tf),
        in_specs=[pl.BlockSpec((tm, D_MODEL), lambda i, f: (i, 0)),
                  pl.BlockSpec((tm, D_MODEL), lambda i, f: (i, 0)),
                  pl.BlockSpec((D_MODEL, tf), lambda i, f: (0, f)),
                  pl.BlockSpec((tf, D_MODEL), lambda i, f: (f, 0)),
                  pl.BlockSpec((1, D_MODEL), lambda i, f: (0, 0))],
        out_specs=pl.BlockSpec((tm, D_MODEL), lambda i, f: (i, 0)),
        out_shape=jax.ShapeDtypeStruct((m, D_MODEL), f32),
        scratch_shapes=[pltpu.VMEM((tm, D_MODEL), f32)],
        compiler_params=_cparams(("arbitrary", "arbitrary")),
        name="mlp_residual_norm",
    )(hn, h, w1, w2, fg.reshape(1, D_MODEL))


def _rwkv_prep_body(*refs, tc, seq, has_prev):
    if has_prev:
        p_ref, halo_ref, prev_ref = refs[:3]
        refs = refs[3:]
    else:
        p_ref, halo_ref = refs[:2]
        prev_ref = None
        refs = refs[2:]
    (mu_ref, vec_ref, w2_ref, a2_ref, g2_ref, e_ref,
     r_out, d_out, k_out, v_out, kap_out, b_out, g_out, bonus_out, sh_ref) = refs
    i = pl.program_id(0)
    p = p_ref[...]
    sh_ref[0:SUBLANES, :] = halo_ref[...]
    sh_ref[SUBLANES:SUBLANES + tc, :] = p
    rolled = sh_ref[SUBLANES - 1:SUBLANES - 1 + tc, :]
    t_in_seq = (i * tc + lax.broadcasted_iota(jnp.int32, (tc, 1), 0)) % seq
    first = prev_ref[...] if has_prev else jnp.zeros_like(p)
    shifted = jnp.where(t_in_seq == 0, first, rolled)
    xm = p + (shifted - p) * mu_ref[...]
    r = xm[:, 0:RWKV_WIDTH]
    k = xm[:, RWKV_WIDTH:2 * RWKV_WIDTH]
    v = xm[:, 2 * RWKV_WIDTH:3 * RWKV_WIDTH]
    tail = xm[:, LORA0:RW_PAD]
    w0, a0, k_k, k_a, r_k = (vec_ref[n:n + 1, :] for n in range(5))
    w = -jax.nn.softplus(-(w0 + _dot(jnp.tanh(tail).astype(bf16), w2_ref[...]))) - 0.5
    a = jax.nn.sigmoid(a0 + _dot(tail.astype(bf16), a2_ref[...]))
    kk = k * k_k
    e = e_ref[...]
    kap = kk / jnp.maximum(jnp.sqrt(_split_dot(kk * kk, e)), 1e-12)
    kn = k * (1.0 + (a - 1.0) * k_a)
    r_out[...] = r
    d_out[...] = -jnp.exp(w)
    k_out[...] = kn
    v_out[...] = v
    kap_out[...] = kap
    b_out[...] = kap * a
    g_out[...] = _dot(jax.nn.sigmoid(tail).astype(bf16), g2_ref[...])
    bonus_out[...] = _split_dot(r * kn * r_k, e) * v


def rwkv_prepare(p_all, prev_exp, mu, vecs, w2p, a2p, g2p, e, seq, tc):
    m = p_all.shape[0]
    has_prev = prev_exp is not None
    row = lambda i: (i, 0)
    const = lambda i: (0, 0)
    halo = lambda i: (jnp.maximum(i * (tc // SUBLANES) - 1, 0), 0)
    in_specs = [pl.BlockSpec((tc, RW_PAD), row), pl.BlockSpec((SUBLANES, RW_PAD), halo)]
    args = [p_all, p_all]
    if has_prev:
        in_specs.append(pl.BlockSpec((tc, RW_PAD), row))
        args.append(prev_exp)
    in_specs += [pl.BlockSpec((1, RW_PAD), const), pl.BlockSpec((SUBLANES, RWKV_WIDTH), const),
                 pl.BlockSpec((LORA_PAD, RWKV_WIDTH), const), pl.BlockSpec((LORA_PAD, RWKV_WIDTH), const),
                 pl.BlockSpec((LORA_PAD, RWKV_WIDTH), const), pl.BlockSpec((RWKV_WIDTH, RWKV_WIDTH), const)]
    args += [mu, vecs, w2p, a2p, g2p, e]
    return pl.pallas_call(
        functools.partial(_rwkv_prep_body, tc=tc, seq=seq, has_prev=has_prev),
        grid=(m // tc,),
        in_specs=in_specs,
        out_specs=[pl.BlockSpec((tc, RWKV_WIDTH), row)] * 8,
        out_shape=[jax.ShapeDtypeStruct((m, RWKV_WIDTH), f32)] * 8,
        scratch_shapes=[pltpu.VMEM((tc + SUBLANES, RW_PAD), f32)],
        compiler_params=_cparams(("arbitrary",)),
        name="rwkv_prepare",
    )(*args)


RWKV_CHUNK = 64
RWKV_PAIRS_PER_STEP = 8


def _split(x):
    hi = x.astype(bf16)
    return hi, (x - hi.astype(f32)).astype(bf16)


def _dot3(a, b, nt=False):
    d = _dot_nt if nt else _dot
    a_hi, a_lo = _split(a)
    b_hi, b_lo = _split(b)
    return d(a_hi, b_hi) + d(a_hi, b_lo) + d(a_lo, b_hi)


def _dot1(a, b, nt=False):
    return (_dot_nt if nt else _dot)(a.astype(bf16), b.astype(bf16))


def _rwkv_chunk_body(r_ref, ld_ref, k_ref, v_ref, kap_ref, b_ref, s0_ref, y_ref, st_ref, s_ref):
    c = pl.program_id(2)
    C = RWKV_CHUNK
    zero = jnp.zeros((HEAD_DIM, HEAD_DIM), f32)

    @pl.when(c == 0)
    def _():
        for p in range(RWKV_PAIRS_PER_STEP):
            top = jnp.concatenate([s0_ref[0, 2 * p], zero], axis=1)
            bot = jnp.concatenate([zero, s0_ref[0, 2 * p + 1]], axis=1)
            s_ref[p] = jnp.concatenate([top, bot], axis=0)

    row = lax.broadcasted_iota(jnp.int32, (2 * C, LANES), 0)
    lane = lax.broadcasted_iota(jnp.int32, (2 * C, LANES), 1)
    top, bot = (row < C) & (lane < HEAD_DIM), (row >= C) & (lane >= HEAD_DIM)
    strict, incl = (lane % HEAD_DIM) < (row % C), (lane % HEAD_DIM) <= (row % C)
    eye = jnp.where(row == lane, 1.0, 0.0)
    lane1 = lax.broadcasted_iota(jnp.int32, (1, LANES), 1)
    m_a, m_b = jnp.where(lane1 < HEAD_DIM, 1.0, 0.0), jnp.where(lane1 >= HEAD_DIM, 1.0, 0.0)
    stack = lambda x: jnp.concatenate([x * m_a, x * m_b], axis=0)
    tri = jnp.where(lax.broadcasted_iota(jnp.int32, (C, C), 1) <= lax.broadcasted_iota(jnp.int32, (C, C), 0),
                    1.0, 0.0).astype(bf16)
    pick = lambda cond_a, xa, cond_b, xb: jnp.where(cond_a, xa, 0.0) + jnp.where(cond_b, xb, 0.0)

    l_hi, rest = _split(ld_ref[0])
    l_mid, l_lo = _split(rest.astype(f32))
    cum_all = _dot(tri, l_hi) + _dot(tri, l_mid) + _dot(tri, l_lo)

    pairs = range(RWKV_PAIRS_PER_STEP)
    lanes = [slice(p * LANES, (p + 1) * LANES) for p in pairs]
    kst, rst, b_t, k_t, g_end, vst, gram = [], [], [], [], [], [], []
    for p in pairs:
        r_, ld_, k_, kap_, b_ = (ref[0, :, lanes[p]] for ref in (r_ref, ld_ref, k_ref, kap_ref, b_ref))
        cum = cum_all[:, lanes[p]]
        g, g_prev, g_inv = jnp.exp(cum), jnp.exp(cum - ld_), jnp.exp(-cum)
        g_end.append(g[C - 1:C, :])
        kst.append(stack(kap_ * g_prev))
        rst.append(stack(r_ * g))
        b_t.append(b_ * g_inv)
        k_t.append(k_ * g_inv)
        vst.append(stack(v_ref[0, :, lanes[p]]))
        gram.append(_dot3(jnp.concatenate([kst[p], rst[p]], axis=0),
                          jnp.concatenate([b_t[p], k_t[p]], axis=0), nt=True))
    l_p, a_ak, a_rb, a_rk = [], [], [], []
    for p in pairs:
        ga, gr = gram[p][0:2 * C], gram[p][2 * C:4 * C]
        ga_r, gr_r = pltpu.roll(ga, HEAD_DIM, 1), pltpu.roll(gr, HEAD_DIM, 1)
        l_p.append(pick(top & strict, ga, bot & strict, ga_r))
        a_ak.append(pick(top & strict, ga_r, bot & strict, ga))
        a_rb.append(pick(top & incl, gr, bot & incl, gr_r))
        a_rk.append(pick(top & incl, gr_r, bot & incl, gr))
    inv, power = [eye - l for l in l_p], list(l_p)
    for _ in range(C.bit_length() - 2):
        power = [_dot1(m, m) for m in power]
        inv = [_dot1(x, eye + m) for x, m in zip(inv, power)]
    resid = [eye - _dot3(eye + l, x) for l, x in zip(l_p, inv)]
    inv = [x + _dot1(x, rs) for x, rs in zip(inv, resid)]
    sst = [s_ref[p] for p in pairs]
    sst_t = [s.T for s in sst]
    rhs = [_dot3(jnp.concatenate([kst[p], a_ak[p]], axis=1), jnp.concatenate([sst_t[p], vst[p]], axis=0))
           for p in pairs]
    u = [-_dot3(inv[p], rhs[p]) for p in pairs]
    for p in pairs:
        y = _dot1(jnp.concatenate([rst[p], a_rb[p], a_rk[p]], axis=1),
                  jnp.concatenate([sst_t[p], u[p], vst[p]], axis=0))
        y_ref[0, :, lanes[p]] = y[0:C] + y[C:2 * C]
    for p in pairs:
        uv = jnp.concatenate([u[p], vst[p]], axis=0)
        bk = jnp.concatenate([stack(b_t[p] * g_end[p]), stack(k_t[p] * g_end[p])], axis=0)
        s_ref[p] = sst[p] * g_end[p] + _dot3(uv.T, bk)

    @pl.when(c == pl.num_programs(2) - 1)
    def _():
        for p in range(RWKV_PAIRS_PER_STEP):
            st_ref[0, 2 * p] = s_ref[p][0:HEAD_DIM, 0:HEAD_DIM]
            st_ref[0, 2 * p + 1] = s_ref[p][HEAD_DIM:, HEAD_DIM:]


def rwkv_chunk_scan(r, ld, k, v, kap, b, s0):
    nbatch, seq = r.shape[:2]
    width = RWKV_PAIRS_PER_STEP * LANES
    blk = pl.BlockSpec((1, RWKV_CHUNK, width), lambda bi, pg, c: (bi, c, pg))
    st = pl.BlockSpec((1, 2 * RWKV_PAIRS_PER_STEP, HEAD_DIM, HEAD_DIM), lambda bi, pg, c: (bi, pg, 0, 0))
    return pl.pallas_call(
        _rwkv_chunk_body,
        grid=(nbatch, RWKV_WIDTH // width, seq // RWKV_CHUNK),
        in_specs=[blk] * 6 + [st],
        out_specs=[blk, st],
        out_shape=[jax.ShapeDtypeStruct((nbatch, seq, RWKV_WIDTH), f32),
                   jax.ShapeDtypeStruct((nbatch, RWKV_HEADS, HEAD_DIM, HEAD_DIM), f32)],
        scratch_shapes=[pltpu.VMEM((RWKV_PAIRS_PER_STEP, 2 * RWKV_CHUNK, LANES), f32)],
        compiler_params=_cparams(("arbitrary", "arbitrary", "arbitrary")),
        name="rwkv_chunk_scan",
    )(r, ld, k, v, kap, b, s0)


def _rwkv_post_body(y_ref, g_ref, bonus_ref, vec_ref, e_ref, o_ref):
    y = y_ref[...]
    e = e_ref[...]
    ln_g, ln_b = vec_ref[5:6, :], vec_ref[6:7, :]
    mu = _split_dot(y, e) * (1.0 / HEAD_DIM)
    yc = y - mu
    var = _split_dot(yc * yc, e) * (1.0 / HEAD_DIM)
    o_ref[...] = (yc * lax.rsqrt(var + GN_EPS) * ln_g + ln_b + bonus_ref[...]) * g_ref[...]


def rwkv_output(y, g, bonus, vecs, e, tc):
    m = y.shape[0]
    row = lambda i: (i, 0)
    const = lambda i: (0, 0)
    return pl.pallas_call(
        _rwkv_post_body,
        grid=(m // tc,),
        in_specs=[pl.BlockSpec((tc, RWKV_WIDTH), row)] * 3 +
                 [pl.BlockSpec((SUBLANES, RWKV_WIDTH), const), pl.BlockSpec((RWKV_WIDTH, RWKV_WIDTH), const)],
        out_specs=pl.BlockSpec((tc, RWKV_WIDTH), row),
        out_shape=jax.ShapeDtypeStruct((m, RWKV_WIDTH), f32),
        compiler_params=_cparams(("arbitrary",)),
        name="rwkv_output",
    )(y, g, bonus, vecs, e)


def rwkv7(p_all, prev, s0, wts, nbatch, seq, tc_prep):
    prev_exp = None
    if prev is not None:
        prev_exp = jnp.repeat(jnp.pad(prev, ((0, 0), (0, RW_PAD - RWKV_PROJ))), seq, axis=0)
    r, ld, k, v, kap, b, g, bonus = rwkv_prepare(
        p_all, prev_exp, wts['mu'], wts['vecs'], wts['w2p'], wts['a2p'], wts['g2p'], wts['e'], seq, tc_prep)
    pad = (-seq) % RWKV_CHUNK
    sh = lambda z: jnp.pad(z.reshape(nbatch, seq, RWKV_WIDTH), ((0, 0), (0, pad), (0, 0)))
    y, s_new = rwkv_chunk_scan(sh(r), sh(ld), sh(k), sh(v), sh(kap), sh(b), s0)
    o = rwkv_output(y[:, :seq].reshape(nbatch * seq, RWKV_WIDTH), g, bonus, wts['vecs'], wts['e'], tc_prep)
    return o, s_new


def _compress_head(x_ref, nchunk, wcat_ref, pe_ref, w1f_ref, b1_ref, w2_ref, b2_ref, acc_ref):
    kv2 = 2 * HEAD_DIM
    pe_term = jnp.concatenate([_dot(pe_ref[k], w1f_ref[k])[0:1, :] for k in range(2)], axis=1)
    bias = pe_term + b1_ref[...]
    acc = jnp.zeros((nchunk, 4 * kv2), f32)
    half = CMP_STRIDE // 2
    for r in range(half):
        xr = jnp.concatenate([x_ref[pl.ds(r, nchunk, stride=CMP_STRIDE), :],
                              x_ref[pl.ds(r + half, nchunk, stride=CMP_STRIDE), :]], axis=1)
        acc = acc + _dot(xr.astype(bf16), wcat_ref[r])
    acc_ref[0:nchunk, :] = acc
    acc_ref[nchunk:nchunk + SUBLANES, :] = jnp.zeros((SUBLANES, 4 * kv2), f32)
    first = jnp.concatenate([acc_ref[0:nchunk, 0:kv2], acc_ref[0:nchunk, 2 * kv2:3 * kv2]], axis=1)
    second = jnp.concatenate([acc_ref[1:nchunk + 1, kv2:2 * kv2], acc_ref[1:nchunk + 1, 3 * kv2:4 * kv2]], axis=1)
    hid = jax.nn.gelu(first + second + bias)
    return _dot(hid.astype(bf16), w2_ref[...]) + b2_ref[...]


def _compress_prompt_body(x_ref, wcat_ref, pe_ref, w1f_ref, b1_ref, w2_ref, b2_ref, o_ref, acc_ref, *, nchunk):
    o_ref[...] = _compress_head(x_ref, nchunk, wcat_ref, pe_ref, w1f_ref, b1_ref, w2_ref, b2_ref, acc_ref)


def _compress_weights(lp):
    w1 = lp['cmp_w1']
    z = jnp.zeros((CMP_STRIDE, HEAD_DIM, CMP_HIDDEN), f32)
    key_rows = jnp.concatenate([w1[0, :CMP_STRIDE], w1[0, CMP_STRIDE:], z, z], axis=-1)
    val_rows = jnp.concatenate([z, z, w1[1, :CMP_STRIDE], w1[1, CMP_STRIDE:]], axis=-1)
    wcat = jnp.concatenate([key_rows, val_rows], axis=1)
    wcat = jnp.concatenate([wcat[:CMP_STRIDE // 2], wcat[CMP_STRIDE // 2:]], axis=1).astype(bf16)
    pe = jnp.broadcast_to(lp['cmp_pe'].reshape(2, 1, CMP_BLOCK * HEAD_DIM), (2, SUBLANES, CMP_BLOCK * HEAD_DIM))
    w2 = lp['cmp_w2']
    zz = jnp.zeros((CMP_HIDDEN, HEAD_DIM), f32)
    w2bd = jnp.concatenate([jnp.concatenate([w2[0], zz], axis=1), jnp.concatenate([zz, w2[1]], axis=1)], axis=0)
    return {'wcat': wcat, 'pe': pe.astype(bf16),
            'w1f': w1.reshape(2, CMP_BLOCK * HEAD_DIM, CMP_HIDDEN).astype(bf16),
            'b1': lp['cmp_b1'].reshape(1, 2 * CMP_HIDDEN), 'w2bd': w2bd.astype(bf16),
            'b2': lp['cmp_b2'].reshape(1, 2 * HEAD_DIM)}


def _cmp_weight_specs(nidx):
    c2 = lambda *a: (0, 0)
    c3 = lambda *a: (0, 0, 0)
    return [pl.BlockSpec((CMP_STRIDE // 2, 4 * HEAD_DIM, 4 * CMP_HIDDEN), c3),
            pl.BlockSpec((2, SUBLANES, CMP_BLOCK * HEAD_DIM), c3),
            pl.BlockSpec((2, CMP_BLOCK * HEAD_DIM, CMP_HIDDEN), c3),
            pl.BlockSpec((1, 2 * CMP_HIDDEN), c2),
            pl.BlockSpec((2 * CMP_HIDDEN, 2 * HEAD_DIM), c2),
            pl.BlockSpec((1, 2 * HEAD_DIM), c2)]


def compress_prompt(p_all, cw, nbatch, seq):
    nchunk = seq // CMP_STRIDE
    kv2 = 2 * HEAD_DIM
    return pl.pallas_call(
        functools.partial(_compress_prompt_body, nchunk=nchunk),
        grid=(nbatch, NSA_KV_HEADS),
        in_specs=[pl.BlockSpec((seq, kv2), lambda b, h: (b, KVC0 // kv2 + h))] + _cmp_weight_specs(2),
        out_specs=pl.BlockSpec((nchunk, kv2), lambda b, h: (b, h)),
        out_shape=jax.ShapeDtypeStruct((nbatch * nchunk, KV_WIDTH), f32),
        scratch_shapes=[pltpu.VMEM((nchunk + SUBLANES, 4 * CMP_HIDDEN), f32)],
        compiler_params=_cparams(("arbitrary", "arbitrary")),
        name="compress_prompt",
    )(p_all, cw['wcat'], cw['pe'], cw['w1f'], cw['b1'], cw['w2bd'], cw['b2'])


KEY_TILE = 256


def _masked_softmax(s, mask):
    s = jnp.where(mask, s, NEG_INF)
    m = jnp.max(s, axis=-1, keepdims=True)
    e = jnp.where(mask, jnp.exp(s - m), 0.0)
    return e / jnp.maximum(jnp.sum(e, axis=-1, keepdims=True), 1e-30)


def _nsa_prompt_body(q_ref, kc_ref, ks_ref, kw_ref, g_ref, wsel_ref, eexp_ref, o_ref, *, nsel, ncmp):
    i = pl.program_id(2)
    qb, grp = Q_BLOCK, NSA_GROUP
    rows = grp * qb
    q = q_ref[...]
    qs = jnp.concatenate([q[:, g * HEAD_DIM:(g + 1) * HEAD_DIM] for g in range(grp)], axis=0) * (HEAD_DIM ** -0.5)
    qp = jnp.concatenate([qs, jnp.zeros_like(qs)], axis=1).astype(bf16)
    qpos = i * qb + lax.broadcasted_iota(jnp.int32, (rows, 1), 0) % qb

    kc = kc_ref[...].astype(bf16)
    s_c = _dot_nt(qp, kc)
    cend = lax.broadcasted_iota(jnp.int32, (1, ncmp), 1) * CMP_STRIDE + (CMP_BLOCK - 1)
    p_c = _masked_softmax(s_c, cend <= qpos)
    o_c = _dot(p_c.astype(bf16), kc)
    imp = p_c[0:qb] + p_c[qb:2 * qb] + p_c[2 * qb:3 * qb] + p_c[3 * qb:4 * qb]
    imp_hi = imp.astype(bf16)
    imp_lo = (imp - imp_hi.astype(f32)).astype(bf16)
    wsel = wsel_ref[...]
    p_slc = _dot_nt(wsel, imp_hi) + _dot_nt(wsel, imp_lo)

    blk = lax.broadcasted_iota(jnp.int32, (nsel, qb), 0)
    qpos_t = i * qb + lax.broadcasted_iota(jnp.int32, (nsel, qb), 1)
    cur = qpos_t // SEL_BLOCK
    forced = (blk == 0) | (blk == cur) | (blk == cur - 1)
    score = jnp.where(forced, FORCE_SCORE, p_slc)
    score = jnp.where(blk * SEL_BLOCK <= qpos_t, score, -1.0)
    rank = jnp.zeros((nsel, qb), f32)
    for jp in range(nsel):
        row = score[jp:jp + 1, :]
        ahead = (row > score) | ((row == score) & (blk > jp))
        rank = rank + jnp.where(ahead, 1.0, 0.0)
    sel_t = jnp.where((rank < float(min(SEL_TOPK, nsel))) & (score >= 0.0), 1.0, 0.0)
    sel = sel_t.T.astype(bf16)

    qpos_q = qpos[0:qb]
    ones_keys = lax.broadcasted_iota(jnp.int32, (1, 2 * HEAD_DIM), 1) < HEAD_DIM

    ntile = ks_ref.shape[0] // KEY_TILE

    def attend(ref, lo, hi, pair, carry, mask_fn):
        tiles = []
        for j in range(2):
            kt = lo + 2 * pair + j
            kt_c = jnp.minimum(kt, ntile - 1)
            k0 = pl.multiple_of(kt_c * KEY_TILE, KEY_TILE)
            kt_tile = ref[pl.ds(k0, KEY_TILE), :].astype(bf16)
            ones_v = jnp.where(ones_keys, jnp.ones_like(kt_tile), kt_tile)
            mask = mask_fn(kt_c, k0 + lax.broadcasted_iota(jnp.int32, (1, KEY_TILE), 1)) & (kt <= hi)
            tiles.append((kt_tile, ones_v, mask))
        scores = [[_dot_nt(qp[g * qb:(g + 1) * qb], kt_tile) for g in range(grp)] for kt_tile, _, _ in tiles]
        state = list(carry)
        for j, (_, ones_v, mask) in enumerate(tiles):
            m_new, probs = [], []
            for g in range(grp):
                s = jnp.where(mask, scores[j][g], NEG_INF)
                m_new.append(jnp.maximum(state[g][0], jnp.max(s, axis=-1, keepdims=True)))
                probs.append(jnp.where(mask, jnp.exp(s - m_new[g]), 0.0).astype(bf16))
            pv = [_dot(probs[g], ones_v) for g in range(grp)]
            state = [(m_new[g], jnp.exp(state[g][0] - m_new[g]) * state[g][1] + pv[g]) for g in range(grp)]
        return tuple(state)

    def sel_mask(kt, kpos):
        return (_dot(sel, eexp_ref[kt]) > 0.5) & (kpos <= qpos_q)

    def win_mask(kt, kpos):
        diff = qpos_q - kpos
        return (diff >= 0) & (diff <= WINDOW)

    per_tile = KEY_TILE // qb
    init = tuple((jnp.full((qb, 1), NEG_INF, f32), jnp.zeros((qb, 2 * HEAD_DIM), f32)) for _ in range(grp))
    hi = i // per_tile
    lo_w = jnp.maximum(i - WINDOW // qb, 0) // per_tile
    res_s = lax.fori_loop(0, hi // 2 + 1, lambda pr, c: attend(ks_ref, 0, hi, pr, c, sel_mask), init)
    res_w = lax.fori_loop(0, (hi - lo_w) // 2 + 1, lambda pr, c: attend(kw_ref, lo_w, hi, pr, c, win_mask), init)
    finish = lambda res: jnp.concatenate([acc / jnp.maximum(acc[:, 0:1], 1e-30) for _, acc in res], axis=0)
    o_s, o_w = finish(res_s), finish(res_w)

    gates = g_ref[0, 0]
    gate = lambda br: jnp.concatenate([gates[:, 3 * g + br:3 * g + br + 1] for g in range(grp)], axis=0)
    out = gate(0) * o_c + gate(1) * o_s + gate(2) * o_w
    o_ref[...] = jnp.concatenate([out[g * qb:(g + 1) * qb, HEAD_DIM:] for g in range(grp)], axis=1)


def nsa_prompt(p_all, kv_cmp, gates, nbatch, seq):
    nqb, nsel, ncmp = seq // Q_BLOCK, seq // SEL_BLOCK, seq // CMP_STRIDE
    kv2 = 2 * HEAD_DIM
    cidx = jnp.arange(ncmp)[None, :] - (SEL_BLOCK // CMP_STRIDE) * jnp.arange(nsel)[:, None]
    mult = jnp.array([1, 2, 2, 2, 1], f32)
    wsel = jnp.where((cidx >= 0) & (cidx <= 4), mult[jnp.clip(cidx, 0, 4)], 0.0).astype(bf16)
    ntile = seq // KEY_TILE
    key_blk = (jnp.arange(ntile)[:, None, None] * KEY_TILE + jnp.arange(KEY_TILE)[None, None, :]) // SEL_BLOCK
    eexp = (key_blk == jnp.arange(nsel)[None, :, None]).astype(bf16)
    return pl.pallas_call(
        functools.partial(_nsa_prompt_body, nsel=nsel, ncmp=ncmp),
        grid=(nbatch, NSA_KV_HEADS, nqb),
        in_specs=[pl.BlockSpec((Q_BLOCK, NSA_GROUP * HEAD_DIM),
                               lambda b, h, i: (b * nqb + i, Q0 // (NSA_GROUP * HEAD_DIM) + h)),
                  pl.BlockSpec((ncmp, kv2), lambda b, h, i: (b, h)),
                  pl.BlockSpec((seq, kv2), lambda b, h, i: (b, KVS0 // kv2 + h)),
                  pl.BlockSpec((seq, kv2), lambda b, h, i: (b, KVW0 // kv2 + h)),
                  pl.BlockSpec((1, 1, Q_BLOCK, 16), lambda b, h, i: (b, h, i, 0)),
                  pl.BlockSpec((nsel, ncmp), lambda b, h, i: (0, 0)),
                  pl.BlockSpec((ntile, nsel, KEY_TILE), lambda b, h, i: (0, 0, 0))],
        out_specs=pl.BlockSpec((Q_BLOCK, NSA_GROUP * HEAD_DIM), lambda b, h, i: (b * nqb + i, h)),
        out_shape=jax.ShapeDtypeStruct((nbatch * seq, NSA_WIDTH), f32),
        compiler_params=_cparams(("arbitrary", "arbitrary", "arbitrary")),
        name="nsa_prompt",
    )(p_all, kv_cmp, p_all, p_all, gates, wsel, eexp)


def _branch_gates(p_all, nbatch, seq):
    g = p_all[:, GN0:GN0 + 3 * NSA_HEADS].reshape(nbatch, seq, NSA_KV_HEADS, 3 * NSA_GROUP)
    return jnp.pad(jnp.transpose(g, (0, 2, 1, 3)), ((0, 0), (0, 0), (0, 0), (0, 16 - 3 * NSA_GROUP)))


PAGES_PER_STEP = 16
CMP_HALVES = 2
TAIL_ROWS = CMP_STRIDE


def _compress_sample_body(pt_ref, *refs, nchunk, half_rows):
    npg = PAGES_PER_STEP
    pages, (nxt_ref, newc_ref) = refs[:npg], refs[npg:npg + 2]
    wcat_ref, pe_ref, w1f_ref, b1_ref, w2_ref, b2_ref, o_ref, xs_ref, acc_ref = refs[npg + 2:]
    half, s = pl.program_id(1), pl.program_id(2)
    kv2 = 2 * HEAD_DIM
    for k in range(npg):
        base = pl.multiple_of((s * npg + k) * PAGE_SIZE, PAGE_SIZE)
        for h in range(NSA_KV_HEADS):
            xs_ref[h, pl.ds(base, PAGE_SIZE), :] = pages[k][0, :, h * kv2:(h + 1) * kv2]

    @pl.when(s == pl.num_programs(2) - 1)
    def _():
        nblk = half_rows // CMP_STRIDE
        pad_rows = xs_ref.shape[1] - half_rows - TAIL_ROWS
        for h in range(NSA_KV_HEADS):
            lanes = slice(h * kv2, (h + 1) * kv2)
            tail = jnp.where(half == CMP_HALVES - 1, newc_ref[0, :, lanes], nxt_ref[0, :, lanes])
            xs_ref[h, half_rows:half_rows + TAIL_ROWS, :] = tail
            xs_ref[h, half_rows + TAIL_ROWS:, :] = jnp.zeros((pad_rows, kv2), f32)
            out = _compress_head(xs_ref.at[h], nchunk, wcat_ref, pe_ref, w1f_ref, b1_ref, w2_ref, b2_ref, acc_ref)
            o_ref[0, :, lanes] = out[0:nblk]


def compress_sample(cache, page_table, newc, cw):
    nbatch, npages = page_table.shape
    npg = PAGES_PER_STEP
    half_pages = npages // CMP_HALVES
    half_rows = half_pages * PAGE_SIZE
    nsteps = half_pages // npg
    nchunk = half_rows // CMP_STRIDE + SUBLANES
    page_spec = lambda k: pl.BlockSpec(
        (1, PAGE_SIZE, KV_WIDTH), lambda b, hf, s, pt: (pt[b, hf * half_pages + s * npg + k], 0, 0))
    nxt_spec = pl.BlockSpec(
        (1, TAIL_ROWS, KV_WIDTH), lambda b, hf, s, pt: (pt[b, jnp.minimum((hf + 1) * half_pages, npages - 1)], 0, 0))
    return pl.pallas_call(
        functools.partial(_compress_sample_body, nchunk=nchunk, half_rows=half_rows),
        grid_spec=pltpu.PrefetchScalarGridSpec(
            num_scalar_prefetch=1,
            grid=(nbatch, CMP_HALVES, nsteps),
            in_specs=[page_spec(k) for k in range(npg)] + [nxt_spec] +
                     [pl.BlockSpec((1, TAIL_ROWS, KV_WIDTH), lambda b, hf, s, pt: (b, 0, 0))] + _cmp_weight_specs(4),
            out_specs=pl.BlockSpec((1, half_rows // CMP_STRIDE, KV_WIDTH), lambda b, hf, s, pt: (b, hf, 0)),
            scratch_shapes=[pltpu.VMEM((NSA_KV_HEADS, nchunk * CMP_STRIDE, 2 * HEAD_DIM), f32),
                            pltpu.VMEM((nchunk + SUBLANES, 4 * CMP_HIDDEN), f32)]),
        out_shape=jax.ShapeDtypeStruct((nbatch, npages * PAGE_SIZE // CMP_STRIDE, KV_WIDTH), f32),
        compiler_params=_cparams(("arbitrary", "arbitrary", "arbitrary")),
        name="compress_sample",
    )(page_table, *([cache] * (npg + 1)), newc, cw['wcat'], cw['pe'], cw['w1f'], cw['b1'], cw['w2bd'], cw['b2'])


def _split_dot_r(e, x):
    hi = x.astype(bf16)
    lo = (x - hi.astype(f32)).astype(bf16)
    return _dot(e, hi) + _dot(e, lo)


def _softmax_rows(s, mask):
    s = jnp.where(mask, s, NEG_INF)
    m = jnp.max(s, axis=0, keepdims=True)
    e = jnp.where(mask, jnp.exp(s - m), 0.0)
    return e / jnp.maximum(jnp.sum(e, axis=0, keepdims=True), 1e-30)


def _row_to_col(row):
    return jnp.broadcast_to(row, (SUBLANES, LANES)).T[:, 0:1]


def _nsa_sample_body(pt_ref, *refs, tnew, nsel, nselp):
    npg = PAGES_PER_STEP
    pages = refs[:npg]
    (qb_ref, qr_ref, kc_ref, win_ref, neww_ref, news_ref, g_ref, wsel_ref, gm_ref, pick_ref, eexp_ref, o_ref,
     selq_ref, score_ref, m_ref, l_ref, acc_ref, oc_ref, ow_ref) = refs[npg:]
    s = pl.program_id(1)
    nselq = selq_ref.shape[1]
    rows_per_head = LANES // NSA_KV_HEADS
    qb = qb_ref[0]
    col = lax.broadcasted_iota(jnp.int32, (1, LANES), 1)
    tq = col % tnew
    qpos = PAST_LEN + tq
    pad_tail = lambda ref: jnp.concatenate(
        [ref[0], jnp.zeros((LANES - TAIL_ROWS, KV_WIDTH), f32)], axis=0).astype(bf16)

    @pl.when(s == 0)
    def _():
        kc = kc_ref[0].astype(bf16)
        ncmp = kc.shape[0]
        cend = lax.broadcasted_iota(jnp.int32, (ncmp, 1), 0) * CMP_STRIDE + (CMP_BLOCK - 1)
        p_c = _softmax_rows(_dot(kc, qb), cend <= qpos)
        oc_ref[...] = _dot(p_c.T.astype(bf16), kc)
        imp = _split_dot(p_c, gm_ref[...])
        p_slc = _split_dot_r(wsel_ref[...], imp)
        blk = lax.broadcasted_iota(jnp.int32, (nselp, LANES), 0)
        cur = qpos // SEL_BLOCK
        forced = (blk == 0) | (blk == cur) | (blk == cur - 1)
        score = jnp.where(forced, FORCE_SCORE, p_slc)
        score = jnp.where((blk * SEL_BLOCK <= qpos) & (blk < nsel), score, -1.0)
        score_ref[...] = score

        def rank_group(gi, rank):
            rows = score_ref[pl.ds(pl.multiple_of(gi * SUBLANES, SUBLANES), SUBLANES), :]
            for j in range(SUBLANES):
                row = rows[j:j + 1, :]
                ahead = (row > score) | ((row == score) & (blk > gi * SUBLANES + j))
                rank = rank + jnp.where(ahead, 1.0, 0.0)
            return rank

        rank = lax.fori_loop(0, nselp // SUBLANES, rank_group, jnp.zeros((nselp, LANES), f32))
        sel_t = jnp.where((rank < float(SEL_TOPK)) & (score >= 0.0), 1.0, 0.0)
        selq_ref[...] = jnp.concatenate([sel_t, jnp.zeros((nselq - nselp, LANES), f32)], axis=0).T

        kw = jnp.concatenate([win_ref[0].astype(bf16), pad_tail(neww_ref)], axis=0)
        idx = lax.broadcasted_iota(jnp.int32, (kw.shape[0], 1), 0)
        wbuf = win_ref.shape[1]
        p_w = _softmax_rows(_dot(kw, qb), (idx <= wbuf + tq) & (idx >= wbuf + tq - WINDOW))
        ow_ref[...] = _dot(p_w.T.astype(bf16), kw)

        m_ref[...] = jnp.full(m_ref.shape, NEG_INF, f32)
        l_ref[...] = jnp.zeros(l_ref.shape, f32)
        acc_ref[...] = jnp.zeros(acc_ref.shape, f32)

    def fold(scores, masks, e_dot_v):
        hrows = [slice(h * rows_per_head, (h + 1) * rows_per_head) for h in range(NSA_KV_HEADS)]
        m_new, alpha, probs = [], [], []
        for h in range(NSA_KV_HEADS):
            sc = jnp.where(masks[h], scores[h], NEG_INF)
            m_old = m_ref[hrows[h], 0:1]
            m_new.append(jnp.maximum(m_old, jnp.max(sc, axis=1, keepdims=True)))
            alpha.append(jnp.exp(m_old - m_new[h]))
            e = jnp.where(masks[h], jnp.exp(sc - m_new[h]), 0.0)
            l_ref[hrows[h], :] = jnp.broadcast_to(
                alpha[h] * l_ref[hrows[h], 0:1] + jnp.sum(e, axis=1, keepdims=True), (rows_per_head, LANES))
            probs.append(e.astype(bf16))
        pv = [e_dot_v(h, probs[h]) for h in range(NSA_KV_HEADS)]
        for h in range(NSA_KV_HEADS):
            acc_ref[hrows[h], :] = alpha[h] * acc_ref[hrows[h], :] + pv[h]
            m_ref[hrows[h], :] = jnp.broadcast_to(m_new[h], (rows_per_head, LANES))

    heads = range(NSA_KV_HEADS)
    hrow = lambda h: slice(h * rows_per_head, (h + 1) * rows_per_head)
    k_t = [jnp.concatenate([pages[k][0, h, 0] for k in range(npg)], axis=1).astype(bf16) for h in heads]
    v_t = [jnp.concatenate([pages[k][0, h, 1] for k in range(npg)], axis=1).astype(bf16) for h in heads]
    step_sel = [_dot(selq_ref[hrow(h), :].astype(bf16), pick_ref[s]) for h in heads]
    picked = [_dot(step_sel[h].astype(bf16), eexp_ref[...]) > 0.5 for h in heads]
    fold([_dot(qr_ref[0, hrow(h), :], k_t[h]) for h in heads], picked, lambda h, e: _dot_nt(e, v_t[h]))

    @pl.when(s == pl.num_programs(1) - 1)
    def _():
        ns = news_ref[0]
        kidx = lax.broadcasted_iota(jnp.int32, (1, TAIL_ROWS), 1)
        t_row = lax.broadcasted_iota(jnp.int32, (rows_per_head, 1), 0) % tnew
        nb = nsel - 1
        k_n = [ns[:, h * 2 * HEAD_DIM:h * 2 * HEAD_DIM + HEAD_DIM].astype(bf16) for h in heads]
        v_n = [ns[:, h * 2 * HEAD_DIM + HEAD_DIM:(h + 1) * 2 * HEAD_DIM].astype(bf16) for h in heads]
        fold([_dot_nt(qr_ref[0, hrow(h), :], k_n[h]) for h in heads],
             [(selq_ref[hrow(h), nb:nb + 1] > 0.5) & (kidx <= t_row) for h in heads],
             lambda h, e: _dot(e, v_n[h]))
        for h in heads:
            rows = hrow(h)
            o_s = acc_ref[rows, :] / jnp.maximum(l_ref[rows, 0:1], 1e-30)
            vals = slice(h * 2 * HEAD_DIM + HEAD_DIM, (h + 1) * 2 * HEAD_DIM)
            o_ref[0, rows, :] = (g_ref[0, 0, rows, 0:HEAD_DIM] * oc_ref[rows, vals] + g_ref[0, 1, rows, 0:HEAD_DIM] * o_s
                                 + g_ref[0, 2, rows, 0:HEAD_DIM] * ow_ref[rows, vals])


def nsa_sample(cache_sel_t, page_table, qblk, qrows, kv_cmp, win_buf, neww, news, gates, tnew):
    nbatch, npages = page_table.shape
    npg = PAGES_PER_STEP
    ncmp = kv_cmp.shape[1]
    nsel = npages * (PAGE_SIZE // SEL_BLOCK) + 1
    nselp = -(-nsel // SUBLANES) * SUBLANES
    cidx = jnp.arange(ncmp)[None, :] - (SEL_BLOCK // CMP_STRIDE) * jnp.arange(nselp)[:, None]
    mult = jnp.array([1, 2, 2, 2, 1], f32)
    wsel = jnp.where((cidx >= 0) & (cidx <= 4), mult[jnp.clip(cidx, 0, 4)], 0.0).astype(bf16)
    c = jnp.arange(LANES)
    same = (c[:, None] // (NSA_GROUP * tnew) == c[None, :] // (NSA_GROUP * tnew)) & (c[:, None] % tnew == c[None, :] % tnew)
    gm = same.astype(bf16)
    nsteps = npages // npg
    blocks_per_step = npg * (PAGE_SIZE // SEL_BLOCK)
    nselq = -(-nselp // LANES) * LANES
    blk_of = jnp.arange(nsteps)[:, None, None] * blocks_per_step + jnp.arange(LANES)[None, None, :]
    pick = ((jnp.arange(nselq)[None, :, None] == blk_of) & (jnp.arange(LANES) < blocks_per_step)).astype(bf16)
    eexp = (jnp.arange(LANES)[:, None] == jnp.arange(npg * PAGE_SIZE)[None, :] // SEL_BLOCK).astype(bf16)
    b3 = lambda b, s, pt: (b, 0, 0)
    c2 = lambda b, s, pt: (0, 0)
    page_spec = lambda k: pl.BlockSpec((1, NSA_KV_HEADS, 2, HEAD_DIM, PAGE_SIZE),
                                       lambda b, s, pt: (pt[b, s * npg + k], 0, 0, 0, 0))
    wbuf = win_buf.shape[1]
    return pl.pallas_call(
        functools.partial(_nsa_sample_body, tnew=tnew, nsel=nsel, nselp=nselp),
        grid_spec=pltpu.PrefetchScalarGridSpec(
            num_scalar_prefetch=1,
            grid=(nbatch, nsteps),
            in_specs=[page_spec(k) for k in range(npg)] + [
                pl.BlockSpec((1, KV_WIDTH, LANES), b3),
                pl.BlockSpec((1, LANES, HEAD_DIM), b3),
                pl.BlockSpec((1, ncmp, KV_WIDTH), b3),
                pl.BlockSpec((1, wbuf, KV_WIDTH), b3),
                pl.BlockSpec((1, TAIL_ROWS, KV_WIDTH), b3),
                pl.BlockSpec((1, TAIL_ROWS, KV_WIDTH), b3),
                pl.BlockSpec((1, 3, LANES, LANES), lambda b, s, pt: (b, 0, 0, 0)),
                pl.BlockSpec((nselp, ncmp), c2),
                pl.BlockSpec((LANES, LANES), c2),
                pl.BlockSpec((nsteps, nselq, LANES), lambda b, s, pt: (0, 0, 0)),
                pl.BlockSpec((LANES, npg * PAGE_SIZE), c2)],
            out_specs=pl.BlockSpec((1, LANES, HEAD_DIM), b3),
            scratch_shapes=[pltpu.VMEM((LANES, nselq), f32), pltpu.VMEM((nselp, LANES), f32),
                            pltpu.VMEM((LANES, LANES), f32), pltpu.VMEM((LANES, LANES), f32),
                            pltpu.VMEM((LANES, HEAD_DIM), f32),
                            pltpu.VMEM((LANES, KV_WIDTH), f32), pltpu.VMEM((LANES, KV_WIDTH), f32)]),
        out_shape=jax.ShapeDtypeStruct((nbatch, LANES, HEAD_DIM), f32),
        compiler_params=_cparams(("arbitrary", "arbitrary")),
        name="nsa_sample",
    )(page_table, *([cache_sel_t] * npg), qblk, qrows, kv_cmp, win_buf, neww, news, gates, wsel, gm, pick, eexp)


def _sample_nsa_inputs(p_all, b, t):
    q = p_all[:, Q0:Q0 + NSA_WIDTH].reshape(b, t, NSA_KV_HEADS, NSA_GROUP, HEAD_DIM) * (HEAD_DIM ** -0.5)
    qt = jnp.transpose(q, (0, 2, 4, 3, 1)).reshape(b, NSA_KV_HEADS, HEAD_DIM, NSA_GROUP * t)
    qt = jnp.pad(qt, ((0, 0), (0, 0), (0, HEAD_DIM), (0, 0)))
    eye = jnp.eye(NSA_KV_HEADS, dtype=f32)
    qblk = (qt[:, :, :, None, :] * eye[None, :, None, :, None]).reshape(b, KV_WIDTH, LANES).astype(bf16)
    qrows = jnp.transpose(q, (0, 2, 3, 1, 4)).reshape(b, LANES, HEAD_DIM).astype(bf16)
    g = p_all[:, GN0:GN0 + 3 * NSA_HEADS].reshape(b, t, NSA_KV_HEADS, NSA_GROUP, 3)
    g = jnp.transpose(g, (0, 4, 2, 3, 1)).reshape(b, 3, LANES)
    gates = jnp.broadcast_to(g[..., None], (b, 3, LANES, LANES))
    tail = lambda c0: jnp.pad(p_all[:, c0:c0 + KV_WIDTH].reshape(b, t, KV_WIDTH), ((0, 0), (0, TAIL_ROWS - t), (0, 0)))
    return qblk, qrows, gates, tail(KVC0), tail(KVS0), tail(KVW0)


def _prepare_weights(lp):
    w_in = lp['w_in']
    nsa_end = NSA_WIDTH + 3 * KV_WIDTH
    w_all = jnp.concatenate([
        w_in[:, NSA_PROJ:NSA_PROJ + RWKV_PROJ], w_in[:, nsa_end:NSA_PROJ],
        jnp.zeros((D_MODEL, RW_PAD - RWKV_PROJ - 3 * NSA_HEADS), w_in.dtype),
        w_in[:, :nsa_end], w_in[:, NSA_PROJ + RWKV_PROJ:]], axis=1).astype(bf16)
    pad_rows = lambda w, r0: jnp.zeros((LORA_PAD, RWKV_WIDTH), f32).at[r0:r0 + w.shape[0]].set(w).astype(bf16)
    head = jnp.arange(RWKV_WIDTH) // HEAD_DIM
    vecs = jnp.stack([lp['rwkv_w0'], lp['rwkv_a0'], lp['rwkv_k_k'], lp['rwkv_k_a'],
                      lp['rwkv_r_k'].reshape(RWKV_WIDTH), lp['rwkv_ln_g'], lp['rwkv_ln_b'],
                      jnp.zeros((RWKV_WIDTH,), f32)])
    return {
        'w_all': w_all,
        'mu': jnp.pad(lp['rwkv_mu'], (0, RW_PAD - RWKV_PROJ)).reshape(1, RW_PAD),
        'vecs': vecs,
        'w2p': pad_rows(lp['rwkv_w2'], 0),
        'a2p': pad_rows(lp['rwkv_a2'], DECAY_LORA),
        'g2p': pad_rows(lp['rwkv_g2'], DECAY_LORA + AAA_LORA),
        'e': (head[:, None] == head[None, :]).astype(bf16),
        'p_nsa': lp['p_nsa'].astype(bf16), 'p_rwkv': lp['p_rwkv'].astype(bf16),
        'w_out': lp['w_out'].astype(bf16),
        'mlp_w1': lp['mlp_w1'].astype(bf16), 'mlp_w2': lp['mlp_w2'].astype(bf16),
    }


def _group_forward(x, pos_rows, prev, s0, lp, wts, final_g, tm, tc_prep, o_nsa_fn):
    nbatch, seq = x.shape[:2]
    x2d = x.reshape(nbatch * seq, D_MODEL)
    tq, tkv = _rot_tables(pos_rows)
    p_all = input_projection(x2d, lp['norm1_g'], wts['w_all'], tq, tkv, tm)
    o_rw, s_new = rwkv7(p_all, prev, s0, wts, nbatch, seq, tc_prep)
    o_nsa = o_nsa_fn(p_all)
    h, hn = merge_project(o_nsa, o_rw, p_all, x2d, wts['p_nsa'], wts['p_rwkv'], wts['w_out'], lp['norm2_g'],
                          min(tm, 256))
    y = mlp_residual_norm(hn, h, wts['mlp_w1'], wts['mlp_w2'], final_g, tm)
    return y.reshape(nbatch, seq, D_MODEL), p_all, s_new


def _kv_rows(p_all, col0, nbatch, seq):
    return p_all[:, col0:col0 + KV_WIDTH].reshape(nbatch, seq, NSA_KV_HEADS, 2, HEAD_DIM)


def sample_nsa_attention(p_all, cache_cmp, cache_sel, win_buf, page_table, cw, b, t):
    qblk, qrows, gates, newc, news, neww = _sample_nsa_inputs(p_all, b, t)
    pool = cache_cmp.shape[0]
    kv_cmp = compress_sample(cache_cmp.reshape(pool, PAGE_SIZE, KV_WIDTH), page_table, newc, cw)
    cache_sel_t = jnp.transpose(cache_sel, (0, 2, 3, 4, 1))
    o = nsa_sample(cache_sel_t, page_table, qblk, qrows, kv_cmp,
                   win_buf.reshape(b, win_buf.shape[1], KV_WIDTH), neww, news, gates, t)
    o = o.reshape(b, NSA_KV_HEADS, NSA_GROUP, t, HEAD_DIM)
    return jnp.transpose(o, (0, 3, 1, 2, 4)).reshape(b * t, NSA_WIDTH)


def kernel(x_prompt, x_sample, cache_cmp_kv, cache_sel_kv, state_nsa_win, state_rwkv, state_rwkv_shift,
           page_table, norm1_g, w_in, cmp_pe, cmp_w1, cmp_b1, cmp_w2, cmp_b2, rwkv_mu, rwkv_w0, rwkv_w2,
           rwkv_a0, rwkv_a2, rwkv_g2, rwkv_k_k, rwkv_k_a, rwkv_r_k, rwkv_ln_g, rwkv_ln_b, p_nsa, p_rwkv,
           w_out, norm2_g, mlp_w1, mlp_w2, final_g):
    l = 0
    lp = {'norm1_g': norm1_g[l], 'w_in': w_in[l], 'cmp_pe': cmp_pe[l], 'cmp_w1': cmp_w1[l],
          'cmp_b1': cmp_b1[l], 'cmp_w2': cmp_w2[l], 'cmp_b2': cmp_b2[l], 'rwkv_mu': rwkv_mu[l],
          'rwkv_w0': rwkv_w0[l], 'rwkv_w2': rwkv_w2[l], 'rwkv_a0': rwkv_a0[l], 'rwkv_a2': rwkv_a2[l],
          'rwkv_g2': rwkv_g2[l], 'rwkv_k_k': rwkv_k_k[l], 'rwkv_k_a': rwkv_k_a[l], 'rwkv_r_k': rwkv_r_k[l],
          'rwkv_ln_g': rwkv_ln_g[l], 'rwkv_ln_b': rwkv_ln_b[l], 'p_nsa': p_nsa[l], 'p_rwkv': p_rwkv[l],
          'w_out': w_out[l], 'norm2_g': norm2_g[l], 'mlp_w1': mlp_w1[l], 'mlp_w2': mlp_w2[l]}
    wts = _prepare_weights(lp)
    bp, tp = x_prompt.shape[:2]
    bs, ts = x_sample.shape[:2]
    cw = _compress_weights(lp)
    zero_state = jnp.zeros((bp, RWKV_HEADS, HEAD_DIM, HEAD_DIM), f32)

    def prompt_nsa(p_all):
        return nsa_prompt(p_all, compress_prompt(p_all, cw, bp, tp), _branch_gates(p_all, bp, tp), bp, tp)

    yp, pp, sp = _group_forward(x_prompt, jnp.arange(tp, dtype=jnp.int32), None, zero_state, lp, wts, final_g,
                                512, 256, prompt_nsa)

    def sample_nsa(p_all):
        return sample_nsa_attention(p_all, cache_cmp_kv[l], cache_sel_kv[l], state_nsa_win[l], page_table, cw, bs, ts)

    pos_s = PAST_LEN + jnp.arange(bs * ts, dtype=jnp.int32) % ts
    ys, ps, ss = _group_forward(x_sample, pos_s, state_rwkv_shift[l], state_rwkv[l], lp, wts, final_g,
                                256, 256, sample_nsa)
    win_new = [jnp.concatenate([state_nsa_win[l], _kv_rows(ps, KVW0, bs, ts)], axis=1)[:, ts:]]
    wlen = min(WINDOW, tp)
    shift = lambda p_all, nb, t: p_all.reshape(nb, t, P_WIDTH)[:, -1, RW0:RW0 + RWKV_PROJ]
    return (yp, ys,
            _kv_rows(pp, KVC0, bp, tp)[None], _kv_rows(ps, KVC0, bs, ts)[None],
            _kv_rows(pp, KVS0, bp, tp)[None], _kv_rows(ps, KVS0, bs, ts)[None],
            _kv_rows(pp, KVW0, bp, tp)[None, :, tp - wlen:], win_new[0][None],
            sp[None], ss[None],
            shift(pp, bp, tp)[None], shift(ps, bs, ts)[None])
```

```python
import functools

import jax
import jax.numpy as jnp
from jax import lax
from jax.experimental import pallas as pl
from jax.experimental.pallas import tpu as pltpu

D_MODEL = 2048
DEPTH = 1
PAST_LEN = 16384
PAGE_SIZE = 128

HEAD_DIM = 64
NSA_HEADS = D_MODEL // (2 * HEAD_DIM)
NSA_KV_HEADS = NSA_HEADS // 4
NSA_GROUP = NSA_HEADS // NSA_KV_HEADS
NSA_WIDTH = NSA_HEADS * HEAD_DIM
KV_WIDTH = NSA_KV_HEADS * 2 * HEAD_DIM
CMP_BLOCK = 32
CMP_STRIDE = 16
CMP_HIDDEN = 2 * HEAD_DIM
SEL_BLOCK = 64
SEL_TOPK = 16
WINDOW = 512
Q_BLOCK = 128
ROT_DIM = HEAD_DIM // 4
ROPE_THETA = 500000.0
RWKV_HEADS = D_MODEL // (2 * HEAD_DIM)
RWKV_WIDTH = RWKV_HEADS * HEAD_DIM
DECAY_LORA = max(32, int(round(1.8 * D_MODEL ** 0.5 / 32)) * 32)
AAA_LORA = DECAY_LORA
GATE_LORA = max(32, int(round(0.6 * D_MODEL ** 0.8 / 32)) * 32)
RWKV_PROJ = 3 * RWKV_WIDTH + DECAY_LORA + AAA_LORA + GATE_LORA
NSA_PROJ = NSA_WIDTH + 3 * KV_WIDTH + 3 * NSA_HEADS
N_IN = NSA_PROJ + RWKV_PROJ + 2 * D_MODEL
D_FF = 4 * D_MODEL
RMS_EPS = 1e-6
GN_EPS = HEAD_DIM * 1e-5
NEG_INF = -1e30
FORCE_SCORE = 1e6

LANES = 128
SUBLANES = 8
VMEM_LIMIT_BYTES = 56 * 1024 * 1024

RW0 = 0
LORA0 = 3 * RWKV_WIDTH
GN0 = RWKV_PROJ
RW_PAD = 3584
LORA_PAD = RW_PAD - LORA0
Q0 = RW_PAD
KVC0 = Q0 + NSA_WIDTH
KVS0 = KVC0 + KV_WIDTH
KVW0 = KVS0 + KV_WIDTH
GM0 = KVW0 + KV_WIDTH
P_WIDTH = GM0 + 2 * D_MODEL
PROJ_TN = 512

f32 = jnp.float32
bf16 = jnp.bfloat16


def _cparams(sem):
    return pltpu.CompilerParams(dimension_semantics=sem, vmem_limit_bytes=VMEM_LIMIT_BYTES)


def _rms(x, g):
    return x * lax.rsqrt(jnp.mean(x * x, axis=-1, keepdims=True) + RMS_EPS) * g


def _dot(a, b):
    return jnp.dot(a, b, preferred_element_type=f32)


def _dot_nt(a, b):
    return lax.dot_general(a, b, (((1,), (1,)), ((), ())), preferred_element_type=f32)


def _split_dot(x, e):
    hi = x.astype(bf16)
    lo = (x - hi.astype(f32)).astype(bf16)
    return _dot(hi, e) + _dot(lo, e)


def _rot_store(acc, tab_ref, o_ref):
    c, s1, s2 = tab_ref[0], tab_ref[1], tab_ref[2]
    for s in range(acc.shape[1] // LANES):
        x = acc[:, s * LANES:(s + 1) * LANES]
        o_ref[:, s * LANES:(s + 1) * LANES] = (
            x * c + pltpu.roll(x, LANES - ROT_DIM // 2, 1) * s1 + pltpu.roll(x, ROT_DIM // 2, 1) * s2)


def _proj_body(x_ref, g_ref, w_ref, tq_ref, tkv_ref, o_ref, xn_ref):
    j = pl.program_id(1)

    @pl.when(j == 0)
    def _():
        xn_ref[...] = _rms(x_ref[...], g_ref[...]).astype(bf16)

    acc = _dot(xn_ref[...], w_ref[...])
    gn_tile = GN0 // PROJ_TN

    @pl.when(j < gn_tile)
    def _():
        o_ref[...] = acc

    @pl.when(j == gn_tile)
    def _():
        lane = lax.broadcasted_iota(jnp.int32, acc.shape, 1)
        o_ref[...] = jnp.where(lane >= GN0 - gn_tile * PROJ_TN, jax.nn.sigmoid(acc), acc)

    @pl.when((j >= Q0 // PROJ_TN) & (j < KVC0 // PROJ_TN))
    def _():
        _rot_store(acc, tq_ref, o_ref)

    @pl.when((j >= KVC0 // PROJ_TN) & (j < GM0 // PROJ_TN))
    def _():
        _rot_store(acc, tkv_ref, o_ref)

    @pl.when(j >= GM0 // PROJ_TN)
    def _():
        o_ref[...] = jax.nn.sigmoid(acc)


def input_projection(x2d, g, w_all, tq, tkv, tm):
    m = x2d.shape[0]
    nt = tq.shape[1] // tm
    return pl.pallas_call(
        _proj_body,
        grid=(m // tm, P_WIDTH // PROJ_TN),
        in_specs=[pl.BlockSpec((tm, D_MODEL), lambda i, j: (i, 0)),
                  pl.BlockSpec((1, D_MODEL), lambda i, j: (0, 0)),
                  pl.BlockSpec((D_MODEL, PROJ_TN), lambda i, j: (0, j)),
                  pl.BlockSpec((3, tm, LANES), lambda i, j: (0, i % nt, 0)),
                  pl.BlockSpec((3, tm, LANES), lambda i, j: (0, i % nt, 0))],
        out_specs=pl.BlockSpec((tm, PROJ_TN), lambda i, j: (i, j)),
        out_shape=jax.ShapeDtypeStruct((m, P_WIDTH), f32),
        scratch_shapes=[pltpu.VMEM((tm, D_MODEL), bf16)],
        compiler_params=_cparams(("arbitrary", "arbitrary")),
        name="input_projection",
    )(x2d, g.reshape(1, D_MODEL), w_all, tq, tkv)


def _rot_tables(pos):
    half = ROT_DIM // 2
    freqs = jnp.power(jnp.float32(ROPE_THETA), -jnp.arange(half, dtype=f32) * 2.0 / ROT_DIM)
    ang = pos.astype(f32)[:, None] * freqs[None, :]
    cos, sin = jnp.cos(ang), jnp.sin(ang)
    lane = jnp.arange(LANES)

    def build(period):
        l = lane % period
        fi = l % half
        c = jnp.where(l < ROT_DIM, cos[:, fi], 1.0)
        s1 = jnp.where(l < half, -sin[:, fi], 0.0)
        s2 = jnp.where((l >= half) & (l < ROT_DIM), sin[:, fi], 0.0)
        return jnp.stack([c, s1, s2]).astype(f32)

    return build(HEAD_DIM), build(2 * HEAD_DIM)


def _merge_body(on_ref, or_ref, ga_ref, gb_ref, x_ref, pn_ref, pr_ref, wo_ref, g2_ref, h_ref, hn_ref):
    a = _dot(on_ref[...].astype(bf16), pn_ref[...])
    b = _dot(or_ref[...].astype(bf16), pr_ref[...])
    mix = ga_ref[...] * a + gb_ref[...] * b
    h = x_ref[...] + _dot(mix.astype(bf16), wo_ref[...])
    h_ref[...] = h
    hn_ref[...] = _rms(h, g2_ref[...]).astype(bf16)


def merge_project(o_nsa, o_rw, p_all, x2d, pn, pr, wo, g2, tm):
    m = x2d.shape[0]
    const = lambda i: (0, 0)
    return pl.pallas_call(
        _merge_body,
        grid=(m // tm,),
        in_specs=[pl.BlockSpec((tm, NSA_WIDTH), lambda i: (i, 0)),
                  pl.BlockSpec((tm, RWKV_WIDTH), lambda i: (i, 0)),
                  pl.BlockSpec((tm, D_MODEL), lambda i: (i, GM0 // D_MODEL)),
                  pl.BlockSpec((tm, D_MODEL), lambda i: (i, GM0 // D_MODEL + 1)),
                  pl.BlockSpec((tm, D_MODEL), lambda i: (i, 0)),
                  pl.BlockSpec((NSA_WIDTH, D_MODEL), const),
                  pl.BlockSpec((RWKV_WIDTH, D_MODEL), const),
                  pl.BlockSpec((D_MODEL, D_MODEL), const),
                  pl.BlockSpec((1, D_MODEL), const)],
        out_specs=[pl.BlockSpec((tm, D_MODEL), lambda i: (i, 0)),
                   pl.BlockSpec((tm, D_MODEL), lambda i: (i, 0))],
        out_shape=[jax.ShapeDtypeStruct((m, D_MODEL), f32),
                   jax.ShapeDtypeStruct((m, D_MODEL), bf16)],
        compiler_params=_cparams(("arbitrary",)),
        name="merge_project",
    )(o_nsa, o_rw, p_all, p_all, x2d, pn, pr, wo, g2.reshape(1, D_MODEL))


def _mlp_body(hn_ref, h_ref, w1_ref, w2_ref, fg_ref, y_ref, acc_ref):
    f = pl.program_id(1)
    u = jnp.square(jnp.maximum(_dot(hn_ref[...], w1_ref[...]), 0.0)).astype(bf16)
    contrib = _dot(u, w2_ref[...])

    @pl.when(f == 0)
    def _():
        acc_ref[...] = contrib

    @pl.when(f > 0)
    def _():
        acc_ref[...] += contrib

    @pl.when(f == pl.num_programs(1) - 1)
    def _():
        y_ref[...] = _rms(h_ref[...] + acc_ref[...], fg_ref[...])


def mlp_residual_norm(hn, h, w1, w2, fg, tm, tf=512):
    m = h.shape[0]
    return pl.pallas_call(
        _mlp_body,
        grid=(m // tm, D_FF // tf),
        in_specs=[pl.BlockSpec((tm, D_MODEL), lambda i, f: (i, 0)),
                  pl.BlockSpec((tm, D_MODEL), lambda i, f: (i, 0)),
                  pl.BlockSpec((D_MODEL, tf), lambda i, f: (0, f)),
                  pl.BlockSpec((tf, D_MODEL), lambda i, f: (f, 0)),
                  pl.BlockSpec((1, D_MODEL), lambda i, f: (0, 0))],
        out_specs=pl.BlockSpec((tm, D_MODEL), lambda i, f: (i, 0)),
        out_shape=jax.ShapeDtypeStruct((m, D_MODEL), f32),
        scratch_shapes=[pltpu.VMEM((tm, D_MODEL), f32)],
        compiler_params=_cparams(("arbitrary", "arbitrary")),
        name="mlp_residual_norm",
    )(hn, h, w1, w2, fg.reshape(1, D_MODEL))


def _rwkv_prep_body(*refs, tc, seq, has_prev):
    if has_prev:
        p_ref, halo_ref, prev_ref = refs[:3]
        refs = refs[3:]
    else:
        p_ref, halo_ref = refs[:2]
        prev_ref = None
        refs = refs[2:]
    (mu_ref, vec_ref, w2_ref, a2_ref, g2_ref, e_ref,
     r_out, d_out, k_out, v_out, kap_out, b_out, g_out, bonus_out, sh_ref) = refs
    i = pl.program_id(0)
    p = p_ref[...]
    sh_ref[0:SUBLANES, :] = halo_ref[...]
    sh_ref[SUBLANES:SUBLANES + tc, :] = p
    rolled = sh_ref[SUBLANES - 1:SUBLANES - 1 + tc, :]
    t_in_seq = (i * tc + lax.broadcasted_iota(jnp.int32, (tc, 1), 0)) % seq
    first = prev_ref[...] if has_prev else jnp.zeros_like(p)
    shifted = jnp.where(t_in_seq == 0, first, rolled)
    xm = p + (shifted - p) * mu_ref[...]
    r = xm[:, 0:RWKV_WIDTH]
    k = xm[:, RWKV_WIDTH:2 * RWKV_WIDTH]
    v = xm[:, 2 * RWKV_WIDTH:3 * RWKV_WIDTH]
    tail = xm[:, LORA0:RW_PAD]
    w0, a0, k_k, k_a, r_k = (vec_ref[n:n + 1, :] for n in range(5))
    w = -jax.nn.softplus(-(w0 + _dot(jnp.tanh(tail).astype(bf16), w2_ref[...]))) - 0.5
    a = jax.nn.sigmoid(a0 + _dot(tail.astype(bf16), a2_ref[...]))
    kk = k * k_k
    e = e_ref[...]
    kap = kk / jnp.maximum(jnp.sqrt(_split_dot(kk * kk, e)), 1e-12)
    kn = k * (1.0 + (a - 1.0) * k_a)
    r_out[...] = r
    d_out[...] = -jnp.exp(w)
    k_out[...] = kn
    v_out[...] = v
    kap_out[...] = kap
    b_out[...] = kap * a
    g_out[...] = _dot(jax.nn.sigmoid(tail).astype(bf16), g2_ref[...])
    bonus_out[...] = _split_dot(r * kn * r_k, e) * v


def rwkv_prepare(p_all, prev_exp, mu, vecs, w2p, a2p, g2p, e, seq, tc):
    m = p_all.shape[0]
    has_prev = prev_exp is not None
    row = lambda i: (i, 0)
    const = lambda i: (0, 0)
    halo = lambda i: (jnp.maximum(i * (tc // SUBLANES) - 1, 0), 0)
    in_specs = [pl.BlockSpec((tc, RW_PAD), row), pl.BlockSpec((SUBLANES, RW_PAD), halo)]
    args = [p_all, p_all]
    if has_prev:
        in_specs.append(pl.BlockSpec((tc, RW_PAD), row))
        args.append(prev_exp)
    in_specs += [pl.BlockSpec((1, RW_PAD), const), pl.BlockSpec((SUBLANES, RWKV_WIDTH), const),
                 pl.BlockSpec((LORA_PAD, RWKV_WIDTH), const), pl.BlockSpec((LORA_PAD, RWKV_WIDTH), const),
                 pl.BlockSpec((LORA_PAD, RWKV_WIDTH), const), pl.BlockSpec((RWKV_WIDTH, RWKV_WIDTH), const)]
    args += [mu, vecs, w2p, a2p, g2p, e]
    return pl.pallas_call(
        functools.partial(_rwkv_prep_body, tc=tc, seq=seq, has_prev=has_prev),
        grid=(m // tc,),
        in_specs=in_specs,
        out_specs=[pl.BlockSpec((tc, RWKV_WIDTH), row)] * 8,
        out_shape=[jax.ShapeDtypeStruct((m, RWKV_WIDTH), f32)] * 8,
        scratch_shapes=[pltpu.VMEM((tc + SUBLANES, RW_PAD), f32)],
        compiler_params=_cparams(("arbitrary",)),
        name="rwkv_prepare",
    )(*args)


RWKV_CHUNK = 64
RWKV_PAIRS_PER_STEP = 8


def _split(x):
    hi = x.astype(bf16)
    return hi, (x - hi.astype(f32)).astype(bf16)


def _dot3(a, b, nt=False):
    d = _dot_nt if nt else _dot
    a_hi, a_lo = _split(a)
    b_hi, b_lo = _split(b)
    return d(a_hi, b_hi) + d(a_hi, b_lo) + d(a_lo, b_hi)


def _dot1(a, b):
    return _dot(a.astype(bf16), b.astype(bf16))


def _rwkv_chunk_body(r_ref, ld_ref, k_ref, v_ref, kap_ref, b_ref, s0_ref, y_ref, st_ref, s_ref):
    c = pl.program_id(2)
    C = RWKV_CHUNK
    zero = jnp.zeros((HEAD_DIM, HEAD_DIM), f32)

    @pl.when(c == 0)
    def _():
        for p in range(RWKV_PAIRS_PER_STEP):
            top = jnp.concatenate([s0_ref[0, 2 * p], zero], axis=1)
            bot = jnp.concatenate([zero, s0_ref[0, 2 * p + 1]], axis=1)
            s_ref[p] = jnp.concatenate([top, bot], axis=0)

    row = lax.broadcasted_iota(jnp.int32, (2 * C, LANES), 0)
    lane = lax.broadcasted_iota(jnp.int32, (2 * C, LANES), 1)
    top, bot = (row < C) & (lane < HEAD_DIM), (row >= C) & (lane >= HEAD_DIM)
    strict, incl = (lane % HEAD_DIM) < (row % C), (lane % HEAD_DIM) <= (row % C)
    eye = jnp.where(row == lane, 1.0, 0.0)
    lane1 = lax.broadcasted_iota(jnp.int32, (1, LANES), 1)
    m_a, m_b = jnp.where(lane1 < HEAD_DIM, 1.0, 0.0), jnp.where(lane1 >= HEAD_DIM, 1.0, 0.0)
    stack = lambda x: jnp.concatenate([x * m_a, x * m_b], axis=0)
    tri = jnp.where(lax.broadcasted_iota(jnp.int32, (C, C), 1) <= lax.broadcasted_iota(jnp.int32, (C, C), 0),
                    1.0, 0.0).astype(bf16)
    pick = lambda cond_a, xa, cond_b, xb: jnp.where(cond_a, xa, 0.0) + jnp.where(cond_b, xb, 0.0)

    l_hi, rest = _split(ld_ref[0])
    l_mid, l_lo = _split(rest.astype(f32))
    cum_all = _dot(tri, l_hi) + _dot(tri, l_mid) + _dot(tri, l_lo)

    pairs = range(RWKV_PAIRS_PER_STEP)
    lanes = [slice(p * LANES, (p + 1) * LANES) for p in pairs]
    kst, rst, b_t, k_t, g_end, vst, gram = [], [], [], [], [], [], []
    for p in pairs:
        r_, ld_, k_, kap_, b_ = (ref[0, :, lanes[p]] for ref in (r_ref, ld_ref, k_ref, kap_ref, b_ref))
        cum = cum_all[:, lanes[p]]
        g, g_prev, g_inv = jnp.exp(cum), jnp.exp(cum - ld_), jnp.exp(-cum)
        g_end.append(g[C - 1:C, :])
        kst.append(stack(kap_ * g_prev))
        rst.append(stack(r_ * g))
        b_t.append(b_ * g_inv)
        k_t.append(k_ * g_inv)
        vst.append(stack(v_ref[0, :, lanes[p]]))
        gram.append(_dot3(jnp.concatenate([kst[p], rst[p]], axis=0),
                          jnp.concatenate([b_t[p], k_t[p]], axis=0), nt=True))
    l_p, a_ak, a_rb, a_rk = [], [], [], []
    for p in pairs:
        ga, gr = gram[p][0:2 * C], gram[p][2 * C:4 * C]
        ga_r, gr_r = pltpu.roll(ga, HEAD_DIM, 1), pltpu.roll(gr, HEAD_DIM, 1)
        l_p.append(pick(top & strict, ga, bot & strict, ga_r))
        a_ak.append(pick(top & strict, ga_r, bot & strict, ga))
        a_rb.append(pick(top & incl, gr, bot & incl, gr_r))
        a_rk.append(pick(top & incl, gr_r, bot & incl, gr))
    inv, power = [eye - l for l in l_p], list(l_p)
    for _ in range(C.bit_length() - 2):
        power = [_dot1(m, m) for m in power]
        inv = [_dot1(x, eye + m) for x, m in zip(inv, power)]
    resid = [eye - _dot3(eye + l, x) for l, x in zip(l_p, inv)]
    inv = [x + _dot1(x, rs) for x, rs in zip(inv, resid)]
    sst = [s_ref[p] for p in pairs]
    sst_t = [s.T for s in sst]
    rhs = [_dot3(jnp.concatenate([kst[p], a_ak[p]], axis=1), jnp.concatenate([sst_t[p], vst[p]], axis=0))
           for p in pairs]
    u = [-_dot3(inv[p], rhs[p]) for p in pairs]
    for p in pairs:
        y = _dot1(jnp.concatenate([rst[p], a_rb[p], a_rk[p]], axis=1),
                  jnp.concatenate([sst_t[p], u[p], vst[p]], axis=0))
        y_ref[0, :, lanes[p]] = y[0:C] + y[C:2 * C]
    for p in pairs:
        uv = jnp.concatenate([u[p], vst[p]], axis=0)
        bk = jnp.concatenate([stack(b_t[p] * g_end[p]), stack(k_t[p] * g_end[p])], axis=0)
        s_ref[p] = sst[p] * g_end[p] + _dot3(uv.T, bk)

    @pl.when(c == pl.num_programs(2) - 1)
    def _():
        for p in range(RWKV_PAIRS_PER_STEP):
            st_ref[0, 2 * p] = s_ref[p][0:HEAD_DIM, 0:HEAD_DIM]
            st_ref[0, 2 * p + 1] = s_ref[p][HEAD_DIM:, HEAD_DIM:]


def rwkv_chunk_scan(r, ld, k, v, kap, b, s0):
    nbatch, seq = r.shape[:2]
    width = RWKV_PAIRS_PER_STEP * LANES
    blk = pl.BlockSpec((1, RWKV_CHUNK, width), lambda bi, pg, c: (bi, c, pg))
    st = pl.BlockSpec((1, 2 * RWKV_PAIRS_PER_STEP, HEAD_DIM, HEAD_DIM), lambda bi, pg, c: (bi, pg, 0, 0))
    return pl.pallas_call(
        _rwkv_chunk_body,
        grid=(nbatch, RWKV_WIDTH // width, seq // RWKV_CHUNK),
        in_specs=[blk] * 6 + [st],
        out_specs=[blk, st],
        out_shape=[jax.ShapeDtypeStruct((nbatch, seq, RWKV_WIDTH), f32),
                   jax.ShapeDtypeStruct((nbatch, RWKV_HEADS, HEAD_DIM, HEAD_DIM), f32)],
        scratch_shapes=[pltpu.VMEM((RWKV_PAIRS_PER_STEP, 2 * RWKV_CHUNK, LANES), f32)],
        compiler_params=_cparams(("arbitrary", "arbitrary", "arbitrary")),
        name="rwkv_chunk_scan",
    )(r, ld, k, v, kap, b, s0)


def _rwkv_post_body(y_ref, g_ref, bonus_ref, vec_ref, e_ref, o_ref):
    y = y_ref[...]
    e = e_ref[...]
    ln_g, ln_b = vec_ref[5:6, :], vec_ref[6:7, :]
    mu = _split_dot(y, e) * (1.0 / HEAD_DIM)
    yc = y - mu
    var = _split_dot(yc * yc, e) * (1.0 / HEAD_DIM)
    o_ref[...] = (yc * lax.rsqrt(var + GN_EPS) * ln_g + ln_b + bonus_ref[...]) * g_ref[...]


def rwkv_output(y, g, bonus, vecs, e, tc):
    m = y.shape[0]
    row = lambda i: (i, 0)
    const = lambda i: (0, 0)
    return pl.pallas_call(
        _rwkv_post_body,
        grid=(m // tc,),
        in_specs=[pl.BlockSpec((tc, RWKV_WIDTH), row)] * 3 +
                 [pl.BlockSpec((SUBLANES, RWKV_WIDTH), const), pl.BlockSpec((RWKV_WIDTH, RWKV_WIDTH), const)],
        out_specs=pl.BlockSpec((tc, RWKV_WIDTH), row),
        out_shape=jax.ShapeDtypeStruct((m, RWKV_WIDTH), f32),
        compiler_params=_cparams(("arbitrary",)),
        name="rwkv_output",
    )(y, g, bonus, vecs, e)


def rwkv7(p_all, prev, s0, wts, nbatch, seq, tc_prep):
    prev_exp = None
    if prev is not None:
        prev_exp = jnp.repeat(jnp.pad(prev, ((0, 0), (0, RW_PAD - RWKV_PROJ))), seq, axis=0)
    r, ld, k, v, kap, b, g, bonus = rwkv_prepare(
        p_all, prev_exp, wts['mu'], wts['vecs'], wts['w2p'], wts['a2p'], wts['g2p'], wts['e'], seq, tc_prep)
    pad = (-seq) % RWKV_CHUNK
    sh = lambda z: jnp.pad(z.reshape(nbatch, seq, RWKV_WIDTH), ((0, 0), (0, pad), (0, 0)))
    y, s_new = rwkv_chunk_scan(sh(r), sh(ld), sh(k), sh(v), sh(kap), sh(b), s0)
    o = rwkv_output(y[:, :seq].reshape(nbatch * seq, RWKV_WIDTH), g, bonus, wts['vecs'], wts['e'], tc_prep)
    return o, s_new


def _compress_head(x_ref, nchunk, wcat_ref, pe_ref, w1f_ref, b1_ref, w2_ref, b2_ref, acc_ref):
    kv2 = 2 * HEAD_DIM
    pe_term = jnp.concatenate([_dot(pe_ref[k], w1f_ref[k])[0:1, :] for k in range(2)], axis=1)
    bias = pe_term + b1_ref[...]
    acc = jnp.zeros((nchunk, 4 * kv2), f32)
    half = CMP_STRIDE // 2
    for r in range(half):
        xr = jnp.concatenate([x_ref[pl.ds(r, nchunk, stride=CMP_STRIDE), :],
                              x_ref[pl.ds(r + half, nchunk, stride=CMP_STRIDE), :]], axis=1)
        acc = acc + _dot(xr.astype(bf16), wcat_ref[r])
    acc_ref[0:nchunk, :] = acc
    acc_ref[nchunk:nchunk + SUBLANES, :] = jnp.zeros((SUBLANES, 4 * kv2), f32)
    first = jnp.concatenate([acc_ref[0:nchunk, 0:kv2], acc_ref[0:nchunk, 2 * kv2:3 * kv2]], axis=1)
    second = jnp.concatenate([acc_ref[1:nchunk + 1, kv2:2 * kv2], acc_ref[1:nchunk + 1, 3 * kv2:4 * kv2]], axis=1)
    hid = jax.nn.gelu(first + second + bias)
    return _dot(hid.astype(bf16), w2_ref[...]) + b2_ref[...]


def _compress_prompt_body(x_ref, wcat_ref, pe_ref, w1f_ref, b1_ref, w2_ref, b2_ref, o_ref, acc_ref, *, nchunk):
    o_ref[...] = _compress_head(x_ref, nchunk, wcat_ref, pe_ref, w1f_ref, b1_ref, w2_ref, b2_ref, acc_ref)


def _compress_weights(lp):
    w1 = lp['cmp_w1']
    z = jnp.zeros((CMP_STRIDE, HEAD_DIM, CMP_HIDDEN), f32)
    key_rows = jnp.concatenate([w1[0, :CMP_STRIDE], w1[0, CMP_STRIDE:], z, z], axis=-1)
    val_rows = jnp.concatenate([z, z, w1[1, :CMP_STRIDE], w1[1, CMP_STRIDE:]], axis=-1)
    wcat = jnp.concatenate([key_rows, val_rows], axis=1)
    wcat = jnp.concatenate([wcat[:CMP_STRIDE // 2], wcat[CMP_STRIDE // 2:]], axis=1).astype(bf16)
    pe = jnp.broadcast_to(lp['cmp_pe'].reshape(2, 1, CMP_BLOCK * HEAD_DIM), (2, SUBLANES, CMP_BLOCK * HEAD_DIM))
    w2 = lp['cmp_w2']
    zz = jnp.zeros((CMP_HIDDEN, HEAD_DIM), f32)
    w2bd = jnp.concatenate([jnp.concatenate([w2[0], zz], axis=1), jnp.concatenate([zz, w2[1]], axis=1)], axis=0)
    return {'wcat': wcat, 'pe': pe.astype(bf16),
            'w1f': w1.reshape(2, CMP_BLOCK * HEAD_DIM, CMP_HIDDEN).astype(bf16),
            'b1': lp['cmp_b1'].reshape(1, 2 * CMP_HIDDEN), 'w2bd': w2bd.astype(bf16),
            'b2': lp['cmp_b2'].reshape(1, 2 * HEAD_DIM)}


def _cmp_weight_specs():
    c2 = lambda *a: (0, 0)
    c3 = lambda *a: (0, 0, 0)
    return [pl.BlockSpec((CMP_STRIDE // 2, 4 * HEAD_DIM, 4 * CMP_HIDDEN), c3),
            pl.BlockSpec((2, SUBLANES, CMP_BLOCK * HEAD_DIM), c3),
            pl.BlockSpec((2, CMP_BLOCK * HEAD_DIM, CMP_HIDDEN), c3),
            pl.BlockSpec((1, 2 * CMP_HIDDEN), c2),
            pl.BlockSpec((2 * CMP_HIDDEN, 2 * HEAD_DIM), c2),
            pl.BlockSpec((1, 2 * HEAD_DIM), c2)]


def compress_prompt(p_all, cw, nbatch, seq):
    nchunk = seq // CMP_STRIDE
    kv2 = 2 * HEAD_DIM
    return pl.pallas_call(
        functools.partial(_compress_prompt_body, nchunk=nchunk),
        grid=(nbatch, NSA_KV_HEADS),
        in_specs=[pl.BlockSpec((seq, kv2), lambda b, h: (b, KVC0 // kv2 + h))] + _cmp_weight_specs(),
        out_specs=pl.BlockSpec((nchunk, kv2), lambda b, h: (b, h)),
        out_shape=jax.ShapeDtypeStruct((nbatch * nchunk, KV_WIDTH), f32),
        scratch_shapes=[pltpu.VMEM((nchunk + SUBLANES, 4 * CMP_HIDDEN), f32)],
        compiler_params=_cparams(("arbitrary", "arbitrary")),
        name="compress_prompt",
    )(p_all, cw['wcat'], cw['pe'], cw['w1f'], cw['b1'], cw['w2bd'], cw['b2'])


KEY_TILE = 256


def _masked_softmax(s, mask):
    s = jnp.where(mask, s, NEG_INF)
    m = jnp.max(s, axis=-1, keepdims=True)
    e = jnp.where(mask, jnp.exp(s - m), 0.0)
    return e / jnp.maximum(jnp.sum(e, axis=-1, keepdims=True), 1e-30)


def _nsa_prompt_body(q_ref, kc_ref, ks_ref, kw_ref, g_ref, wsel_ref, eexp_ref, o_ref, *, nsel, ncmp):
    i = pl.program_id(2)
    qb, grp = Q_BLOCK, NSA_GROUP
    rows = grp * qb
    q = q_ref[...]
    qs = jnp.concatenate([q[:, g * HEAD_DIM:(g + 1) * HEAD_DIM] for g in range(grp)], axis=0) * (HEAD_DIM ** -0.5)
    qp = jnp.concatenate([qs, jnp.zeros_like(qs)], axis=1).astype(bf16)
    qpos = i * qb + lax.broadcasted_iota(jnp.int32, (rows, 1), 0) % qb

    kc = kc_ref[...].astype(bf16)
    s_c = _dot_nt(qp, kc)
    cend = lax.broadcasted_iota(jnp.int32, (1, ncmp), 1) * CMP_STRIDE + (CMP_BLOCK - 1)
    p_c = _masked_softmax(s_c, cend <= qpos)
    o_c = _dot(p_c.astype(bf16), kc)
    imp = p_c[0:qb] + p_c[qb:2 * qb] + p_c[2 * qb:3 * qb] + p_c[3 * qb:4 * qb]
    imp_hi = imp.astype(bf16)
    imp_lo = (imp - imp_hi.astype(f32)).astype(bf16)
    wsel = wsel_ref[...]
    p_slc = _dot_nt(wsel, imp_hi) + _dot_nt(wsel, imp_lo)

    blk = lax.broadcasted_iota(jnp.int32, (nsel, qb), 0)
    qpos_t = i * qb + lax.broadcasted_iota(jnp.int32, (nsel, qb), 1)
    cur = qpos_t // SEL_BLOCK
    forced = (blk == 0) | (blk == cur) | (blk == cur - 1)
    score = jnp.where(forced, FORCE_SCORE, p_slc)
    score = jnp.where(blk * SEL_BLOCK <= qpos_t, score, -1.0)
    rank = jnp.zeros((nsel, qb), f32)
    for jp in range(nsel):
        row = score[jp:jp + 1, :]
        ahead = (row > score) | ((row == score) & (blk > jp))
        rank = rank + jnp.where(ahead, 1.0, 0.0)
    sel_t = jnp.where((rank < float(min(SEL_TOPK, nsel))) & (score >= 0.0), 1.0, 0.0)
    sel = sel_t.T.astype(bf16)

    qpos_q = qpos[0:qb]
    ones_keys = lax.broadcasted_iota(jnp.int32, (1, 2 * HEAD_DIM), 1) < HEAD_DIM

    ntile = ks_ref.shape[0] // KEY_TILE

    def attend(ref, lo, hi, pair, carry, mask_fn):
        tiles = []
        for j in range(2):
            kt = lo + 2 * pair + j
            kt_c = jnp.minimum(kt, ntile - 1)
            k0 = pl.multiple_of(kt_c * KEY_TILE, KEY_TILE)
            kt_tile = ref[pl.ds(k0, KEY_TILE), :].astype(bf16)
            ones_v = jnp.where(ones_keys, jnp.ones_like(kt_tile), kt_tile)
            mask = mask_fn(kt_c, k0 + lax.broadcasted_iota(jnp.int32, (1, KEY_TILE), 1)) & (kt <= hi)
            tiles.append((kt_tile, ones_v, mask))
        scores = [[_dot_nt(qp[g * qb:(g + 1) * qb], kt_tile) for g in range(grp)] for kt_tile, _, _ in tiles]
        state = list(carry)
        for j, (_, ones_v, mask) in enumerate(tiles):
            m_new, probs = [], []
            for g in range(grp):
                s = jnp.where(mask, scores[j][g], NEG_INF)
                m_new.append(jnp.maximum(state[g][0], jnp.max(s, axis=-1, keepdims=True)))
                probs.append(jnp.where(mask, jnp.exp(s - m_new[g]), 0.0).astype(bf16))
            pv = [_dot(probs[g], ones_v) for g in range(grp)]
            state = [(m_new[g], jnp.exp(state[g][0] - m_new[g]) * state[g][1] + pv[g]) for g in range(grp)]
        return tuple(state)

    def sel_mask(kt, kpos):
        return (_dot(sel, eexp_ref[kt]) > 0.5) & (kpos <= qpos_q)

    def win_mask(kt, kpos):
        diff = qpos_q - kpos
        return (diff >= 0) & (diff <= WINDOW)

    per_tile = KEY_TILE // qb
    init = tuple((jnp.full((qb, 1), NEG_INF, f32), jnp.zeros((qb, 2 * HEAD_DIM), f32)) for _ in range(grp))
    hi = i // per_tile
    lo_w = jnp.maximum(i - WINDOW // qb, 0) // per_tile
    res_s = lax.fori_loop(0, hi // 2 + 1, lambda pr, c: attend(ks_ref, 0, hi, pr, c, sel_mask), init)
    res_w = lax.fori_loop(0, (hi - lo_w) // 2 + 1, lambda pr, c: attend(kw_ref, lo_w, hi, pr, c, win_mask), init)
    finish = lambda res: jnp.concatenate([acc / jnp.maximum(acc[:, 0:1], 1e-30) for _, acc in res], axis=0)
    o_s, o_w = finish(res_s), finish(res_w)

    gates = g_ref[0, 0]
    gate = lambda br: jnp.concatenate([gates[:, 3 * g + br:3 * g + br + 1] for g in range(grp)], axis=0)
    out = gate(0) * o_c + gate(1) * o_s + gate(2) * o_w
    o_ref[...] = jnp.concatenate([out[g * qb:(g + 1) * qb, HEAD_DIM:] for g in range(grp)], axis=1)


def nsa_prompt(p_all, kv_cmp, gates, nbatch, seq):
    nqb, nsel, ncmp = seq // Q_BLOCK, seq // SEL_BLOCK, seq // CMP_STRIDE
    kv2 = 2 * HEAD_DIM
    cidx = jnp.arange(ncmp)[None, :] - (SEL_BLOCK // CMP_STRIDE) * jnp.arange(nsel)[:, None]
    mult = jnp.array([1, 2, 2, 2, 1], f32)
    wsel = jnp.where((cidx >= 0) & (cidx <= 4), mult[jnp.clip(cidx, 0, 4)], 0.0).astype(bf16)
    ntile = seq // KEY_TILE
    key_blk = (jnp.arange(ntile)[:, None, None] * KEY_TILE + jnp.arange(KEY_TILE)[None, None, :]) // SEL_BLOCK
    eexp = (key_blk == jnp.arange(nsel)[None, :, None]).astype(bf16)
    return pl.pallas_call(
        functools.partial(_nsa_prompt_body, nsel=nsel, ncmp=ncmp),
        grid=(nbatch, NSA_KV_HEADS, nqb),
        in_specs=[pl.BlockSpec((Q_BLOCK, NSA_GROUP * HEAD_DIM),
                               lambda b, h, i: (b * nqb + i, Q0 // (NSA_GROUP * HEAD_DIM) + h)),
                  pl.BlockSpec((ncmp, kv2), lambda b, h, i: (b, h)),
                  pl.BlockSpec((seq, kv2), lambda b, h, i: (b, KVS0 // kv2 + h)),
                  pl.BlockSpec((seq, kv2), lambda b, h, i: (b, KVW0 // kv2 + h)),
                  pl.BlockSpec((1, 1, Q_BLOCK, 16), lambda b, h, i: (b, h, i, 0)),
                  pl.BlockSpec((nsel, ncmp), lambda b, h, i: (0, 0)),
                  pl.BlockSpec((ntile, nsel, KEY_TILE), lambda b, h, i: (0, 0, 0))],
        out_specs=pl.BlockSpec((Q_BLOCK, NSA_GROUP * HEAD_DIM), lambda b, h, i: (b * nqb + i, h)),
        out_shape=jax.ShapeDtypeStruct((nbatch * seq, NSA_WIDTH), f32),
        compiler_params=_cparams(("arbitrary", "arbitrary", "arbitrary")),
        name="nsa_prompt",
    )(p_all, kv_cmp, p_all, p_all, gates, wsel, eexp)


def _branch_gates(p_all, nbatch, seq):
    g = p_all[:, GN0:GN0 + 3 * NSA_HEADS].reshape(nbatch, seq, NSA_KV_HEADS, 3 * NSA_GROUP)
    return jnp.pad(jnp.transpose(g, (0, 2, 1, 3)), ((0, 0), (0, 0), (0, 0), (0, 16 - 3 * NSA_GROUP)))


PAGES_PER_STEP = 16
CMP_HALVES = 2
TAIL_ROWS = CMP_STRIDE


def _compress_sample_body(pt_ref, *refs, nchunk, half_rows):
    npg = PAGES_PER_STEP
    pages, (nxt_ref, newc_ref) = refs[:npg], refs[npg:npg + 2]
    wcat_ref, pe_ref, w1f_ref, b1_ref, w2_ref, b2_ref, o_ref, xs_ref, acc_ref = refs[npg + 2:]
    half, s = pl.program_id(1), pl.program_id(2)
    kv2 = 2 * HEAD_DIM
    eye = (lax.broadcasted_iota(jnp.int32, (PAGE_SIZE, PAGE_SIZE), 0) ==
           lax.broadcasted_iota(jnp.int32, (PAGE_SIZE, PAGE_SIZE), 1)).astype(bf16)

    def page_rows(ref):
        stored = jnp.concatenate([ref[0, h, kv] for h in range(NSA_KV_HEADS) for kv in range(2)], axis=0)
        return _dot_nt(eye, stored.astype(bf16))

    rows = [page_rows(pages[k]) for k in range(npg)]
    for k in range(npg):
        base = pl.multiple_of((s * npg + k) * PAGE_SIZE, PAGE_SIZE)
        for h in range(NSA_KV_HEADS):
            xs_ref[h, pl.ds(base, PAGE_SIZE), :] = rows[k][:, h * kv2:(h + 1) * kv2]

    @pl.when(s == pl.num_programs(2) - 1)
    def _():
        nblk = half_rows // CMP_STRIDE
        pad_rows = xs_ref.shape[1] - half_rows - TAIL_ROWS
        nxt = page_rows(nxt_ref)[0:TAIL_ROWS]
        for h in range(NSA_KV_HEADS):
            lanes = slice(h * kv2, (h + 1) * kv2)
            tail = jnp.where(half == CMP_HALVES - 1, newc_ref[0, :, lanes], nxt[:, lanes])
            xs_ref[h, half_rows:half_rows + TAIL_ROWS, :] = tail
            xs_ref[h, half_rows + TAIL_ROWS:, :] = jnp.zeros((pad_rows, kv2), f32)
            out = _compress_head(xs_ref.at[h], nchunk, wcat_ref, pe_ref, w1f_ref, b1_ref, w2_ref, b2_ref, acc_ref)
            o_ref[0, :, lanes] = out[0:nblk]


def compress_sample(cache, page_table, newc, cw):
    nbatch, npages = page_table.shape
    npg = PAGES_PER_STEP
    half_pages = npages // CMP_HALVES
    half_rows = half_pages * PAGE_SIZE
    nsteps = half_pages // npg
    nchunk = half_rows // CMP_STRIDE + SUBLANES
    stored = (1, NSA_KV_HEADS, 2, HEAD_DIM, PAGE_SIZE)
    page_spec = lambda k: pl.BlockSpec(
        stored, lambda b, hf, s, pt: (pt[b, hf * half_pages + s * npg + k], 0, 0, 0, 0))
    nxt_spec = pl.BlockSpec(
        stored, lambda b, hf, s, pt: (pt[b, jnp.minimum((hf + 1) * half_pages, npages - 1)], 0, 0, 0, 0))
    return pl.pallas_call(
        functools.partial(_compress_sample_body, nchunk=nchunk, half_rows=half_rows),
        grid_spec=pltpu.PrefetchScalarGridSpec(
            num_scalar_prefetch=1,
            grid=(nbatch, CMP_HALVES, nsteps),
            in_specs=[page_spec(k) for k in range(npg)] + [nxt_spec] +
                     [pl.BlockSpec((1, TAIL_ROWS, KV_WIDTH), lambda b, hf, s, pt: (b, 0, 0))] + _cmp_weight_specs(),
            out_specs=pl.BlockSpec((1, half_rows // CMP_STRIDE, KV_WIDTH), lambda b, hf, s, pt: (b, hf, 0)),
            scratch_shapes=[pltpu.VMEM((NSA_KV_HEADS, nchunk * CMP_STRIDE, 2 * HEAD_DIM), f32),
                            pltpu.VMEM((nchunk + SUBLANES, 4 * CMP_HIDDEN), f32)]),
        out_shape=jax.ShapeDtypeStruct((nbatch, npages * PAGE_SIZE // CMP_STRIDE, KV_WIDTH), f32),
        compiler_params=_cparams(("arbitrary", "arbitrary", "arbitrary")),
        name="compress_sample",
    )(page_table, *([cache] * (npg + 1)), newc, cw['wcat'], cw['pe'], cw['w1f'], cw['b1'], cw['w2bd'], cw['b2'])


def _split_dot_r(e, x):
    hi = x.astype(bf16)
    lo = (x - hi.astype(f32)).astype(bf16)
    return _dot(e, hi) + _dot(e, lo)


def _softmax_rows(s, mask):
    s = jnp.where(mask, s, NEG_INF)
    m = jnp.max(s, axis=0, keepdims=True)
    e = jnp.where(mask, jnp.exp(s - m), 0.0)
    return e / jnp.maximum(jnp.sum(e, axis=0, keepdims=True), 1e-30)


def _nsa_sample_body(pt_ref, *refs, tnew, nsel, nselp):
    npg = PAGES_PER_STEP
    pages = refs[:npg]
    (qb_ref, qr_ref, kc_ref, win_ref, neww_ref, news_ref, g_ref, wsel_ref, gm_ref, pick_ref, eexp_ref, o_ref,
     selq_ref, score_ref, m_ref, l_ref, acc_ref, oc_ref, ow_ref) = refs[npg:]
    s = pl.program_id(1)
    nselq = selq_ref.shape[1]
    rows_per_head = LANES // NSA_KV_HEADS
    qb = qb_ref[0]
    col = lax.broadcasted_iota(jnp.int32, (1, LANES), 1)
    tq = col % tnew
    qpos = PAST_LEN + tq
    pad_tail = lambda ref: jnp.concatenate(
        [ref[0], jnp.zeros((LANES - TAIL_ROWS, KV_WIDTH), f32)], axis=0).astype(bf16)

    @pl.when(s == 0)
    def _():
        kc = kc_ref[0].astype(bf16)
        ncmp = kc.shape[0]
        cend = lax.broadcasted_iota(jnp.int32, (ncmp, 1), 0) * CMP_STRIDE + (CMP_BLOCK - 1)
        p_c = _softmax_rows(_dot(kc, qb), cend <= qpos)
        oc_ref[...] = _dot(p_c.T.astype(bf16), kc)
        imp = _split_dot(p_c, gm_ref[...])
        p_slc = _split_dot_r(wsel_ref[...], imp)
        blk = lax.broadcasted_iota(jnp.int32, (nselp, LANES), 0)
        cur = qpos // SEL_BLOCK
        forced = (blk == 0) | (blk == cur) | (blk == cur - 1)
        score = jnp.where(forced, FORCE_SCORE, p_slc)
        score = jnp.where((blk * SEL_BLOCK <= qpos) & (blk < nsel), score, -1.0)
        score_ref[...] = score

        def rank_group(gi, rank):
            rows = score_ref[pl.ds(pl.multiple_of(gi * SUBLANES, SUBLANES), SUBLANES), :]
            for j in range(SUBLANES):
                row = rows[j:j + 1, :]
                ahead = (row > score) | ((row == score) & (blk > gi * SUBLANES + j))
                rank = rank + jnp.where(ahead, 1.0, 0.0)
            return rank

        rank = lax.fori_loop(0, nselp // SUBLANES, rank_group, jnp.zeros((nselp, LANES), f32))
        sel_t = jnp.where((rank < float(SEL_TOPK)) & (score >= 0.0), 1.0, 0.0)
        selq_ref[...] = jnp.concatenate([sel_t, jnp.zeros((nselq - nselp, LANES), f32)], axis=0).T

        kw = jnp.concatenate([win_ref[0].astype(bf16), pad_tail(neww_ref)], axis=0)
        idx = lax.broadcasted_iota(jnp.int32, (kw.shape[0], 1), 0)
        wbuf = win_ref.shape[1]
        p_w = _softmax_rows(_dot(kw, qb), (idx <= wbuf + tq) & (idx >= wbuf + tq - WINDOW))
        ow_ref[...] = _dot(p_w.T.astype(bf16), kw)

        m_ref[...] = jnp.full(m_ref.shape, NEG_INF, f32)
        l_ref[...] = jnp.zeros(l_ref.shape, f32)
        acc_ref[...] = jnp.zeros(acc_ref.shape, f32)

    def fold(scores, masks, e_dot_v):
        hrows = [slice(h * rows_per_head, (h + 1) * rows_per_head) for h in range(NSA_KV_HEADS)]
        m_new, alpha, probs = [], [], []
        for h in range(NSA_KV_HEADS):
            sc = jnp.where(masks[h], scores[h], NEG_INF)
            m_old = m_ref[hrows[h], 0:1]
            m_new.append(jnp.maximum(m_old, jnp.max(sc, axis=1, keepdims=True)))
            alpha.append(jnp.exp(m_old - m_new[h]))
            e = jnp.where(masks[h], jnp.exp(sc - m_new[h]), 0.0)
            l_ref[hrows[h], :] = jnp.broadcast_to(
                alpha[h] * l_ref[hrows[h], 0:1] + jnp.sum(e, axis=1, keepdims=True), (rows_per_head, LANES))
            probs.append(e.astype(bf16))
        pv = [e_dot_v(h, probs[h]) for h in range(NSA_KV_HEADS)]
        for h in range(NSA_KV_HEADS):
            acc_ref[hrows[h], :] = alpha[h] * acc_ref[hrows[h], :] + pv[h]
            m_ref[hrows[h], :] = jnp.broadcast_to(m_new[h], (rows_per_head, LANES))

    heads = range(NSA_KV_HEADS)
    hrow = lambda h: slice(h * rows_per_head, (h + 1) * rows_per_head)
    k_t = [jnp.concatenate([pages[k][0, h, 0] for k in range(npg)], axis=1).astype(bf16) for h in heads]
    v_t = [jnp.concatenate([pages[k][0, h, 1] for k in range(npg)], axis=1).astype(bf16) for h in heads]
    step_sel = [_dot(selq_ref[hrow(h), :].astype(bf16), pick_ref[s]) for h in heads]
    picked = [_dot(step_sel[h].astype(bf16), eexp_ref[...]) > 0.5 for h in heads]
    fold([_dot(qr_ref[0, hrow(h), :], k_t[h]) for h in heads], picked, lambda h, e: _dot_nt(e, v_t[h]))

    @pl.when(s == pl.num_programs(1) - 1)
    def _():
        ns = news_ref[0]
        kidx = lax.broadcasted_iota(jnp.int32, (1, TAIL_ROWS), 1)
        t_row = lax.broadcasted_iota(jnp.int32, (rows_per_head, 1), 0) % tnew
        nb = nsel - 1
        k_n = [ns[:, h * 2 * HEAD_DIM:h * 2 * HEAD_DIM + HEAD_DIM].astype(bf16) for h in heads]
        v_n = [ns[:, h * 2 * HEAD_DIM + HEAD_DIM:(h + 1) * 2 * HEAD_DIM].astype(bf16) for h in heads]
        fold([_dot_nt(qr_ref[0, hrow(h), :], k_n[h]) for h in heads],
             [(selq_ref[hrow(h), nb:nb + 1] > 0.5) & (kidx <= t_row) for h in heads],
             lambda h, e: _dot(e, v_n[h]))
        for h in heads:
            rows = hrow(h)
            o_s = acc_ref[rows, :] / jnp.maximum(l_ref[rows, 0:1], 1e-30)
            vals = slice(h * 2 * HEAD_DIM + HEAD_DIM, (h + 1) * 2 * HEAD_DIM)
            o_ref[0, rows, :] = (g_ref[0, 0, rows, 0:HEAD_DIM] * oc_ref[rows, vals] + g_ref[0, 1, rows, 0:HEAD_DIM] * o_s
                                 + g_ref[0, 2, rows, 0:HEAD_DIM] * ow_ref[rows, vals])


def nsa_sample(cache_sel_t, page_table, qblk, qrows, kv_cmp, win_buf, neww, news, gates, tnew):
    nbatch, npages = page_table.shape
    npg = PAGES_PER_STEP
    ncmp = kv_cmp.shape[1]
    nsel = npages * (PAGE_SIZE // SEL_BLOCK) + 1
    nselp = -(-nsel // SUBLANES) * SUBLANES
    cidx = jnp.arange(ncmp)[None, :] - (SEL_BLOCK // CMP_STRIDE) * jnp.arange(nselp)[:, None]
    mult = jnp.array([1, 2, 2, 2, 1], f32)
    wsel = jnp.where((cidx >= 0) & (cidx <= 4), mult[jnp.clip(cidx, 0, 4)], 0.0).astype(bf16)
    c = jnp.arange(LANES)
    same = (c[:, None] // (NSA_GROUP * tnew) == c[None, :] // (NSA_GROUP * tnew)) & (c[:, None] % tnew == c[None, :] % tnew)
    gm = same.astype(bf16)
    nsteps = npages // npg
    blocks_per_step = npg * (PAGE_SIZE // SEL_BLOCK)
    nselq = -(-nselp // LANES) * LANES
    blk_of = jnp.arange(nsteps)[:, None, None] * blocks_per_step + jnp.arange(LANES)[None, None, :]
    pick = ((jnp.arange(nselq)[None, :, None] == blk_of) & (jnp.arange(LANES) < blocks_per_step)).astype(bf16)
    eexp = (jnp.arange(LANES)[:, None] == jnp.arange(npg * PAGE_SIZE)[None, :] // SEL_BLOCK).astype(bf16)
    b3 = lambda b, s, pt: (b, 0, 0)
    c2 = lambda b, s, pt: (0, 0)
    page_spec = lambda k: pl.BlockSpec((1, NSA_KV_HEADS, 2, HEAD_DIM, PAGE_SIZE),
                                       lambda b, s, pt: (pt[b, s * npg + k], 0, 0, 0, 0))
    wbuf = win_buf.shape[1]
    return pl.pallas_call(
        functools.partial(_nsa_sample_body, tnew=tnew, nsel=nsel, nselp=nselp),
        grid_spec=pltpu.PrefetchScalarGridSpec(
            num_scalar_prefetch=1,
            grid=(nbatch, nsteps),
            in_specs=[page_spec(k) for k in range(npg)] + [
                pl.BlockSpec((1, KV_WIDTH, LANES), b3),
                pl.BlockSpec((1, LANES, HEAD_DIM), b3),
                pl.BlockSpec((1, ncmp, KV_WIDTH), b3),
                pl.BlockSpec((1, wbuf, KV_WIDTH), b3),
                pl.BlockSpec((1, TAIL_ROWS, KV_WIDTH), b3),
                pl.BlockSpec((1, TAIL_ROWS, KV_WIDTH), b3),
                pl.BlockSpec((1, 3, LANES, LANES), lambda b, s, pt: (b, 0, 0, 0)),
                pl.BlockSpec((nselp, ncmp), c2),
                pl.BlockSpec((LANES, LANES), c2),
                pl.BlockSpec((nsteps, nselq, LANES), lambda b, s, pt: (0, 0, 0)),
                pl.BlockSpec((LANES, npg * PAGE_SIZE), c2)],
            out_specs=pl.BlockSpec((1, LANES, HEAD_DIM), b3),
            scratch_shapes=[pltpu.VMEM((LANES, nselq), f32), pltpu.VMEM((nselp, LANES), f32),
                            pltpu.VMEM((LANES, LANES), f32), pltpu.VMEM((LANES, LANES), f32),
                            pltpu.VMEM((LANES, HEAD_DIM), f32),
                            pltpu.VMEM((LANES, KV_WIDTH), f32), pltpu.VMEM((LANES, KV_WIDTH), f32)]),
        out_shape=jax.ShapeDtypeStruct((nbatch, LANES, HEAD_DIM), f32),
        compiler_params=_cparams(("arbitrary", "arbitrary")),
        name="nsa_sample",
    )(page_table, *([cache_sel_t] * npg), qblk, qrows, kv_cmp, win_buf, neww, news, gates, wsel, gm, pick, eexp)


def _sample_nsa_inputs(p_all, b, t):
    q = p_all[:, Q0:Q0 + NSA_WIDTH].reshape(b, t, NSA_KV_HEADS, NSA_GROUP, HEAD_DIM) * (HEAD_DIM ** -0.5)
    qt = jnp.transpose(q, (0, 2, 4, 3, 1)).reshape(b, NSA_KV_HEADS, HEAD_DIM, NSA_GROUP * t)
    qt = jnp.pad(qt, ((0, 0), (0, 0), (0, HEAD_DIM), (0, 0)))
    eye = jnp.eye(NSA_KV_HEADS, dtype=f32)
    qblk = (qt[:, :, :, None, :] * eye[None, :, None, :, None]).reshape(b, KV_WIDTH, LANES).astype(bf16)
    qrows = jnp.transpose(q, (0, 2, 3, 1, 4)).reshape(b, LANES, HEAD_DIM).astype(bf16)
    g = p_all[:, GN0:GN0 + 3 * NSA_HEADS].reshape(b, t, NSA_KV_HEADS, NSA_GROUP, 3)
    g = jnp.transpose(g, (0, 4, 2, 3, 1)).reshape(b, 3, LANES)
    gates = jnp.broadcast_to(g[..., None], (b, 3, LANES, LANES))
    tail = lambda c0: jnp.pad(p_all[:, c0:c0 + KV_WIDTH].reshape(b, t, KV_WIDTH), ((0, 0), (0, TAIL_ROWS - t), (0, 0)))
    return qblk, qrows, gates, tail(KVC0), tail(KVS0), tail(KVW0)


def _prepare_weights(lp):
    w_in = lp['w_in']
    nsa_end = NSA_WIDTH + 3 * KV_WIDTH
    w_all = jnp.concatenate([
        w_in[:, NSA_PROJ:NSA_PROJ + RWKV_PROJ], w_in[:, nsa_end:NSA_PROJ],
        jnp.zeros((D_MODEL, RW_PAD - RWKV_PROJ - 3 * NSA_HEADS), w_in.dtype),
        w_in[:, :nsa_end], w_in[:, NSA_PROJ + RWKV_PROJ:]], axis=1).astype(bf16)
    pad_rows = lambda w, r0: jnp.zeros((LORA_PAD, RWKV_WIDTH), f32).at[r0:r0 + w.shape[0]].set(w).astype(bf16)
    head = jnp.arange(RWKV_WIDTH) // HEAD_DIM
    vecs = jnp.stack([lp['rwkv_w0'], lp['rwkv_a0'], lp['rwkv_k_k'], lp['rwkv_k_a'],
                      lp['rwkv_r_k'].reshape(RWKV_WIDTH), lp['rwkv_ln_g'], lp['rwkv_ln_b'],
                      jnp.zeros((RWKV_WIDTH,), f32)])
    return {
        'w_all': w_all,
        'mu': jnp.pad(lp['rwkv_mu'], (0, RW_PAD - RWKV_PROJ)).reshape(1, RW_PAD),
        'vecs': vecs,
        'w2p': pad_rows(lp['rwkv_w2'], 0),
        'a2p': pad_rows(lp['rwkv_a2'], DECAY_LORA),
        'g2p': pad_rows(lp['rwkv_g2'], DECAY_LORA + AAA_LORA),
        'e': (head[:, None] == head[None, :]).astype(bf16),
        'p_nsa': lp['p_nsa'].astype(bf16), 'p_rwkv': lp['p_rwkv'].astype(bf16),
        'w_out': lp['w_out'].astype(bf16),
        'mlp_w1': lp['mlp_w1'].astype(bf16), 'mlp_w2': lp['mlp_w2'].astype(bf16),
    }


DENSE_ROW_TILE = 512
SMALL_ROW_TILE = 256


def _group_forward(x, pos_rows, prev, s0, lp, wts, final_g, o_nsa_fn):
    nbatch, seq = x.shape[:2]
    x2d = x.reshape(nbatch * seq, D_MODEL)
    tm = min(DENSE_ROW_TILE, nbatch * seq)
    ts = min(SMALL_ROW_TILE, nbatch * seq)
    tq, tkv = _rot_tables(pos_rows)
    p_all = input_projection(x2d, lp['norm1_g'], wts['w_all'], tq, tkv, tm)
    o_rw, s_new = rwkv7(p_all, prev, s0, wts, nbatch, seq, ts)
    o_nsa = o_nsa_fn(p_all)
    h, hn = merge_project(o_nsa, o_rw, p_all, x2d, wts['p_nsa'], wts['p_rwkv'], wts['w_out'], lp['norm2_g'], ts)
    y = mlp_residual_norm(hn, h, wts['mlp_w1'], wts['mlp_w2'], final_g, tm)
    return y.reshape(nbatch, seq, D_MODEL), p_all, s_new


def _kv_rows(p_all, col0, nbatch, seq):
    return p_all[:, col0:col0 + KV_WIDTH].reshape(nbatch, seq, NSA_KV_HEADS, 2, HEAD_DIM)


def sample_nsa_attention(p_all, cache_cmp, cache_sel, win_buf, page_table, cw, b, t):
    qblk, qrows, gates, newc, news, neww = _sample_nsa_inputs(p_all, b, t)
    cache_cmp_t = jnp.transpose(cache_cmp, (0, 2, 3, 4, 1))
    cache_sel_t = jnp.transpose(cache_sel, (0, 2, 3, 4, 1))
    kv_cmp = compress_sample(cache_cmp_t, page_table, newc, cw)
    o = nsa_sample(cache_sel_t, page_table, qblk, qrows, kv_cmp,
                   win_buf.reshape(b, win_buf.shape[1], KV_WIDTH), neww, news, gates, t)
    o = o.reshape(b, NSA_KV_HEADS, NSA_GROUP, t, HEAD_DIM)
    return jnp.transpose(o, (0, 3, 1, 2, 4)).reshape(b * t, NSA_WIDTH)


def kernel(x_prompt, x_sample, cache_cmp_kv, cache_sel_kv, state_nsa_win, state_rwkv, state_rwkv_shift,
           page_table, norm1_g, w_in, cmp_pe, cmp_w1, cmp_b1, cmp_w2, cmp_b2, rwkv_mu, rwkv_w0, rwkv_w2,
           rwkv_a0, rwkv_a2, rwkv_g2, rwkv_k_k, rwkv_k_a, rwkv_r_k, rwkv_ln_g, rwkv_ln_b, p_nsa, p_rwkv,
           w_out, norm2_g, mlp_w1, mlp_w2, final_g):
    l = 0
    lp = {'norm1_g': norm1_g[l], 'w_in': w_in[l], 'cmp_pe': cmp_pe[l], 'cmp_w1': cmp_w1[l],
          'cmp_b1': cmp_b1[l], 'cmp_w2': cmp_w2[l], 'cmp_b2': cmp_b2[l], 'rwkv_mu': rwkv_mu[l],
          'rwkv_w0': rwkv_w0[l], 'rwkv_w2': rwkv_w2[l], 'rwkv_a0': rwkv_a0[l], 'rwkv_a2': rwkv_a2[l],
          'rwkv_g2': rwkv_g2[l], 'rwkv_k_k': rwkv_k_k[l], 'rwkv_k_a': rwkv_k_a[l], 'rwkv_r_k': rwkv_r_k[l],
          'rwkv_ln_g': rwkv_ln_g[l], 'rwkv_ln_b': rwkv_ln_b[l], 'p_nsa': p_nsa[l], 'p_rwkv': p_rwkv[l],
          'w_out': w_out[l], 'norm2_g': norm2_g[l], 'mlp_w1': mlp_w1[l], 'mlp_w2': mlp_w2[l]}
    wts = _prepare_weights(lp)
    bp, tp = x_prompt.shape[:2]
    bs, ts = x_sample.shape[:2]
    cw = _compress_weights(lp)
    zero_state = jnp.zeros((bp, RWKV_HEADS, HEAD_DIM, HEAD_DIM), f32)

    def prompt_nsa(p_all):
        return nsa_prompt(p_all, compress_prompt(p_all, cw, bp, tp), _branch_gates(p_all, bp, tp), bp, tp)

    yp, pp, sp = _group_forward(x_prompt, jnp.arange(tp, dtype=jnp.int32), None, zero_state, lp, wts, final_g,
                                prompt_nsa)

    def sample_nsa(p_all):
        return sample_nsa_attention(p_all, cache_cmp_kv[l], cache_sel_kv[l], state_nsa_win[l], page_table, cw, bs, ts)

    pos_s = PAST_LEN + jnp.arange(bs * ts, dtype=jnp.int32) % ts
    ys, ps, ss = _group_forward(x_sample, pos_s, state_rwkv_shift[l], state_rwkv[l], lp, wts, final_g,
                                sample_nsa)
    win_new = [jnp.concatenate([state_nsa_win[l], _kv_rows(ps, KVW0, bs, ts)], axis=1)[:, ts:]]
    wlen = min(WINDOW, tp)
    shift = lambda p_all, nb, t: p_all.reshape(nb, t, P_WIDTH)[:, -1, RW0:RW0 + RWKV_PROJ]
    return (yp, ys,
            _kv_rows(pp, KVC0, bp, tp)[None], _kv_rows(ps, KVC0, bs, ts)[None],
            _kv_rows(pp, KVS0, bp, tp)[None], _kv_rows(ps, KVS0, bs, ts)[None],
            _kv_rows(pp, KVW0, bp, tp)[None, :, tp - wlen:], win_new[0][None],
            sp[None], ss[None],
            shift(pp, bp, tp)[None], shift(ps, bs, ts)[None])
```

```python
import functools

import jax
import jax.numpy as jnp
from jax import lax
from jax.experimental import pallas as pl
from jax.experimental.pallas import tpu as pltpu

D_MODEL = 2048
DEPTH = 1
PAST_LEN = 16384
PAGE_SIZE = 128

HEAD_DIM = 64
NSA_HEADS = D_MODEL // (2 * HEAD_DIM)
NSA_KV_HEADS = NSA_HEADS // 4
NSA_GROUP = NSA_HEADS // NSA_KV_HEADS
NSA_WIDTH = NSA_HEADS * HEAD_DIM
KV_WIDTH = NSA_KV_HEADS * 2 * HEAD_DIM
CMP_BLOCK = 32
CMP_STRIDE = 16
CMP_HIDDEN = 2 * HEAD_DIM
SEL_BLOCK = 64
SEL_TOPK = 16
WINDOW = 512
Q_BLOCK = 128
ROT_DIM = HEAD_DIM // 4
ROPE_THETA = 500000.0
RWKV_HEADS = D_MODEL // (2 * HEAD_DIM)
RWKV_WIDTH = RWKV_HEADS * HEAD_DIM
DECAY_LORA = max(32, int(round(1.8 * D_MODEL ** 0.5 / 32)) * 32)
AAA_LORA = DECAY_LORA
GATE_LORA = max(32, int(round(0.6 * D_MODEL ** 0.8 / 32)) * 32)
RWKV_PROJ = 3 * RWKV_WIDTH + DECAY_LORA + AAA_LORA + GATE_LORA
NSA_PROJ = NSA_WIDTH + 3 * KV_WIDTH + 3 * NSA_HEADS
N_IN = NSA_PROJ + RWKV_PROJ + 2 * D_MODEL
D_FF = 4 * D_MODEL
RMS_EPS = 1e-6
GN_EPS = HEAD_DIM * 1e-5
NEG_INF = -1e30
FORCE_SCORE = 1e6

LANES = 128
SUBLANES = 8
VMEM_LIMIT_BYTES = 56 * 1024 * 1024

RW0 = 0
LORA0 = 3 * RWKV_WIDTH
GN0 = RWKV_PROJ
RW_PAD = 3584
LORA_PAD = RW_PAD - LORA0
Q0 = RW_PAD
KVC0 = Q0 + NSA_WIDTH
KVS0 = KVC0 + KV_WIDTH
KVW0 = KVS0 + KV_WIDTH
GM0 = KVW0 + KV_WIDTH
P_WIDTH = GM0 + 2 * D_MODEL
PROJ_TN = 512

f32 = jnp.float32
bf16 = jnp.bfloat16


def _cparams(sem):
    return pltpu.CompilerParams(dimension_semantics=sem, vmem_limit_bytes=VMEM_LIMIT_BYTES)


def _rms(x, g):
    return x * lax.rsqrt(jnp.mean(x * x, axis=-1, keepdims=True) + RMS_EPS) * g


def _dot(a, b):
    return jnp.dot(a, b, preferred_element_type=f32)


def _dot_nt(a, b):
    return lax.dot_general(a, b, (((1,), (1,)), ((), ())), preferred_element_type=f32)


def _split_dot(x, e):
    hi = x.astype(bf16)
    lo = (x - hi.astype(f32)).astype(bf16)
    return _dot(hi, e) + _dot(lo, e)


def _rot_store(acc, tab_ref, o_ref):
    c, s1, s2 = tab_ref[0], tab_ref[1], tab_ref[2]
    for s in range(acc.shape[1] // LANES):
        x = acc[:, s * LANES:(s + 1) * LANES]
        o_ref[:, s * LANES:(s + 1) * LANES] = (
            x * c + pltpu.roll(x, LANES - ROT_DIM // 2, 1) * s1 + pltpu.roll(x, ROT_DIM // 2, 1) * s2)


def _proj_body(x_ref, g_ref, w_ref, tq_ref, tkv_ref, o_ref, xn_ref):
    j = pl.program_id(1)

    @pl.when(j == 0)
    def _():
        xn_ref[...] = _rms(x_ref[...], g_ref[...]).astype(bf16)

    acc = _dot(xn_ref[...], w_ref[...])
    gn_tile = GN0 // PROJ_TN

    @pl.when(j < gn_tile)
    def _():
        o_ref[...] = acc

    @pl.when(j == gn_tile)
    def _():
        lane = lax.broadcasted_iota(jnp.int32, acc.shape, 1)
        o_ref[...] = jnp.where(lane >= GN0 - gn_tile * PROJ_TN, jax.nn.sigmoid(acc), acc)

    @pl.when((j >= Q0 // PROJ_TN) & (j < KVC0 // PROJ_TN))
    def _():
        _rot_store(acc, tq_ref, o_ref)

    @pl.when((j >= KVC0 // PROJ_TN) & (j < GM0 // PROJ_TN))
    def _():
        _rot_store(acc, tkv_ref, o_ref)

    @pl.when(j >= GM0 // PROJ_TN)
    def _():
        o_ref[...] = jax.nn.sigmoid(acc)


def input_projection(x2d, g, w_all, tq, tkv, tm):
    m = x2d.shape[0]
    nt = tq.shape[1] // tm
    return pl.pallas_call(
        _proj_body,
        grid=(m // tm, P_WIDTH // PROJ_TN),
        in_specs=[pl.BlockSpec((tm, D_MODEL), lambda i, j: (i, 0)),
                  pl.BlockSpec((1, D_MODEL), lambda i, j: (0, 0)),
                  pl.BlockSpec((D_MODEL, PROJ_TN), lambda i, j: (0, j)),
                  pl.BlockSpec((3, tm, LANES), lambda i, j: (0, i % nt, 0)),
                  pl.BlockSpec((3, tm, LANES), lambda i, j: (0, i % nt, 0))],
        out_specs=pl.BlockSpec((tm, PROJ_TN), lambda i, j: (i, j)),
        out_shape=jax.ShapeDtypeStruct((m, P_WIDTH), f32),
        scratch_shapes=[pltpu.VMEM((tm, D_MODEL), bf16)],
        compiler_params=_cparams(("arbitrary", "arbitrary")),
        name="input_projection",
    )(x2d, g.reshape(1, D_MODEL), w_all, tq, tkv)


def _rot_tables(pos):
    half = ROT_DIM // 2
    freqs = jnp.power(jnp.float32(ROPE_THETA), -jnp.arange(half, dtype=f32) * 2.0 / ROT_DIM)
    ang = pos.astype(f32)[:, None] * freqs[None, :]
    cos, sin = jnp.cos(ang), jnp.sin(ang)
    lane = jnp.arange(LANES)

    def build(period):
        l = lane % period
        fi = l % half
        c = jnp.where(l < ROT_DIM, cos[:, fi], 1.0)
        s1 = jnp.where(l < half, -sin[:, fi], 0.0)
        s2 = jnp.where((l >= half) & (l < ROT_DIM), sin[:, fi], 0.0)
        return jnp.stack([c, s1, s2]).astype(f32)

    return build(HEAD_DIM), build(2 * HEAD_DIM)


def _merge_body(on_ref, or_ref, ga_ref, gb_ref, x_ref, pn_ref, pr_ref, wo_ref, g2_ref, h_ref, hn_ref):
    a = _dot(on_ref[...].astype(bf16), pn_ref[...])
    b = _dot(or_ref[...].astype(bf16), pr_ref[...])
    mix = ga_ref[...] * a + gb_ref[...] * b
    h = x_ref[...] + _dot(mix.astype(bf16), wo_ref[...])
    h_ref[...] = h
    hn_ref[...] = _rms(h, g2_ref[...]).astype(bf16)


def merge_project(o_nsa, o_rw, p_all, x2d, pn, pr, wo, g2, tm):
    m = x2d.shape[0]
    const = lambda i: (0, 0)
    return pl.pallas_call(
        _merge_body,
        grid=(m // tm,),
        in_specs=[pl.BlockSpec((tm, NSA_WIDTH), lambda i: (i, 0)),
                  pl.BlockSpec((tm, RWKV_WIDTH), lambda i: (i, 0)),
                  pl.BlockSpec((tm, D_MODEL), lambda i: (i, GM0 // D_MODEL)),
                  pl.BlockSpec((tm, D_MODEL), lambda i: (i, GM0 // D_MODEL + 1)),
                  pl.BlockSpec((tm, D_MODEL), lambda i: (i, 0)),
                  pl.BlockSpec((NSA_WIDTH, D_MODEL), const),
                  pl.BlockSpec((RWKV_WIDTH, D_MODEL), const),
                  pl.BlockSpec((D_MODEL, D_MODEL), const),
                  pl.BlockSpec((1, D_MODEL), const)],
        out_specs=[pl.BlockSpec((tm, D_MODEL), lambda i: (i, 0)),
                   pl.BlockSpec((tm, D_MODEL), lambda i: (i, 0))],
        out_shape=[jax.ShapeDtypeStruct((m, D_MODEL), f32),
                   jax.ShapeDtypeStruct((m, D_MODEL), bf16)],
        compiler_params=_cparams(("arbitrary",)),
        name="merge_project",
    )(o_nsa, o_rw, p_all, p_all, x2d, pn, pr, wo, g2.reshape(1, D_MODEL))


def _mlp_body(hn_ref, h_ref, w1_ref, w2_ref, fg_ref, y_ref, acc_ref):
    f = pl.program_id(1)
    u = jnp.square(jnp.maximum(_dot(hn_ref[...], w1_ref[...]), 0.0)).astype(bf16)
    contrib = _dot(u, w2_ref[...])

    @pl.when(f == 0)
    def _():
        acc_ref[...] = contrib

    @pl.when(f > 0)
    def _():
        acc_ref[...] += contrib

    @pl.when(f == pl.num_programs(1) - 1)
    def _():
        y_ref[...] = _rms(h_ref[...] + acc_ref[...], fg_ref[...])


MLP_FF_TILE = 1024


def mlp_residual_norm(hn, h, w1, w2, fg, tm, tf=MLP_FF_TILE):
    m = h.shape[0]
    return pl.pallas_call(
        _mlp_body,
        grid=(m // tm, D_FF // tf),
        in_specs=[pl.BlockSpec((tm, D_MODEL), lambda i, f: (i, 0)),
                  pl.BlockSpec((tm, D_MODEL), lambda i, f: (i, 0)),
                  pl.BlockSpec((D_MODEL, tf), lambda i, f: (0, f)),
                  pl.BlockSpec((tf, D_MODEL), lambda i, f: (f, 0)),
                  pl.BlockSpec((1, D_MODEL), lambda i, f: (0, 0))],
        out_specs=pl.BlockSpec((tm, D_MODEL), lambda i, f: (i, 0)),
        out_shape=jax.ShapeDtypeStruct((m, D_MODEL), f32),
        scratch_shapes=[pltpu.VMEM((tm, D_MODEL), f32)],
        compiler_params=_cparams(("arbitrary", "arbitrary")),
        name="mlp_residual_norm",
    )(hn, h, w1, w2, fg.reshape(1, D_MODEL))


def _rwkv_prep_body(*refs, tc, seq, has_prev):
    if has_prev:
        p_ref, halo_ref, prev_ref = refs[:3]
        refs = refs[3:]
    else:
        p_ref, halo_ref = refs[:2]
        prev_ref = None
        refs = refs[2:]
    (mu_ref, vec_ref, w2_ref, a2_ref, g2_ref, e_ref,
     r_out, d_out, k_out, v_out, kap_out, b_out, g_out, bonus_out, sh_ref) = refs
    i = pl.program_id(0)
    p = p_ref[...]
    sh_ref[0:SUBLANES, :] = halo_ref[...]
    sh_ref[SUBLANES:SUBLANES + tc, :] = p
    rolled = sh_ref[SUBLANES - 1:SUBLANES - 1 + tc, :]
    t_in_seq = (i * tc + lax.broadcasted_iota(jnp.int32, (tc, 1), 0)) % seq
    first = prev_ref[...] if has_prev else jnp.zeros_like(p)
    shifted = jnp.where(t_in_seq == 0, first, rolled)
    xm = p + (shifted - p) * mu_ref[...]
    r = xm[:, 0:RWKV_WIDTH]
    k = xm[:, RWKV_WIDTH:2 * RWKV_WIDTH]
    v = xm[:, 2 * RWKV_WIDTH:3 * RWKV_WIDTH]
    tail = xm[:, LORA0:RW_PAD]
    w0, a0, k_k, k_a, r_k = (vec_ref[n:n + 1, :] for n in range(5))
    w = -jax.nn.softplus(-(w0 + _dot(jnp.tanh(tail).astype(bf16), w2_ref[...]))) - 0.5
    a = jax.nn.sigmoid(a0 + _dot(tail.astype(bf16), a2_ref[...]))
    kk = k * k_k
    e = e_ref[...]
    kap = kk / jnp.maximum(jnp.sqrt(_split_dot(kk * kk, e)), 1e-12)
    kn = k * (1.0 + (a - 1.0) * k_a)
    r_out[...] = r
    d_out[...] = -jnp.exp(w)
    k_out[...] = kn
    v_out[...] = v
    kap_out[...] = kap
    b_out[...] = kap * a
    g_out[...] = _dot(jax.nn.sigmoid(tail).astype(bf16), g2_ref[...])
    bonus_out[...] = _split_dot(r * kn * r_k, e) * v


def rwkv_prepare(p_all, prev_exp, mu, vecs, w2p, a2p, g2p, e, seq, tc):
    m = p_all.shape[0]
    has_prev = prev_exp is not None
    row = lambda i: (i, 0)
    const = lambda i: (0, 0)
    halo = lambda i: (jnp.maximum(i * (tc // SUBLANES) - 1, 0), 0)
    in_specs = [pl.BlockSpec((tc, RW_PAD), row), pl.BlockSpec((SUBLANES, RW_PAD), halo)]
    args = [p_all, p_all]
    if has_prev:
        in_specs.append(pl.BlockSpec((tc, RW_PAD), row))
        args.append(prev_exp)
    in_specs += [pl.BlockSpec((1, RW_PAD), const), pl.BlockSpec((SUBLANES, RWKV_WIDTH), const),
                 pl.BlockSpec((LORA_PAD, RWKV_WIDTH), const), pl.BlockSpec((LORA_PAD, RWKV_WIDTH), const),
                 pl.BlockSpec((LORA_PAD, RWKV_WIDTH), const), pl.BlockSpec((RWKV_WIDTH, RWKV_WIDTH), const)]
    args += [mu, vecs, w2p, a2p, g2p, e]
    return pl.pallas_call(
        functools.partial(_rwkv_prep_body, tc=tc, seq=seq, has_prev=has_prev),
        grid=(m // tc,),
        in_specs=in_specs,
        out_specs=[pl.BlockSpec((tc, RWKV_WIDTH), row)] * 8,
        out_shape=[jax.ShapeDtypeStruct((m, RWKV_WIDTH), f32)] * 8,
        scratch_shapes=[pltpu.VMEM((tc + SUBLANES, RW_PAD), f32)],
        compiler_params=_cparams(("arbitrary",)),
        name="rwkv_prepare",
    )(*args)


RWKV_CHUNK = 64
RWKV_PAIRS_PER_STEP = 8


def _split(x):
    hi = x.astype(bf16)
    return hi, (x - hi.astype(f32)).astype(bf16)


def _dot3(a, b, nt=False):
    d = _dot_nt if nt else _dot
    a_hi, a_lo = _split(a)
    b_hi, b_lo = _split(b)
    return d(a_hi, b_hi) + d(a_hi, b_lo) + d(a_lo, b_hi)


def _dot1(a, b):
    return _dot(a.astype(bf16), b.astype(bf16))


def _rwkv_chunk_body(r_ref, ld_ref, k_ref, v_ref, kap_ref, b_ref, s0_ref, y_ref, st_ref, s_ref):
    c = pl.program_id(2)
    C = RWKV_CHUNK
    zero = jnp.zeros((HEAD_DIM, HEAD_DIM), f32)

    @pl.when(c == 0)
    def _():
        for p in range(RWKV_PAIRS_PER_STEP):
            top = jnp.concatenate([s0_ref[0, 2 * p], zero], axis=1)
            bot = jnp.concatenate([zero, s0_ref[0, 2 * p + 1]], axis=1)
            s_ref[p] = jnp.concatenate([top, bot], axis=0)

    row = lax.broadcasted_iota(jnp.int32, (2 * C, LANES), 0)
    lane = lax.broadcasted_iota(jnp.int32, (2 * C, LANES), 1)
    top, bot = (row < C) & (lane < HEAD_DIM), (row >= C) & (lane >= HEAD_DIM)
    strict, incl = (lane % HEAD_DIM) < (row % C), (lane % HEAD_DIM) <= (row % C)
    eye = jnp.where(row == lane, 1.0, 0.0)
    lane1 = lax.broadcasted_iota(jnp.int32, (1, LANES), 1)
    m_a, m_b = jnp.where(lane1 < HEAD_DIM, 1.0, 0.0), jnp.where(lane1 >= HEAD_DIM, 1.0, 0.0)
    stack = lambda x: jnp.concatenate([x * m_a, x * m_b], axis=0)
    tri = jnp.where(lax.broadcasted_iota(jnp.int32, (C, C), 1) <= lax.broadcasted_iota(jnp.int32, (C, C), 0),
                    1.0, 0.0).astype(bf16)
    pick = lambda cond_a, xa, cond_b, xb: jnp.where(cond_a, xa, 0.0) + jnp.where(cond_b, xb, 0.0)

    l_hi, rest = _split(ld_ref[0])
    l_mid, l_lo = _split(rest.astype(f32))
    cum_all = _dot(tri, l_hi) + _dot(tri, l_mid) + _dot(tri, l_lo)

    pairs = range(RWKV_PAIRS_PER_STEP)
    lanes = [slice(p * LANES, (p + 1) * LANES) for p in pairs]
    kst, rst, b_t, k_t, g_end, vst, gram = [], [], [], [], [], [], []
    for p in pairs:
        r_, ld_, k_, kap_, b_ = (ref[0, :, lanes[p]] for ref in (r_ref, ld_ref, k_ref, kap_ref, b_ref))
        cum = cum_all[:, lanes[p]]
        g, g_prev, g_inv = jnp.exp(cum), jnp.exp(cum - ld_), jnp.exp(-cum)
        g_end.append(g[C - 1:C, :])
        kst.append(stack(kap_ * g_prev))
        rst.append(stack(r_ * g))
        b_t.append(b_ * g_inv)
        k_t.append(k_ * g_inv)
        vst.append(stack(v_ref[0, :, lanes[p]]))
        gram.append(_dot3(jnp.concatenate([kst[p], rst[p]], axis=0),
                          jnp.concatenate([b_t[p], k_t[p]], axis=0), nt=True))
    l_p, a_ak, a_rb, a_rk = [], [], [], []
    for p in pairs:
        ga, gr = gram[p][0:2 * C], gram[p][2 * C:4 * C]
        ga_r, gr_r = pltpu.roll(ga, HEAD_DIM, 1), pltpu.roll(gr, HEAD_DIM, 1)
        l_p.append(pick(top & strict, ga, bot & strict, ga_r))
        a_ak.append(pick(top & strict, ga_r, bot & strict, ga))
        a_rb.append(pick(top & incl, gr, bot & incl, gr_r))
        a_rk.append(pick(top & incl, gr_r, bot & incl, gr))
    inv, power = [eye - l for l in l_p], list(l_p)
    for _ in range(C.bit_length() - 2):
        power = [_dot1(m, m) for m in power]
        inv = [_dot1(x, eye + m) for x, m in zip(inv, power)]
    resid = [eye - _dot3(eye + l, x) for l, x in zip(l_p, inv)]
    inv = [x + _dot1(x, rs) for x, rs in zip(inv, resid)]
    sst = [s_ref[p] for p in pairs]
    sst_t = [s.T for s in sst]
    rhs = [_dot3(jnp.concatenate([kst[p], a_ak[p]], axis=1), jnp.concatenate([sst_t[p], vst[p]], axis=0))
           for p in pairs]
    u = [-_dot3(inv[p], rhs[p]) for p in pairs]
    for p in pairs:
        y = _dot1(jnp.concatenate([rst[p], a_rb[p], a_rk[p]], axis=1),
                  jnp.concatenate([sst_t[p], u[p], vst[p]], axis=0))
        y_ref[0, :, lanes[p]] = y[0:C] + y[C:2 * C]
    for p in pairs:
        uv = jnp.concatenate([u[p], vst[p]], axis=0)
        bk = jnp.concatenate([stack(b_t[p] * g_end[p]), stack(k_t[p] * g_end[p])], axis=0)
        s_ref[p] = sst[p] * g_end[p] + _dot3(uv.T, bk)

    @pl.when(c == pl.num_programs(2) - 1)
    def _():
        for p in range(RWKV_PAIRS_PER_STEP):
            st_ref[0, 2 * p] = s_ref[p][0:HEAD_DIM, 0:HEAD_DIM]
            st_ref[0, 2 * p + 1] = s_ref[p][HEAD_DIM:, HEAD_DIM:]


def rwkv_chunk_scan(r, ld, k, v, kap, b, s0):
    nbatch, seq = r.shape[:2]
    width = RWKV_PAIRS_PER_STEP * LANES
    blk = pl.BlockSpec((1, RWKV_CHUNK, width), lambda bi, pg, c: (bi, c, pg))
    st = pl.BlockSpec((1, 2 * RWKV_PAIRS_PER_STEP, HEAD_DIM, HEAD_DIM), lambda bi, pg, c: (bi, pg, 0, 0))
    return pl.pallas_call(
        _rwkv_chunk_body,
        grid=(nbatch, RWKV_WIDTH // width, seq // RWKV_CHUNK),
        in_specs=[blk] * 6 + [st],
        out_specs=[blk, st],
        out_shape=[jax.ShapeDtypeStruct((nbatch, seq, RWKV_WIDTH), f32),
                   jax.ShapeDtypeStruct((nbatch, RWKV_HEADS, HEAD_DIM, HEAD_DIM), f32)],
        scratch_shapes=[pltpu.VMEM((RWKV_PAIRS_PER_STEP, 2 * RWKV_CHUNK, LANES), f32)],
        compiler_params=_cparams(("arbitrary", "arbitrary", "arbitrary")),
        name="rwkv_chunk_scan",
    )(r, ld, k, v, kap, b, s0)


def _rwkv_post_body(y_ref, g_ref, bonus_ref, vec_ref, e_ref, o_ref):
    y = y_ref[...]
    e = e_ref[...]
    ln_g, ln_b = vec_ref[5:6, :], vec_ref[6:7, :]
    mu = _split_dot(y, e) * (1.0 / HEAD_DIM)
    yc = y - mu
    var = _split_dot(yc * yc, e) * (1.0 / HEAD_DIM)
    o_ref[...] = (yc * lax.rsqrt(var + GN_EPS) * ln_g + ln_b + bonus_ref[...]) * g_ref[...]


def rwkv_output(y, g, bonus, vecs, e, tc):
    m = y.shape[0]
    row = lambda i: (i, 0)
    const = lambda i: (0, 0)
    return pl.pallas_call(
        _rwkv_post_body,
        grid=(m // tc,),
        in_specs=[pl.BlockSpec((tc, RWKV_WIDTH), row)] * 3 +
                 [pl.BlockSpec((SUBLANES, RWKV_WIDTH), const), pl.BlockSpec((RWKV_WIDTH, RWKV_WIDTH), const)],
        out_specs=pl.BlockSpec((tc, RWKV_WIDTH), row),
        out_shape=jax.ShapeDtypeStruct((m, RWKV_WIDTH), f32),
        compiler_params=_cparams(("arbitrary",)),
        name="rwkv_output",
    )(y, g, bonus, vecs, e)


def rwkv7(p_all, prev, s0, wts, nbatch, seq, tc_prep):
    prev_exp = None
    if prev is not None:
        prev_exp = jnp.repeat(jnp.pad(prev, ((0, 0), (0, RW_PAD - RWKV_PROJ))), seq, axis=0)
    r, ld, k, v, kap, b, g, bonus = rwkv_prepare(
        p_all, prev_exp, wts['mu'], wts['vecs'], wts['w2p'], wts['a2p'], wts['g2p'], wts['e'], seq, tc_prep)
    pad = (-seq) % RWKV_CHUNK
    sh = lambda z: jnp.pad(z.reshape(nbatch, seq, RWKV_WIDTH), ((0, 0), (0, pad), (0, 0)))
    y, s_new = rwkv_chunk_scan(sh(r), sh(ld), sh(k), sh(v), sh(kap), sh(b), s0)
    o = rwkv_output(y[:, :seq].reshape(nbatch * seq, RWKV_WIDTH), g, bonus, wts['vecs'], wts['e'], tc_prep)
    return o, s_new


def _compress_heads(x_refs, nchunk, wcat_ref, pe_ref, w1f_ref, b1_ref, w2_ref, b2_ref, acc_refs):
    kv2 = 2 * HEAD_DIM
    pe_term = jnp.concatenate([_dot(pe_ref[k], w1f_ref[k])[0:1, :] for k in range(2)], axis=1)
    bias = pe_term + b1_ref[...]
    half = CMP_STRIDE // 2
    for x_ref, acc_ref in zip(x_refs, acc_refs):
        acc = jnp.zeros((nchunk, 4 * kv2), f32)
        for r in range(half):
            xr = jnp.concatenate([x_ref[pl.ds(r, nchunk, stride=CMP_STRIDE), :],
                                  x_ref[pl.ds(r + half, nchunk, stride=CMP_STRIDE), :]], axis=1)
            acc = acc + _dot(xr.astype(bf16), wcat_ref[r])
        acc_ref[0:nchunk, :] = acc
        acc_ref[nchunk:nchunk + SUBLANES, :] = jnp.zeros((SUBLANES, 4 * kv2), f32)
    hidden = []
    for acc_ref in acc_refs:
        first = jnp.concatenate([acc_ref[0:nchunk, 0:kv2], acc_ref[0:nchunk, 2 * kv2:3 * kv2]], axis=1)
        second = jnp.concatenate([acc_ref[1:nchunk + 1, kv2:2 * kv2], acc_ref[1:nchunk + 1, 3 * kv2:4 * kv2]], axis=1)
        hidden.append(jax.nn.gelu(first + second + bias).astype(bf16))
    return [_dot(hid, w2_ref[...]) + b2_ref[...] for hid in hidden]


def _compress_prompt_body(x_ref, wcat_ref, pe_ref, w1f_ref, b1_ref, w2_ref, b2_ref, o_ref, acc_ref, *, nchunk):
    o_ref[...] = _compress_heads([x_ref], nchunk, wcat_ref, pe_ref, w1f_ref, b1_ref, w2_ref, b2_ref, [acc_ref])[0]


def _compress_weights(lp):
    w1 = lp['cmp_w1']
    z = jnp.zeros((CMP_STRIDE, HEAD_DIM, CMP_HIDDEN), f32)
    key_rows = jnp.concatenate([w1[0, :CMP_STRIDE], w1[0, CMP_STRIDE:], z, z], axis=-1)
    val_rows = jnp.concatenate([z, z, w1[1, :CMP_STRIDE], w1[1, CMP_STRIDE:]], axis=-1)
    wcat = jnp.concatenate([key_rows, val_rows], axis=1)
    wcat = jnp.concatenate([wcat[:CMP_STRIDE // 2], wcat[CMP_STRIDE // 2:]], axis=1).astype(bf16)
    pe = jnp.broadcast_to(lp['cmp_pe'].reshape(2, 1, CMP_BLOCK * HEAD_DIM), (2, SUBLANES, CMP_BLOCK * HEAD_DIM))
    w2 = lp['cmp_w2']
    zz = jnp.zeros((CMP_HIDDEN, HEAD_DIM), f32)
    w2bd = jnp.concatenate([jnp.concatenate([w2[0], zz], axis=1), jnp.concatenate([zz, w2[1]], axis=1)], axis=0)
    return {'wcat': wcat, 'pe': pe.astype(bf16),
            'w1f': w1.reshape(2, CMP_BLOCK * HEAD_DIM, CMP_HIDDEN).astype(bf16),
            'b1': lp['cmp_b1'].reshape(1, 2 * CMP_HIDDEN), 'w2bd': w2bd.astype(bf16),
            'b2': lp['cmp_b2'].reshape(1, 2 * HEAD_DIM)}


def _cmp_weight_specs():
    c2 = lambda *a: (0, 0)
    c3 = lambda *a: (0, 0, 0)
    return [pl.BlockSpec((CMP_STRIDE // 2, 4 * HEAD_DIM, 4 * CMP_HIDDEN), c3),
            pl.BlockSpec((2, SUBLANES, CMP_BLOCK * HEAD_DIM), c3),
            pl.BlockSpec((2, CMP_BLOCK * HEAD_DIM, CMP_HIDDEN), c3),
            pl.BlockSpec((1, 2 * CMP_HIDDEN), c2),
            pl.BlockSpec((2 * CMP_HIDDEN, 2 * HEAD_DIM), c2),
            pl.BlockSpec((1, 2 * HEAD_DIM), c2)]


def compress_prompt(p_all, cw, nbatch, seq):
    nchunk = seq // CMP_STRIDE
    kv2 = 2 * HEAD_DIM
    return pl.pallas_call(
        functools.partial(_compress_prompt_body, nchunk=nchunk),
        grid=(nbatch, NSA_KV_HEADS),
        in_specs=[pl.BlockSpec((seq, kv2), lambda b, h: (b, KVC0 // kv2 + h))] + _cmp_weight_specs(),
        out_specs=pl.BlockSpec((nchunk, kv2), lambda b, h: (b, h)),
        out_shape=jax.ShapeDtypeStruct((nbatch * nchunk, KV_WIDTH), f32),
        scratch_shapes=[pltpu.VMEM((nchunk + SUBLANES, 4 * CMP_HIDDEN), f32)],
        compiler_params=_cparams(("arbitrary", "arbitrary")),
        name="compress_prompt",
    )(p_all, cw['wcat'], cw['pe'], cw['w1f'], cw['b1'], cw['w2bd'], cw['b2'])


KEY_TILE = 256


def _masked_softmax(s, mask):
    s = jnp.where(mask, s, NEG_INF)
    m = jnp.max(s, axis=-1, keepdims=True)
    e = jnp.where(mask, jnp.exp(s - m), 0.0)
    return e / jnp.maximum(jnp.sum(e, axis=-1, keepdims=True), 1e-30)


def _nsa_prompt_body(q_ref, kc_ref, ks_ref, kw_ref, g_ref, wsel_ref, eexp_ref, o_ref, *, nsel, ncmp):
    i = pl.program_id(2)
    qb, grp = Q_BLOCK, NSA_GROUP
    rows = grp * qb
    q = q_ref[...]
    qs = jnp.concatenate([q[:, g * HEAD_DIM:(g + 1) * HEAD_DIM] for g in range(grp)], axis=0) * (HEAD_DIM ** -0.5)
    qp = jnp.concatenate([qs, jnp.zeros_like(qs)], axis=1).astype(bf16)
    qpos = i * qb + lax.broadcasted_iota(jnp.int32, (rows, 1), 0) % qb

    kc = kc_ref[...].astype(bf16)
    s_c = _dot_nt(qp, kc)
    cend = lax.broadcasted_iota(jnp.int32, (1, ncmp), 1) * CMP_STRIDE + (CMP_BLOCK - 1)
    p_c = _masked_softmax(s_c, cend <= qpos)
    o_c = _dot(p_c.astype(bf16), kc)
    imp = p_c[0:qb] + p_c[qb:2 * qb] + p_c[2 * qb:3 * qb] + p_c[3 * qb:4 * qb]
    imp_hi = imp.astype(bf16)
    imp_lo = (imp - imp_hi.astype(f32)).astype(bf16)
    wsel = wsel_ref[...]
    p_slc = _dot_nt(wsel, imp_hi) + _dot_nt(wsel, imp_lo)

    blk = lax.broadcasted_iota(jnp.int32, (nsel, qb), 0)
    qpos_t = i * qb + lax.broadcasted_iota(jnp.int32, (nsel, qb), 1)
    cur = qpos_t // SEL_BLOCK
    forced = (blk == 0) | (blk == cur) | (blk == cur - 1)
    score = jnp.where(forced, FORCE_SCORE, p_slc)
    score = jnp.where(blk * SEL_BLOCK <= qpos_t, score, -1.0)
    rank = jnp.zeros((nsel, qb), f32)
    for jp in range(nsel):
        row = score[jp:jp + 1, :]
        ahead = (row > score) | ((row == score) & (blk > jp))
        rank = rank + jnp.where(ahead, 1.0, 0.0)
    sel_t = jnp.where((rank < float(min(SEL_TOPK, nsel))) & (score >= 0.0), 1.0, 0.0)
    sel = sel_t.T.astype(bf16)

    qpos_q = qpos[0:qb]
    ones_keys = lax.broadcasted_iota(jnp.int32, (1, 2 * HEAD_DIM), 1) < HEAD_DIM

    ntile = ks_ref.shape[0] // KEY_TILE

    def attend(ref, lo, hi, pair, carry, mask_fn):
        tiles = []
        for j in range(2):
            kt = lo + 2 * pair + j
            kt_c = jnp.minimum(kt, ntile - 1)
            k0 = pl.multiple_of(kt_c * KEY_TILE, KEY_TILE)
            kt_tile = ref[pl.ds(k0, KEY_TILE), :].astype(bf16)
            ones_v = jnp.where(ones_keys, jnp.ones_like(kt_tile), kt_tile)
            mask = mask_fn(kt_c, k0 + lax.broadcasted_iota(jnp.int32, (1, KEY_TILE), 1)) & (kt <= hi)
            tiles.append((kt_tile, ones_v, mask))
        scores = [[_dot_nt(qp[g * qb:(g + 1) * qb], kt_tile) for g in range(grp)] for kt_tile, _, _ in tiles]
        state = list(carry)
        for j, (_, ones_v, mask) in enumerate(tiles):
            m_new, probs = [], []
            for g in range(grp):
                s = jnp.where(mask, scores[j][g], NEG_INF)
                m_new.append(jnp.maximum(state[g][0], jnp.max(s, axis=-1, keepdims=True)))
                probs.append(jnp.where(mask, jnp.exp(s - m_new[g]), 0.0).astype(bf16))
            pv = [_dot(probs[g], ones_v) for g in range(grp)]
            state = [(m_new[g], jnp.exp(state[g][0] - m_new[g]) * state[g][1] + pv[g]) for g in range(grp)]
        return tuple(state)

    def sel_mask(kt, kpos):
        return (_dot(sel, eexp_ref[kt]) > 0.5) & (kpos <= qpos_q)

    def win_mask(kt, kpos):
        diff = qpos_q - kpos
        return (diff >= 0) & (diff <= WINDOW)

    per_tile = KEY_TILE // qb
    init = tuple((jnp.full((qb, 1), NEG_INF, f32), jnp.zeros((qb, 2 * HEAD_DIM), f32)) for _ in range(grp))
    hi = i // per_tile
    lo_w = jnp.maximum(i - WINDOW // qb, 0) // per_tile
    res_s = lax.fori_loop(0, hi // 2 + 1, lambda pr, c: attend(ks_ref, 0, hi, pr, c, sel_mask), init)
    res_w = lax.fori_loop(0, (hi - lo_w) // 2 + 1, lambda pr, c: attend(kw_ref, lo_w, hi, pr, c, win_mask), init)
    finish = lambda res: jnp.concatenate([acc / jnp.maximum(acc[:, 0:1], 1e-30) for _, acc in res], axis=0)
    o_s, o_w = finish(res_s), finish(res_w)

    gates = g_ref[0, 0]
    gate = lambda br: jnp.concatenate([gates[:, 3 * g + br:3 * g + br + 1] for g in range(grp)], axis=0)
    out = gate(0) * o_c + gate(1) * o_s + gate(2) * o_w
    o_ref[...] = jnp.concatenate([out[g * qb:(g + 1) * qb, HEAD_DIM:] for g in range(grp)], axis=1)


def nsa_prompt(p_all, kv_cmp, gates, nbatch, seq):
    nqb, nsel, ncmp = seq // Q_BLOCK, seq // SEL_BLOCK, seq // CMP_STRIDE
    kv2 = 2 * HEAD_DIM
    cidx = jnp.arange(ncmp)[None, :] - (SEL_BLOCK // CMP_STRIDE) * jnp.arange(nsel)[:, None]
    mult = jnp.array([1, 2, 2, 2, 1], f32)
    wsel = jnp.where((cidx >= 0) & (cidx <= 4), mult[jnp.clip(cidx, 0, 4)], 0.0).astype(bf16)
    ntile = seq // KEY_TILE
    key_blk = (jnp.arange(ntile)[:, None, None] * KEY_TILE + jnp.arange(KEY_TILE)[None, None, :]) // SEL_BLOCK
    eexp = (key_blk == jnp.arange(nsel)[None, :, None]).astype(bf16)
    return pl.pallas_call(
        functools.partial(_nsa_prompt_body, nsel=nsel, ncmp=ncmp),
        grid=(nbatch, NSA_KV_HEADS, nqb),
        in_specs=[pl.BlockSpec((Q_BLOCK, NSA_GROUP * HEAD_DIM),
                               lambda b, h, i: (b * nqb + i, Q0 // (NSA_GROUP * HEAD_DIM) + h)),
                  pl.BlockSpec((ncmp, kv2), lambda b, h, i: (b, h)),
                  pl.BlockSpec((seq, kv2), lambda b, h, i: (b, KVS0 // kv2 + h)),
                  pl.BlockSpec((seq, kv2), lambda b, h, i: (b, KVW0 // kv2 + h)),
                  pl.BlockSpec((1, 1, Q_BLOCK, 16), lambda b, h, i: (b, h, i, 0)),
                  pl.BlockSpec((nsel, ncmp), lambda b, h, i: (0, 0)),
                  pl.BlockSpec((ntile, nsel, KEY_TILE), lambda b, h, i: (0, 0, 0))],
        out_specs=pl.BlockSpec((Q_BLOCK, NSA_GROUP * HEAD_DIM), lambda b, h, i: (b * nqb + i, h)),
        out_shape=jax.ShapeDtypeStruct((nbatch * seq, NSA_WIDTH), f32),
        compiler_params=_cparams(("arbitrary", "arbitrary", "arbitrary")),
        name="nsa_prompt",
    )(p_all, kv_cmp, p_all, p_all, gates, wsel, eexp)


def _branch_gates(p_all, nbatch, seq):
    g = p_all[:, GN0:GN0 + 3 * NSA_HEADS].reshape(nbatch, seq, NSA_KV_HEADS, 3 * NSA_GROUP)
    return jnp.pad(jnp.transpose(g, (0, 2, 1, 3)), ((0, 0), (0, 0), (0, 0), (0, 16 - 3 * NSA_GROUP)))


PAGES_PER_STEP = 16
CMP_HALVES = 2
TAIL_ROWS = CMP_STRIDE


def _compress_sample_body(pt_ref, *refs, nchunk, half_rows):
    npg = PAGES_PER_STEP
    pages, (nxt_ref, newc_ref) = refs[:npg], refs[npg:npg + 2]
    wcat_ref, pe_ref, w1f_ref, b1_ref, w2_ref, b2_ref, o_ref, xs_ref, acc_ref = refs[npg + 2:]
    half, s = pl.program_id(1), pl.program_id(2)
    kv2 = 2 * HEAD_DIM
    eye = (lax.broadcasted_iota(jnp.int32, (PAGE_SIZE, PAGE_SIZE), 0) ==
           lax.broadcasted_iota(jnp.int32, (PAGE_SIZE, PAGE_SIZE), 1)).astype(bf16)

    def page_rows(ref):
        stored = jnp.concatenate([ref[0, h, kv] for h in range(NSA_KV_HEADS) for kv in range(2)], axis=0)
        return _dot_nt(eye, stored.astype(bf16))

    rows = [page_rows(pages[k]) for k in range(npg)]
    for k in range(npg):
        base = pl.multiple_of((s * npg + k) * PAGE_SIZE, PAGE_SIZE)
        for h in range(NSA_KV_HEADS):
            xs_ref[h, pl.ds(base, PAGE_SIZE), :] = rows[k][:, h * kv2:(h + 1) * kv2]

    @pl.when(s == pl.num_programs(2) - 1)
    def _():
        nblk = half_rows // CMP_STRIDE
        pad_rows = xs_ref.shape[1] - half_rows - TAIL_ROWS
        nxt = page_rows(nxt_ref)[0:TAIL_ROWS]
        heads = range(NSA_KV_HEADS)
        for h in heads:
            lanes = slice(h * kv2, (h + 1) * kv2)
            tail = jnp.where(half == CMP_HALVES - 1, newc_ref[0, :, lanes], nxt[:, lanes])
            xs_ref[h, half_rows:half_rows + TAIL_ROWS, :] = tail
            xs_ref[h, half_rows + TAIL_ROWS:, :] = jnp.zeros((pad_rows, kv2), f32)
        outs = _compress_heads([xs_ref.at[h] for h in heads], nchunk, wcat_ref, pe_ref, w1f_ref, b1_ref, w2_ref,
                               b2_ref, [acc_ref.at[h] for h in heads])
        for h in heads:
            o_ref[0, :, h * kv2:(h + 1) * kv2] = outs[h][0:nblk]


def compress_sample(cache, page_table, newc, cw):
    nbatch, npages = page_table.shape
    npg = PAGES_PER_STEP
    half_pages = npages // CMP_HALVES
    half_rows = half_pages * PAGE_SIZE
    nsteps = half_pages // npg
    nchunk = half_rows // CMP_STRIDE + SUBLANES
    stored = (1, NSA_KV_HEADS, 2, HEAD_DIM, PAGE_SIZE)
    page_spec = lambda k: pl.BlockSpec(
        stored, lambda b, hf, s, pt: (pt[b, hf * half_pages + s * npg + k], 0, 0, 0, 0))
    nxt_spec = pl.BlockSpec(
        stored, lambda b, hf, s, pt: (pt[b, jnp.minimum((hf + 1) * half_pages, npages - 1)], 0, 0, 0, 0))
    return pl.pallas_call(
        functools.partial(_compress_sample_body, nchunk=nchunk, half_rows=half_rows),
        grid_spec=pltpu.PrefetchScalarGridSpec(
            num_scalar_prefetch=1,
            grid=(nbatch, CMP_HALVES, nsteps),
            in_specs=[page_spec(k) for k in range(npg)] + [nxt_spec] +
                     [pl.BlockSpec((1, TAIL_ROWS, KV_WIDTH), lambda b, hf, s, pt: (b, 0, 0))] + _cmp_weight_specs(),
            out_specs=pl.BlockSpec((1, half_rows // CMP_STRIDE, KV_WIDTH), lambda b, hf, s, pt: (b, hf, 0)),
            scratch_shapes=[pltpu.VMEM((NSA_KV_HEADS, nchunk * CMP_STRIDE, 2 * HEAD_DIM), f32),
                            pltpu.VMEM((NSA_KV_HEADS, nchunk + SUBLANES, 4 * CMP_HIDDEN), f32)]),
        out_shape=jax.ShapeDtypeStruct((nbatch, npages * PAGE_SIZE // CMP_STRIDE, KV_WIDTH), f32),
        compiler_params=_cparams(("arbitrary", "arbitrary", "arbitrary")),
        name="compress_sample",
    )(page_table, *([cache] * (npg + 1)), newc, cw['wcat'], cw['pe'], cw['w1f'], cw['b1'], cw['w2bd'], cw['b2'])


def _split_dot_r(e, x):
    hi = x.astype(bf16)
    lo = (x - hi.astype(f32)).astype(bf16)
    return _dot(e, hi) + _dot(e, lo)


def _softmax_rows(s, mask):
    s = jnp.where(mask, s, NEG_INF)
    m = jnp.max(s, axis=0, keepdims=True)
    e = jnp.where(mask, jnp.exp(s - m), 0.0)
    return e / jnp.maximum(jnp.sum(e, axis=0, keepdims=True), 1e-30)


def _nsa_sample_body(pt_ref, *refs, tnew, nsel, nselp):
    npg = PAGES_PER_STEP
    pages = refs[:npg]
    (qb_ref, qr_ref, kc_ref, win_ref, neww_ref, news_ref, g_ref, wsel_ref, gm_ref, pick_ref, eexp_ref, o_ref,
     selq_ref, score_ref, m_ref, l_ref, acc_ref, oc_ref, ow_ref) = refs[npg:]
    s = pl.program_id(1)
    nselq = selq_ref.shape[1]
    rows_per_head = LANES // NSA_KV_HEADS
    qb = qb_ref[0]
    col = lax.broadcasted_iota(jnp.int32, (1, LANES), 1)
    tq = col % tnew
    qpos = PAST_LEN + tq
    pad_tail = lambda ref: jnp.concatenate(
        [ref[0], jnp.zeros((LANES - TAIL_ROWS, KV_WIDTH), f32)], axis=0).astype(bf16)

    @pl.when(s == 0)
    def _():
        kc = kc_ref[0].astype(bf16)
        ncmp = kc.shape[0]
        cend = lax.broadcasted_iota(jnp.int32, (ncmp, 1), 0) * CMP_STRIDE + (CMP_BLOCK - 1)
        p_c = _softmax_rows(_dot(kc, qb), cend <= qpos)
        oc_ref[...] = _dot(p_c.T.astype(bf16), kc)
        imp = _split_dot(p_c, gm_ref[...])
        p_slc = _split_dot_r(wsel_ref[...], imp)
        blk = lax.broadcasted_iota(jnp.int32, (nselp, LANES), 0)
        cur = qpos // SEL_BLOCK
        forced = (blk == 0) | (blk == cur) | (blk == cur - 1)
        score = jnp.where(forced, FORCE_SCORE, p_slc)
        score = jnp.where((blk * SEL_BLOCK <= qpos) & (blk < nsel), score, -1.0)
        score_ref[...] = score

        def rank_group(gi, rank):
            rows = score_ref[pl.ds(pl.multiple_of(gi * SUBLANES, SUBLANES), SUBLANES), :]
            for j in range(SUBLANES):
                row = rows[j:j + 1, :]
                ahead = (row > score) | ((row == score) & (blk > gi * SUBLANES + j))
                rank = rank + jnp.where(ahead, 1.0, 0.0)
            return rank

        rank = lax.fori_loop(0, nselp // SUBLANES, rank_group, jnp.zeros((nselp, LANES), f32))
        sel_t = jnp.where((rank < float(SEL_TOPK)) & (score >= 0.0), 1.0, 0.0)
        selq_ref[...] = jnp.concatenate([sel_t, jnp.zeros((nselq - nselp, LANES), f32)], axis=0).T

        kw = jnp.concatenate([win_ref[0].astype(bf16), pad_tail(neww_ref)], axis=0)
        idx = lax.broadcasted_iota(jnp.int32, (kw.shape[0], 1), 0)
        wbuf = win_ref.shape[1]
        p_w = _softmax_rows(_dot(kw, qb), (idx <= wbuf + tq) & (idx >= wbuf + tq - WINDOW))
        ow_ref[...] = _dot(p_w.T.astype(bf16), kw)

        m_ref[...] = jnp.full(m_ref.shape, NEG_INF, f32)
        l_ref[...] = jnp.zeros(l_ref.shape, f32)
        acc_ref[...] = jnp.zeros(acc_ref.shape, f32)

    def fold(scores, masks, e_dot_v):
        hrows = [slice(h * rows_per_head, (h + 1) * rows_per_head) for h in range(NSA_KV_HEADS)]
        m_new, alpha, probs = [], [], []
        for h in range(NSA_KV_HEADS):
            sc = jnp.where(masks[h], scores[h], NEG_INF)
            m_old = m_ref[hrows[h], 0:1]
            m_new.append(jnp.maximum(m_old, jnp.max(sc, axis=1, keepdims=True)))
            alpha.append(jnp.exp(m_old - m_new[h]))
            e = jnp.where(masks[h], jnp.exp(sc - m_new[h]), 0.0)
            l_ref[hrows[h], :] = jnp.broadcast_to(
                alpha[h] * l_ref[hrows[h], 0:1] + jnp.sum(e, axis=1, keepdims=True), (rows_per_head, LANES))
            probs.append(e.astype(bf16))
        pv = [e_dot_v(h, probs[h]) for h in range(NSA_KV_HEADS)]
        for h in range(NSA_KV_HEADS):
            acc_ref[hrows[h], :] = alpha[h] * acc_ref[hrows[h], :] + pv[h]
            m_ref[hrows[h], :] = jnp.broadcast_to(m_new[h], (rows_per_head, LANES))

    heads = range(NSA_KV_HEADS)
    hrow = lambda h: slice(h * rows_per_head, (h + 1) * rows_per_head)
    k_t = [jnp.concatenate([pages[k][0, h, 0] for k in range(npg)], axis=1).astype(bf16) for h in heads]
    v_t = [jnp.concatenate([pages[k][0, h, 1] for k in range(npg)], axis=1).astype(bf16) for h in heads]
    step_sel = [_dot(selq_ref[hrow(h), :].astype(bf16), pick_ref[s]) for h in heads]
    picked = [_dot(step_sel[h].astype(bf16), eexp_ref[...]) > 0.5 for h in heads]
    fold([_dot(qr_ref[0, hrow(h), :], k_t[h]) for h in heads], picked, lambda h, e: _dot_nt(e, v_t[h]))

    @pl.when(s == pl.num_programs(1) - 1)
    def _():
        ns = news_ref[0]
        kidx = lax.broadcasted_iota(jnp.int32, (1, TAIL_ROWS), 1)
        t_row = lax.broadcasted_iota(jnp.int32, (rows_per_head, 1), 0) % tnew
        nb = nsel - 1
        k_n = [ns[:, h * 2 * HEAD_DIM:h * 2 * HEAD_DIM + HEAD_DIM].astype(bf16) for h in heads]
        v_n = [ns[:, h * 2 * HEAD_DIM + HEAD_DIM:(h + 1) * 2 * HEAD_DIM].astype(bf16) for h in heads]
        fold([_dot_nt(qr_ref[0, hrow(h), :], k_n[h]) for h in heads],
             [(selq_ref[hrow(h), nb:nb + 1] > 0.5) & (kidx <= t_row) for h in heads],
             lambda h, e: _dot(e, v_n[h]))
        for h in heads:
            rows = hrow(h)
            o_s = acc_ref[rows, :] / jnp.maximum(l_ref[rows, 0:1], 1e-30)
            vals = slice(h * 2 * HEAD_DIM + HEAD_DIM, (h + 1) * 2 * HEAD_DIM)
            o_ref[0, rows, :] = (g_ref[0, 0, rows, 0:HEAD_DIM] * oc_ref[rows, vals] + g_ref[0, 1, rows, 0:HEAD_DIM] * o_s
                                 + g_ref[0, 2, rows, 0:HEAD_DIM] * ow_ref[rows, vals])


def nsa_sample(cache_sel_t, page_table, qblk, qrows, kv_cmp, win_buf, neww, news, gates, tnew):
    nbatch, npages = page_table.shape
    npg = PAGES_PER_STEP
    ncmp = kv_cmp.shape[1]
    nsel = npages * (PAGE_SIZE // SEL_BLOCK) + 1
    nselp = -(-nsel // SUBLANES) * SUBLANES
    cidx = jnp.arange(ncmp)[None, :] - (SEL_BLOCK // CMP_STRIDE) * jnp.arange(nselp)[:, None]
    mult = jnp.array([1, 2, 2, 2, 1], f32)
    wsel = jnp.where((cidx >= 0) & (cidx <= 4), mult[jnp.clip(cidx, 0, 4)], 0.0).astype(bf16)
    c = jnp.arange(LANES)
    same = (c[:, None] // (NSA_GROUP * tnew) == c[None, :] // (NSA_GROUP * tnew)) & (c[:, None] % tnew == c[None, :] % tnew)
    gm = same.astype(bf16)
    nsteps = npages // npg
    blocks_per_step = npg * (PAGE_SIZE // SEL_BLOCK)
    nselq = -(-nselp // LANES) * LANES
    blk_of = jnp.arange(nsteps)[:, None, None] * blocks_per_step + jnp.arange(LANES)[None, None, :]
    pick = ((jnp.arange(nselq)[None, :, None] == blk_of) & (jnp.arange(LANES) < blocks_per_step)).astype(bf16)
    eexp = (jnp.arange(LANES)[:, None] == jnp.arange(npg * PAGE_SIZE)[None, :] // SEL_BLOCK).astype(bf16)
    b3 = lambda b, s, pt: (b, 0, 0)
    c2 = lambda b, s, pt: (0, 0)
    page_spec = lambda k: pl.BlockSpec((1, NSA_KV_HEADS, 2, HEAD_DIM, PAGE_SIZE),
                                       lambda b, s, pt: (pt[b, s * npg + k], 0, 0, 0, 0))
    wbuf = win_buf.shape[1]
    return pl.pallas_call(
        functools.partial(_nsa_sample_body, tnew=tnew, nsel=nsel, nselp=nselp),
        grid_spec=pltpu.PrefetchScalarGridSpec(
            num_scalar_prefetch=1,
            grid=(nbatch, nsteps),
            in_specs=[page_spec(k) for k in range(npg)] + [
                pl.BlockSpec((1, KV_WIDTH, LANES), b3),
                pl.BlockSpec((1, LANES, HEAD_DIM), b3),
                pl.BlockSpec((1, ncmp, KV_WIDTH), b3),
                pl.BlockSpec((1, wbuf, KV_WIDTH), b3),
                pl.BlockSpec((1, TAIL_ROWS, KV_WIDTH), b3),
                pl.BlockSpec((1, TAIL_ROWS, KV_WIDTH), b3),
                pl.BlockSpec((1, 3, LANES, LANES), lambda b, s, pt: (b, 0, 0, 0)),
                pl.BlockSpec((nselp, ncmp), c2),
                pl.BlockSpec((LANES, LANES), c2),
                pl.BlockSpec((nsteps, nselq, LANES), lambda b, s, pt: (0, 0, 0)),
                pl.BlockSpec((LANES, npg * PAGE_SIZE), c2)],
            out_specs=pl.BlockSpec((1, LANES, HEAD_DIM), b3),
            scratch_shapes=[pltpu.VMEM((LANES, nselq), f32), pltpu.VMEM((nselp, LANES), f32),
                            pltpu.VMEM((LANES, LANES), f32), pltpu.VMEM((LANES, LANES), f32),
                            pltpu.VMEM((LANES, HEAD_DIM), f32),
                            pltpu.VMEM((LANES, KV_WIDTH), f32), pltpu.VMEM((LANES, KV_WIDTH), f32)]),
        out_shape=jax.ShapeDtypeStruct((nbatch, LANES, HEAD_DIM), f32),
        compiler_params=_cparams(("arbitrary", "arbitrary")),
        name="nsa_sample",
    )(page_table, *([cache_sel_t] * npg), qblk, qrows, kv_cmp, win_buf, neww, news, gates, wsel, gm, pick, eexp)


def _sample_nsa_inputs(p_all, b, t):
    q = p_all[:, Q0:Q0 + NSA_WIDTH].reshape(b, t, NSA_KV_HEADS, NSA_GROUP, HEAD_DIM) * (HEAD_DIM ** -0.5)
    qt = jnp.transpose(q, (0, 2, 4, 3, 1)).reshape(b, NSA_KV_HEADS, HEAD_DIM, NSA_GROUP * t)
    qt = jnp.pad(qt, ((0, 0), (0, 0), (0, HEAD_DIM), (0, 0)))
    eye = jnp.eye(NSA_KV_HEADS, dtype=f32)
    qblk = (qt[:, :, :, None, :] * eye[None, :, None, :, None]).reshape(b, KV_WIDTH, LANES).astype(bf16)
    qrows = jnp.transpose(q, (0, 2, 3, 1, 4)).reshape(b, LANES, HEAD_DIM).astype(bf16)
    g = p_all[:, GN0:GN0 + 3 * NSA_HEADS].reshape(b, t, NSA_KV_HEADS, NSA_GROUP, 3)
    g = jnp.transpose(g, (0, 4, 2, 3, 1)).reshape(b, 3, LANES)
    gates = jnp.broadcast_to(g[..., None], (b, 3, LANES, LANES))
    tail = lambda c0: jnp.pad(p_all[:, c0:c0 + KV_WIDTH].reshape(b, t, KV_WIDTH), ((0, 0), (0, TAIL_ROWS - t), (0, 0)))
    return qblk, qrows, gates, tail(KVC0), tail(KVS0), tail(KVW0)


def _prepare_weights(lp):
    w_in = lp['w_in']
    nsa_end = NSA_WIDTH + 3 * KV_WIDTH
    w_all = jnp.concatenate([
        w_in[:, NSA_PROJ:NSA_PROJ + RWKV_PROJ], w_in[:, nsa_end:NSA_PROJ],
        jnp.zeros((D_MODEL, RW_PAD - RWKV_PROJ - 3 * NSA_HEADS), w_in.dtype),
        w_in[:, :nsa_end], w_in[:, NSA_PROJ + RWKV_PROJ:]], axis=1).astype(bf16)
    pad_rows = lambda w, r0: jnp.zeros((LORA_PAD, RWKV_WIDTH), f32).at[r0:r0 + w.shape[0]].set(w).astype(bf16)
    head = jnp.arange(RWKV_WIDTH) // HEAD_DIM
    vecs = jnp.stack([lp['rwkv_w0'], lp['rwkv_a0'], lp['rwkv_k_k'], lp['rwkv_k_a'],
                      lp['rwkv_r_k'].reshape(RWKV_WIDTH), lp['rwkv_ln_g'], lp['rwkv_ln_b'],
                      jnp.zeros((RWKV_WIDTH,), f32)])
    return {
        'w_all': w_all,
        'mu': jnp.pad(lp['rwkv_mu'], (0, RW_PAD - RWKV_PROJ)).reshape(1, RW_PAD),
        'vecs': vecs,
        'w2p': pad_rows(lp['rwkv_w2'], 0),
        'a2p': pad_rows(lp['rwkv_a2'], DECAY_LORA),
        'g2p': pad_rows(lp['rwkv_g2'], DECAY_LORA + AAA_LORA),
        'e': (head[:, None] == head[None, :]).astype(bf16),
        'p_nsa': lp['p_nsa'].astype(bf16), 'p_rwkv': lp['p_rwkv'].astype(bf16),
        'w_out': lp['w_out'].astype(bf16),
        'mlp_w1': lp['mlp_w1'].astype(bf16), 'mlp_w2': lp['mlp_w2'].astype(bf16),
    }


DENSE_ROW_TILE = 512
SMALL_ROW_TILE = 256


def _group_forward(x, pos_rows, prev, s0, lp, wts, final_g, o_nsa_fn):
    nbatch, seq = x.shape[:2]
    x2d = x.reshape(nbatch * seq, D_MODEL)
    tm = min(DENSE_ROW_TILE, nbatch * seq)
    ts = min(SMALL_ROW_TILE, nbatch * seq)
    tq, tkv = _rot_tables(pos_rows)
    p_all = input_projection(x2d, lp['norm1_g'], wts['w_all'], tq, tkv, tm)
    o_rw, s_new = rwkv7(p_all, prev, s0, wts, nbatch, seq, ts)
    o_nsa = o_nsa_fn(p_all)
    h, hn = merge_project(o_nsa, o_rw, p_all, x2d, wts['p_nsa'], wts['p_rwkv'], wts['w_out'], lp['norm2_g'], ts)
    y = mlp_residual_norm(hn, h, wts['mlp_w1'], wts['mlp_w2'], final_g, tm)
    return y.reshape(nbatch, seq, D_MODEL), p_all, s_new


def _kv_rows(p_all, col0, nbatch, seq):
    return p_all[:, col0:col0 + KV_WIDTH].reshape(nbatch, seq, NSA_KV_HEADS, 2, HEAD_DIM)


def sample_nsa_attention(p_all, cache_cmp, cache_sel, win_buf, page_table, cw, b, t):
    qblk, qrows, gates, newc, news, neww = _sample_nsa_inputs(p_all, b, t)
    cache_cmp_t = jnp.transpose(cache_cmp, (0, 2, 3, 4, 1))
    cache_sel_t = jnp.transpose(cache_sel, (0, 2, 3, 4, 1))
    kv_cmp = compress_sample(cache_cmp_t, page_table, newc, cw)
    o = nsa_sample(cache_sel_t, page_table, qblk, qrows, kv_cmp,
                   win_buf.reshape(b, win_buf.shape[1], KV_WIDTH), neww, news, gates, t)
    o = o.reshape(b, NSA_KV_HEADS, NSA_GROUP, t, HEAD_DIM)
    return jnp.transpose(o, (0, 3, 1, 2, 4)).reshape(b * t, NSA_WIDTH)


def kernel(x_prompt, x_sample, cache_cmp_kv, cache_sel_kv, state_nsa_win, state_rwkv, state_rwkv_shift,
           page_table, norm1_g, w_in, cmp_pe, cmp_w1, cmp_b1, cmp_w2, cmp_b2, rwkv_mu, rwkv_w0, rwkv_w2,
           rwkv_a0, rwkv_a2, rwkv_g2, rwkv_k_k, rwkv_k_a, rwkv_r_k, rwkv_ln_g, rwkv_ln_b, p_nsa, p_rwkv,
           w_out, norm2_g, mlp_w1, mlp_w2, final_g):
    l = 0
    lp = {'norm1_g': norm1_g[l], 'w_in': w_in[l], 'cmp_pe': cmp_pe[l], 'cmp_w1': cmp_w1[l],
          'cmp_b1': cmp_b1[l], 'cmp_w2': cmp_w2[l], 'cmp_b2': cmp_b2[l], 'rwkv_mu': rwkv_mu[l],
          'rwkv_w0': rwkv_w0[l], 'rwkv_w2': rwkv_w2[l], 'rwkv_a0': rwkv_a0[l], 'rwkv_a2': rwkv_a2[l],
          'rwkv_g2': rwkv_g2[l], 'rwkv_k_k': rwkv_k_k[l], 'rwkv_k_a': rwkv_k_a[l], 'rwkv_r_k': rwkv_r_k[l],
          'rwkv_ln_g': rwkv_ln_g[l], 'rwkv_ln_b': rwkv_ln_b[l], 'p_nsa': p_nsa[l], 'p_rwkv': p_rwkv[l],
          'w_out': w_out[l], 'norm2_g': norm2_g[l], 'mlp_w1': mlp_w1[l], 'mlp_w2': mlp_w2[l]}
    wts = _prepare_weights(lp)
    bp, tp = x_prompt.shape[:2]
    bs, ts = x_sample.shape[:2]
    cw = _compress_weights(lp)
    zero_state = jnp.zeros((bp, RWKV_HEADS, HEAD_DIM, HEAD_DIM), f32)

    def prompt_nsa(p_all):
        return nsa_prompt(p_all, compress_prompt(p_all, cw, bp, tp), _branch_gates(p_all, bp, tp), bp, tp)

    yp, pp, sp = _group_forward(x_prompt, jnp.arange(tp, dtype=jnp.int32), None, zero_state, lp, wts, final_g,
                                prompt_nsa)

    def sample_nsa(p_all):
        return sample_nsa_attention(p_all, cache_cmp_kv[l], cache_sel_kv[l], state_nsa_win[l], page_table, cw, bs, ts)

    pos_s = PAST_LEN + jnp.arange(bs * ts, dtype=jnp.int32) % ts
    ys, ps, ss = _group_forward(x_sample, pos_s, state_rwkv_shift[l], state_rwkv[l], lp, wts, final_g,
                                sample_nsa)
    win_new = [jnp.concatenate([state_nsa_win[l], _kv_rows(ps, KVW0, bs, ts)], axis=1)[:, ts:]]
    wlen = min(WINDOW, tp)
    shift = lambda p_all, nb, t: p_all.reshape(nb, t, P_WIDTH)[:, -1, RW0:RW0 + RWKV_PROJ]
    return (yp, ys,
            _kv_rows(pp, KVC0, bp, tp)[None], _kv_rows(ps, KVC0, bs, ts)[None],
            _kv_rows(pp, KVS0, bp, tp)[None], _kv_rows(ps, KVS0, bs, ts)[None],
            _kv_rows(pp, KVW0, bp, tp)[None, :, tp - wlen:], win_new[0][None],
            sp[None], ss[None],
            shift(pp, bp, tp)[None], shift(ps, bs, ts)[None])
```

```python
import functools

import jax
import jax.numpy as jnp
from jax import lax
from jax.experimental import pallas as pl
from jax.experimental.pallas import tpu as pltpu

D_MODEL = 2048
DEPTH = 1
PAST_LEN = 16384
PAGE_SIZE = 128

HEAD_DIM = 64
NSA_HEADS = D_MODEL // (2 * HEAD_DIM)
NSA_KV_HEADS = NSA_HEADS // 4
NSA_GROUP = NSA_HEADS // NSA_KV_HEADS
NSA_WIDTH = NSA_HEADS * HEAD_DIM
KV_WIDTH = NSA_KV_HEADS * 2 * HEAD_DIM
CMP_BLOCK = 32
CMP_STRIDE = 16
CMP_HIDDEN = 2 * HEAD_DIM
SEL_BLOCK = 64
SEL_TOPK = 16
WINDOW = 512
Q_BLOCK = 128
ROT_DIM = HEAD_DIM // 4
ROPE_THETA = 500000.0
RWKV_HEADS = D_MODEL // (2 * HEAD_DIM)
RWKV_WIDTH = RWKV_HEADS * HEAD_DIM
DECAY_LORA = max(32, int(round(1.8 * D_MODEL ** 0.5 / 32)) * 32)
AAA_LORA = DECAY_LORA
GATE_LORA = max(32, int(round(0.6 * D_MODEL ** 0.8 / 32)) * 32)
RWKV_PROJ = 3 * RWKV_WIDTH + DECAY_LORA + AAA_LORA + GATE_LORA
NSA_PROJ = NSA_WIDTH + 3 * KV_WIDTH + 3 * NSA_HEADS
N_IN = NSA_PROJ + RWKV_PROJ + 2 * D_MODEL
D_FF = 4 * D_MODEL
RMS_EPS = 1e-6
GN_EPS = HEAD_DIM * 1e-5
NEG_INF = -1e30
FORCE_SCORE = 1e6

LANES = 128
SUBLANES = 8
VMEM_LIMIT_BYTES = 56 * 1024 * 1024

RW0 = 0
LORA0 = 3 * RWKV_WIDTH
GN0 = RWKV_PROJ
RW_PAD = 3584
LORA_PAD = RW_PAD - LORA0
Q0 = RW_PAD
KVC0 = Q0 + NSA_WIDTH
KVS0 = KVC0 + KV_WIDTH
KVW0 = KVS0 + KV_WIDTH
GM0 = KVW0 + KV_WIDTH
P_WIDTH = GM0 + 2 * D_MODEL
PROJ_TN = 512

f32 = jnp.float32
bf16 = jnp.bfloat16


def _cparams(sem):
    return pltpu.CompilerParams(dimension_semantics=sem, vmem_limit_bytes=VMEM_LIMIT_BYTES)


def _rms(x, g):
    return x * lax.rsqrt(jnp.mean(x * x, axis=-1, keepdims=True) + RMS_EPS) * g


def _dot(a, b):
    return jnp.dot(a, b, preferred_element_type=f32)


def _dot_nt(a, b):
    return lax.dot_general(a, b, (((1,), (1,)), ((), ())), preferred_element_type=f32)


def _split_dot(x, e):
    hi = x.astype(bf16)
    lo = (x - hi.astype(f32)).astype(bf16)
    return _dot(hi, e) + _dot(lo, e)


def _rot_store(acc, tab_ref, o_ref):
    c, s1, s2 = tab_ref[0], tab_ref[1], tab_ref[2]
    for s in range(acc.shape[1] // LANES):
        x = acc[:, s * LANES:(s + 1) * LANES]
        o_ref[:, s * LANES:(s + 1) * LANES] = (
            x * c + pltpu.roll(x, LANES - ROT_DIM // 2, 1) * s1 + pltpu.roll(x, ROT_DIM // 2, 1) * s2)


def _proj_body(x_ref, g_ref, w_ref, tq_ref, tkv_ref, o_ref, xn_ref):
    j = pl.program_id(1)

    @pl.when(j == 0)
    def _():
        xn_ref[...] = _rms(x_ref[...], g_ref[...]).astype(bf16)

    acc = _dot(xn_ref[...], w_ref[...])
    gn_tile = GN0 // PROJ_TN

    @pl.when(j < gn_tile)
    def _():
        o_ref[...] = acc

    @pl.when(j == gn_tile)
    def _():
        lane = lax.broadcasted_iota(jnp.int32, acc.shape, 1)
        o_ref[...] = jnp.where(lane >= GN0 - gn_tile * PROJ_TN, jax.nn.sigmoid(acc), acc)

    @pl.when((j >= Q0 // PROJ_TN) & (j < KVC0 // PROJ_TN))
    def _():
        _rot_store(acc, tq_ref, o_ref)

    @pl.when((j >= KVC0 // PROJ_TN) & (j < GM0 // PROJ_TN))
    def _():
        _rot_store(acc, tkv_ref, o_ref)

    @pl.when(j >= GM0 // PROJ_TN)
    def _():
        o_ref[...] = jax.nn.sigmoid(acc)


def input_projection(x2d, g, w_all, tq, tkv, tm):
    m = x2d.shape[0]
    nt = tq.shape[1] // tm
    return pl.pallas_call(
        _proj_body,
        grid=(m // tm, P_WIDTH // PROJ_TN),
        in_specs=[pl.BlockSpec((tm, D_MODEL), lambda i, j: (i, 0)),
                  pl.BlockSpec((1, D_MODEL), lambda i, j: (0, 0)),
                  pl.BlockSpec((D_MODEL, PROJ_TN), lambda i, j: (0, j)),
                  pl.BlockSpec((3, tm, LANES), lambda i, j: (0, i % nt, 0)),
                  pl.BlockSpec((3, tm, LANES), lambda i, j: (0, i % nt, 0))],
        out_specs=pl.BlockSpec((tm, PROJ_TN), lambda i, j: (i, j)),
        out_shape=jax.ShapeDtypeStruct((m, P_WIDTH), f32),
        scratch_shapes=[pltpu.VMEM((tm, D_MODEL), bf16)],
        compiler_params=_cparams(("arbitrary", "arbitrary")),
        name="input_projection",
    )(x2d, g.reshape(1, D_MODEL), w_all, tq, tkv)


def _rot_tables(pos):
    half = ROT_DIM // 2
    freqs = jnp.power(jnp.float32(ROPE_THETA), -jnp.arange(half, dtype=f32) * 2.0 / ROT_DIM)
    ang = pos.astype(f32)[:, None] * freqs[None, :]
    cos, sin = jnp.cos(ang), jnp.sin(ang)
    lane = jnp.arange(LANES)

    def build(period):
        l = lane % period
        fi = l % half
        c = jnp.where(l < ROT_DIM, cos[:, fi], 1.0)
        s1 = jnp.where(l < half, -sin[:, fi], 0.0)
        s2 = jnp.where((l >= half) & (l < ROT_DIM), sin[:, fi], 0.0)
        return jnp.stack([c, s1, s2]).astype(f32)

    return build(HEAD_DIM), build(2 * HEAD_DIM)


def _merge_body(on_ref, or_ref, ga_ref, gb_ref, x_ref, pn_ref, pr_ref, wo_ref, g2_ref, h_ref, hn_ref):
    a = _dot(on_ref[...].astype(bf16), pn_ref[...])
    b = _dot(or_ref[...].astype(bf16), pr_ref[...])
    mix = ga_ref[...] * a + gb_ref[...] * b
    h = x_ref[...] + _dot(mix.astype(bf16), wo_ref[...])
    h_ref[...] = h
    hn_ref[...] = _rms(h, g2_ref[...]).astype(bf16)


def merge_project(o_nsa, o_rw, p_all, x2d, pn, pr, wo, g2, tm):
    m = x2d.shape[0]
    const = lambda i: (0, 0)
    return pl.pallas_call(
        _merge_body,
        grid=(m // tm,),
        in_specs=[pl.BlockSpec((tm, NSA_WIDTH), lambda i: (i, 0)),
                  pl.BlockSpec((tm, RWKV_WIDTH), lambda i: (i, 0)),
                  pl.BlockSpec((tm, D_MODEL), lambda i: (i, GM0 // D_MODEL)),
                  pl.BlockSpec((tm, D_MODEL), lambda i: (i, GM0 // D_MODEL + 1)),
                  pl.BlockSpec((tm, D_MODEL), lambda i: (i, 0)),
                  pl.BlockSpec((NSA_WIDTH, D_MODEL), const),
                  pl.BlockSpec((RWKV_WIDTH, D_MODEL), const),
                  pl.BlockSpec((D_MODEL, D_MODEL), const),
                  pl.BlockSpec((1, D_MODEL), const)],
        out_specs=[pl.BlockSpec((tm, D_MODEL), lambda i: (i, 0)),
                   pl.BlockSpec((tm, D_MODEL), lambda i: (i, 0))],
        out_shape=[jax.ShapeDtypeStruct((m, D_MODEL), f32),
                   jax.ShapeDtypeStruct((m, D_MODEL), bf16)],
        compiler_params=_cparams(("arbitrary",)),
        name="merge_project",
    )(o_nsa, o_rw, p_all, p_all, x2d, pn, pr, wo, g2.reshape(1, D_MODEL))


def _mlp_body(hn_ref, h_ref, w1_ref, w2_ref, fg_ref, y_ref, acc_ref):
    f = pl.program_id(1)
    u = jnp.square(jnp.maximum(_dot(hn_ref[...], w1_ref[...]), 0.0)).astype(bf16)
    contrib = _dot(u, w2_ref[...])

    @pl.when(f == 0)
    def _():
        acc_ref[...] = contrib

    @pl.when(f > 0)
    def _():
        acc_ref[...] += contrib

    @pl.when(f == pl.num_programs(1) - 1)
    def _():
        y_ref[...] = _rms(h_ref[...] + acc_ref[...], fg_ref[...])


MLP_FF_TILE = 1024


def mlp_residual_norm(hn, h, w1, w2, fg, tm, tf=MLP_FF_TILE):
    m = h.shape[0]
    return pl.pallas_call(
        _mlp_body,
        grid=(m // tm, D_FF // tf),
        in_specs=[pl.BlockSpec((tm, D_MODEL), lambda i, f: (i, 0)),
                  pl.BlockSpec((tm, D_MODEL), lambda i, f: (i, 0)),
                  pl.BlockSpec((D_MODEL, tf), lambda i, f: (0, f)),
                  pl.BlockSpec((tf, D_MODEL), lambda i, f: (f, 0)),
                  pl.BlockSpec((1, D_MODEL), lambda i, f: (0, 0))],
        out_specs=pl.BlockSpec((tm, D_MODEL), lambda i, f: (i, 0)),
        out_shape=jax.ShapeDtypeStruct((m, D_MODEL), f32),
        scratch_shapes=[pltpu.VMEM((tm, D_MODEL), f32)],
        compiler_params=_cparams(("arbitrary", "arbitrary")),
        name="mlp_residual_norm",
    )(hn, h, w1, w2, fg.reshape(1, D_MODEL))


def _rwkv_prep_body(*refs, tc, seq, has_prev):
    if has_prev:
        p_ref, halo_ref, prev_ref = refs[:3]
        refs = refs[3:]
    else:
        p_ref, halo_ref = refs[:2]
        prev_ref = None
        refs = refs[2:]
    (mu_ref, vec_ref, w2_ref, a2_ref, g2_ref, e_ref,
     r_out, d_out, k_out, v_out, kap_out, b_out, g_out, bonus_out, sh_ref) = refs
    i = pl.program_id(0)
    p = p_ref[...]
    sh_ref[0:SUBLANES, :] = halo_ref[...]
    sh_ref[SUBLANES:SUBLANES + tc, :] = p
    rolled = sh_ref[SUBLANES - 1:SUBLANES - 1 + tc, :]
    t_in_seq = (i * tc + lax.broadcasted_iota(jnp.int32, (tc, 1), 0)) % seq
    first = prev_ref[...] if has_prev else jnp.zeros_like(p)
    shifted = jnp.where(t_in_seq == 0, first, rolled)
    xm = p + (shifted - p) * mu_ref[...]
    r = xm[:, 0:RWKV_WIDTH]
    k = xm[:, RWKV_WIDTH:2 * RWKV_WIDTH]
    v = xm[:, 2 * RWKV_WIDTH:3 * RWKV_WIDTH]
    tail = xm[:, LORA0:RW_PAD]
    w0, a0, k_k, k_a, r_k = (vec_ref[n:n + 1, :] for n in range(5))
    w = -jax.nn.softplus(-(w0 + _dot(jnp.tanh(tail).astype(bf16), w2_ref[...]))) - 0.5
    a = jax.nn.sigmoid(a0 + _dot(tail.astype(bf16), a2_ref[...]))
    kk = k * k_k
    e = e_ref[...]
    kap = kk / jnp.maximum(jnp.sqrt(_split_dot(kk * kk, e)), 1e-12)
    kn = k * (1.0 + (a - 1.0) * k_a)
    r_out[...] = r
    d_out[...] = -jnp.exp(w)
    k_out[...] = kn
    v_out[...] = v
    kap_out[...] = kap
    b_out[...] = kap * a
    g_out[...] = _dot(jax.nn.sigmoid(tail).astype(bf16), g2_ref[...])
    bonus_out[...] = _split_dot(r * kn * r_k, e) * v


def rwkv_prepare(p_all, prev_exp, mu, vecs, w2p, a2p, g2p, e, seq, tc):
    m = p_all.shape[0]
    has_prev = prev_exp is not None
    row = lambda i: (i, 0)
    const = lambda i: (0, 0)
    halo = lambda i: (jnp.maximum(i * (tc // SUBLANES) - 1, 0), 0)
    in_specs = [pl.BlockSpec((tc, RW_PAD), row), pl.BlockSpec((SUBLANES, RW_PAD), halo)]
    args = [p_all, p_all]
    if has_prev:
        in_specs.append(pl.BlockSpec((tc, RW_PAD), row))
        args.append(prev_exp)
    in_specs += [pl.BlockSpec((1, RW_PAD), const), pl.BlockSpec((SUBLANES, RWKV_WIDTH), const),
                 pl.BlockSpec((LORA_PAD, RWKV_WIDTH), const), pl.BlockSpec((LORA_PAD, RWKV_WIDTH), const),
                 pl.BlockSpec((LORA_PAD, RWKV_WIDTH), const), pl.BlockSpec((RWKV_WIDTH, RWKV_WIDTH), const)]
    args += [mu, vecs, w2p, a2p, g2p, e]
    return pl.pallas_call(
        functools.partial(_rwkv_prep_body, tc=tc, seq=seq, has_prev=has_prev),
        grid=(m // tc,),
        in_specs=in_specs,
        out_specs=[pl.BlockSpec((tc, RWKV_WIDTH), row)] * 8,
        out_shape=[jax.ShapeDtypeStruct((m, RWKV_WIDTH), f32)] * 8,
        scratch_shapes=[pltpu.VMEM((tc + SUBLANES, RW_PAD), f32)],
        compiler_params=_cparams(("arbitrary",)),
        name="rwkv_prepare",
    )(*args)


RWKV_CHUNK = 64
RWKV_PAIRS_PER_STEP = 8


def _split(x):
    hi = x.astype(bf16)
    return hi, (x - hi.astype(f32)).astype(bf16)


def _dot3(a, b, nt=False):
    d = _dot_nt if nt else _dot
    a_hi, a_lo = _split(a)
    b_hi, b_lo = _split(b)
    return d(a_hi, b_hi) + d(a_hi, b_lo) + d(a_lo, b_hi)


def _dot1(a, b):
    return _dot(a.astype(bf16), b.astype(bf16))


def _rwkv_chunk_body(r_ref, ld_ref, k_ref, v_ref, kap_ref, b_ref, s0_ref, y_ref, st_ref, s_ref):
    c = pl.program_id(2)
    C = RWKV_CHUNK
    zero = jnp.zeros((HEAD_DIM, HEAD_DIM), f32)

    @pl.when(c == 0)
    def _():
        for p in range(RWKV_PAIRS_PER_STEP):
            top = jnp.concatenate([s0_ref[0, 2 * p], zero], axis=1)
            bot = jnp.concatenate([zero, s0_ref[0, 2 * p + 1]], axis=1)
            s_ref[p] = jnp.concatenate([top, bot], axis=0)

    row = lax.broadcasted_iota(jnp.int32, (2 * C, LANES), 0)
    lane = lax.broadcasted_iota(jnp.int32, (2 * C, LANES), 1)
    top, bot = (row < C) & (lane < HEAD_DIM), (row >= C) & (lane >= HEAD_DIM)
    strict, incl = (lane % HEAD_DIM) < (row % C), (lane % HEAD_DIM) <= (row % C)
    eye = jnp.where(row == lane, 1.0, 0.0)
    lane1 = lax.broadcasted_iota(jnp.int32, (1, LANES), 1)
    m_a, m_b = jnp.where(lane1 < HEAD_DIM, 1.0, 0.0), jnp.where(lane1 >= HEAD_DIM, 1.0, 0.0)
    stack = lambda x: jnp.concatenate([x * m_a, x * m_b], axis=0)
    tri = jnp.where(lax.broadcasted_iota(jnp.int32, (C, C), 1) <= lax.broadcasted_iota(jnp.int32, (C, C), 0),
                    1.0, 0.0).astype(bf16)
    pick = lambda cond_a, xa, cond_b, xb: jnp.where(cond_a, xa, 0.0) + jnp.where(cond_b, xb, 0.0)

    l_hi, rest = _split(ld_ref[0])
    l_mid, l_lo = _split(rest.astype(f32))
    cum_all = _dot(tri, l_hi) + _dot(tri, l_mid) + _dot(tri, l_lo)

    pairs = range(RWKV_PAIRS_PER_STEP)
    lanes = [slice(p * LANES, (p + 1) * LANES) for p in pairs]
    kst, rst, b_t, k_t, g_end, vst, gram = [], [], [], [], [], [], []
    for p in pairs:
        r_, ld_, k_, kap_, b_ = (ref[0, :, lanes[p]] for ref in (r_ref, ld_ref, k_ref, kap_ref, b_ref))
        cum = cum_all[:, lanes[p]]
        g, g_prev, g_inv = jnp.exp(cum), jnp.exp(cum - ld_), jnp.exp(-cum)
        g_end.append(g[C - 1:C, :])
        kst.append(stack(kap_ * g_prev))
        rst.append(stack(r_ * g))
        b_t.append(b_ * g_inv)
        k_t.append(k_ * g_inv)
        vst.append(stack(v_ref[0, :, lanes[p]]))
        gram.append(_dot3(jnp.concatenate([kst[p], rst[p]], axis=0),
                          jnp.concatenate([b_t[p], k_t[p]], axis=0), nt=True))
    l_p, a_ak, a_rb, a_rk = [], [], [], []
    for p in pairs:
        ga, gr = gram[p][0:2 * C], gram[p][2 * C:4 * C]
        ga_r, gr_r = pltpu.roll(ga, HEAD_DIM, 1), pltpu.roll(gr, HEAD_DIM, 1)
        l_p.append(pick(top & strict, ga, bot & strict, ga_r))
        a_ak.append(pick(top & strict, ga_r, bot & strict, ga))
        a_rb.append(pick(top & incl, gr, bot & incl, gr_r))
        a_rk.append(pick(top & incl, gr_r, bot & incl, gr))
    inv, power = [eye - l for l in l_p], list(l_p)
    for _ in range(C.bit_length() - 2):
        power = [_dot1(m, m) for m in power]
        inv = [_dot1(x, eye + m) for x, m in zip(inv, power)]
    resid = [eye - _dot3(eye + l, x) for l, x in zip(l_p, inv)]
    inv = [x + _dot1(x, rs) for x, rs in zip(inv, resid)]
    sst = [s_ref[p] for p in pairs]
    sst_t = [s.T for s in sst]
    rhs = [_dot3(jnp.concatenate([kst[p], a_ak[p]], axis=1), jnp.concatenate([sst_t[p], vst[p]], axis=0))
           for p in pairs]
    u = [-_dot3(inv[p], rhs[p]) for p in pairs]
    for p in pairs:
        y = _dot1(jnp.concatenate([rst[p], a_rb[p], a_rk[p]], axis=1),
                  jnp.concatenate([sst_t[p], u[p], vst[p]], axis=0))
        y_ref[0, :, lanes[p]] = y[0:C] + y[C:2 * C]
    for p in pairs:
        uv = jnp.concatenate([u[p], vst[p]], axis=0)
        bk = jnp.concatenate([stack(b_t[p] * g_end[p]), stack(k_t[p] * g_end[p])], axis=0)
        s_ref[p] = sst[p] * g_end[p] + _dot3(uv.T, bk)

    @pl.when(c == pl.num_programs(2) - 1)
    def _():
        for p in range(RWKV_PAIRS_PER_STEP):
            st_ref[0, 2 * p] = s_ref[p][0:HEAD_DIM, 0:HEAD_DIM]
            st_ref[0, 2 * p + 1] = s_ref[p][HEAD_DIM:, HEAD_DIM:]


def rwkv_chunk_scan(r, ld, k, v, kap, b, s0):
    nbatch, seq = r.shape[:2]
    width = RWKV_PAIRS_PER_STEP * LANES
    blk = pl.BlockSpec((1, RWKV_CHUNK, width), lambda bi, pg, c: (bi, c, pg))
    st = pl.BlockSpec((1, 2 * RWKV_PAIRS_PER_STEP, HEAD_DIM, HEAD_DIM), lambda bi, pg, c: (bi, pg, 0, 0))
    return pl.pallas_call(
        _rwkv_chunk_body,
        grid=(nbatch, RWKV_WIDTH // width, seq // RWKV_CHUNK),
        in_specs=[blk] * 6 + [st],
        out_specs=[blk, st],
        out_shape=[jax.ShapeDtypeStruct((nbatch, seq, RWKV_WIDTH), f32),
                   jax.ShapeDtypeStruct((nbatch, RWKV_HEADS, HEAD_DIM, HEAD_DIM), f32)],
        scratch_shapes=[pltpu.VMEM((RWKV_PAIRS_PER_STEP, 2 * RWKV_CHUNK, LANES), f32)],
        compiler_params=_cparams(("arbitrary", "arbitrary", "arbitrary")),
        name="rwkv_chunk_scan",
    )(r, ld, k, v, kap, b, s0)


def _rwkv_post_body(y_ref, g_ref, bonus_ref, vec_ref, e_ref, o_ref):
    y = y_ref[...]
    e = e_ref[...]
    ln_g, ln_b = vec_ref[5:6, :], vec_ref[6:7, :]
    mu = _split_dot(y, e) * (1.0 / HEAD_DIM)
    yc = y - mu
    var = _split_dot(yc * yc, e) * (1.0 / HEAD_DIM)
    o_ref[...] = (yc * lax.rsqrt(var + GN_EPS) * ln_g + ln_b + bonus_ref[...]) * g_ref[...]


def rwkv_output(y, g, bonus, vecs, e, tc):
    m = y.shape[0]
    row = lambda i: (i, 0)
    const = lambda i: (0, 0)
    return pl.pallas_call(
        _rwkv_post_body,
        grid=(m // tc,),
        in_specs=[pl.BlockSpec((tc, RWKV_WIDTH), row)] * 3 +
                 [pl.BlockSpec((SUBLANES, RWKV_WIDTH), const), pl.BlockSpec((RWKV_WIDTH, RWKV_WIDTH), const)],
        out_specs=pl.BlockSpec((tc, RWKV_WIDTH), row),
        out_shape=jax.ShapeDtypeStruct((m, RWKV_WIDTH), f32),
        compiler_params=_cparams(("arbitrary",)),
        name="rwkv_output",
    )(y, g, bonus, vecs, e)


def rwkv7(p_all, prev, s0, wts, nbatch, seq, tc_prep):
    prev_exp = None
    if prev is not None:
        prev_exp = jnp.repeat(jnp.pad(prev, ((0, 0), (0, RW_PAD - RWKV_PROJ))), seq, axis=0)
    r, ld, k, v, kap, b, g, bonus = rwkv_prepare(
        p_all, prev_exp, wts['mu'], wts['vecs'], wts['w2p'], wts['a2p'], wts['g2p'], wts['e'], seq, tc_prep)
    pad = (-seq) % RWKV_CHUNK
    sh = lambda z: jnp.pad(z.reshape(nbatch, seq, RWKV_WIDTH), ((0, 0), (0, pad), (0, 0)))
    y, s_new = rwkv_chunk_scan(sh(r), sh(ld), sh(k), sh(v), sh(kap), sh(b), s0)
    o = rwkv_output(y[:, :seq].reshape(nbatch * seq, RWKV_WIDTH), g, bonus, wts['vecs'], wts['e'], tc_prep)
    return o, s_new


def _compress_heads(x_refs, nchunk, wcat_ref, pe_ref, w1f_ref, b1_ref, w2_ref, b2_ref, acc_refs):
    kv2 = 2 * HEAD_DIM
    pe_term = jnp.concatenate([_dot(pe_ref[k], w1f_ref[k])[0:1, :] for k in range(2)], axis=1)
    bias = pe_term + b1_ref[...]
    half = CMP_STRIDE // 2
    for x_ref, acc_ref in zip(x_refs, acc_refs):
        acc = jnp.zeros((nchunk, 4 * kv2), f32)
        for r in range(half):
            xr = jnp.concatenate([x_ref[pl.ds(r, nchunk, stride=CMP_STRIDE), :],
                                  x_ref[pl.ds(r + half, nchunk, stride=CMP_STRIDE), :]], axis=1)
            acc = acc + _dot(xr.astype(bf16), wcat_ref[r])
        acc_ref[0:nchunk, :] = acc
        acc_ref[nchunk:nchunk + SUBLANES, :] = jnp.zeros((SUBLANES, 4 * kv2), f32)
    hidden = []
    for acc_ref in acc_refs:
        first = jnp.concatenate([acc_ref[0:nchunk, 0:kv2], acc_ref[0:nchunk, 2 * kv2:3 * kv2]], axis=1)
        second = jnp.concatenate([acc_ref[1:nchunk + 1, kv2:2 * kv2], acc_ref[1:nchunk + 1, 3 * kv2:4 * kv2]], axis=1)
        hidden.append(jax.nn.gelu(first + second + bias).astype(bf16))
    return [_dot(hid, w2_ref[...]) + b2_ref[...] for hid in hidden]


def _compress_prompt_body(x_ref, wcat_ref, pe_ref, w1f_ref, b1_ref, w2_ref, b2_ref, o_ref, acc_ref, *, nchunk):
    o_ref[...] = _compress_heads([x_ref], nchunk, wcat_ref, pe_ref, w1f_ref, b1_ref, w2_ref, b2_ref, [acc_ref])[0]


def _compress_weights(lp):
    w1 = lp['cmp_w1']
    z = jnp.zeros((CMP_STRIDE, HEAD_DIM, CMP_HIDDEN), f32)
    key_rows = jnp.concatenate([w1[0, :CMP_STRIDE], w1[0, CMP_STRIDE:], z, z], axis=-1)
    val_rows = jnp.concatenate([z, z, w1[1, :CMP_STRIDE], w1[1, CMP_STRIDE:]], axis=-1)
    wcat = jnp.concatenate([key_rows, val_rows], axis=1)
    wcat = jnp.concatenate([wcat[:CMP_STRIDE // 2], wcat[CMP_STRIDE // 2:]], axis=1).astype(bf16)
    pe = jnp.broadcast_to(lp['cmp_pe'].reshape(2, 1, CMP_BLOCK * HEAD_DIM), (2, SUBLANES, CMP_BLOCK * HEAD_DIM))
    w2 = lp['cmp_w2']
    zz = jnp.zeros((CMP_HIDDEN, HEAD_DIM), f32)
    w2bd = jnp.concatenate([jnp.concatenate([w2[0], zz], axis=1), jnp.concatenate([zz, w2[1]], axis=1)], axis=0)
    return {'wcat': wcat, 'pe': pe.astype(bf16),
            'w1f': w1.reshape(2, CMP_BLOCK * HEAD_DIM, CMP_HIDDEN).astype(bf16),
            'b1': lp['cmp_b1'].reshape(1, 2 * CMP_HIDDEN), 'w2bd': w2bd.astype(bf16),
            'b2': lp['cmp_b2'].reshape(1, 2 * HEAD_DIM)}


def _cmp_weight_specs():
    c2 = lambda *a: (0, 0)
    c3 = lambda *a: (0, 0, 0)
    return [pl.BlockSpec((CMP_STRIDE // 2, 4 * HEAD_DIM, 4 * CMP_HIDDEN), c3),
            pl.BlockSpec((2, SUBLANES, CMP_BLOCK * HEAD_DIM), c3),
            pl.BlockSpec((2, CMP_BLOCK * HEAD_DIM, CMP_HIDDEN), c3),
            pl.BlockSpec((1, 2 * CMP_HIDDEN), c2),
            pl.BlockSpec((2 * CMP_HIDDEN, 2 * HEAD_DIM), c2),
            pl.BlockSpec((1, 2 * HEAD_DIM), c2)]


def compress_prompt(p_all, cw, nbatch, seq):
    nchunk = seq // CMP_STRIDE
    kv2 = 2 * HEAD_DIM
    return pl.pallas_call(
        functools.partial(_compress_prompt_body, nchunk=nchunk),
        grid=(nbatch, NSA_KV_HEADS),
        in_specs=[pl.BlockSpec((seq, kv2), lambda b, h: (b, KVC0 // kv2 + h))] + _cmp_weight_specs(),
        out_specs=pl.BlockSpec((nchunk, kv2), lambda b, h: (b, h)),
        out_shape=jax.ShapeDtypeStruct((nbatch * nchunk, KV_WIDTH), f32),
        scratch_shapes=[pltpu.VMEM((nchunk + SUBLANES, 4 * CMP_HIDDEN), f32)],
        compiler_params=_cparams(("arbitrary", "arbitrary")),
        name="compress_prompt",
    )(p_all, cw['wcat'], cw['pe'], cw['w1f'], cw['b1'], cw['w2bd'], cw['b2'])


KEY_TILE = 256
TILES_PER_ITER = 3


def _masked_softmax(s, mask):
    s = jnp.where(mask, s, NEG_INF)
    m = jnp.max(s, axis=-1, keepdims=True)
    e = jnp.where(mask, jnp.exp(s - m), 0.0)
    return e / jnp.maximum(jnp.sum(e, axis=-1, keepdims=True), 1e-30)


def _nsa_prompt_body(q_ref, kc_ref, ks_ref, kw_ref, g_ref, wsel_ref, eexp_ref, o_ref, *, nsel, ncmp):
    i = pl.program_id(2)
    qb, grp = Q_BLOCK, NSA_GROUP
    rows = grp * qb
    q = q_ref[...]
    qs = jnp.concatenate([q[:, g * HEAD_DIM:(g + 1) * HEAD_DIM] for g in range(grp)], axis=0) * (HEAD_DIM ** -0.5)
    qp = jnp.concatenate([qs, jnp.zeros_like(qs)], axis=1).astype(bf16)
    qpos = i * qb + lax.broadcasted_iota(jnp.int32, (rows, 1), 0) % qb

    kc = kc_ref[...].astype(bf16)
    s_c = _dot_nt(qp, kc)
    cend = lax.broadcasted_iota(jnp.int32, (1, ncmp), 1) * CMP_STRIDE + (CMP_BLOCK - 1)
    p_c = _masked_softmax(s_c, cend <= qpos)
    o_c = _dot(p_c.astype(bf16), kc)
    imp = p_c[0:qb] + p_c[qb:2 * qb] + p_c[2 * qb:3 * qb] + p_c[3 * qb:4 * qb]
    imp_hi = imp.astype(bf16)
    imp_lo = (imp - imp_hi.astype(f32)).astype(bf16)
    wsel = wsel_ref[...]
    p_slc = _dot_nt(wsel, imp_hi) + _dot_nt(wsel, imp_lo)

    blk = lax.broadcasted_iota(jnp.int32, (nsel, qb), 0)
    qpos_t = i * qb + lax.broadcasted_iota(jnp.int32, (nsel, qb), 1)
    cur = qpos_t // SEL_BLOCK
    forced = (blk == 0) | (blk == cur) | (blk == cur - 1)
    score = jnp.where(forced, FORCE_SCORE, p_slc)
    score = jnp.where(blk * SEL_BLOCK <= qpos_t, score, -1.0)
    rank = jnp.zeros((nsel, qb), f32)
    for jp in range(nsel):
        row = score[jp:jp + 1, :]
        ahead = (row > score) | ((row == score) & (blk > jp))
        rank = rank + jnp.where(ahead, 1.0, 0.0)
    sel_t = jnp.where((rank < float(min(SEL_TOPK, nsel))) & (score >= 0.0), 1.0, 0.0)
    sel = sel_t.T.astype(bf16)

    qpos_q = qpos[0:qb]
    ones_keys = lax.broadcasted_iota(jnp.int32, (1, 2 * HEAD_DIM), 1) < HEAD_DIM

    ntile = ks_ref.shape[0] // KEY_TILE

    def attend(ref, lo, hi, pair, carry, mask_fn):
        tiles = []
        for j in range(TILES_PER_ITER):
            kt = lo + TILES_PER_ITER * pair + j
            kt_c = jnp.minimum(kt, ntile - 1)
            k0 = pl.multiple_of(kt_c * KEY_TILE, KEY_TILE)
            kt_tile = ref[pl.ds(k0, KEY_TILE), :].astype(bf16)
            ones_v = jnp.where(ones_keys, jnp.ones_like(kt_tile), kt_tile)
            mask = mask_fn(kt_c, k0 + lax.broadcasted_iota(jnp.int32, (1, KEY_TILE), 1)) & (kt <= hi)
            tiles.append((kt_tile, ones_v, mask))
        scores = [[_dot_nt(qp[g * qb:(g + 1) * qb], kt_tile) for g in range(grp)] for kt_tile, _, _ in tiles]
        state = list(carry)
        for j, (_, ones_v, mask) in enumerate(tiles):
            m_new, probs = [], []
            for g in range(grp):
                s = jnp.where(mask, scores[j][g], NEG_INF)
                m_new.append(jnp.maximum(state[g][0], jnp.max(s, axis=-1, keepdims=True)))
                probs.append(jnp.where(mask, jnp.exp(s - m_new[g]), 0.0).astype(bf16))
            pv = [_dot(probs[g], ones_v) for g in range(grp)]
            state = [(m_new[g], jnp.exp(state[g][0] - m_new[g]) * state[g][1] + pv[g]) for g in range(grp)]
        return tuple(state)

    def sel_mask(kt, kpos):
        return (_dot(sel, eexp_ref[kt]) > 0.5) & (kpos <= qpos_q)

    def win_mask(kt, kpos):
        diff = qpos_q - kpos
        return (diff >= 0) & (diff <= WINDOW)

    per_tile = KEY_TILE // qb
    init = tuple((jnp.full((qb, 1), NEG_INF, f32), jnp.zeros((qb, 2 * HEAD_DIM), f32)) for _ in range(grp))
    hi = i // per_tile
    lo_w = jnp.maximum(i - WINDOW // qb, 0) // per_tile
    res_s = lax.fori_loop(0, hi // TILES_PER_ITER + 1, lambda pr, c: attend(ks_ref, 0, hi, pr, c, sel_mask), init)
    res_w = lax.fori_loop(0, (hi - lo_w) // TILES_PER_ITER + 1,
                          lambda pr, c: attend(kw_ref, lo_w, hi, pr, c, win_mask), init)
    finish = lambda res: jnp.concatenate([acc / jnp.maximum(acc[:, 0:1], 1e-30) for _, acc in res], axis=0)
    o_s, o_w = finish(res_s), finish(res_w)

    gates = g_ref[0, 0]
    gate = lambda br: jnp.concatenate([gates[:, 3 * g + br:3 * g + br + 1] for g in range(grp)], axis=0)
    out = gate(0) * o_c + gate(1) * o_s + gate(2) * o_w
    o_ref[...] = jnp.concatenate([out[g * qb:(g + 1) * qb, HEAD_DIM:] for g in range(grp)], axis=1)


def nsa_prompt(p_all, kv_cmp, gates, nbatch, seq):
    nqb, nsel, ncmp = seq // Q_BLOCK, seq // SEL_BLOCK, seq // CMP_STRIDE
    kv2 = 2 * HEAD_DIM
    cidx = jnp.arange(ncmp)[None, :] - (SEL_BLOCK // CMP_STRIDE) * jnp.arange(nsel)[:, None]
    mult = jnp.array([1, 2, 2, 2, 1], f32)
    wsel = jnp.where((cidx >= 0) & (cidx <= 4), mult[jnp.clip(cidx, 0, 4)], 0.0).astype(bf16)
    ntile = seq // KEY_TILE
    key_blk = (jnp.arange(ntile)[:, None, None] * KEY_TILE + jnp.arange(KEY_TILE)[None, None, :]) // SEL_BLOCK
    eexp = (key_blk == jnp.arange(nsel)[None, :, None]).astype(bf16)
    return pl.pallas_call(
        functools.partial(_nsa_prompt_body, nsel=nsel, ncmp=ncmp),
        grid=(nbatch, NSA_KV_HEADS, nqb),
        in_specs=[pl.BlockSpec((Q_BLOCK, NSA_GROUP * HEAD_DIM),
                               lambda b, h, i: (b * nqb + i, Q0 // (NSA_GROUP * HEAD_DIM) + h)),
                  pl.BlockSpec((ncmp, kv2), lambda b, h, i: (b, h)),
                  pl.BlockSpec((seq, kv2), lambda b, h, i: (b, KVS0 // kv2 + h)),
                  pl.BlockSpec((seq, kv2), lambda b, h, i: (b, KVW0 // kv2 + h)),
                  pl.BlockSpec((1, 1, Q_BLOCK, 16), lambda b, h, i: (b, h, i, 0)),
                  pl.BlockSpec((nsel, ncmp), lambda b, h, i: (0, 0)),
                  pl.BlockSpec((ntile, nsel, KEY_TILE), lambda b, h, i: (0, 0, 0))],
        out_specs=pl.BlockSpec((Q_BLOCK, NSA_GROUP * HEAD_DIM), lambda b, h, i: (b * nqb + i, h)),
        out_shape=jax.ShapeDtypeStruct((nbatch * seq, NSA_WIDTH), f32),
        compiler_params=_cparams(("arbitrary", "arbitrary", "arbitrary")),
        name="nsa_prompt",
    )(p_all, kv_cmp, p_all, p_all, gates, wsel, eexp)


def _branch_gates(p_all, nbatch, seq):
    g = p_all[:, GN0:GN0 + 3 * NSA_HEADS].reshape(nbatch, seq, NSA_KV_HEADS, 3 * NSA_GROUP)
    return jnp.pad(jnp.transpose(g, (0, 2, 1, 3)), ((0, 0), (0, 0), (0, 0), (0, 16 - 3 * NSA_GROUP)))


PAGES_PER_STEP = 16
CMP_HALVES = 2
TAIL_ROWS = CMP_STRIDE


def _compress_sample_body(pt_ref, *refs, nchunk, half_rows):
    npg = PAGES_PER_STEP
    pages, (nxt_ref, newc_ref) = refs[:npg], refs[npg:npg + 2]
    wcat_ref, pe_ref, w1f_ref, b1_ref, w2_ref, b2_ref, o_ref, xs_ref, acc_ref = refs[npg + 2:]
    half, s = pl.program_id(1), pl.program_id(2)
    kv2 = 2 * HEAD_DIM
    eye = (lax.broadcasted_iota(jnp.int32, (PAGE_SIZE, PAGE_SIZE), 0) ==
           lax.broadcasted_iota(jnp.int32, (PAGE_SIZE, PAGE_SIZE), 1)).astype(bf16)

    def page_rows(ref):
        stored = jnp.concatenate([ref[0, h, kv] for h in range(NSA_KV_HEADS) for kv in range(2)], axis=0)
        return _dot_nt(eye, stored.astype(bf16))

    rows = [page_rows(pages[k]) for k in range(npg)]
    for k in range(npg):
        base = pl.multiple_of((s * npg + k) * PAGE_SIZE, PAGE_SIZE)
        for h in range(NSA_KV_HEADS):
            xs_ref[h, pl.ds(base, PAGE_SIZE), :] = rows[k][:, h * kv2:(h + 1) * kv2]

    @pl.when(s == pl.num_programs(2) - 1)
    def _():
        nblk = half_rows // CMP_STRIDE
        pad_rows = xs_ref.shape[1] - half_rows - TAIL_ROWS
        nxt = page_rows(nxt_ref)[0:TAIL_ROWS]
        heads = range(NSA_KV_HEADS)
        for h in heads:
            lanes = slice(h * kv2, (h + 1) * kv2)
            tail = jnp.where(half == CMP_HALVES - 1, newc_ref[0, :, lanes], nxt[:, lanes])
            xs_ref[h, half_rows:half_rows + TAIL_ROWS, :] = tail
            xs_ref[h, half_rows + TAIL_ROWS:, :] = jnp.zeros((pad_rows, kv2), f32)
        outs = _compress_heads([xs_ref.at[h] for h in heads], nchunk, wcat_ref, pe_ref, w1f_ref, b1_ref, w2_ref,
                               b2_ref, [acc_ref.at[h] for h in heads])
        for h in heads:
            o_ref[0, :, h * kv2:(h + 1) * kv2] = outs[h][0:nblk]


def compress_sample(cache, page_table, newc, cw):
    nbatch, npages = page_table.shape
    npg = PAGES_PER_STEP
    half_pages = npages // CMP_HALVES
    half_rows = half_pages * PAGE_SIZE
    nsteps = half_pages // npg
    nchunk = half_rows // CMP_STRIDE + SUBLANES
    stored = (1, NSA_KV_HEADS, 2, HEAD_DIM, PAGE_SIZE)
    page_spec = lambda k: pl.BlockSpec(
        stored, lambda b, hf, s, pt: (pt[b, hf * half_pages + s * npg + k], 0, 0, 0, 0))
    nxt_spec = pl.BlockSpec(
        stored, lambda b, hf, s, pt: (pt[b, jnp.minimum((hf + 1) * half_pages, npages - 1)], 0, 0, 0, 0))
    return pl.pallas_call(
        functools.partial(_compress_sample_body, nchunk=nchunk, half_rows=half_rows),
        grid_spec=pltpu.PrefetchScalarGridSpec(
            num_scalar_prefetch=1,
            grid=(nbatch, CMP_HALVES, nsteps),
            in_specs=[page_spec(k) for k in range(npg)] + [nxt_spec] +
                     [pl.BlockSpec((1, TAIL_ROWS, KV_WIDTH), lambda b, hf, s, pt: (b, 0, 0))] + _cmp_weight_specs(),
            out_specs=pl.BlockSpec((1, half_rows // CMP_STRIDE, KV_WIDTH), lambda b, hf, s, pt: (b, hf, 0)),
            scratch_shapes=[pltpu.VMEM((NSA_KV_HEADS, nchunk * CMP_STRIDE, 2 * HEAD_DIM), f32),
                            pltpu.VMEM((NSA_KV_HEADS, nchunk + SUBLANES, 4 * CMP_HIDDEN), f32)]),
        out_shape=jax.ShapeDtypeStruct((nbatch, npages * PAGE_SIZE // CMP_STRIDE, KV_WIDTH), f32),
        compiler_params=_cparams(("arbitrary", "arbitrary", "arbitrary")),
        name="compress_sample",
    )(page_table, *([cache] * (npg + 1)), newc, cw['wcat'], cw['pe'], cw['w1f'], cw['b1'], cw['w2bd'], cw['b2'])


def _split_dot_r(e, x):
    hi = x.astype(bf16)
    lo = (x - hi.astype(f32)).astype(bf16)
    return _dot(e, hi) + _dot(e, lo)


def _softmax_rows(s, mask):
    s = jnp.where(mask, s, NEG_INF)
    m = jnp.max(s, axis=0, keepdims=True)
    e = jnp.where(mask, jnp.exp(s - m), 0.0)
    return e / jnp.maximum(jnp.sum(e, axis=0, keepdims=True), 1e-30)


def _nsa_sample_body(pt_ref, *refs, tnew, nsel, nselp):
    npg = PAGES_PER_STEP
    pages = refs[:npg]
    (qb_ref, qr_ref, kc_ref, win_ref, neww_ref, news_ref, g_ref, wsel_ref, gm_ref, pick_ref, eexp_ref, o_ref,
     selq_ref, score_ref, m_ref, l_ref, acc_ref, oc_ref, ow_ref) = refs[npg:]
    s = pl.program_id(1)
    nselq = selq_ref.shape[1]
    rows_per_head = LANES // NSA_KV_HEADS
    qb = qb_ref[0]
    col = lax.broadcasted_iota(jnp.int32, (1, LANES), 1)
    tq = col % tnew
    qpos = PAST_LEN + tq
    pad_tail = lambda ref: jnp.concatenate(
        [ref[0], jnp.zeros((LANES - TAIL_ROWS, KV_WIDTH), f32)], axis=0).astype(bf16)

    @pl.when(s == 0)
    def _():
        kc = kc_ref[0].astype(bf16)
        ncmp = kc.shape[0]
        cend = lax.broadcasted_iota(jnp.int32, (ncmp, 1), 0) * CMP_STRIDE + (CMP_BLOCK - 1)
        p_c = _softmax_rows(_dot(kc, qb), cend <= qpos)
        oc_ref[...] = _dot(p_c.T.astype(bf16), kc)
        imp = _split_dot(p_c, gm_ref[...])
        p_slc = _split_dot_r(wsel_ref[...], imp)
        blk = lax.broadcasted_iota(jnp.int32, (nselp, LANES), 0)
        cur = qpos // SEL_BLOCK
        forced = (blk == 0) | (blk == cur) | (blk == cur - 1)
        score = jnp.where(forced, FORCE_SCORE, p_slc)
        score = jnp.where((blk * SEL_BLOCK <= qpos) & (blk < nsel), score, -1.0)
        score_ref[...] = score

        def rank_group(gi, rank):
            rows = score_ref[pl.ds(pl.multiple_of(gi * SUBLANES, SUBLANES), SUBLANES), :]
            for j in range(SUBLANES):
                row = rows[j:j + 1, :]
                ahead = (row > score) | ((row == score) & (blk > gi * SUBLANES + j))
                rank = rank + jnp.where(ahead, 1.0, 0.0)
            return rank

        rank = lax.fori_loop(0, nselp // SUBLANES, rank_group, jnp.zeros((nselp, LANES), f32))
        sel_t = jnp.where((rank < float(SEL_TOPK)) & (score >= 0.0), 1.0, 0.0)
        selq_ref[...] = jnp.concatenate([sel_t, jnp.zeros((nselq - nselp, LANES), f32)], axis=0).T

        kw = jnp.concatenate([win_ref[0].astype(bf16), pad_tail(neww_ref)], axis=0)
        idx = lax.broadcasted_iota(jnp.int32, (kw.shape[0], 1), 0)
        wbuf = win_ref.shape[1]
        p_w = _softmax_rows(_dot(kw, qb), (idx <= wbuf + tq) & (idx >= wbuf + tq - WINDOW))
        ow_ref[...] = _dot(p_w.T.astype(bf16), kw)

        m_ref[...] = jnp.full(m_ref.shape, NEG_INF, f32)
        l_ref[...] = jnp.zeros(l_ref.shape, f32)
        acc_ref[...] = jnp.zeros(acc_ref.shape, f32)

    def fold(scores, masks, e_dot_v):
        hrows = [slice(h * rows_per_head, (h + 1) * rows_per_head) for h in range(NSA_KV_HEADS)]
        m_new, alpha, probs = [], [], []
        for h in range(NSA_KV_HEADS):
            sc = jnp.where(masks[h], scores[h], NEG_INF)
            m_old = m_ref[hrows[h], 0:1]
            m_new.append(jnp.maximum(m_old, jnp.max(sc, axis=1, keepdims=True)))
            alpha.append(jnp.exp(m_old - m_new[h]))
            e = jnp.where(masks[h], jnp.exp(sc - m_new[h]), 0.0)
            l_ref[hrows[h], :] = jnp.broadcast_to(
                alpha[h] * l_ref[hrows[h], 0:1] + jnp.sum(e, axis=1, keepdims=True), (rows_per_head, LANES))
            probs.append(e.astype(bf16))
        pv = [e_dot_v(h, probs[h]) for h in range(NSA_KV_HEADS)]
        for h in range(NSA_KV_HEADS):
            acc_ref[hrows[h], :] = alpha[h] * acc_ref[hrows[h], :] + pv[h]
            m_ref[hrows[h], :] = jnp.broadcast_to(m_new[h], (rows_per_head, LANES))

    heads = range(NSA_KV_HEADS)
    hrow = lambda h: slice(h * rows_per_head, (h + 1) * rows_per_head)
    k_t = [jnp.concatenate([pages[k][0, h, 0] for k in range(npg)], axis=1).astype(bf16) for h in heads]
    v_t = [jnp.concatenate([pages[k][0, h, 1] for k in range(npg)], axis=1).astype(bf16) for h in heads]
    step_sel = [_dot(selq_ref[hrow(h), :].astype(bf16), pick_ref[s]) for h in heads]
    picked = [_dot(step_sel[h].astype(bf16), eexp_ref[...]) > 0.5 for h in heads]
    fold([_dot(qr_ref[0, hrow(h), :], k_t[h]) for h in heads], picked, lambda h, e: _dot_nt(e, v_t[h]))

    @pl.when(s == pl.num_programs(1) - 1)
    def _():
        ns = news_ref[0]
        kidx = lax.broadcasted_iota(jnp.int32, (1, TAIL_ROWS), 1)
        t_row = lax.broadcasted_iota(jnp.int32, (rows_per_head, 1), 0) % tnew
        nb = nsel - 1
        k_n = [ns[:, h * 2 * HEAD_DIM:h * 2 * HEAD_DIM + HEAD_DIM].astype(bf16) for h in heads]
        v_n = [ns[:, h * 2 * HEAD_DIM + HEAD_DIM:(h + 1) * 2 * HEAD_DIM].astype(bf16) for h in heads]
        fold([_dot_nt(qr_ref[0, hrow(h), :], k_n[h]) for h in heads],
             [(selq_ref[hrow(h), nb:nb + 1] > 0.5) & (kidx <= t_row) for h in heads],
             lambda h, e: _dot(e, v_n[h]))
        for h in heads:
            rows = hrow(h)
            o_s = acc_ref[rows, :] / jnp.maximum(l_ref[rows, 0:1], 1e-30)
            vals = slice(h * 2 * HEAD_DIM + HEAD_DIM, (h + 1) * 2 * HEAD_DIM)
            o_ref[0, rows, :] = (g_ref[0, 0, rows, 0:HEAD_DIM] * oc_ref[rows, vals] + g_ref[0, 1, rows, 0:HEAD_DIM] * o_s
                                 + g_ref[0, 2, rows, 0:HEAD_DIM] * ow_ref[rows, vals])


def nsa_sample(cache_sel_t, page_table, qblk, qrows, kv_cmp, win_buf, neww, news, gates, tnew):
    nbatch, npages = page_table.shape
    npg = PAGES_PER_STEP
    ncmp = kv_cmp.shape[1]
    nsel = npages * (PAGE_SIZE // SEL_BLOCK) + 1
    nselp = -(-nsel // SUBLANES) * SUBLANES
    cidx = jnp.arange(ncmp)[None, :] - (SEL_BLOCK // CMP_STRIDE) * jnp.arange(nselp)[:, None]
    mult = jnp.array([1, 2, 2, 2, 1], f32)
    wsel = jnp.where((cidx >= 0) & (cidx <= 4), mult[jnp.clip(cidx, 0, 4)], 0.0).astype(bf16)
    c = jnp.arange(LANES)
    same = (c[:, None] // (NSA_GROUP * tnew) == c[None, :] // (NSA_GROUP * tnew)) & (c[:, None] % tnew == c[None, :] % tnew)
    gm = same.astype(bf16)
    nsteps = npages // npg
    blocks_per_step = npg * (PAGE_SIZE // SEL_BLOCK)
    nselq = -(-nselp // LANES) * LANES
    blk_of = jnp.arange(nsteps)[:, None, None] * blocks_per_step + jnp.arange(LANES)[None, None, :]
    pick = ((jnp.arange(nselq)[None, :, None] == blk_of) & (jnp.arange(LANES) < blocks_per_step)).astype(bf16)
    eexp = (jnp.arange(LANES)[:, None] == jnp.arange(npg * PAGE_SIZE)[None, :] // SEL_BLOCK).astype(bf16)
    b3 = lambda b, s, pt: (b, 0, 0)
    c2 = lambda b, s, pt: (0, 0)
    page_spec = lambda k: pl.BlockSpec((1, NSA_KV_HEADS, 2, HEAD_DIM, PAGE_SIZE),
                                       lambda b, s, pt: (pt[b, s * npg + k], 0, 0, 0, 0))
    wbuf = win_buf.shape[1]
    return pl.pallas_call(
        functools.partial(_nsa_sample_body, tnew=tnew, nsel=nsel, nselp=nselp),
        grid_spec=pltpu.PrefetchScalarGridSpec(
            num_scalar_prefetch=1,
            grid=(nbatch, nsteps),
            in_specs=[page_spec(k) for k in range(npg)] + [
                pl.BlockSpec((1, KV_WIDTH, LANES), b3),
                pl.BlockSpec((1, LANES, HEAD_DIM), b3),
                pl.BlockSpec((1, ncmp, KV_WIDTH), b3),
                pl.BlockSpec((1, wbuf, KV_WIDTH), b3),
                pl.BlockSpec((1, TAIL_ROWS, KV_WIDTH), b3),
                pl.BlockSpec((1, TAIL_ROWS, KV_WIDTH), b3),
                pl.BlockSpec((1, 3, LANES, LANES), lambda b, s, pt: (b, 0, 0, 0)),
                pl.BlockSpec((nselp, ncmp), c2),
                pl.BlockSpec((LANES, LANES), c2),
                pl.BlockSpec((nsteps, nselq, LANES), lambda b, s, pt: (0, 0, 0)),
                pl.BlockSpec((LANES, npg * PAGE_SIZE), c2)],
            out_specs=pl.BlockSpec((1, LANES, HEAD_DIM), b3),
            scratch_shapes=[pltpu.VMEM((LANES, nselq), f32), pltpu.VMEM((nselp, LANES), f32),
                            pltpu.VMEM((LANES, LANES), f32), pltpu.VMEM((LANES, LANES), f32),
                            pltpu.VMEM((LANES, HEAD_DIM), f32),
                            pltpu.VMEM((LANES, KV_WIDTH), f32), pltpu.VMEM((LANES, KV_WIDTH), f32)]),
        out_shape=jax.ShapeDtypeStruct((nbatch, LANES, HEAD_DIM), f32),
        compiler_params=_cparams(("arbitrary", "arbitrary")),
        name="nsa_sample",
    )(page_table, *([cache_sel_t] * npg), qblk, qrows, kv_cmp, win_buf, neww, news, gates, wsel, gm, pick, eexp)


def _sample_nsa_inputs(p_all, b, t):
    q = p_all[:, Q0:Q0 + NSA_WIDTH].reshape(b, t, NSA_KV_HEADS, NSA_GROUP, HEAD_DIM) * (HEAD_DIM ** -0.5)
    qt = jnp.transpose(q, (0, 2, 4, 3, 1)).reshape(b, NSA_KV_HEADS, HEAD_DIM, NSA_GROUP * t)
    qt = jnp.pad(qt, ((0, 0), (0, 0), (0, HEAD_DIM), (0, 0)))
    eye = jnp.eye(NSA_KV_HEADS, dtype=f32)
    qblk = (qt[:, :, :, None, :] * eye[None, :, None, :, None]).reshape(b, KV_WIDTH, LANES).astype(bf16)
    qrows = jnp.transpose(q, (0, 2, 3, 1, 4)).reshape(b, LANES, HEAD_DIM).astype(bf16)
    g = p_all[:, GN0:GN0 + 3 * NSA_HEADS].reshape(b, t, NSA_KV_HEADS, NSA_GROUP, 3)
    g = jnp.transpose(g, (0, 4, 2, 3, 1)).reshape(b, 3, LANES)
    gates = jnp.broadcast_to(g[..., None], (b, 3, LANES, LANES))
    tail = lambda c0: jnp.pad(p_all[:, c0:c0 + KV_WIDTH].reshape(b, t, KV_WIDTH), ((0, 0), (0, TAIL_ROWS - t), (0, 0)))
    return qblk, qrows, gates, tail(KVC0), tail(KVS0), tail(KVW0)


def _prepare_weights(lp):
    w_in = lp['w_in']
    nsa_end = NSA_WIDTH + 3 * KV_WIDTH
    w_all = jnp.concatenate([
        w_in[:, NSA_PROJ:NSA_PROJ + RWKV_PROJ], w_in[:, nsa_end:NSA_PROJ],
        jnp.zeros((D_MODEL, RW_PAD - RWKV_PROJ - 3 * NSA_HEADS), w_in.dtype),
        w_in[:, :nsa_end], w_in[:, NSA_PROJ + RWKV_PROJ:]], axis=1).astype(bf16)
    pad_rows = lambda w, r0: jnp.zeros((LORA_PAD, RWKV_WIDTH), f32).at[r0:r0 + w.shape[0]].set(w).astype(bf16)
    head = jnp.arange(RWKV_WIDTH) // HEAD_DIM
    vecs = jnp.stack([lp['rwkv_w0'], lp['rwkv_a0'], lp['rwkv_k_k'], lp['rwkv_k_a'],
                      lp['rwkv_r_k'].reshape(RWKV_WIDTH), lp['rwkv_ln_g'], lp['rwkv_ln_b'],
                      jnp.zeros((RWKV_WIDTH,), f32)])
    return {
        'w_all': w_all,
        'mu': jnp.pad(lp['rwkv_mu'], (0, RW_PAD - RWKV_PROJ)).reshape(1, RW_PAD),
        'vecs': vecs,
        'w2p': pad_rows(lp['rwkv_w2'], 0),
        'a2p': pad_rows(lp['rwkv_a2'], DECAY_LORA),
        'g2p': pad_rows(lp['rwkv_g2'], DECAY_LORA + AAA_LORA),
        'e': (head[:, None] == head[None, :]).astype(bf16),
        'p_nsa': lp['p_nsa'].astype(bf16), 'p_rwkv': lp['p_rwkv'].astype(bf16),
        'w_out': lp['w_out'].astype(bf16),
        'mlp_w1': lp['mlp_w1'].astype(bf16), 'mlp_w2': lp['mlp_w2'].astype(bf16),
    }


PROJ_ROW_TILE = 1024
DENSE_ROW_TILE = 512
SMALL_ROW_TILE = 256


def _group_forward(x, pos_rows, prev, s0, lp, wts, final_g, o_nsa_fn):
    nbatch, seq = x.shape[:2]
    x2d = x.reshape(nbatch * seq, D_MODEL)
    tm = min(DENSE_ROW_TILE, nbatch * seq)
    ts = min(SMALL_ROW_TILE, nbatch * seq)
    tq, tkv = _rot_tables(pos_rows)
    p_all = input_projection(x2d, lp['norm1_g'], wts['w_all'], tq, tkv, min(PROJ_ROW_TILE, nbatch * seq))
    o_rw, s_new = rwkv7(p_all, prev, s0, wts, nbatch, seq, ts)
    o_nsa = o_nsa_fn(p_all)
    h, hn = merge_project(o_nsa, o_rw, p_all, x2d, wts['p_nsa'], wts['p_rwkv'], wts['w_out'], lp['norm2_g'], ts)
    y = mlp_residual_norm(hn, h, wts['mlp_w1'], wts['mlp_w2'], final_g, tm)
    return y.reshape(nbatch, seq, D_MODEL), p_all, s_new


def _kv_rows(p_all, col0, nbatch, seq):
    return p_all[:, col0:col0 + KV_WIDTH].reshape(nbatch, seq, NSA_KV_HEADS, 2, HEAD_DIM)


def sample_nsa_attention(p_all, cache_cmp, cache_sel, win_buf, page_table, cw, b, t):
    qblk, qrows, gates, newc, news, neww = _sample_nsa_inputs(p_all, b, t)
    cache_cmp_t = jnp.transpose(cache_cmp, (0, 2, 3, 4, 1))
    cache_sel_t = jnp.transpose(cache_sel, (0, 2, 3, 4, 1))
    kv_cmp = compress_sample(cache_cmp_t, page_table, newc, cw)
    o = nsa_sample(cache_sel_t, page_table, qblk, qrows, kv_cmp,
                   win_buf.reshape(b, win_buf.shape[1], KV_WIDTH), neww, news, gates, t)
    o = o.reshape(b, NSA_KV_HEADS, NSA_GROUP, t, HEAD_DIM)
    return jnp.transpose(o, (0, 3, 1, 2, 4)).reshape(b * t, NSA_WIDTH)


def kernel(x_prompt, x_sample, cache_cmp_kv, cache_sel_kv, state_nsa_win, state_rwkv, state_rwkv_shift,
           page_table, norm1_g, w_in, cmp_pe, cmp_w1, cmp_b1, cmp_w2, cmp_b2, rwkv_mu, rwkv_w0, rwkv_w2,
           rwkv_a0, rwkv_a2, rwkv_g2, rwkv_k_k, rwkv_k_a, rwkv_r_k, rwkv_ln_g, rwkv_ln_b, p_nsa, p_rwkv,
           w_out, norm2_g, mlp_w1, mlp_w2, final_g):
    l = 0
    lp = {'norm1_g': norm1_g[l], 'w_in': w_in[l], 'cmp_pe': cmp_pe[l], 'cmp_w1': cmp_w1[l],
          'cmp_b1': cmp_b1[l], 'cmp_w2': cmp_w2[l], 'cmp_b2': cmp_b2[l], 'rwkv_mu': rwkv_mu[l],
          'rwkv_w0': rwkv_w0[l], 'rwkv_w2': rwkv_w2[l], 'rwkv_a0': rwkv_a0[l], 'rwkv_a2': rwkv_a2[l],
          'rwkv_g2': rwkv_g2[l], 'rwkv_k_k': rwkv_k_k[l], 'rwkv_k_a': rwkv_k_a[l], 'rwkv_r_k': rwkv_r_k[l],
          'rwkv_ln_g': rwkv_ln_g[l], 'rwkv_ln_b': rwkv_ln_b[l], 'p_nsa': p_nsa[l], 'p_rwkv': p_rwkv[l],
          'w_out': w_out[l], 'norm2_g': norm2_g[l], 'mlp_w1': mlp_w1[l], 'mlp_w2': mlp_w2[l]}
    wts = _prepare_weights(lp)
    bp, tp = x_prompt.shape[:2]
    bs, ts = x_sample.shape[:2]
    cw = _compress_weights(lp)
    zero_state = jnp.zeros((bp, RWKV_HEADS, HEAD_DIM, HEAD_DIM), f32)

    def prompt_nsa(p_all):
        return nsa_prompt(p_all, compress_prompt(p_all, cw, bp, tp), _branch_gates(p_all, bp, tp), bp, tp)

    yp, pp, sp = _group_forward(x_prompt, jnp.arange(tp, dtype=jnp.int32), None, zero_state, lp, wts, final_g,
                                prompt_nsa)

    def sample_nsa(p_all):
        return sample_nsa_attention(p_all, cache_cmp_kv[l], cache_sel_kv[l], state_nsa_win[l], page_table, cw, bs, ts)

    pos_s = PAST_LEN + jnp.arange(bs * ts, dtype=jnp.int32) % ts
    ys, ps, ss = _group_forward(x_sample, pos_s, state_rwkv_shift[l], state_rwkv[l], lp, wts, final_g,
                                sample_nsa)
    win_new = [jnp.concatenate([state_nsa_win[l], _kv_rows(ps, KVW0, bs, ts)], axis=1)[:, ts:]]
    wlen = min(WINDOW, tp)
    shift = lambda p_all, nb, t: p_all.reshape(nb, t, P_WIDTH)[:, -1, RW0:RW0 + RWKV_PROJ]
    return (yp, ys,
            _kv_rows(pp, KVC0, bp, tp)[None], _kv_rows(ps, KVC0, bs, ts)[None],
            _kv_rows(pp, KVS0, bp, tp)[None], _kv_rows(ps, KVS0, bs, ts)[None],
            _kv_rows(pp, KVW0, bp, tp)[None, :, tp - wlen:], win_new[0][None],
            sp[None], ss[None],
            shift(pp, bp, tp)[None], shift(ps, bs, ts)[None])
```

```python
import functools

import jax
import jax.numpy as jnp
from jax import lax
from jax.experimental import pallas as pl
from jax.experimental.pallas import tpu as pltpu

D_MODEL = 2048
DEPTH = 1
PAST_LEN = 16384
PAGE_SIZE = 128

HEAD_DIM = 64
NSA_HEADS = D_MODEL // (2 * HEAD_DIM)
NSA_KV_HEADS = NSA_HEADS // 4
NSA_GROUP = NSA_HEADS // NSA_KV_HEADS
NSA_WIDTH = NSA_HEADS * HEAD_DIM
KV_WIDTH = NSA_KV_HEADS * 2 * HEAD_DIM
CMP_BLOCK = 32
CMP_STRIDE = 16
CMP_HIDDEN = 2 * HEAD_DIM
SEL_BLOCK = 64
SEL_TOPK = 16
WINDOW = 512
Q_BLOCK = 128
ROT_DIM = HEAD_DIM // 4
ROPE_THETA = 500000.0
RWKV_HEADS = D_MODEL // (2 * HEAD_DIM)
RWKV_WIDTH = RWKV_HEADS * HEAD_DIM
DECAY_LORA = max(32, int(round(1.8 * D_MODEL ** 0.5 / 32)) * 32)
AAA_LORA = DECAY_LORA
GATE_LORA = max(32, int(round(0.6 * D_MODEL ** 0.8 / 32)) * 32)
RWKV_PROJ = 3 * RWKV_WIDTH + DECAY_LORA + AAA_LORA + GATE_LORA
NSA_PROJ = NSA_WIDTH + 3 * KV_WIDTH + 3 * NSA_HEADS
N_IN = NSA_PROJ + RWKV_PROJ + 2 * D_MODEL
D_FF = 4 * D_MODEL
RMS_EPS = 1e-6
GN_EPS = HEAD_DIM * 1e-5
NEG_INF = -1e30
FORCE_SCORE = 1e6

LANES = 128
SUBLANES = 8
VMEM_LIMIT_BYTES = 56 * 1024 * 1024

RW0 = 0
LORA0 = 3 * RWKV_WIDTH
GN0 = RWKV_PROJ
RW_PAD = 3584
LORA_PAD = RW_PAD - LORA0
Q0 = RW_PAD
KVC0 = Q0 + NSA_WIDTH
KVS0 = KVC0 + KV_WIDTH
KVW0 = KVS0 + KV_WIDTH
GM0 = KVW0 + KV_WIDTH
P_WIDTH = GM0 + 2 * D_MODEL
PROJ_TN = 512

f32 = jnp.float32
bf16 = jnp.bfloat16


def _cparams(sem):
    return pltpu.CompilerParams(dimension_semantics=sem, vmem_limit_bytes=VMEM_LIMIT_BYTES)


def _rms(x, g):
    return x * lax.rsqrt(jnp.mean(x * x, axis=-1, keepdims=True) + RMS_EPS) * g


def _dot(a, b):
    return jnp.dot(a, b, preferred_element_type=f32)


def _dot_nt(a, b):
    return lax.dot_general(a, b, (((1,), (1,)), ((), ())), preferred_element_type=f32)


def _split_dot(x, e):
    hi = x.astype(bf16)
    lo = (x - hi.astype(f32)).astype(bf16)
    return _dot(hi, e) + _dot(lo, e)


def _rot_store(acc, tab_ref, o_ref):
    c, s1, s2 = tab_ref[0], tab_ref[1], tab_ref[2]
    for s in range(acc.shape[1] // LANES):
        x = acc[:, s * LANES:(s + 1) * LANES]
        o_ref[:, s * LANES:(s + 1) * LANES] = (
            x * c + pltpu.roll(x, LANES - ROT_DIM // 2, 1) * s1 + pltpu.roll(x, ROT_DIM // 2, 1) * s2)


def _proj_body(x_ref, g_ref, w_ref, tq_ref, tkv_ref, o_ref, xn_ref):
    j = pl.program_id(1)

    @pl.when(j == 0)
    def _():
        xn_ref[...] = _rms(x_ref[...], g_ref[...]).astype(bf16)

    acc = _dot(xn_ref[...], w_ref[...])
    gn_tile = GN0 // PROJ_TN

    @pl.when(j < gn_tile)
    def _():
        o_ref[...] = acc

    @pl.when(j == gn_tile)
    def _():
        lane = lax.broadcasted_iota(jnp.int32, acc.shape, 1)
        o_ref[...] = jnp.where(lane >= GN0 - gn_tile * PROJ_TN, jax.nn.sigmoid(acc), acc)

    @pl.when((j >= Q0 // PROJ_TN) & (j < KVC0 // PROJ_TN))
    def _():
        _rot_store(acc, tq_ref, o_ref)

    @pl.when((j >= KVC0 // PROJ_TN) & (j < GM0 // PROJ_TN))
    def _():
        _rot_store(acc, tkv_ref, o_ref)

    @pl.when(j >= GM0 // PROJ_TN)
    def _():
        o_ref[...] = jax.nn.sigmoid(acc)


def input_projection(x2d, g, w_all, tq, tkv, tm):
    m = x2d.shape[0]
    nt = tq.shape[1] // tm
    return pl.pallas_call(
        _proj_body,
        grid=(m // tm, P_WIDTH // PROJ_TN),
        in_specs=[pl.BlockSpec((tm, D_MODEL), lambda i, j: (i, 0)),
                  pl.BlockSpec((1, D_MODEL), lambda i, j: (0, 0)),
                  pl.BlockSpec((D_MODEL, PROJ_TN), lambda i, j: (0, j)),
                  pl.BlockSpec((3, tm, LANES), lambda i, j: (0, i % nt, 0)),
                  pl.BlockSpec((3, tm, LANES), lambda i, j: (0, i % nt, 0))],
        out_specs=pl.BlockSpec((tm, PROJ_TN), lambda i, j: (i, j)),
        out_shape=jax.ShapeDtypeStruct((m, P_WIDTH), f32),
        scratch_shapes=[pltpu.VMEM((tm, D_MODEL), bf16)],
        compiler_params=_cparams(("arbitrary", "arbitrary")),
        name="input_projection",
    )(x2d, g.reshape(1, D_MODEL), w_all, tq, tkv)


def _rot_tables(pos):
    half = ROT_DIM // 2
    freqs = jnp.power(jnp.float32(ROPE_THETA), -jnp.arange(half, dtype=f32) * 2.0 / ROT_DIM)
    ang = pos.astype(f32)[:, None] * freqs[None, :]
    cos, sin = jnp.cos(ang), jnp.sin(ang)
    lane = jnp.arange(LANES)

    def build(period):
        l = lane % period
        fi = l % half
        c = jnp.where(l < ROT_DIM, cos[:, fi], 1.0)
        s1 = jnp.where(l < half, -sin[:, fi], 0.0)
        s2 = jnp.where((l >= half) & (l < ROT_DIM), sin[:, fi], 0.0)
        return jnp.stack([c, s1, s2]).astype(f32)

    return build(HEAD_DIM), build(2 * HEAD_DIM)


def _merge_body(on_ref, y_ref, g_ref, bonus_ref, ga_ref, gb_ref, x_ref, pn_ref, pr_ref, wo_ref, g2_ref, vec_ref,
                e_ref, h_ref, hn_ref):
    y = y_ref[...]
    e = e_ref[...]
    ln_g, ln_b = vec_ref[5:6, :], vec_ref[6:7, :]
    mu = _split_dot(y, e) * (1.0 / HEAD_DIM)
    yc = y - mu
    var = _split_dot(yc * yc, e) * (1.0 / HEAD_DIM)
    o_rw = (yc * lax.rsqrt(var + GN_EPS) * ln_g + ln_b + bonus_ref[...]) * g_ref[...]
    a = _dot(on_ref[...].astype(bf16), pn_ref[...])
    b = _dot(o_rw.astype(bf16), pr_ref[...])
    mix = ga_ref[...] * a + gb_ref[...] * b
    h = x_ref[...] + _dot(mix.astype(bf16), wo_ref[...])
    h_ref[...] = h
    hn_ref[...] = _rms(h, g2_ref[...]).astype(bf16)


def merge_project(o_nsa, y, g, bonus, p_all, x2d, pn, pr, wo, g2, vecs, e, tm):
    m = x2d.shape[0]
    const = lambda i: (0, 0)
    row = lambda i: (i, 0)
    resident = lambda shape: pl.BlockSpec(shape, const, pipeline_mode=pl.Buffered(1))
    return pl.pallas_call(
        _merge_body,
        grid=(m // tm,),
        in_specs=[pl.BlockSpec((tm, NSA_WIDTH), row),
                  pl.BlockSpec((tm, RWKV_WIDTH), row),
                  pl.BlockSpec((tm, RWKV_WIDTH), row),
                  pl.BlockSpec((tm, RWKV_WIDTH), row),
                  pl.BlockSpec((tm, D_MODEL), lambda i: (i, GM0 // D_MODEL)),
                  pl.BlockSpec((tm, D_MODEL), lambda i: (i, GM0 // D_MODEL + 1)),
                  pl.BlockSpec((tm, D_MODEL), row),
                  resident((NSA_WIDTH, D_MODEL)),
                  resident((RWKV_WIDTH, D_MODEL)),
                  resident((D_MODEL, D_MODEL)),
                  pl.BlockSpec((1, D_MODEL), const),
                  pl.BlockSpec((SUBLANES, RWKV_WIDTH), const),
                  resident((RWKV_WIDTH, RWKV_WIDTH))],
        out_specs=[pl.BlockSpec((tm, D_MODEL), row),
                   pl.BlockSpec((tm, D_MODEL), row)],
        out_shape=[jax.ShapeDtypeStruct((m, D_MODEL), f32),
                   jax.ShapeDtypeStruct((m, D_MODEL), bf16)],
        compiler_params=_cparams(("arbitrary",)),
        name="merge_project",
    )(o_nsa, y, g, bonus, p_all, p_all, x2d, pn, pr, wo, g2.reshape(1, D_MODEL), vecs, e)


def _mlp_body(hn_ref, h_ref, w1_ref, w2_ref, fg_ref, y_ref, acc_ref):
    f = pl.program_id(1)
    u = jnp.square(jnp.maximum(_dot(hn_ref[...], w1_ref[...]), 0.0)).astype(bf16)
    contrib = _dot(u, w2_ref[...])

    @pl.when(f == 0)
    def _():
        acc_ref[...] = contrib

    @pl.when(f > 0)
    def _():
        acc_ref[...] += contrib

    @pl.when(f == pl.num_programs(1) - 1)
    def _():
        y_ref[...] = _rms(h_ref[...] + acc_ref[...], fg_ref[...])


MLP_FF_TILE = 1024


def mlp_residual_norm(hn, h, w1, w2, fg, tm, tf=MLP_FF_TILE):
    m = h.shape[0]
    return pl.pallas_call(
        _mlp_body,
        grid=(m // tm, D_FF // tf),
        in_specs=[pl.BlockSpec((tm, D_MODEL), lambda i, f: (i, 0)),
                  pl.BlockSpec((tm, D_MODEL), lambda i, f: (i, 0)),
                  pl.BlockSpec((D_MODEL, tf), lambda i, f: (0, f)),
                  pl.BlockSpec((tf, D_MODEL), lambda i, f: (f, 0)),
                  pl.BlockSpec((1, D_MODEL), lambda i, f: (0, 0))],
        out_specs=pl.BlockSpec((tm, D_MODEL), lambda i, f: (i, 0)),
        out_shape=jax.ShapeDtypeStruct((m, D_MODEL), f32),
        scratch_shapes=[pltpu.VMEM((tm, D_MODEL), f32)],
        compiler_params=_cparams(("arbitrary", "arbitrary")),
        name="mlp_residual_norm",
    )(hn, h, w1, w2, fg.reshape(1, D_MODEL))


def _rwkv_prep_body(*refs, tc, seq, has_prev):
    if has_prev:
        p_ref, halo_ref, prev_ref = refs[:3]
        refs = refs[3:]
    else:
        p_ref, halo_ref = refs[:2]
        prev_ref = None
        refs = refs[2:]
    (mu_ref, vec_ref, w2_ref, a2_ref, g2_ref, e_ref,
     r_out, d_out, k_out, v_out, kap_out, b_out, g_out, bonus_out, sh_ref) = refs
    i = pl.program_id(0)
    p = p_ref[...]
    sh_ref[0:SUBLANES, :] = halo_ref[...]
    sh_ref[SUBLANES:SUBLANES + tc, :] = p
    rolled = sh_ref[SUBLANES - 1:SUBLANES - 1 + tc, :]
    t_in_seq = (i * tc + lax.broadcasted_iota(jnp.int32, (tc, 1), 0)) % seq
    first = prev_ref[...] if has_prev else jnp.zeros_like(p)
    shifted = jnp.where(t_in_seq == 0, first, rolled)
    xm = p + (shifted - p) * mu_ref[...]
    r = xm[:, 0:RWKV_WIDTH]
    k = xm[:, RWKV_WIDTH:2 * RWKV_WIDTH]
    v = xm[:, 2 * RWKV_WIDTH:3 * RWKV_WIDTH]
    tail = xm[:, LORA0:RW_PAD]
    w0, a0, k_k, k_a, r_k = (vec_ref[n:n + 1, :] for n in range(5))
    w = -jax.nn.softplus(-(w0 + _dot(jnp.tanh(tail).astype(bf16), w2_ref[...]))) - 0.5
    a = jax.nn.sigmoid(a0 + _dot(tail.astype(bf16), a2_ref[...]))
    kk = k * k_k
    e = e_ref[...]
    kap = kk / jnp.maximum(jnp.sqrt(_split_dot(kk * kk, e)), 1e-12)
    kn = k * (1.0 + (a - 1.0) * k_a)
    r_out[...] = r
    d_out[...] = -jnp.exp(w)
    k_out[...] = kn
    v_out[...] = v
    kap_out[...] = kap
    b_out[...] = kap * a
    g_out[...] = _dot(jax.nn.sigmoid(tail).astype(bf16), g2_ref[...])
    bonus_out[...] = _split_dot(r * kn * r_k, e) * v


def rwkv_prepare(p_all, prev_exp, mu, vecs, w2p, a2p, g2p, e, seq, tc):
    m = p_all.shape[0]
    has_prev = prev_exp is not None
    row = lambda i: (i, 0)
    const = lambda i: (0, 0)
    halo = lambda i: (jnp.maximum(i * (tc // SUBLANES) - 1, 0), 0)
    in_specs = [pl.BlockSpec((tc, RW_PAD), row), pl.BlockSpec((SUBLANES, RW_PAD), halo)]
    args = [p_all, p_all]
    if has_prev:
        in_specs.append(pl.BlockSpec((tc, RW_PAD), row))
        args.append(prev_exp)
    in_specs += [pl.BlockSpec((1, RW_PAD), const), pl.BlockSpec((SUBLANES, RWKV_WIDTH), const),
                 pl.BlockSpec((LORA_PAD, RWKV_WIDTH), const), pl.BlockSpec((LORA_PAD, RWKV_WIDTH), const),
                 pl.BlockSpec((LORA_PAD, RWKV_WIDTH), const), pl.BlockSpec((RWKV_WIDTH, RWKV_WIDTH), const)]
    args += [mu, vecs, w2p, a2p, g2p, e]
    return pl.pallas_call(
        functools.partial(_rwkv_prep_body, tc=tc, seq=seq, has_prev=has_prev),
        grid=(m // tc,),
        in_specs=in_specs,
        out_specs=[pl.BlockSpec((tc, RWKV_WIDTH), row)] * 8,
        out_shape=[jax.ShapeDtypeStruct((m, RWKV_WIDTH), f32)] * 8,
        scratch_shapes=[pltpu.VMEM((tc + SUBLANES, RW_PAD), f32)],
        compiler_params=_cparams(("arbitrary",)),
        name="rwkv_prepare",
    )(*args)


RWKV_CHUNK = 64
RWKV_PAIRS_PER_STEP = 8


def _split(x):
    hi = x.astype(bf16)
    return hi, (x - hi.astype(f32)).astype(bf16)


def _dot3(a, b, nt=False):
    d = _dot_nt if nt else _dot
    a_hi, a_lo = _split(a)
    b_hi, b_lo = _split(b)
    return d(a_hi, b_hi) + d(a_hi, b_lo) + d(a_lo, b_hi)


def _dot1(a, b):
    return _dot(a.astype(bf16), b.astype(bf16))


def _rwkv_chunk_body(r_ref, ld_ref, k_ref, v_ref, kap_ref, b_ref, s0_ref, y_ref, st_ref, s_ref):
    c = pl.program_id(2)
    C = RWKV_CHUNK
    zero = jnp.zeros((HEAD_DIM, HEAD_DIM), f32)

    @pl.when(c == 0)
    def _():
        for p in range(RWKV_PAIRS_PER_STEP):
            top = jnp.concatenate([s0_ref[0, 2 * p], zero], axis=1)
            bot = jnp.concatenate([zero, s0_ref[0, 2 * p + 1]], axis=1)
            s_ref[p] = jnp.concatenate([top, bot], axis=0)

    row = lax.broadcasted_iota(jnp.int32, (2 * C, LANES), 0)
    lane = lax.broadcasted_iota(jnp.int32, (2 * C, LANES), 1)
    top, bot = (row < C) & (lane < HEAD_DIM), (row >= C) & (lane >= HEAD_DIM)
    strict, incl = (lane % HEAD_DIM) < (row % C), (lane % HEAD_DIM) <= (row % C)
    eye = jnp.where(row == lane, 1.0, 0.0)
    lane1 = lax.broadcasted_iota(jnp.int32, (1, LANES), 1)
    m_a, m_b = jnp.where(lane1 < HEAD_DIM, 1.0, 0.0), jnp.where(lane1 >= HEAD_DIM, 1.0, 0.0)
    stack = lambda x: jnp.concatenate([x * m_a, x * m_b], axis=0)
    tri = jnp.where(lax.broadcasted_iota(jnp.int32, (C, C), 1) <= lax.broadcasted_iota(jnp.int32, (C, C), 0),
                    1.0, 0.0).astype(bf16)
    pick = lambda cond_a, xa, cond_b, xb: jnp.where(cond_a, xa, 0.0) + jnp.where(cond_b, xb, 0.0)

    l_hi, rest = _split(ld_ref[0])
    l_mid, l_lo = _split(rest.astype(f32))
    cum_all = _dot(tri, l_hi) + _dot(tri, l_mid) + _dot(tri, l_lo)

    pairs = range(RWKV_PAIRS_PER_STEP)
    lanes = [slice(p * LANES, (p + 1) * LANES) for p in pairs]
    kst, rst, b_t, k_t, g_end, vst, gram = [], [], [], [], [], [], []
    for p in pairs:
        r_, ld_, k_, kap_, b_ = (ref[0, :, lanes[p]] for ref in (r_ref, ld_ref, k_ref, kap_ref, b_ref))
        cum = cum_all[:, lanes[p]]
        g, g_prev, g_inv = jnp.exp(cum), jnp.exp(cum - ld_), jnp.exp(-cum)
        g_end.append(g[C - 1:C, :])
        kst.append(stack(kap_ * g_prev))
        rst.append(stack(r_ * g))
        b_t.append(b_ * g_inv)
        k_t.append(k_ * g_inv)
        vst.append(stack(v_ref[0, :, lanes[p]]))
        gram.append(_dot3(jnp.concatenate([kst[p], rst[p]], axis=0),
                          jnp.concatenate([b_t[p], k_t[p]], axis=0), nt=True))
    l_p, a_ak, a_rb, a_rk = [], [], [], []
    for p in pairs:
        ga, gr = gram[p][0:2 * C], gram[p][2 * C:4 * C]
        ga_r, gr_r = pltpu.roll(ga, HEAD_DIM, 1), pltpu.roll(gr, HEAD_DIM, 1)
        l_p.append(pick(top & strict, ga, bot & strict, ga_r))
        a_ak.append(pick(top & strict, ga_r, bot & strict, ga))
        a_rb.append(pick(top & incl, gr, bot & incl, gr_r))
        a_rk.append(pick(top & incl, gr_r, bot & incl, gr))
    inv, power = [eye - l for l in l_p], list(l_p)
    for _ in range(C.bit_length() - 2):
        power = [_dot1(m, m) for m in power]
        inv = [_dot1(x, eye + m) for x, m in zip(inv, power)]
    resid = [eye - _dot3(eye + l, x) for l, x in zip(l_p, inv)]
    inv = [x + _dot1(x, rs) for x, rs in zip(inv, resid)]
    sst = [s_ref[p] for p in pairs]
    sst_t = [s.T for s in sst]
    rhs = [_dot3(jnp.concatenate([kst[p], a_ak[p]], axis=1), jnp.concatenate([sst_t[p], vst[p]], axis=0))
           for p in pairs]
    u = [-_dot3(inv[p], rhs[p]) for p in pairs]
    for p in pairs:
        y = _dot1(jnp.concatenate([rst[p], a_rb[p], a_rk[p]], axis=1),
                  jnp.concatenate([sst_t[p], u[p], vst[p]], axis=0))
        y_ref[0, :, lanes[p]] = y[0:C] + y[C:2 * C]
    for p in pairs:
        uv = jnp.concatenate([u[p], vst[p]], axis=0)
        bk = jnp.concatenate([stack(b_t[p] * g_end[p]), stack(k_t[p] * g_end[p])], axis=0)
        s_ref[p] = sst[p] * g_end[p] + _dot3(uv.T, bk)

    @pl.when(c == pl.num_programs(2) - 1)
    def _():
        for p in range(RWKV_PAIRS_PER_STEP):
            st_ref[0, 2 * p] = s_ref[p][0:HEAD_DIM, 0:HEAD_DIM]
            st_ref[0, 2 * p + 1] = s_ref[p][HEAD_DIM:, HEAD_DIM:]


def rwkv_chunk_scan(r, ld, k, v, kap, b, s0):
    nbatch, seq = r.shape[:2]
    width = RWKV_PAIRS_PER_STEP * LANES
    blk = pl.BlockSpec((1, RWKV_CHUNK, width), lambda bi, pg, c: (bi, c, pg))
    st = pl.BlockSpec((1, 2 * RWKV_PAIRS_PER_STEP, HEAD_DIM, HEAD_DIM), lambda bi, pg, c: (bi, pg, 0, 0))
    return pl.pallas_call(
        _rwkv_chunk_body,
        grid=(nbatch, RWKV_WIDTH // width, seq // RWKV_CHUNK),
        in_specs=[blk] * 6 + [st],
        out_specs=[blk, st],
        out_shape=[jax.ShapeDtypeStruct((nbatch, seq, RWKV_WIDTH), f32),
                   jax.ShapeDtypeStruct((nbatch, RWKV_HEADS, HEAD_DIM, HEAD_DIM), f32)],
        scratch_shapes=[pltpu.VMEM((RWKV_PAIRS_PER_STEP, 2 * RWKV_CHUNK, LANES), f32)],
        compiler_params=_cparams(("arbitrary", "arbitrary", "arbitrary")),
        name="rwkv_chunk_scan",
    )(r, ld, k, v, kap, b, s0)


def rwkv7(p_all, prev, s0, wts, nbatch, seq, tc_prep):
    prev_exp = None
    if prev is not None:
        prev_exp = jnp.repeat(jnp.pad(prev, ((0, 0), (0, RW_PAD - RWKV_PROJ))), seq, axis=0)
    r, ld, k, v, kap, b, g, bonus = rwkv_prepare(
        p_all, prev_exp, wts['mu'], wts['vecs'], wts['w2p'], wts['a2p'], wts['g2p'], wts['e'], seq, tc_prep)
    pad = (-seq) % RWKV_CHUNK
    sh = lambda z: jnp.pad(z.reshape(nbatch, seq, RWKV_WIDTH), ((0, 0), (0, pad), (0, 0)))
    y, s_new = rwkv_chunk_scan(sh(r), sh(ld), sh(k), sh(v), sh(kap), sh(b), s0)
    return (y[:, :seq].reshape(nbatch * seq, RWKV_WIDTH), g, bonus), s_new


def _compress_heads(x_refs, nchunk, wcat_ref, pe_ref, w1f_ref, b1_ref, w2_ref, b2_ref, acc_refs):
    kv2 = 2 * HEAD_DIM
    pe_term = jnp.concatenate([_dot(pe_ref[k], w1f_ref[k])[0:1, :] for k in range(2)], axis=1)
    bias = pe_term + b1_ref[...]
    half = CMP_STRIDE // 2
    for x_ref, acc_ref in zip(x_refs, acc_refs):
        acc = jnp.zeros((nchunk, 4 * kv2), f32)
        for r in range(half):
            xr = jnp.concatenate([x_ref[pl.ds(r, nchunk, stride=CMP_STRIDE), :],
                                  x_ref[pl.ds(r + half, nchunk, stride=CMP_STRIDE), :]], axis=1)
            acc = acc + _dot(xr.astype(bf16), wcat_ref[r])
        acc_ref[0:nchunk, :] = acc
        acc_ref[nchunk:nchunk + SUBLANES, :] = jnp.zeros((SUBLANES, 4 * kv2), f32)
    hidden = []
    for acc_ref in acc_refs:
        first = jnp.concatenate([acc_ref[0:nchunk, 0:kv2], acc_ref[0:nchunk, 2 * kv2:3 * kv2]], axis=1)
        second = jnp.concatenate([acc_ref[1:nchunk + 1, kv2:2 * kv2], acc_ref[1:nchunk + 1, 3 * kv2:4 * kv2]], axis=1)
        hidden.append(jax.nn.gelu(first + second + bias).astype(bf16))
    return [_dot(hid, w2_ref[...]) + b2_ref[...] for hid in hidden]


def _compress_prompt_body(x_ref, wcat_ref, pe_ref, w1f_ref, b1_ref, w2_ref, b2_ref, o_ref, acc_ref, *, nchunk):
    o_ref[...] = _compress_heads([x_ref], nchunk, wcat_ref, pe_ref, w1f_ref, b1_ref, w2_ref, b2_ref, [acc_ref])[0]


def _compress_weights(lp):
    w1 = lp['cmp_w1']
    z = jnp.zeros((CMP_STRIDE, HEAD_DIM, CMP_HIDDEN), f32)
    key_rows = jnp.concatenate([w1[0, :CMP_STRIDE], w1[0, CMP_STRIDE:], z, z], axis=-1)
    val_rows = jnp.concatenate([z, z, w1[1, :CMP_STRIDE], w1[1, CMP_STRIDE:]], axis=-1)
    wcat = jnp.concatenate([key_rows, val_rows], axis=1)
    wcat = jnp.concatenate([wcat[:CMP_STRIDE // 2], wcat[CMP_STRIDE // 2:]], axis=1).astype(bf16)
    pe = jnp.broadcast_to(lp['cmp_pe'].reshape(2, 1, CMP_BLOCK * HEAD_DIM), (2, SUBLANES, CMP_BLOCK * HEAD_DIM))
    w2 = lp['cmp_w2']
    zz = jnp.zeros((CMP_HIDDEN, HEAD_DIM), f32)
    w2bd = jnp.concatenate([jnp.concatenate([w2[0], zz], axis=1), jnp.concatenate([zz, w2[1]], axis=1)], axis=0)
    return {'wcat': wcat, 'pe': pe.astype(bf16),
            'w1f': w1.reshape(2, CMP_BLOCK * HEAD_DIM, CMP_HIDDEN).astype(bf16),
            'b1': lp['cmp_b1'].reshape(1, 2 * CMP_HIDDEN), 'w2bd': w2bd.astype(bf16),
            'b2': lp['cmp_b2'].reshape(1, 2 * HEAD_DIM)}


def _cmp_weight_specs():
    c2 = lambda *a: (0, 0)
    c3 = lambda *a: (0, 0, 0)
    return [pl.BlockSpec((CMP_STRIDE // 2, 4 * HEAD_DIM, 4 * CMP_HIDDEN), c3),
            pl.BlockSpec((2, SUBLANES, CMP_BLOCK * HEAD_DIM), c3),
            pl.BlockSpec((2, CMP_BLOCK * HEAD_DIM, CMP_HIDDEN), c3),
            pl.BlockSpec((1, 2 * CMP_HIDDEN), c2),
            pl.BlockSpec((2 * CMP_HIDDEN, 2 * HEAD_DIM), c2),
            pl.BlockSpec((1, 2 * HEAD_DIM), c2)]


def compress_prompt(p_all, cw, nbatch, seq):
    nchunk = seq // CMP_STRIDE
    kv2 = 2 * HEAD_DIM
    return pl.pallas_call(
        functools.partial(_compress_prompt_body, nchunk=nchunk),
        grid=(nbatch, NSA_KV_HEADS),
        in_specs=[pl.BlockSpec((seq, kv2), lambda b, h: (b, KVC0 // kv2 + h))] + _cmp_weight_specs(),
        out_specs=pl.BlockSpec((nchunk, kv2), lambda b, h: (b, h)),
        out_shape=jax.ShapeDtypeStruct((nbatch * nchunk, KV_WIDTH), f32),
        scratch_shapes=[pltpu.VMEM((nchunk + SUBLANES, 4 * CMP_HIDDEN), f32)],
        compiler_params=_cparams(("arbitrary", "arbitrary")),
        name="compress_prompt",
    )(p_all, cw['wcat'], cw['pe'], cw['w1f'], cw['b1'], cw['w2bd'], cw['b2'])


KEY_TILE = 256
TILES_PER_ITER = 3


def _masked_softmax(s, mask):
    s = jnp.where(mask, s, NEG_INF)
    m = jnp.max(s, axis=-1, keepdims=True)
    e = jnp.where(mask, jnp.exp(s - m), 0.0)
    return e / jnp.maximum(jnp.sum(e, axis=-1, keepdims=True), 1e-30)


def _nsa_prompt_body(q_ref, kc_ref, ks_ref, kw_ref, g_ref, wsel_ref, eexp_ref, o_ref, *, nsel, ncmp):
    i = pl.program_id(2)
    qb, grp = Q_BLOCK, NSA_GROUP
    rows = grp * qb
    q = q_ref[...]
    qs = jnp.concatenate([q[:, g * HEAD_DIM:(g + 1) * HEAD_DIM] for g in range(grp)], axis=0) * (HEAD_DIM ** -0.5)
    qp = jnp.concatenate([qs, jnp.zeros_like(qs)], axis=1).astype(bf16)
    qpos = i * qb + lax.broadcasted_iota(jnp.int32, (rows, 1), 0) % qb

    kc = kc_ref[...].astype(bf16)
    s_c = _dot_nt(qp, kc)
    cend = lax.broadcasted_iota(jnp.int32, (1, ncmp), 1) * CMP_STRIDE + (CMP_BLOCK - 1)
    p_c = _masked_softmax(s_c, cend <= qpos)
    o_c = _dot(p_c.astype(bf16), kc)
    imp = p_c[0:qb] + p_c[qb:2 * qb] + p_c[2 * qb:3 * qb] + p_c[3 * qb:4 * qb]
    imp_hi = imp.astype(bf16)
    imp_lo = (imp - imp_hi.astype(f32)).astype(bf16)
    wsel = wsel_ref[...]
    p_slc = _dot_nt(wsel, imp_hi) + _dot_nt(wsel, imp_lo)

    blk = lax.broadcasted_iota(jnp.int32, (nsel, qb), 0)
    qpos_t = i * qb + lax.broadcasted_iota(jnp.int32, (nsel, qb), 1)
    cur = qpos_t // SEL_BLOCK
    forced = (blk == 0) | (blk == cur) | (blk == cur - 1)
    score = jnp.where(forced, FORCE_SCORE, p_slc)
    score = jnp.where(blk * SEL_BLOCK <= qpos_t, score, -1.0)
    rank = jnp.zeros((nsel, qb), f32)
    for jp in range(nsel):
        row = score[jp:jp + 1, :]
        ahead = (row > score) | ((row == score) & (blk > jp))
        rank = rank + jnp.where(ahead, 1.0, 0.0)
    sel_t = jnp.where((rank < float(min(SEL_TOPK, nsel))) & (score >= 0.0), 1.0, 0.0)
    sel = sel_t.T.astype(bf16)

    qpos_q = qpos[0:qb]
    ones_keys = lax.broadcasted_iota(jnp.int32, (1, 2 * HEAD_DIM), 1) < HEAD_DIM

    ntile = ks_ref.shape[0] // KEY_TILE

    def attend(ref, lo, hi, pair, carry, mask_fn):
        tiles = []
        for j in range(TILES_PER_ITER):
            kt = lo + TILES_PER_ITER * pair + j
            kt_c = jnp.minimum(kt, ntile - 1)
            k0 = pl.multiple_of(kt_c * KEY_TILE, KEY_TILE)
            kt_tile = ref[pl.ds(k0, KEY_TILE), :].astype(bf16)
            ones_v = jnp.where(ones_keys, jnp.ones_like(kt_tile), kt_tile)
            mask = mask_fn(kt_c, k0 + lax.broadcasted_iota(jnp.int32, (1, KEY_TILE), 1)) & (kt <= hi)
            tiles.append((kt_tile, ones_v, mask))
        scores = [[_dot_nt(qp[g * qb:(g + 1) * qb], kt_tile) for g in range(grp)] for kt_tile, _, _ in tiles]
        state = list(carry)
        for j, (_, ones_v, mask) in enumerate(tiles):
            m_new, probs = [], []
            for g in range(grp):
                s = jnp.where(mask, scores[j][g], NEG_INF)
                m_new.append(jnp.maximum(state[g][0], jnp.max(s, axis=-1, keepdims=True)))
                probs.append(jnp.where(mask, jnp.exp(s - m_new[g]), 0.0).astype(bf16))
            pv = [_dot(probs[g], ones_v) for g in range(grp)]
            state = [(m_new[g], jnp.exp(state[g][0] - m_new[g]) * state[g][1] + pv[g]) for g in range(grp)]
        return tuple(state)

    def sel_mask(kt, kpos):
        return (_dot(sel, eexp_ref[kt]) > 0.5) & (kpos <= qpos_q)

    def win_mask(kt, kpos):
        diff = qpos_q - kpos
        return (diff >= 0) & (diff <= WINDOW)

    per_tile = KEY_TILE // qb
    init = tuple((jnp.full((qb, 1), NEG_INF, f32), jnp.zeros((qb, 2 * HEAD_DIM), f32)) for _ in range(grp))
    hi = i // per_tile
    lo_w = jnp.maximum(i - WINDOW // qb, 0) // per_tile
    res_s = lax.fori_loop(0, hi // TILES_PER_ITER + 1, lambda pr, c: attend(ks_ref, 0, hi, pr, c, sel_mask), init)
    res_w = lax.fori_loop(0, (hi - lo_w) // TILES_PER_ITER + 1,
                          lambda pr, c: attend(kw_ref, lo_w, hi, pr, c, win_mask), init)
    finish = lambda res: jnp.concatenate([acc / jnp.maximum(acc[:, 0:1], 1e-30) for _, acc in res], axis=0)
    o_s, o_w = finish(res_s), finish(res_w)

    gates = g_ref[0, 0]
    gate = lambda br: jnp.concatenate([gates[:, 3 * g + br:3 * g + br + 1] for g in range(grp)], axis=0)
    out = gate(0) * o_c + gate(1) * o_s + gate(2) * o_w
    o_ref[...] = jnp.concatenate([out[g * qb:(g + 1) * qb, HEAD_DIM:] for g in range(grp)], axis=1)


def nsa_prompt(p_all, kv_cmp, gates, nbatch, seq):
    nqb, nsel, ncmp = seq // Q_BLOCK, seq // SEL_BLOCK, seq // CMP_STRIDE
    kv2 = 2 * HEAD_DIM
    cidx = jnp.arange(ncmp)[None, :] - (SEL_BLOCK // CMP_STRIDE) * jnp.arange(nsel)[:, None]
    mult = jnp.array([1, 2, 2, 2, 1], f32)
    wsel = jnp.where((cidx >= 0) & (cidx <= 4), mult[jnp.clip(cidx, 0, 4)], 0.0).astype(bf16)
    ntile = seq // KEY_TILE
    key_blk = (jnp.arange(ntile)[:, None, None] * KEY_TILE + jnp.arange(KEY_TILE)[None, None, :]) // SEL_BLOCK
    eexp = (key_blk == jnp.arange(nsel)[None, :, None]).astype(bf16)
    return pl.pallas_call(
        functools.partial(_nsa_prompt_body, nsel=nsel, ncmp=ncmp),
        grid=(nbatch, NSA_KV_HEADS, nqb),
        in_specs=[pl.BlockSpec((Q_BLOCK, NSA_GROUP * HEAD_DIM),
                               lambda b, h, i: (b * nqb + i, Q0 // (NSA_GROUP * HEAD_DIM) + h)),
                  pl.BlockSpec((ncmp, kv2), lambda b, h, i: (b, h)),
                  pl.BlockSpec((seq, kv2), lambda b, h, i: (b, KVS0 // kv2 + h)),
                  pl.BlockSpec((seq, kv2), lambda b, h, i: (b, KVW0 // kv2 + h)),
                  pl.BlockSpec((1, 1, Q_BLOCK, 16), lambda b, h, i: (b, h, i, 0)),
                  pl.BlockSpec((nsel, ncmp), lambda b, h, i: (0, 0)),
                  pl.BlockSpec((ntile, nsel, KEY_TILE), lambda b, h, i: (0, 0, 0))],
        out_specs=pl.BlockSpec((Q_BLOCK, NSA_GROUP * HEAD_DIM), lambda b, h, i: (b * nqb + i, h)),
        out_shape=jax.ShapeDtypeStruct((nbatch * seq, NSA_WIDTH), f32),
        compiler_params=_cparams(("arbitrary", "arbitrary", "arbitrary")),
        name="nsa_prompt",
    )(p_all, kv_cmp, p_all, p_all, gates, wsel, eexp)


def _branch_gates(p_all, nbatch, seq):
    g = p_all[:, GN0:GN0 + 3 * NSA_HEADS].reshape(nbatch, seq, NSA_KV_HEADS, 3 * NSA_GROUP)
    return jnp.pad(jnp.transpose(g, (0, 2, 1, 3)), ((0, 0), (0, 0), (0, 0), (0, 16 - 3 * NSA_GROUP)))


PAGES_PER_STEP = 32
CMP_HALVES = 2
TAIL_ROWS = CMP_STRIDE


def _compress_sample_body(pt_ref, *refs, nchunk, half_rows):
    npg = PAGES_PER_STEP
    pages, (nxt_ref, newc_ref) = refs[:npg], refs[npg:npg + 2]
    wcat_ref, pe_ref, w1f_ref, b1_ref, w2_ref, b2_ref, o_ref, xs_ref, acc_ref = refs[npg + 2:]
    half, s = pl.program_id(1), pl.program_id(2)
    kv2 = 2 * HEAD_DIM
    eye = (lax.broadcasted_iota(jnp.int32, (PAGE_SIZE, PAGE_SIZE), 0) ==
           lax.broadcasted_iota(jnp.int32, (PAGE_SIZE, PAGE_SIZE), 1)).astype(bf16)

    def page_rows(ref):
        stored = jnp.concatenate([ref[0, h, kv] for h in range(NSA_KV_HEADS) for kv in range(2)], axis=0)
        return _dot_nt(eye, stored.astype(bf16))

    rows = [page_rows(pages[k]) for k in range(npg)]
    for k in range(npg):
        base = pl.multiple_of((s * npg + k) * PAGE_SIZE, PAGE_SIZE)
        for h in range(NSA_KV_HEADS):
            xs_ref[h, pl.ds(base, PAGE_SIZE), :] = rows[k][:, h * kv2:(h + 1) * kv2]

    @pl.when(s == pl.num_programs(2) - 1)
    def _():
        nblk = half_rows // CMP_STRIDE
        pad_rows = xs_ref.shape[1] - half_rows - TAIL_ROWS
        nxt = page_rows(nxt_ref)[0:TAIL_ROWS]
        heads = range(NSA_KV_HEADS)
        for h in heads:
            lanes = slice(h * kv2, (h + 1) * kv2)
            tail = jnp.where(half == CMP_HALVES - 1, newc_ref[0, :, lanes], nxt[:, lanes])
            xs_ref[h, half_rows:half_rows + TAIL_ROWS, :] = tail
            xs_ref[h, half_rows + TAIL_ROWS:, :] = jnp.zeros((pad_rows, kv2), f32)
        outs = _compress_heads([xs_ref.at[h] for h in heads], nchunk, wcat_ref, pe_ref, w1f_ref, b1_ref, w2_ref,
                               b2_ref, [acc_ref.at[h] for h in heads])
        for h in heads:
            o_ref[0, :, h * kv2:(h + 1) * kv2] = outs[h][0:nblk]


def compress_sample(cache, page_table, newc, cw):
    nbatch, npages = page_table.shape
    npg = PAGES_PER_STEP
    half_pages = npages // CMP_HALVES
    half_rows = half_pages * PAGE_SIZE
    nsteps = half_pages // npg
    nchunk = half_rows // CMP_STRIDE + SUBLANES
    stored = (1, NSA_KV_HEADS, 2, HEAD_DIM, PAGE_SIZE)
    page_spec = lambda k: pl.BlockSpec(
        stored, lambda b, hf, s, pt: (pt[b, hf * half_pages + s * npg + k], 0, 0, 0, 0))
    nxt_spec = pl.BlockSpec(
        stored, lambda b, hf, s, pt: (pt[b, jnp.minimum((hf + 1) * half_pages, npages - 1)], 0, 0, 0, 0))
    return pl.pallas_call(
        functools.partial(_compress_sample_body, nchunk=nchunk, half_rows=half_rows),
        grid_spec=pltpu.PrefetchScalarGridSpec(
            num_scalar_prefetch=1,
            grid=(nbatch, CMP_HALVES, nsteps),
            in_specs=[page_spec(k) for k in range(npg)] + [nxt_spec] +
                     [pl.BlockSpec((1, TAIL_ROWS, KV_WIDTH), lambda b, hf, s, pt: (b, 0, 0))] + _cmp_weight_specs(),
            out_specs=pl.BlockSpec((1, half_rows // CMP_STRIDE, KV_WIDTH), lambda b, hf, s, pt: (b, hf, 0)),
            scratch_shapes=[pltpu.VMEM((NSA_KV_HEADS, nchunk * CMP_STRIDE, 2 * HEAD_DIM), f32),
                            pltpu.VMEM((NSA_KV_HEADS, nchunk + SUBLANES, 4 * CMP_HIDDEN), f32)]),
        out_shape=jax.ShapeDtypeStruct((nbatch, npages * PAGE_SIZE // CMP_STRIDE, KV_WIDTH), f32),
        compiler_params=_cparams(("arbitrary", "arbitrary", "arbitrary")),
        name="compress_sample",
    )(page_table, *([cache] * (npg + 1)), newc, cw['wcat'], cw['pe'], cw['w1f'], cw['b1'], cw['w2bd'], cw['b2'])


def _split_dot_r(e, x):
    hi = x.astype(bf16)
    lo = (x - hi.astype(f32)).astype(bf16)
    return _dot(e, hi) + _dot(e, lo)


def _softmax_rows(s, mask):
    s = jnp.where(mask, s, NEG_INF)
    m = jnp.max(s, axis=0, keepdims=True)
    e = jnp.where(mask, jnp.exp(s - m), 0.0)
    return e / jnp.maximum(jnp.sum(e, axis=0, keepdims=True), 1e-30)


def _nsa_sample_body(pt_ref, *refs, tnew, nsel, nselp):
    npg = PAGES_PER_STEP
    pages = refs[:npg]
    (qb_ref, qr_ref, kc_ref, win_ref, neww_ref, news_ref, g_ref, wsel_ref, gm_ref, pick_ref, eexp_ref, o_ref,
     selq_ref, score_ref, m_ref, l_ref, acc_ref, oc_ref, ow_ref) = refs[npg:]
    s = pl.program_id(1)
    nselq = selq_ref.shape[1]
    rows_per_head = LANES // NSA_KV_HEADS
    qb = qb_ref[0]
    col = lax.broadcasted_iota(jnp.int32, (1, LANES), 1)
    tq = col % tnew
    qpos = PAST_LEN + tq
    pad_tail = lambda ref: jnp.concatenate(
        [ref[0], jnp.zeros((LANES - TAIL_ROWS, KV_WIDTH), f32)], axis=0).astype(bf16)

    @pl.when(s == 0)
    def _():
        kc = kc_ref[0].astype(bf16)
        ncmp = kc.shape[0]
        cend = lax.broadcasted_iota(jnp.int32, (ncmp, 1), 0) * CMP_STRIDE + (CMP_BLOCK - 1)
        p_c = _softmax_rows(_dot(kc, qb), cend <= qpos)
        oc_ref[...] = _dot(p_c.T.astype(bf16), kc)
        imp = _split_dot(p_c, gm_ref[...])
        p_slc = _split_dot_r(wsel_ref[...], imp)
        blk = lax.broadcasted_iota(jnp.int32, (nselp, LANES), 0)
        cur = qpos // SEL_BLOCK
        forced = (blk == 0) | (blk == cur) | (blk == cur - 1)
        score = jnp.where(forced, FORCE_SCORE, p_slc)
        score = jnp.where((blk * SEL_BLOCK <= qpos) & (blk < nsel), score, -1.0)
        score_ref[...] = score

        def rank_group(gi, rank):
            rows = score_ref[pl.ds(pl.multiple_of(gi * SUBLANES, SUBLANES), SUBLANES), :]
            for j in range(SUBLANES):
                row = rows[j:j + 1, :]
                ahead = (row > score) | ((row == score) & (blk > gi * SUBLANES + j))
                rank = rank + jnp.where(ahead, 1.0, 0.0)
            return rank

        rank = lax.fori_loop(0, nselp // SUBLANES, rank_group, jnp.zeros((nselp, LANES), f32))
        sel_t = jnp.where((rank < float(SEL_TOPK)) & (score >= 0.0), 1.0, 0.0)
        selq_ref[...] = jnp.concatenate([sel_t, jnp.zeros((nselq - nselp, LANES), f32)], axis=0).T

        kw = jnp.concatenate([win_ref[0].astype(bf16), pad_tail(neww_ref)], axis=0)
        idx = lax.broadcasted_iota(jnp.int32, (kw.shape[0], 1), 0)
        wbuf = win_ref.shape[1]
        p_w = _softmax_rows(_dot(kw, qb), (idx <= wbuf + tq) & (idx >= wbuf + tq - WINDOW))
        ow_ref[...] = _dot(p_w.T.astype(bf16), kw)

        m_ref[...] = jnp.full(m_ref.shape, NEG_INF, f32)
        l_ref[...] = jnp.zeros(l_ref.shape, f32)
        acc_ref[...] = jnp.zeros(acc_ref.shape, f32)

    def fold(scores, masks, e_dot_v):
        hrows = [slice(h * rows_per_head, (h + 1) * rows_per_head) for h in range(NSA_KV_HEADS)]
        m_new, alpha, probs = [], [], []
        for h in range(NSA_KV_HEADS):
            sc = jnp.where(masks[h], scores[h], NEG_INF)
            m_old = m_ref[hrows[h], 0:1]
            m_new.append(jnp.maximum(m_old, jnp.max(sc, axis=1, keepdims=True)))
            alpha.append(jnp.exp(m_old - m_new[h]))
            e = jnp.where(masks[h], jnp.exp(sc - m_new[h]), 0.0)
            l_ref[hrows[h], :] = jnp.broadcast_to(
                alpha[h] * l_ref[hrows[h], 0:1] + jnp.sum(e, axis=1, keepdims=True), (rows_per_head, LANES))
            probs.append(e.astype(bf16))
        pv = [e_dot_v(h, probs[h]) for h in range(NSA_KV_HEADS)]
        for h in range(NSA_KV_HEADS):
            acc_ref[hrows[h], :] = alpha[h] * acc_ref[hrows[h], :] + pv[h]
            m_ref[hrows[h], :] = jnp.broadcast_to(m_new[h], (rows_per_head, LANES))

    heads = range(NSA_KV_HEADS)
    hrow = lambda h: slice(h * rows_per_head, (h + 1) * rows_per_head)
    k_t = [jnp.concatenate([pages[k][0, h, 0] for k in range(npg)], axis=1).astype(bf16) for h in heads]
    v_t = [jnp.concatenate([pages[k][0, h, 1] for k in range(npg)], axis=1).astype(bf16) for h in heads]
    step_sel = [_dot(selq_ref[hrow(h), :].astype(bf16), pick_ref[s]) for h in heads]
    picked = [_dot(step_sel[h].astype(bf16), eexp_ref[...]) > 0.5 for h in heads]
    fold([_dot(qr_ref[0, hrow(h), :], k_t[h]) for h in heads], picked, lambda h, e: _dot_nt(e, v_t[h]))

    @pl.when(s == pl.num_programs(1) - 1)
    def _():
        ns = news_ref[0]
        kidx = lax.broadcasted_iota(jnp.int32, (1, TAIL_ROWS), 1)
        t_row = lax.broadcasted_iota(jnp.int32, (rows_per_head, 1), 0) % tnew
        nb = nsel - 1
        k_n = [ns[:, h * 2 * HEAD_DIM:h * 2 * HEAD_DIM + HEAD_DIM].astype(bf16) for h in heads]
        v_n = [ns[:, h * 2 * HEAD_DIM + HEAD_DIM:(h + 1) * 2 * HEAD_DIM].astype(bf16) for h in heads]
        fold([_dot_nt(qr_ref[0, hrow(h), :], k_n[h]) for h in heads],
             [(selq_ref[hrow(h), nb:nb + 1] > 0.5) & (kidx <= t_row) for h in heads],
             lambda h, e: _dot(e, v_n[h]))
        for h in heads:
            rows = hrow(h)
            o_s = acc_ref[rows, :] / jnp.maximum(l_ref[rows, 0:1], 1e-30)
            vals = slice(h * 2 * HEAD_DIM + HEAD_DIM, (h + 1) * 2 * HEAD_DIM)
            o_ref[0, rows, :] = (g_ref[0, 0, rows, 0:HEAD_DIM] * oc_ref[rows, vals] + g_ref[0, 1, rows, 0:HEAD_DIM] * o_s
                                 + g_ref[0, 2, rows, 0:HEAD_DIM] * ow_ref[rows, vals])


def nsa_sample(cache_sel_t, page_table, qblk, qrows, kv_cmp, win_buf, neww, news, gates, tnew):
    nbatch, npages = page_table.shape
    npg = PAGES_PER_STEP
    ncmp = kv_cmp.shape[1]
    nsel = npages * (PAGE_SIZE // SEL_BLOCK) + 1
    nselp = -(-nsel // SUBLANES) * SUBLANES
    cidx = jnp.arange(ncmp)[None, :] - (SEL_BLOCK // CMP_STRIDE) * jnp.arange(nselp)[:, None]
    mult = jnp.array([1, 2, 2, 2, 1], f32)
    wsel = jnp.where((cidx >= 0) & (cidx <= 4), mult[jnp.clip(cidx, 0, 4)], 0.0).astype(bf16)
    c = jnp.arange(LANES)
    same = (c[:, None] // (NSA_GROUP * tnew) == c[None, :] // (NSA_GROUP * tnew)) & (c[:, None] % tnew == c[None, :] % tnew)
    gm = same.astype(bf16)
    nsteps = npages // npg
    blocks_per_step = npg * (PAGE_SIZE // SEL_BLOCK)
    nselq = -(-nselp // LANES) * LANES
    blk_of = jnp.arange(nsteps)[:, None, None] * blocks_per_step + jnp.arange(LANES)[None, None, :]
    pick = ((jnp.arange(nselq)[None, :, None] == blk_of) & (jnp.arange(LANES) < blocks_per_step)).astype(bf16)
    eexp = (jnp.arange(LANES)[:, None] == jnp.arange(npg * PAGE_SIZE)[None, :] // SEL_BLOCK).astype(bf16)
    b3 = lambda b, s, pt: (b, 0, 0)
    c2 = lambda b, s, pt: (0, 0)
    page_spec = lambda k: pl.BlockSpec((1, NSA_KV_HEADS, 2, HEAD_DIM, PAGE_SIZE),
                                       lambda b, s, pt: (pt[b, s * npg + k], 0, 0, 0, 0))
    wbuf = win_buf.shape[1]
    return pl.pallas_call(
        functools.partial(_nsa_sample_body, tnew=tnew, nsel=nsel, nselp=nselp),
        grid_spec=pltpu.PrefetchScalarGridSpec(
            num_scalar_prefetch=1,
            grid=(nbatch, nsteps),
            in_specs=[page_spec(k) for k in range(npg)] + [
                pl.BlockSpec((1, KV_WIDTH, LANES), b3),
                pl.BlockSpec((1, LANES, HEAD_DIM), b3),
                pl.BlockSpec((1, ncmp, KV_WIDTH), b3),
                pl.BlockSpec((1, wbuf, KV_WIDTH), b3),
                pl.BlockSpec((1, TAIL_ROWS, KV_WIDTH), b3),
                pl.BlockSpec((1, TAIL_ROWS, KV_WIDTH), b3),
                pl.BlockSpec((1, 3, LANES, LANES), lambda b, s, pt: (b, 0, 0, 0)),
                pl.BlockSpec((nselp, ncmp), c2),
                pl.BlockSpec((LANES, LANES), c2),
                pl.BlockSpec((nsteps, nselq, LANES), lambda b, s, pt: (0, 0, 0)),
                pl.BlockSpec((LANES, npg * PAGE_SIZE), c2)],
            out_specs=pl.BlockSpec((1, LANES, HEAD_DIM), b3),
            scratch_shapes=[pltpu.VMEM((LANES, nselq), f32), pltpu.VMEM((nselp, LANES), f32),
                            pltpu.VMEM((LANES, LANES), f32), pltpu.VMEM((LANES, LANES), f32),
                            pltpu.VMEM((LANES, HEAD_DIM), f32),
                            pltpu.VMEM((LANES, KV_WIDTH), f32), pltpu.VMEM((LANES, KV_WIDTH), f32)]),
        out_shape=jax.ShapeDtypeStruct((nbatch, LANES, HEAD_DIM), f32),
        compiler_params=_cparams(("arbitrary", "arbitrary")),
        name="nsa_sample",
    )(page_table, *([cache_sel_t] * npg), qblk, qrows, kv_cmp, win_buf, neww, news, gates, wsel, gm, pick, eexp)


def _sample_nsa_inputs(p_all, b, t):
    q = p_all[:, Q0:Q0 + NSA_WIDTH].reshape(b, t, NSA_KV_HEADS, NSA_GROUP, HEAD_DIM) * (HEAD_DIM ** -0.5)
    qt = jnp.transpose(q, (0, 2, 4, 3, 1)).reshape(b, NSA_KV_HEADS, HEAD_DIM, NSA_GROUP * t)
    qt = jnp.pad(qt, ((0, 0), (0, 0), (0, HEAD_DIM), (0, 0)))
    eye = jnp.eye(NSA_KV_HEADS, dtype=f32)
    qblk = (qt[:, :, :, None, :] * eye[None, :, None, :, None]).reshape(b, KV_WIDTH, LANES).astype(bf16)
    qrows = jnp.transpose(q, (0, 2, 3, 1, 4)).reshape(b, LANES, HEAD_DIM).astype(bf16)
    g = p_all[:, GN0:GN0 + 3 * NSA_HEADS].reshape(b, t, NSA_KV_HEADS, NSA_GROUP, 3)
    g = jnp.transpose(g, (0, 4, 2, 3, 1)).reshape(b, 3, LANES)
    gates = jnp.broadcast_to(g[..., None], (b, 3, LANES, LANES))
    tail = lambda c0: jnp.pad(p_all[:, c0:c0 + KV_WIDTH].reshape(b, t, KV_WIDTH), ((0, 0), (0, TAIL_ROWS - t), (0, 0)))
    return qblk, qrows, gates, tail(KVC0), tail(KVS0), tail(KVW0)


def _prepare_weights(lp):
    w_in = lp['w_in']
    nsa_end = NSA_WIDTH + 3 * KV_WIDTH
    w_all = jnp.concatenate([
        w_in[:, NSA_PROJ:NSA_PROJ + RWKV_PROJ], w_in[:, nsa_end:NSA_PROJ],
        jnp.zeros((D_MODEL, RW_PAD - RWKV_PROJ - 3 * NSA_HEADS), w_in.dtype),
        w_in[:, :nsa_end], w_in[:, NSA_PROJ + RWKV_PROJ:]], axis=1).astype(bf16)
    pad_rows = lambda w, r0: jnp.zeros((LORA_PAD, RWKV_WIDTH), f32).at[r0:r0 + w.shape[0]].set(w).astype(bf16)
    head = jnp.arange(RWKV_WIDTH) // HEAD_DIM
    vecs = jnp.stack([lp['rwkv_w0'], lp['rwkv_a0'], lp['rwkv_k_k'], lp['rwkv_k_a'],
                      lp['rwkv_r_k'].reshape(RWKV_WIDTH), lp['rwkv_ln_g'], lp['rwkv_ln_b'],
                      jnp.zeros((RWKV_WIDTH,), f32)])
    return {
        'w_all': w_all,
        'mu': jnp.pad(lp['rwkv_mu'], (0, RW_PAD - RWKV_PROJ)).reshape(1, RW_PAD),
        'vecs': vecs,
        'w2p': pad_rows(lp['rwkv_w2'], 0),
        'a2p': pad_rows(lp['rwkv_a2'], DECAY_LORA),
        'g2p': pad_rows(lp['rwkv_g2'], DECAY_LORA + AAA_LORA),
        'e': (head[:, None] == head[None, :]).astype(bf16),
        'p_nsa': lp['p_nsa'].astype(bf16), 'p_rwkv': lp['p_rwkv'].astype(bf16),
        'w_out': lp['w_out'].astype(bf16),
        'mlp_w1': lp['mlp_w1'].astype(bf16), 'mlp_w2': lp['mlp_w2'].astype(bf16),
    }


PROJ_ROW_TILE = 1024
DENSE_ROW_TILE = 512
SMALL_ROW_TILE = 256


def _group_forward(x, pos_rows, prev, s0, lp, wts, final_g, o_nsa_fn):
    nbatch, seq = x.shape[:2]
    x2d = x.reshape(nbatch * seq, D_MODEL)
    tm = min(DENSE_ROW_TILE, nbatch * seq)
    ts = min(SMALL_ROW_TILE, nbatch * seq)
    tq, tkv = _rot_tables(pos_rows)
    p_all = input_projection(x2d, lp['norm1_g'], wts['w_all'], tq, tkv, min(PROJ_ROW_TILE, nbatch * seq))
    (y_rw, g_rw, bonus_rw), s_new = rwkv7(p_all, prev, s0, wts, nbatch, seq, ts)
    o_nsa = o_nsa_fn(p_all)
    h, hn = merge_project(o_nsa, y_rw, g_rw, bonus_rw, p_all, x2d, wts['p_nsa'], wts['p_rwkv'], wts['w_out'],
                          lp['norm2_g'], wts['vecs'], wts['e'], ts)
    y = mlp_residual_norm(hn, h, wts['mlp_w1'], wts['mlp_w2'], final_g, tm)
    return y.reshape(nbatch, seq, D_MODEL), p_all, s_new


def _kv_rows(p_all, col0, nbatch, seq):
    return p_all[:, col0:col0 + KV_WIDTH].reshape(nbatch, seq, NSA_KV_HEADS, 2, HEAD_DIM)


def sample_nsa_attention(p_all, cache_cmp, cache_sel, win_buf, page_table, cw, b, t):
    qblk, qrows, gates, newc, news, neww = _sample_nsa_inputs(p_all, b, t)
    cache_cmp_t = jnp.transpose(cache_cmp, (0, 2, 3, 4, 1))
    cache_sel_t = jnp.transpose(cache_sel, (0, 2, 3, 4, 1))
    kv_cmp = compress_sample(cache_cmp_t, page_table, newc, cw)
    o = nsa_sample(cache_sel_t, page_table, qblk, qrows, kv_cmp,
                   win_buf.reshape(b, win_buf.shape[1], KV_WIDTH), neww, news, gates, t)
    o = o.reshape(b, NSA_KV_HEADS, NSA_GROUP, t, HEAD_DIM)
    return jnp.transpose(o, (0, 3, 1, 2, 4)).reshape(b * t, NSA_WIDTH)


def kernel(x_prompt, x_sample, cache_cmp_kv, cache_sel_kv, state_nsa_win, state_rwkv, state_rwkv_shift,
           page_table, norm1_g, w_in, cmp_pe, cmp_w1, cmp_b1, cmp_w2, cmp_b2, rwkv_mu, rwkv_w0, rwkv_w2,
           rwkv_a0, rwkv_a2, rwkv_g2, rwkv_k_k, rwkv_k_a, rwkv_r_k, rwkv_ln_g, rwkv_ln_b, p_nsa, p_rwkv,
           w_out, norm2_g, mlp_w1, mlp_w2, final_g):
    l = 0
    lp = {'norm1_g': norm1_g[l], 'w_in': w_in[l], 'cmp_pe': cmp_pe[l], 'cmp_w1': cmp_w1[l],
          'cmp_b1': cmp_b1[l], 'cmp_w2': cmp_w2[l], 'cmp_b2': cmp_b2[l], 'rwkv_mu': rwkv_mu[l],
          'rwkv_w0': rwkv_w0[l], 'rwkv_w2': rwkv_w2[l], 'rwkv_a0': rwkv_a0[l], 'rwkv_a2': rwkv_a2[l],
          'rwkv_g2': rwkv_g2[l], 'rwkv_k_k': rwkv_k_k[l], 'rwkv_k_a': rwkv_k_a[l], 'rwkv_r_k': rwkv_r_k[l],
          'rwkv_ln_g': rwkv_ln_g[l], 'rwkv_ln_b': rwkv_ln_b[l], 'p_nsa': p_nsa[l], 'p_rwkv': p_rwkv[l],
          'w_out': w_out[l], 'norm2_g': norm2_g[l], 'mlp_w1': mlp_w1[l], 'mlp_w2': mlp_w2[l]}
    wts = _prepare_weights(lp)
    bp, tp = x_prompt.shape[:2]
    bs, ts = x_sample.shape[:2]
    cw = _compress_weights(lp)
    zero_state = jnp.zeros((bp, RWKV_HEADS, HEAD_DIM, HEAD_DIM), f32)

    def prompt_nsa(p_all):
        return nsa_prompt(p_all, compress_prompt(p_all, cw, bp, tp), _branch_gates(p_all, bp, tp), bp, tp)

    yp, pp, sp = _group_forward(x_prompt, jnp.arange(tp, dtype=jnp.int32), None, zero_state, lp, wts, final_g,
                                prompt_nsa)

    def sample_nsa(p_all):
        return sample_nsa_attention(p_all, cache_cmp_kv[l], cache_sel_kv[l], state_nsa_win[l], page_table, cw, bs, ts)

    pos_s = PAST_LEN + jnp.arange(bs * ts, dtype=jnp.int32) % ts
    ys, ps, ss = _group_forward(x_sample, pos_s, state_rwkv_shift[l], state_rwkv[l], lp, wts, final_g,
                                sample_nsa)
    win_new = [jnp.concatenate([state_nsa_win[l], _kv_rows(ps, KVW0, bs, ts)], axis=1)[:, ts:]]
    wlen = min(WINDOW, tp)
    shift = lambda p_all, nb, t: p_all.reshape(nb, t, P_WIDTH)[:, -1, RW0:RW0 + RWKV_PROJ]
    return (yp, ys,
            _kv_rows(pp, KVC0, bp, tp)[None], _kv_rows(ps, KVC0, bs, ts)[None],
            _kv_rows(pp, KVS0, bp, tp)[None], _kv_rows(ps, KVS0, bs, ts)[None],
            _kv_rows(pp, KVW0, bp, tp)[None, :, tp - wlen:], win_new[0][None],
            sp[None], ss[None],
            shift(pp, bp, tp)[None], shift(ps, bs, ts)[None])
```
